```python
import functools
import jax, jax.numpy as jnp
from jax import lax
import numpy as np

D_MODEL = 1024
BATCH = 4
SEQ = 8192
DEPTH = 1
DEC_BATCH = 8
DEC_SEQ = 32
PAST_LEN = 4096

CHUNK = 64
D_MIX = D_MODEL
M_HEADS = 4
M_DIM = (D_MIX // 2) // M_HEADS
A_HEADS = 8
A_KV_HEADS = 2
A_GROUP = A_HEADS // A_KV_HEADS
A_DIM = (D_MIX // 2) // A_HEADS
WINDOW = 128
WIN_CHUNKS = WINDOW // CHUNK
D_FF = 4 * D_MODEL
EPS = 1e-6
M_W = M_HEADS * M_DIM
A_QW = A_HEADS * A_DIM
A_KW = A_KV_HEADS * A_DIM
SPLITS = (M_W, M_W, M_W, M_W, M_HEADS, M_HEADS, A_QW, A_KW, A_KW)
N_IN = sum(SPLITS)

kernel_name = 'hybrid_mlstm_swa_stream_step'


def rmsnorm(x, g):
    xf = x.astype(jnp.float32)
    y = xf * lax.rsqrt(jnp.mean(xf * xf, axis=-1, keepdims=True) + EPS)
    return (y * g.astype(jnp.float32)).astype(x.dtype)


def alibi_slopes():
    return jnp.exp2(-8.0 * jnp.arange(1, A_HEADS + 1, dtype=jnp.float32) / A_HEADS)


def ada_modulation(c, w_ada, b_ada, dtype):
    mod = jax.nn.silu(c.astype(jnp.float32)) @ w_ada.astype(jnp.float32) + b_ada.astype(jnp.float32)
    return [m[:, None, :].astype(dtype) for m in jnp.split(mod, 6, axis=-1)]


def mlstm_chunkwise(q, k, v, log_i, log_f, c0, n0, m0, chunk):
    B, L, H, _ = q.shape
    DV = v.shape[-1]
    nc = L // chunk

    def blocks(a):
        return a.reshape(B, nc, chunk, H, a.shape[-1]).transpose(1, 0, 3, 2, 4)

    def gblocks(a):
        return a.reshape(B, nc, chunk, H).transpose(1, 0, 3, 2)

    causal = jnp.tril(jnp.ones((chunk, chunk), dtype=bool))

    def step(carry, inp):
        c, n, m = carry
        qc, kc, vc, li, lf = inp
        b = jnp.cumsum(lf, axis=-1)
        dmat = jnp.where(causal, b[..., :, None] - b[..., None, :] + li[..., None, :], -jnp.inf)
        inter = b + m[..., None]
        m_t = jnp.maximum(inter, jnp.max(dmat, axis=-1))
        w_intra = jnp.exp(dmat - m_t[..., None])
        w_inter = jnp.exp(inter - m_t)
        s = jnp.einsum('bhtd,bhsd->bhts', qc, kc) * w_intra
        num = jnp.einsum('bhts,bhsv->bhtv', s, vc) + w_inter[..., None] * jnp.einsum('bhtd,bhdv->bhtv', qc, c)
        den = jnp.sum(s, axis=-1) + w_inter * jnp.einsum('bhtd,bhd->bht', qc, n)
        h = num / jnp.maximum(jnp.abs(den), jnp.exp(-m_t))[..., None]
        m_new = m_t[..., -1]
        w_src = jnp.exp(b[..., -1:] - b + li - m_new[..., None])
        w_old = jnp.exp(b[..., -1] + m - m_new)
        c_new = w_old[..., None, None] * c + jnp.einsum('bhs,bhsd,bhsv->bhdv', w_src, kc, vc)
        n_new = w_old[..., None] * n + jnp.einsum('bhs,bhsd->bhd', w_src, kc)
        return (c_new, n_new, m_new), h

    (c, n, m), h = lax.scan(step, (c0, n0, m0),
                            (blocks(q), blocks(k), blocks(v), gblocks(log_i), gblocks(log_f)))
    h = h.transpose(1, 0, 3, 2, 4).reshape(B, L, H, DV)
    return h, c, n, m


def band_attention(qb, kb, vb, q_pos, k_pos, sinks):
    slopes = alibi_slopes().reshape(A_KV_HEADS, A_GROUP)
    qc = q_pos // CHUNK
    kc = k_pos // CHUNK
    allowed = ((kc[:, None, :] <= qc[:, :, None]) & (kc[:, None, :] >= qc[:, :, None] - WIN_CHUNKS)
               & (k_pos[:, None, :] >= 0))
    dist = jnp.abs(q_pos[:, :, None] - k_pos[:, None, :]).astype(jnp.float32)
    logits = jnp.einsum('bntkgd,bnskd->bnkgts', qb.astype(jnp.float32), kb.astype(jnp.float32)) * (A_DIM ** -0.5)
    logits = logits - slopes[None, None, :, :, None, None] * dist[None, :, None, None, :, :]
    logits = jnp.where(allowed[None, :, None, None, :, :], logits, -jnp.inf)
    sink = sinks.astype(jnp.float32).reshape(A_KV_HEADS, A_GROUP)[None, None, :, :, None, None]
    mx = jnp.maximum(jnp.max(logits, axis=-1, keepdims=True), sink)
    p = jnp.exp(logits - mx)
    probs = p / (jnp.sum(p, axis=-1, keepdims=True) + jnp.exp(sink - mx))
    return jnp.einsum('bnkgts,bnskd->bntkgd', probs, vb.astype(jnp.float32))


def swa_prompt(aq, ak, av, sinks):
    B, L = aq.shape[:2]
    nb = L // CHUNK
    pad = WIN_CHUNKS * CHUNK
    qb = aq.reshape(B, nb, CHUNK, A_KV_HEADS, A_GROUP, A_DIM)

    def band(a):
        ap = jnp.pad(a, ((0, 0), (pad, 0), (0, 0), (0, 0))).reshape(B, nb + WIN_CHUNKS, CHUNK, A_KV_HEADS, A_DIM)
        return jnp.concatenate([ap[:, j:j + nb] for j in range(WIN_CHUNKS + 1)], axis=2)

    q_pos = jnp.arange(L, dtype=jnp.int32).reshape(nb, CHUNK)
    k_pos = ((jnp.arange(nb, dtype=jnp.int32)[:, None] - WIN_CHUNKS) * CHUNK
             + jnp.arange((WIN_CHUNKS + 1) * CHUNK, dtype=jnp.int32)[None, :])
    out = band_attention(qb, band(ak), band(av), q_pos, k_pos, sinks)
    return out.reshape(B, L, A_QW)


def prompt_mixers(mq, mk, mv, log_i, log_f, aq, ak, av, sinks):
    B = mq.shape[0]
    c0 = jnp.zeros((B, M_HEADS, M_DIM, M_DIM), jnp.float32)
    n0 = jnp.zeros((B, M_HEADS, M_DIM), jnp.float32)
    m0 = jnp.zeros((B, M_HEADS), jnp.float32)
    h_m, c, n, m = mlstm_chunkwise(mq, mk, mv, log_i, log_f, c0, n0, m0, CHUNK)
    attn = swa_prompt(aq, ak, av, sinks)
    return h_m, attn, (ak[:, -WINDOW:], av[:, -WINDOW:], c, n, m)


def sample_mixers(mq, mk, mv, log_i, log_f, aq, ak, av, sinks, cache_k, cache_v, st_c, st_n, st_m):
    B, T = mq.shape[:2]
    h_m, c, n, m = mlstm_chunkwise(mq, mk, mv, log_i, log_f, st_c.astype(jnp.float32),
                                   st_n.astype(jnp.float32), st_m.astype(jnp.float32), T)
    wb = cache_k.shape[1]
    k_all = jnp.concatenate([cache_k.astype(ak.dtype), ak], axis=1)
    v_all = jnp.concatenate([cache_v.astype(av.dtype), av], axis=1)
    q_pos = (PAST_LEN + jnp.arange(T, dtype=jnp.int32))[None, :]
    k_pos = (PAST_LEN - wb + jnp.arange(wb + T, dtype=jnp.int32))[None, :]
    attn = band_attention(aq.reshape(B, 1, T, A_KV_HEADS, A_GROUP, A_DIM), k_all[:, None], v_all[:, None],
                          q_pos, k_pos, sinks).reshape(B, T, A_QW)
    return h_m, attn, (k_all[:, -wb:], v_all[:, -wb:], c, n, m)


def trunk_layer(x, c, mixers, w_ada, b_ada, g_norm1, w_in, b_gates, g_q, g_k, sinks,
                g_mlstm_out, w_out, g_norm2, w_up, w_down):
    B, L, _ = x.shape
    sh1, sc1, ga1, sh2, sc2, ga2 = ada_modulation(c, w_ada, b_ada, x.dtype)
    h = rmsnorm(x, g_norm1) * (1 + sc1) + sh1
    z = h @ w_in
    offsets = [int(o) for o in np.cumsum(SPLITS)[:-1]]
    mq, mk, mv, mo, mi, mf, aq, ak, av = jnp.split(z, offsets, axis=-1)
    f32 = jnp.float32
    mq = mq.astype(f32).reshape(B, L, M_HEADS, M_DIM)
    mk = mk.astype(f32).reshape(B, L, M_HEADS, M_DIM) * (M_DIM ** -0.5)
    mv = mv.astype(f32).reshape(B, L, M_HEADS, M_DIM)
    gates = jnp.concatenate([mi, mf], axis=-1).astype(f32) + b_gates.astype(f32)
    log_i = gates[..., :M_HEADS]
    log_f = jax.nn.log_sigmoid(gates[..., M_HEADS:])
    aq = rmsnorm(aq.reshape(B, L, A_HEADS, A_DIM), g_q)
    ak = rmsnorm(ak.reshape(B, L, A_KV_HEADS, A_DIM), g_k)
    av = av.reshape(B, L, A_KV_HEADS, A_DIM)
    h_m, attn, state = mixers(mq, mk, mv, log_i, log_f, aq, ak, av, sinks)
    h_m = rmsnorm(h_m, g_mlstm_out) * jax.nn.sigmoid(mo.astype(f32).reshape(B, L, M_HEADS, M_DIM))
    mix = jnp.concatenate([h_m.reshape(B, L, M_W).astype(x.dtype), attn.astype(x.dtype)], axis=-1) @ w_out
    x = x + ga1 * mix
    h2 = rmsnorm(x, g_norm2) * (1 + sc2) + sh2
    u = jnp.square(jax.nn.relu(h2 @ w_up))
    x = x + ga2 * (u @ w_down)
    return x, state


def setup_inputs(seed: int = 0) -> dict:
    key = jax.random.key(seed)
    ks = jax.random.split(key, 24)
    f32 = jnp.float32
    wb = min(WINDOW, PAST_LEN)

    def nrm(k, shape, scale):
        return jax.random.normal(k, shape, f32) * scale

    b_i = nrm(ks[13], (DEPTH, M_HEADS), 0.1)
    b_f = jnp.linspace(3.0, 6.0, M_HEADS, dtype=f32)[None, :] + nrm(ks[14], (DEPTH, M_HEADS), 0.1)
    return {
        'x_prompt': nrm(ks[0], (BATCH, SEQ, D_MODEL), 1.0),
        'x_sample': nrm(ks[1], (DEC_BATCH, DEC_SEQ, D_MODEL), 1.0),
        'c_prompt': nrm(ks[2], (BATCH, D_MODEL), 1.0),
        'c_sample': nrm(ks[3], (DEC_BATCH, D_MODEL), 1.0),
        'cache_swa_k': nrm(ks[4], (DEPTH, DEC_BATCH, wb, A_KV_HEADS, A_DIM), 1.0),
        'cache_swa_v': nrm(ks[5], (DEPTH, DEC_BATCH, wb, A_KV_HEADS, A_DIM), 1.0),
        'state_mlstm_C': nrm(ks[6], (DEPTH, DEC_BATCH, M_HEADS, M_DIM, M_DIM), 0.1),
        'state_mlstm_n': nrm(ks[7], (DEPTH, DEC_BATCH, M_HEADS, M_DIM), 0.1),
        'state_mlstm_m': nrm(ks[8], (DEPTH, DEC_BATCH, M_HEADS), 1.0),
        'w_ada': nrm(ks[9], (DEPTH, D_MODEL, 6 * D_MODEL), 0.5 * D_MODEL ** -0.5),
        'b_ada': nrm(ks[10], (DEPTH, 6 * D_MODEL), 0.02),
        'g_norm1': 1.0 + nrm(ks[11], (DEPTH, D_MODEL), 0.02),
        'w_in': nrm(ks[12], (DEPTH, D_MODEL, N_IN), D_MODEL ** -0.5),
        'b_gates': jnp.concatenate([b_i, b_f], axis=-1),
        'g_q': 1.0 + nrm(ks[15], (DEPTH, A_DIM), 0.02),
        'g_k': 1.0 + nrm(ks[16], (DEPTH, A_DIM), 0.02),
        'sinks': nrm(ks[17], (DEPTH, A_HEADS), 0.5),
        'g_mlstm_out': 1.0 + nrm(ks[18], (DEPTH, M_HEADS, M_DIM), 0.02),
        'w_out': nrm(ks[19], (DEPTH, D_MIX, D_MODEL), D_MIX ** -0.5),
        'g_norm2': 1.0 + nrm(ks[20], (DEPTH, D_MODEL), 0.02),
        'w_up': nrm(ks[21], (DEPTH, D_MODEL, D_FF), D_MODEL ** -0.5),
        'w_down': nrm(ks[22], (DEPTH, D_FF, D_MODEL), D_FF ** -0.5),
    }


def reference(x_prompt, x_sample, c_prompt, c_sample, cache_swa_k, cache_swa_v, state_mlstm_C,
              state_mlstm_n, state_mlstm_m, w_ada, b_ada, g_norm1, w_in, b_gates, g_q, g_k, sinks,
              g_mlstm_out, w_out, g_norm2, w_up, w_down):
    xp, xs = x_prompt, x_sample
    sp, ss = [], []
    for l in range(DEPTH):
        weights = (w_ada[l], b_ada[l], g_norm1[l], w_in[l], b_gates[l], g_q[l], g_k[l], sinks[l],
                   g_mlstm_out[l], w_out[l], g_norm2[l], w_up[l], w_down[l])
        xp, st_p = trunk_layer(xp, c_prompt, prompt_mixers, *weights)
        smix = functools.partial(sample_mixers, cache_k=cache_swa_k[l], cache_v=cache_swa_v[l],
                                 st_c=state_mlstm_C[l], st_n=state_mlstm_n[l], st_m=state_mlstm_m[l])
        xs, st_s = trunk_layer(xs, c_sample, smix, *weights)
        sp.append(st_p)
        ss.append(st_s)

    def stack(sts, i):
        return jnp.stack([s[i] for s in sts], axis=0)

    return (xp, xs,
            stack(sp, 0), stack(sp, 1), stack(sp, 2), stack(sp, 3), stack(sp, 4),
            stack(ss, 0), stack(ss, 1), stack(ss, 2), stack(ss, 3), stack(ss, 4))
```

```python
import functools

import jax
import jax.numpy as jnp
from jax import lax
from jax.experimental import pallas as pl
from jax.experimental.pallas import tpu as pltpu

F32 = jnp.float32
BF16 = jnp.bfloat16

D_MODEL = 1024
CHUNK = 64
M_HEADS = 4
M_DIM = 128
M_W = M_HEADS * M_DIM
A_HEADS = 8
A_KV_HEADS = 2
A_DIM = 64
A_QW = A_HEADS * A_DIM
A_KW = A_KV_HEADS * A_DIM
WIN_CHUNKS = 2
D_FF = 4 * D_MODEL
EPS = 1e-6
PAST_LEN = 4096

LANES = 128
N_MAIN = 4 * M_W + A_QW + 2 * A_KW
N_PROJ = N_MAIN + LANES
VMEM_LIMIT = 56 * 1024 * 1024
NEG_INF = float("-inf")


def _params(sem):
    return pltpu.CompilerParams(dimension_semantics=sem, vmem_limit_bytes=VMEM_LIMIT)


def _const_spec(shape):
    nd = len(shape)
    return pl.BlockSpec(shape, lambda *_: (0,) * nd, pipeline_mode=pl.Buffered(1))


def _ada_body(c_ref, w_ref, b_ref, o_ref):
    c = c_ref[...]
    s = c * jax.nn.sigmoid(c)
    o_ref[...] = jnp.dot(s.astype(BF16), w_ref[...].astype(BF16),
                         preferred_element_type=F32) + b_ref[...]


def _ada_mod(c_all, w_ada, b_ada):
    rows = c_all.shape[0]
    return pl.pallas_call(
        _ada_body,
        grid=(6,),
        in_specs=[pl.BlockSpec((rows, D_MODEL), lambda j: (0, 0)),
                  pl.BlockSpec((D_MODEL, D_MODEL), lambda j: (0, j)),
                  pl.BlockSpec((1, D_MODEL), lambda j: (0, j))],
        out_specs=pl.BlockSpec((rows, D_MODEL), lambda j: (0, j)),
        out_shape=jax.ShapeDtypeStruct((rows, 6 * D_MODEL), F32),
        compiler_params=_params(("arbitrary",)),
        name="ada_mod",
    )(c_all, w_ada, b_ada)


def _group_sumsq(a, p_ref):
    sq = a * a
    hi = sq.astype(BF16)
    lo = (sq - hi.astype(F32)).astype(BF16)
    p = p_ref[...]
    return (jnp.dot(hi, p, preferred_element_type=F32)
            + jnp.dot(lo, p, preferred_element_type=F32))


def _in_proj_body(x_ref, sc_ref, sh_ref, g1_ref, w_ref, bg_ref, gq_ref, gk_ref, pq_ref, pk_ref,
                  mq_ref, mk_ref, mv_ref, og_ref, gt_ref, aq_ref, ak_ref, av_ref):
    s, r, d = x_ref.shape
    m = s * r
    x = x_ref[...]
    ms = jnp.mean(x * x, axis=-1, keepdims=True)
    y = x * lax.rsqrt(ms + EPS) * g1_ref[...]
    h = y * (1.0 + sc_ref[...]) + sh_ref[...]
    hb = h.reshape(m, d).astype(BF16)

    def proj(lo, width):
        return jnp.dot(hb, w_ref[:, lo:lo + width], preferred_element_type=F32)

    mq_ref[...] = proj(0, M_W).astype(BF16).reshape(s, r, M_W)
    mk_ref[...] = (proj(M_W, M_W) * (M_DIM ** -0.5)).astype(BF16).reshape(s, r, M_W)
    mv_ref[...] = proj(2 * M_W, M_W).astype(BF16).reshape(s, r, M_W)
    og_ref[...] = jax.nn.sigmoid(proj(3 * M_W, M_W)).astype(BF16).reshape(s, r, M_W)

    aq = proj(4 * M_W, A_QW)
    aqn = aq * lax.rsqrt(_group_sumsq(aq, pq_ref) * (1.0 / A_DIM) + EPS) * gq_ref[...]
    aq_ref[...] = aqn.astype(BF16).reshape(s, r, A_QW)
    ak = proj(4 * M_W + A_QW, A_KW)
    akn = ak * lax.rsqrt(_group_sumsq(ak, pk_ref) * (1.0 / A_DIM) + EPS) * gk_ref[...]
    ak_ref[...] = akn.reshape(s, r, A_KW)
    av_ref[...] = proj(4 * M_W + A_QW + A_KW, A_KW).reshape(s, r, A_KW)

    zg = proj(N_MAIN, LANES) + bg_ref[...]
    lane = lax.broadcasted_iota(jnp.int32, zg.shape, 1)
    log_sig = jnp.minimum(zg, 0.0) - jnp.log1p(jnp.exp(-jnp.abs(zg)))
    gates = jnp.where(lane < M_HEADS, zg, log_sig)
    gt_ref[...] = gates.T[:2 * M_HEADS, :]


def _in_proj(x, sc, sh, g1, w_proj, b_gates_pad, gq_t, gk_t, pq, pk, *, seqs, rows):
    ns, length, d = x.shape
    grid = (ns // seqs, length // rows)
    m = seqs * rows
    n_tok = ns * length

    def tile(width):
        return pl.BlockSpec((seqs, rows, width), lambda i, j: (i, j, 0))

    mod = pl.BlockSpec((seqs, 1, d), lambda i, j: (i, 0, 0))
    steps_per_seq = length // rows
    out_shapes = (
        jax.ShapeDtypeStruct((ns, length, M_W), BF16),
        jax.ShapeDtypeStruct((ns, length, M_W), BF16),
        jax.ShapeDtypeStruct((ns, length, M_W), BF16),
        jax.ShapeDtypeStruct((ns, length, M_W), BF16),
        jax.ShapeDtypeStruct((2 * M_HEADS, n_tok), F32),
        jax.ShapeDtypeStruct((ns, length, A_QW), BF16),
        jax.ShapeDtypeStruct((ns, length, A_KW), F32),
        jax.ShapeDtypeStruct((ns, length, A_KW), F32),
    )
    return pl.pallas_call(
        _in_proj_body,
        grid=grid,
        in_specs=[tile(d), mod, mod,
                  _const_spec((1, d)), _const_spec((d, N_PROJ)), _const_spec((1, LANES)),
                  _const_spec((1, A_QW)), _const_spec((1, A_KW)),
                  _const_spec((A_QW, A_QW)), _const_spec((A_KW, A_KW))],
        out_specs=(tile(M_W), tile(M_W), tile(M_W), tile(M_W),
                   pl.BlockSpec((2 * M_HEADS, m), lambda i, j: (0, i * steps_per_seq + j)),
                   tile(A_QW), tile(A_KW), tile(A_KW)),
        out_shape=out_shapes,
        compiler_params=_params(("arbitrary", "arbitrary")),
        name="in_proj",
    )(x, sc, sh, g1, w_proj, b_gates_pad, gq_t, gk_t, pq, pk)


def _mlstm_body(q_ref, k_ref, v_ref, og_ref, gt_ref, c0_ref, n0_ref, m0_ref, gout_ref,
                h_ref, c_out_ref, n_out_ref, m_out_ref, caug_ref, m_ref):
    t = q_ref.shape[1]
    ci = pl.program_id(1)
    eye_d = (lax.broadcasted_iota(jnp.int32, (M_DIM, M_DIM), 0)
             == lax.broadcasted_iota(jnp.int32, (M_DIM, M_DIM), 1))

    @pl.when(ci == 0)
    def _():
        for hd in range(M_HEADS):
            caug_ref[hd, :, :M_DIM] = c0_ref[0, hd]
            n_row = jnp.broadcast_to(n0_ref[0, hd:hd + 1, :], (M_DIM, M_DIM))
            n_col = jnp.sum(jnp.where(eye_d, n_row, 0.0), axis=1, keepdims=True)
            caug_ref[hd, :, M_DIM:] = jnp.broadcast_to(n_col, (M_DIM, M_DIM))
        m_ref[...] = m0_ref[0]

    row = lax.broadcasted_iota(jnp.int32, (t, t), 0)
    col = lax.broadcasted_iota(jnp.int32, (t, t), 1)
    tril = col <= row
    eye = col == row

    def to_col(r):
        return jnp.sum(jnp.where(eye, jnp.broadcast_to(r, (t, t)), 0.0), axis=1, keepdims=True)

    gt = gt_ref[...]
    for hd in range(M_HEADS):
        sl = slice(hd * M_DIM, (hd + 1) * M_DIM)
        li = gt[hd:hd + 1, :]
        lf = gt[M_HEADS + hd:M_HEADS + hd + 1, :]
        lf_col = to_col(lf)
        b_col = jnp.sum(jnp.where(tril, jnp.broadcast_to(lf, (t, t)), 0.0), axis=1, keepdims=True)
        b_row = jnp.sum(jnp.where(row <= col, jnp.broadcast_to(lf_col, (t, t)), 0.0),
                        axis=0, keepdims=True)
        c_row = li - b_row
        c_col = to_col(c_row)
        c_mat = jnp.broadcast_to(c_row, (t, t))
        m_prev = m_ref[hd:hd + 1, :]
        g_col = jnp.maximum(jnp.max(jnp.where(tril, c_mat, NEG_INF), axis=1, keepdims=True), m_prev)
        g_last = jnp.maximum(jnp.max(c_row, axis=1, keepdims=True), m_prev)
        m_t = b_col + g_col
        w_intra = jnp.exp(jnp.where(tril, c_mat - g_col, NEG_INF))
        w_inter = jnp.exp(m_prev - g_col)
        w_src = jnp.exp(c_col - g_last)
        w_old = jnp.exp(m_prev - g_last)

        q = q_ref[0, :, sl]
        k = k_ref[0, :, sl]
        v = v_ref[0, :, sl]
        s_mat = lax.dot_general(q, k, (((1,), (1,)), ((), ())), preferred_element_type=F32) * w_intra
        caug = caug_ref[hd]
        inter = jnp.dot(q, caug.astype(BF16), preferred_element_type=F32)
        num = (jnp.dot(s_mat.astype(BF16), v, preferred_element_type=F32)
               + w_inter * inter[:, :M_DIM])
        den = jnp.sum(s_mat, axis=1, keepdims=True) + w_inter * inter[:, M_DIM:M_DIM + 1]
        hh = num / jnp.maximum(jnp.abs(den), jnp.exp(-m_t))

        rhs = jnp.concatenate([v.astype(F32) * w_src, jnp.broadcast_to(w_src, (t, M_DIM))],
                              axis=1).astype(BF16)
        upd = lax.dot_general(k, rhs, (((0,), (0,)), ((), ())), preferred_element_type=F32)
        caug_ref[hd] = w_old * caug + upd
        m_ref[hd:hd + 1, :] = b_row[:, t - 1:t] + g_last

        hn = hh * lax.rsqrt(jnp.mean(hh * hh, axis=-1, keepdims=True) + EPS) * gout_ref[hd:hd + 1, :]
        h_ref[0, :, sl] = (hn * og_ref[0, :, sl].astype(F32)).astype(BF16)

    @pl.when(ci == pl.num_programs(1) - 1)
    def _():
        for hd in range(M_HEADS):
            c_out_ref[0, hd] = caug_ref[hd, :, :M_DIM]
            n_out_ref[0, hd:hd + 1, :] = jnp.sum(
                jnp.where(eye_d, caug_ref[hd, :, M_DIM:], 0.0), axis=0, keepdims=True)
        m_out_ref[0] = m_ref[...]


def _mlstm(mq, mk, mv, og, gt, c0, n0, m0, gout, *, chunk):
    ns, length, _ = mq.shape
    nc = length // chunk
    tile = pl.BlockSpec((1, chunk, M_W), lambda b, c: (b, c, 0))
    if gt.ndim == 2:
        gt_spec = pl.BlockSpec((2 * M_HEADS, chunk), lambda b, c: (0, b * nc + c))
    else:
        gt_spec = pl.BlockSpec((None, 2 * M_HEADS, chunk), lambda b, c: (b, 0, 0))
    st_c = pl.BlockSpec((1, M_HEADS, M_DIM, M_DIM), lambda b, c: (b, 0, 0, 0))
    st_n = pl.BlockSpec((1, M_HEADS, M_DIM), lambda b, c: (b, 0, 0))
    st_m = pl.BlockSpec((1, M_HEADS, 1), lambda b, c: (b, 0, 0))
    return pl.pallas_call(
        _mlstm_body,
        grid=(ns, nc),
        in_specs=[tile, tile, tile, tile, gt_spec, st_c, st_n, st_m,
                  _const_spec((M_HEADS, M_DIM))],
        out_specs=(tile, st_c, st_n, st_m),
        out_shape=(jax.ShapeDtypeStruct((ns, length, M_W), BF16),
                   jax.ShapeDtypeStruct((ns, M_HEADS, M_DIM, M_DIM), F32),
                   jax.ShapeDtypeStruct((ns, M_HEADS, M_DIM), F32),
                   jax.ShapeDtypeStruct((ns, M_HEADS, 1), F32)),
        scratch_shapes=[pltpu.VMEM((M_HEADS, M_DIM, 2 * M_DIM), F32),
                        pltpu.VMEM((M_HEADS, 1), F32)],
        compiler_params=_params(("arbitrary", "arbitrary")),
        name="mlstm",
    )(mq, mk, mv, og, gt, c0, n0, m0, gout)


def _swa_core(q, k_cat, v_cat, q_pos, k_pos, sink_ref, o_ref):
    tq = q.shape[0]
    tk = k_cat.shape[0]
    shift = CHUNK.bit_length() - 1
    assert 1 << shift == CHUNK
    qc = lax.shift_right_arithmetic(q_pos, shift)
    kc = lax.shift_right_arithmetic(k_pos, shift)
    allowed = (kc <= qc) & (kc >= qc - WIN_CHUNKS) & (k_pos >= 0)
    dist = jnp.abs(q_pos - k_pos).astype(F32)
    lane = lax.broadcasted_iota(jnp.int32, (tk, LANES), 1)
    low = lane < A_DIM

    def halves(a, j):
        if j == 0:
            lo_half = jnp.where(low, a, 0.0)
            return lo_half, pltpu.roll(lo_half, A_DIM, axis=1)
        hi_half = jnp.where(low, 0.0, a)
        return pltpu.roll(hi_half, A_DIM, axis=1), hi_half

    kz = [tuple(h.astype(BF16) for h in halves(k_cat, j)) for j in range(A_KV_HEADS)]
    vz = [tuple(h.astype(BF16) for h in halves(v_cat, j)) for j in range(A_KV_HEADS)]

    for p in range(A_HEADS // 2):
        j = (2 * p) // (A_HEADS // A_KV_HEADS)
        q_pair = q[:, p * LANES:(p + 1) * LANES]
        acc = None
        for e in range(2):
            head = 2 * p + e
            slope = 2.0 ** (-8.0 * (head + 1) / A_HEADS)
            sink = sink_ref[head]
            logits = lax.dot_general(q_pair, kz[j][e], (((1,), (1,)), ((), ())),
                                     preferred_element_type=F32)
            logits = jnp.where(allowed, logits - slope * dist, NEG_INF)
            mx = jnp.maximum(jnp.max(logits, axis=1, keepdims=True), sink)
            pr = jnp.exp(logits - mx)
            denom = jnp.sum(pr, axis=1, keepdims=True) + jnp.exp(sink - mx)
            part = jnp.dot(pr.astype(BF16), vz[j][e], preferred_element_type=F32) / denom
            acc = part if acc is None else acc + part
        o_ref[0, :, p * LANES:(p + 1) * LANES] = acc.astype(o_ref.dtype)


def _swa_prompt_body(sink_ref, q_ref, kp_ref, kc_ref, vp_ref, vc_ref, o_ref):
    tq = q_ref.shape[1]
    i = pl.program_id(1)
    k_cat = jnp.concatenate([kp_ref[0], kc_ref[0]], axis=0)
    v_cat = jnp.concatenate([vp_ref[0], vc_ref[0]], axis=0)
    q_pos = i * tq + lax.broadcasted_iota(jnp.int32, (tq, 1), 0)
    k_pos = (i - 1) * tq + lax.broadcasted_iota(jnp.int32, (1, 2 * tq), 1)
    _swa_core(q_ref[0], k_cat, v_cat, q_pos, k_pos, sink_ref, o_ref)


def _swa_prompt(aq, ak, av, sinks):
    ns, length, _ = aq.shape
    tq = 2 * CHUNK
    cur = lambda w: pl.BlockSpec((1, tq, w), lambda b, i: (b, i, 0))
    prev = lambda w: pl.BlockSpec((1, tq, w), lambda b, i: (b, jnp.maximum(i - 1, 0), 0))
    return pl.pallas_call(
        _swa_prompt_body,
        grid=(ns, length // tq),
        in_specs=[pl.BlockSpec(memory_space=pltpu.SMEM),
                  cur(A_QW), prev(A_KW), cur(A_KW), prev(A_KW), cur(A_KW)],
        out_specs=cur(A_QW),
        out_shape=jax.ShapeDtypeStruct((ns, length, A_QW), BF16),
        compiler_params=_params(("arbitrary", "arbitrary")),
        name="swa_prompt",
    )(sinks, aq, ak, ak, av, av)


def _swa_sample_body(sink_ref, q_ref, kn_ref, vn_ref, ck_ref, cv_ref, o_ref):
    tq = q_ref.shape[1]
    wb = ck_ref.shape[1]
    tk = 2 * LANES
    pad = jnp.zeros((tk - wb - tq, A_KW), F32)
    k_cat = jnp.concatenate([ck_ref[0], kn_ref[0], pad], axis=0)
    v_cat = jnp.concatenate([cv_ref[0], vn_ref[0], pad], axis=0)
    q_pos = PAST_LEN + lax.broadcasted_iota(jnp.int32, (tq, 1), 0)
    idx = lax.broadcasted_iota(jnp.int32, (1, tk), 1)
    k_pos = jnp.where(idx < wb + tq, PAST_LEN - wb + idx, -1)
    _swa_core(q_ref[0], k_cat, v_cat, q_pos, k_pos, sink_ref, o_ref)


def _swa_sample(aq, ak, av, cache_k, cache_v, sinks):
    ns, tq, _ = aq.shape
    wb = cache_k.shape[1]
    new = lambda w: pl.BlockSpec((1, tq, w), lambda b: (b, 0, 0))
    old = pl.BlockSpec((1, wb, A_KW), lambda b: (b, 0, 0))
    return pl.pallas_call(
        _swa_sample_body,
        grid=(ns,),
        in_specs=[pl.BlockSpec(memory_space=pltpu.SMEM),
                  new(A_QW), new(A_KW), new(A_KW), old, old],
        out_specs=new(A_QW),
        out_shape=jax.ShapeDtypeStruct((ns, tq, A_QW), BF16),
        compiler_params=_params(("arbitrary",)),
        name="swa_sample",
    )(sinks, aq, ak, av, cache_k, cache_v)


FF_BLOCK = 1024


def _out_ffn_body(x_ref, hm_ref, at_ref, ga1_ref, sc2_ref, sh2_ref, ga2_ref, g2_ref,
                  wo_ref, wu_ref, wd_ref, y_ref):
    s, r, d = x_ref.shape
    m = s * r
    hm = hm_ref[...].reshape(m, M_W)
    at = at_ref[...].reshape(m, A_QW)
    mix = (jnp.dot(hm, wo_ref[:M_W, :], preferred_element_type=F32)
           + jnp.dot(at, wo_ref[M_W:, :], preferred_element_type=F32))
    x1 = x_ref[...] + ga1_ref[...] * mix.reshape(s, r, d)
    ms = jnp.mean(x1 * x1, axis=-1, keepdims=True)
    h2 = (x1 * lax.rsqrt(ms + EPS) * g2_ref[...]) * (1.0 + sc2_ref[...]) + sh2_ref[...]
    hb = h2.reshape(m, d).astype(BF16)
    acc = jnp.zeros((m, d), F32)
    for c in range(D_FF // FF_BLOCK):
        u = jnp.dot(hb, wu_ref[:, c * FF_BLOCK:(c + 1) * FF_BLOCK], preferred_element_type=F32)
        u = jnp.square(jnp.maximum(u, 0.0)).astype(BF16)
        acc = acc + jnp.dot(u, wd_ref[c * FF_BLOCK:(c + 1) * FF_BLOCK, :],
                            preferred_element_type=F32)
    y_ref[...] = x1 + ga2_ref[...] * acc.reshape(s, r, d)


def _out_ffn(x, hm, at, ga1, sc2, sh2, ga2, g2, w_out, w_up, w_down, *, seqs, rows):
    ns, length, d = x.shape
    tile = lambda w: pl.BlockSpec((seqs, rows, w), lambda i, j: (i, j, 0))
    mod = pl.BlockSpec((seqs, 1, d), lambda i, j: (i, 0, 0))
    return pl.pallas_call(
        _out_ffn_body,
        grid=(ns // seqs, length // rows),
        in_specs=[tile(d), tile(M_W), tile(A_QW), mod, mod, mod, mod,
                  _const_spec((1, d)), _const_spec((d, d)), _const_spec((d, D_FF)),
                  _const_spec((D_FF, d))],
        out_specs=tile(d),
        out_shape=jax.ShapeDtypeStruct((ns, length, d), F32),
        compiler_params=_params(("arbitrary", "arbitrary")),
        name="out_ffn",
    )(x, hm, at, ga1, sc2, sh2, ga2, g2, w_out, w_up, w_down)


def _block_ones(n):
    g = jnp.arange(n, dtype=jnp.int32) // A_DIM
    return (g[:, None] == g[None, :]).astype(BF16)


def kernel(x_prompt, x_sample, c_prompt, c_sample, cache_swa_k, cache_swa_v, state_mlstm_C,
           state_mlstm_n, state_mlstm_m, w_ada, b_ada, g_norm1, w_in, b_gates, g_q, g_k, sinks,
           g_mlstm_out, w_out, g_norm2, w_up, w_down):
    depth = w_ada.shape[0]
    assert depth == 1
    bp, lp, d = x_prompt.shape
    bs, ls, _ = x_sample.shape
    wb = cache_swa_k.shape[2]

    pq = _block_ones(A_QW)
    pk = _block_ones(A_KW)
    xp, xs = x_prompt, x_sample
    for l in range(depth):
        wl = w_in[l]
        g_lo = 4 * M_W
        g_hi = g_lo + 2 * M_HEADS
        w_proj = jnp.concatenate(
            [wl[:, :g_lo], wl[:, g_hi:], wl[:, g_lo:g_hi],
             jnp.zeros((d, LANES - 2 * M_HEADS), wl.dtype)], axis=1).astype(BF16)
        bg = jnp.concatenate([b_gates[l], jnp.zeros((LANES - 2 * M_HEADS,), F32)])[None, :]
        gq_t = (jnp.tile(g_q[l], A_HEADS) * (A_DIM ** -0.5))[None, :]
        gk_t = jnp.tile(g_k[l], A_KV_HEADS)[None, :]
        g1 = g_norm1[l][None, :]
        g2 = g_norm2[l][None, :]
        wo = w_out[l].astype(BF16)
        wu = w_up[l].astype(BF16)
        wd = w_down[l].astype(BF16)

        c_all = jnp.concatenate([c_prompt, c_sample, jnp.zeros((16 - bp - bs, d), F32)], axis=0)
        mod = _ada_mod(c_all, w_ada[l], b_ada[l][None, :])
        mod_p = [mod[:bp, i * d:(i + 1) * d][:, None, :] for i in range(6)]
        mod_s = [mod[bp:bp + bs, i * d:(i + 1) * d][:, None, :] for i in range(6)]

        def run(x, mods, seqs, rows, mixer):
            sh1, sc1, ga1, sh2, sc2, ga2 = mods
            mq, mk, mv, og, gt, aq, ak, av = _in_proj(
                x, sc1, sh1, g1, w_proj, bg, gq_t, gk_t, pq, pk, seqs=seqs, rows=rows)
            hm, at, state = mixer(mq, mk, mv, og, gt, aq, ak, av)
            y = _out_ffn(x, hm, at, ga1, sc2, sh2, ga2, g2, wo, wu, wd, seqs=seqs, rows=rows)
            return y, state

        def prompt_mixer(mq, mk, mv, og, gt, aq, ak, av):
            c0 = jnp.zeros((bp, M_HEADS, M_DIM, M_DIM), F32)
            n0 = jnp.zeros((bp, M_HEADS, M_DIM), F32)
            m0 = jnp.zeros((bp, M_HEADS, 1), F32)
            hm, c, n, m = _mlstm(mq, mk, mv, og, gt, c0, n0, m0, g_mlstm_out[l], chunk=256)
            at = _swa_prompt(aq, ak, av, sinks[l])
            win = WIN_CHUNKS * CHUNK
            k_tail = ak[:, -win:].reshape(bp, win, A_KV_HEADS, A_DIM)
            v_tail = av[:, -win:].reshape(bp, win, A_KV_HEADS, A_DIM)
            return hm, at, (k_tail, v_tail, c, n, m[..., 0])

        def sample_mixer(mq, mk, mv, og, gt, aq, ak, av):
            gt3 = gt.reshape(2 * M_HEADS, bs, ls).transpose(1, 0, 2)
            hm, c, n, m = _mlstm(mq, mk, mv, og, gt3, state_mlstm_C[l], state_mlstm_n[l],
                                 state_mlstm_m[l][..., None], g_mlstm_out[l], chunk=ls)
            ck = cache_swa_k[l].reshape(bs, wb, A_KW)
            cv = cache_swa_v[l].reshape(bs, wb, A_KW)
            at = _swa_sample(aq, ak, av, ck, cv, sinks[l])
            k_all = jnp.concatenate([ck, ak], axis=1)[:, -wb:].reshape(bs, wb, A_KV_HEADS, A_DIM)
            v_all = jnp.concatenate([cv, av], axis=1)[:, -wb:].reshape(bs, wb, A_KV_HEADS, A_DIM)
            return hm, at, (k_all, v_all, c, n, m[..., 0])

        xp, st_p = run(xp, mod_p, 1, 512, prompt_mixer)
        xs, st_s = run(xs, mod_s, bs, ls, sample_mixer)

    return (xp, xs) + tuple(s[None] for s in st_p) + tuple(s[None] for s in st_s)
```

```python
import functools

import jax
import jax.numpy as jnp
from jax import lax
from jax.experimental import pallas as pl
from jax.experimental.pallas import tpu as pltpu

F32 = jnp.float32
BF16 = jnp.bfloat16

D_MODEL = 1024
CHUNK = 64
M_HEADS = 4
M_DIM = 128
M_W = M_HEADS * M_DIM
A_HEADS = 8
A_KV_HEADS = 2
A_DIM = 64
A_QW = A_HEADS * A_DIM
A_KW = A_KV_HEADS * A_DIM
WIN_CHUNKS = 2
D_FF = 4 * D_MODEL
EPS = 1e-6
PAST_LEN = 4096

LANES = 128
N_MAIN = 4 * M_W + A_QW + 2 * A_KW
N_PROJ = N_MAIN + LANES
VMEM_LIMIT = 56 * 1024 * 1024
NEG_INF = float("-inf")
ROW_TILE = 512
MLSTM_CHUNK = 256


def _params(sem):
    return pltpu.CompilerParams(dimension_semantics=sem, vmem_limit_bytes=VMEM_LIMIT)


def _const_spec(shape):
    nd = len(shape)
    return pl.BlockSpec(shape, lambda *_: (0,) * nd, pipeline_mode=pl.Buffered(1))


def _ada_body(c_ref, w_ref, b_ref, o_ref):
    c = c_ref[...]
    s = c * jax.nn.sigmoid(c)
    o_ref[...] = jnp.dot(s.astype(BF16), w_ref[...].astype(BF16),
                         preferred_element_type=F32) + b_ref[...]


def _ada_mod(c_all, w_ada, b_ada):
    rows = c_all.shape[0]
    return pl.pallas_call(
        _ada_body,
        grid=(6,),
        in_specs=[pl.BlockSpec((rows, D_MODEL), lambda j: (0, 0)),
                  pl.BlockSpec((D_MODEL, D_MODEL), lambda j: (0, j)),
                  pl.BlockSpec((1, D_MODEL), lambda j: (0, j))],
        out_specs=pl.BlockSpec((rows, D_MODEL), lambda j: (0, j)),
        out_shape=jax.ShapeDtypeStruct((rows, 6 * D_MODEL), F32),
        compiler_params=_params(("arbitrary",)),
        name="ada_mod",
    )(c_all, w_ada, b_ada)


def _group_sumsq(a, p_ref):
    sq = a * a
    hi = sq.astype(BF16)
    lo = (sq - hi.astype(F32)).astype(BF16)
    p = p_ref[...]
    return (jnp.dot(hi, p, preferred_element_type=F32)
            + jnp.dot(lo, p, preferred_element_type=F32))


def _segment_prefix(x, op, identity, seg):
    assert x.shape[1] % LANES == 0 and x.shape[1] % seg == 0
    pos = lax.broadcasted_iota(jnp.int32, x.shape, 1) % seg
    d = 1
    while d < seg:
        x = op(x, jnp.where(pos >= d, pltpu.roll(x, d, axis=1), identity))
        d *= 2
    return x


GATE_ROWS = 24


def _in_proj_body(x_ref, sc_ref, sh_ref, g1_ref, w_ref, bg_ref, gq_ref, gk_ref, pq_ref, pk_ref,
                  mq_ref, mk_ref, mv_ref, og_ref, gt_ref, aq_ref, ak_ref, av_ref, *, gate_chunk):
    s, r, d = x_ref.shape
    m = s * r
    x = x_ref[...]
    ms = jnp.mean(x * x, axis=-1, keepdims=True)
    y = x * lax.rsqrt(ms + EPS) * g1_ref[...]
    h = y * (1.0 + sc_ref[...]) + sh_ref[...]
    hb = h.reshape(m, d).astype(BF16)

    def proj(lo, width):
        return jnp.dot(hb, w_ref[:, lo:lo + width], preferred_element_type=F32)

    zg = proj(N_MAIN, LANES) + bg_ref[...]
    lane = lax.broadcasted_iota(jnp.int32, zg.shape, 1)
    log_sig = jnp.minimum(zg, 0.0) - jnp.log1p(jnp.exp(-jnp.abs(zg)))
    gates = jnp.where(lane < M_HEADS, zg, log_sig)
    gt = gates.T[:2 * M_HEADS, :]
    b8 = _segment_prefix(gt, jnp.add, 0.0, gate_chunk)
    c8 = gt - pltpu.roll(b8, M_HEADS, axis=0)
    cm8 = _segment_prefix(c8, jnp.maximum, NEG_INF, gate_chunk)
    gt_ref[...] = jnp.concatenate([c8, b8, cm8], axis=0)

    mq_ref[...] = proj(0, M_W).astype(BF16).reshape(s, r, M_W)
    mk_ref[...] = (proj(M_W, M_W) * (M_DIM ** -0.5)).astype(BF16).reshape(s, r, M_W)
    mv_ref[...] = proj(2 * M_W, M_W).astype(BF16).reshape(s, r, M_W)
    og_ref[...] = jax.nn.sigmoid(proj(3 * M_W, M_W)).astype(BF16).reshape(s, r, M_W)

    aq = proj(4 * M_W, A_QW)
    aqn = aq * lax.rsqrt(_group_sumsq(aq, pq_ref) * (1.0 / A_DIM) + EPS) * gq_ref[...]
    aq_ref[...] = aqn.astype(BF16).reshape(s, r, A_QW)
    ak = proj(4 * M_W + A_QW, A_KW)
    akn = ak * lax.rsqrt(_group_sumsq(ak, pk_ref) * (1.0 / A_DIM) + EPS) * gk_ref[...]
    ak_ref[...] = akn.reshape(s, r, A_KW)
    av_ref[...] = proj(4 * M_W + A_QW + A_KW, A_KW).reshape(s, r, A_KW)


def _in_proj(x, sc, sh, g1, w_proj, b_gates_pad, gq_t, gk_t, pq, pk, *, seqs, rows, gate_chunk):
    ns, length, d = x.shape
    grid = (ns // seqs, length // rows)
    m = seqs * rows
    n_tok = ns * length

    def tile(width):
        return pl.BlockSpec((seqs, rows, width), lambda i, j: (i, j, 0))

    mod = pl.BlockSpec((seqs, 1, d), lambda i, j: (i, 0, 0))
    steps_per_seq = length // rows
    out_shapes = (
        jax.ShapeDtypeStruct((ns, length, M_W), BF16),
        jax.ShapeDtypeStruct((ns, length, M_W), BF16),
        jax.ShapeDtypeStruct((ns, length, M_W), BF16),
        jax.ShapeDtypeStruct((ns, length, M_W), BF16),
        jax.ShapeDtypeStruct((GATE_ROWS, n_tok), F32),
        jax.ShapeDtypeStruct((ns, length, A_QW), BF16),
        jax.ShapeDtypeStruct((ns, length, A_KW), F32),
        jax.ShapeDtypeStruct((ns, length, A_KW), F32),
    )
    return pl.pallas_call(
        functools.partial(_in_proj_body, gate_chunk=gate_chunk),
        grid=grid,
        in_specs=[tile(d), mod, mod,
                  _const_spec((1, d)), _const_spec((d, N_PROJ)), _const_spec((1, LANES)),
                  _const_spec((1, A_QW)), _const_spec((1, A_KW)),
                  _const_spec((A_QW, A_QW)), _const_spec((A_KW, A_KW))],
        out_specs=(tile(M_W), tile(M_W), tile(M_W), tile(M_W),
                   pl.BlockSpec((GATE_ROWS, m), lambda i, j: (0, i * steps_per_seq + j)),
                   tile(A_QW), tile(A_KW), tile(A_KW)),
        out_shape=out_shapes,
        compiler_params=_params(("arbitrary", "arbitrary")),
        name="in_proj",
    )(x, sc, sh, g1, w_proj, b_gates_pad, gq_t, gk_t, pq, pk)


def _col_bcast(r):
    t = r.shape[1]
    if t % LANES == 0:
        return jnp.broadcast_to(r, (LANES, t)).T
    src = lax.broadcasted_iota(jnp.int32, (t, t), 0)
    dst = lax.broadcasted_iota(jnp.int32, (t, t), 1)
    col = jnp.sum(jnp.where(src == dst, jnp.broadcast_to(r, (t, t)), 0.0), axis=1, keepdims=True)
    return jnp.broadcast_to(col, (t, LANES))


def _mlstm_body(q_ref, k_ref, v_ref, og_ref, gt_ref, c0_ref, n0_ref, m0_ref, gout_ref,
                h_ref, c_out_ref, n_out_ref, m_out_ref, caug_ref, m_ref, *, chunk):
    t = chunk
    n_chunks = q_ref.shape[1] // t
    ci = pl.program_id(1)
    eye_d = (lax.broadcasted_iota(jnp.int32, (M_DIM, M_DIM), 0)
             == lax.broadcasted_iota(jnp.int32, (M_DIM, M_DIM), 1))

    @pl.when(ci == 0)
    def _():
        for hd in range(M_HEADS):
            caug_ref[hd, :, :M_DIM] = c0_ref[0, hd]
            n_row = jnp.broadcast_to(n0_ref[0, hd:hd + 1, :], (M_DIM, M_DIM))
            n_col = jnp.sum(jnp.where(eye_d, n_row, 0.0), axis=1, keepdims=True)
            caug_ref[hd, :, M_DIM:] = jnp.broadcast_to(n_col, (M_DIM, M_DIM))
        m_ref[...] = m0_ref[0]

    tril = (lax.broadcasted_iota(jnp.int32, (t, t), 1) <= lax.broadcasted_iota(jnp.int32, (t, t), 0))
    m_prev = m_ref[...]
    for j in range(n_chunks):
        rows = slice(j * t, (j + 1) * t)
        c = gt_ref[0:M_HEADS, rows]
        b = gt_ref[12:12 + M_HEADS, rows]
        g = jnp.maximum(gt_ref[16:16 + M_HEADS, rows], m_prev)
        g_last = g[:, t - 1:t]
        e_all = jnp.exp(-(b + g))
        w_src_all = jnp.exp(c - g_last)
        w_inter_all = jnp.exp(m_prev - g)
        w_old_all = jnp.exp(m_prev - g_last)
        m_prev = b[:, t - 1:t] + g_last

        for hd in range(M_HEADS):
            sl = slice(hd * M_DIM, (hd + 1) * M_DIM)
            hr = slice(hd, hd + 1)
            g_b = _col_bcast(g[hr])
            e_b = _col_bcast(e_all[hr])
            w_src = _col_bcast(w_src_all[hr])
            w_inter = _col_bcast(w_inter_all[hr])
            g_tt = g_b[:, :t] if t <= LANES else jnp.concatenate([g_b] * (t // LANES), axis=1)
            w_intra = jnp.exp(jnp.where(tril, jnp.broadcast_to(c[hr], (t, t)) - g_tt, NEG_INF))

            q = q_ref[0, rows, sl]
            k = k_ref[0, rows, sl]
            v = v_ref[0, rows, sl]
            s_mat = lax.dot_general(q, k, (((1,), (1,)), ((), ())),
                                    preferred_element_type=F32) * w_intra
            caug = caug_ref[hd]
            inter = jnp.dot(q, caug.astype(BF16), preferred_element_type=F32)
            num = (jnp.dot(s_mat.astype(BF16), v, preferred_element_type=F32)
                   + w_inter * inter[:, :M_DIM])
            den = jnp.sum(s_mat, axis=1, keepdims=True) + w_inter * inter[:, M_DIM:]
            hh = num / jnp.maximum(jnp.abs(den), e_b)

            rhs = jnp.concatenate([v.astype(F32) * w_src, w_src], axis=1).astype(BF16)
            upd = lax.dot_general(k, rhs, (((0,), (0,)), ((), ())), preferred_element_type=F32)
            caug_ref[hd] = w_old_all[hr] * caug + upd

            hn = (hh * lax.rsqrt(jnp.mean(hh * hh, axis=-1, keepdims=True) + EPS)
                  * gout_ref[hr, :])
            h_ref[0, rows, sl] = (hn * og_ref[0, rows, sl].astype(F32)).astype(BF16)
    m_ref[...] = m_prev

    @pl.when(ci == pl.num_programs(1) - 1)
    def _():
        for hd in range(M_HEADS):
            c_out_ref[0, hd] = caug_ref[hd, :, :M_DIM]
            n_out_ref[0, hd:hd + 1, :] = jnp.sum(
                jnp.where(eye_d, caug_ref[hd, :, M_DIM:], 0.0), axis=0, keepdims=True)
        m_out_ref[0] = m_ref[...]


def _mlstm(mq, mk, mv, og, gt, c0, n0, m0, gout, *, chunk, chunks_per_step):
    ns, length, _ = mq.shape
    rows = chunk * chunks_per_step
    steps = length // rows
    tile = pl.BlockSpec((1, rows, M_W), lambda b, c: (b, c, 0))
    if gt.ndim == 2:
        gt_spec = pl.BlockSpec((GATE_ROWS, rows), lambda b, c: (0, b * steps + c))
    else:
        gt_spec = pl.BlockSpec((None, GATE_ROWS, rows), lambda b, c: (b, 0, 0))
    st_c = pl.BlockSpec((1, M_HEADS, M_DIM, M_DIM), lambda b, c: (b, 0, 0, 0))
    st_n = pl.BlockSpec((1, M_HEADS, M_DIM), lambda b, c: (b, 0, 0))
    st_m = pl.BlockSpec((1, M_HEADS, 1), lambda b, c: (b, 0, 0))
    return pl.pallas_call(
        functools.partial(_mlstm_body, chunk=chunk),
        grid=(ns, steps),
        in_specs=[tile, tile, tile, tile, gt_spec, st_c, st_n, st_m,
                  _const_spec((M_HEADS, M_DIM))],
        out_specs=(tile, st_c, st_n, st_m),
        out_shape=(jax.ShapeDtypeStruct((ns, length, M_W), BF16),
                   jax.ShapeDtypeStruct((ns, M_HEADS, M_DIM, M_DIM), F32),
                   jax.ShapeDtypeStruct((ns, M_HEADS, M_DIM), F32),
                   jax.ShapeDtypeStruct((ns, M_HEADS, 1), F32)),
        scratch_shapes=[pltpu.VMEM((M_HEADS, M_DIM, 2 * M_DIM), F32),
                        pltpu.VMEM((M_HEADS, 1), F32)],
        compiler_params=_params(("arbitrary", "arbitrary")),
        name="mlstm",
    )(mq, mk, mv, og, gt, c0, n0, m0, gout)


def _swa_core(q, k_cat, v_cat, q_pos, k_pos, sink_ref, o_ref):
    tq = q.shape[0]
    tk = k_cat.shape[0]
    shift = CHUNK.bit_length() - 1
    assert 1 << shift == CHUNK
    qc = lax.shift_right_arithmetic(q_pos, shift)
    kc = lax.shift_right_arithmetic(k_pos, shift)
    allowed = (kc <= qc) & (kc >= qc - WIN_CHUNKS) & (k_pos >= 0)
    dist = jnp.abs(q_pos - k_pos).astype(F32)
    lane = lax.broadcasted_iota(jnp.int32, (tk, LANES), 1)
    low = lane < A_DIM

    def halves(a, j):
        if j == 0:
            lo_half = jnp.where(low, a, 0.0)
            return lo_half, pltpu.roll(lo_half, A_DIM, axis=1)
        hi_half = jnp.where(low, 0.0, a)
        return pltpu.roll(hi_half, A_DIM, axis=1), hi_half

    kz = [tuple(h.astype(BF16) for h in halves(k_cat, j)) for j in range(A_KV_HEADS)]
    vz = [tuple(h.astype(BF16) for h in halves(v_cat, j)) for j in range(A_KV_HEADS)]

    for p in range(A_HEADS // 2):
        j = (2 * p) // (A_HEADS // A_KV_HEADS)
        q_pair = q[:, p * LANES:(p + 1) * LANES]
        acc = None
        for e in range(2):
            head = 2 * p + e
            slope = 2.0 ** (-8.0 * (head + 1) / A_HEADS)
            sink = sink_ref[head]
            logits = lax.dot_general(q_pair, kz[j][e], (((1,), (1,)), ((), ())),
                                     preferred_element_type=F32)
            logits = jnp.where(allowed, logits - slope * dist, NEG_INF)
            mx = jnp.maximum(jnp.max(logits, axis=1, keepdims=True), sink)
            pr = jnp.exp(logits - mx)
            denom = jnp.sum(pr, axis=1, keepdims=True) + jnp.exp(sink - mx)
            part = jnp.dot(pr.astype(BF16), vz[j][e], preferred_element_type=F32) / denom
            acc = part if acc is None else acc + part
        o_ref[0, :, p * LANES:(p + 1) * LANES] = acc.astype(o_ref.dtype)


def _swa_prompt_body(sink_ref, q_ref, kp_ref, kc_ref, vp_ref, vc_ref, o_ref):
    tq = q_ref.shape[1]
    i = pl.program_id(1)
    k_cat = jnp.concatenate([kp_ref[0], kc_ref[0]], axis=0)
    v_cat = jnp.concatenate([vp_ref[0], vc_ref[0]], axis=0)
    q_pos = i * tq + lax.broadcasted_iota(jnp.int32, (tq, 1), 0)
    k_pos = (i - 1) * tq + lax.broadcasted_iota(jnp.int32, (1, 2 * tq), 1)
    _swa_core(q_ref[0], k_cat, v_cat, q_pos, k_pos, sink_ref, o_ref)


def _swa_prompt(aq, ak, av, sinks):
    ns, length, _ = aq.shape
    tq = 2 * CHUNK
    cur = lambda w: pl.BlockSpec((1, tq, w), lambda b, i: (b, i, 0))
    prev = lambda w: pl.BlockSpec((1, tq, w), lambda b, i: (b, jnp.maximum(i - 1, 0), 0))
    return pl.pallas_call(
        _swa_prompt_body,
        grid=(ns, length // tq),
        in_specs=[pl.BlockSpec(memory_space=pltpu.SMEM),
                  cur(A_QW), prev(A_KW), cur(A_KW), prev(A_KW), cur(A_KW)],
        out_specs=cur(A_QW),
        out_shape=jax.ShapeDtypeStruct((ns, length, A_QW), BF16),
        compiler_params=_params(("arbitrary", "arbitrary")),
        name="swa_prompt",
    )(sinks, aq, ak, ak, av, av)


def _swa_sample_body(sink_ref, q_ref, kn_ref, vn_ref, ck_ref, cv_ref, o_ref):
    tq = q_ref.shape[1]
    wb = ck_ref.shape[1]
    tk = 2 * LANES
    pad = jnp.zeros((tk - wb - tq, A_KW), F32)
    k_cat = jnp.concatenate([ck_ref[0], kn_ref[0], pad], axis=0)
    v_cat = jnp.concatenate([cv_ref[0], vn_ref[0], pad], axis=0)
    q_pos = PAST_LEN + lax.broadcasted_iota(jnp.int32, (tq, 1), 0)
    idx = lax.broadcasted_iota(jnp.int32, (1, tk), 1)
    k_pos = jnp.where(idx < wb + tq, PAST_LEN - wb + idx, -1)
    _swa_core(q_ref[0], k_cat, v_cat, q_pos, k_pos, sink_ref, o_ref)


def _swa_sample(aq, ak, av, cache_k, cache_v, sinks):
    ns, tq, _ = aq.shape
    wb = cache_k.shape[1]
    new = lambda w: pl.BlockSpec((1, tq, w), lambda b: (b, 0, 0))
    old = pl.BlockSpec((1, wb, A_KW), lambda b: (b, 0, 0))
    return pl.pallas_call(
        _swa_sample_body,
        grid=(ns,),
        in_specs=[pl.BlockSpec(memory_space=pltpu.SMEM),
                  new(A_QW), new(A_KW), new(A_KW), old, old],
        out_specs=new(A_QW),
        out_shape=jax.ShapeDtypeStruct((ns, tq, A_QW), BF16),
        compiler_params=_params(("arbitrary",)),
        name="swa_sample",
    )(sinks, aq, ak, av, cache_k, cache_v)


FF_BLOCK = 1024


def _out_ffn_body(x_ref, hm_ref, at_ref, ga1_ref, sc2_ref, sh2_ref, ga2_ref, g2_ref,
                  wo_ref, wu_ref, wd_ref, y_ref):
    s, r, d = x_ref.shape
    m = s * r
    hm = hm_ref[...].reshape(m, M_W)
    at = at_ref[...].reshape(m, A_QW)
    mix = (jnp.dot(hm, wo_ref[:M_W, :], preferred_element_type=F32)
           + jnp.dot(at, wo_ref[M_W:, :], preferred_element_type=F32))
    x1 = x_ref[...] + ga1_ref[...] * mix.reshape(s, r, d)
    ms = jnp.mean(x1 * x1, axis=-1, keepdims=True)
    h2 = (x1 * lax.rsqrt(ms + EPS) * g2_ref[...]) * (1.0 + sc2_ref[...]) + sh2_ref[...]
    hb = h2.reshape(m, d).astype(BF16)
    acc = jnp.zeros((m, d), F32)
    for c in range(D_FF // FF_BLOCK):
        u = jnp.dot(hb, wu_ref[:, c * FF_BLOCK:(c + 1) * FF_BLOCK], preferred_element_type=F32)
        u = jnp.square(jnp.maximum(u, 0.0)).astype(BF16)
        acc = acc + jnp.dot(u, wd_ref[c * FF_BLOCK:(c + 1) * FF_BLOCK, :],
                            preferred_element_type=F32)
    y_ref[...] = x1 + ga2_ref[...] * acc.reshape(s, r, d)


def _out_ffn(x, hm, at, ga1, sc2, sh2, ga2, g2, w_out, w_up, w_down, *, seqs, rows):
    ns, length, d = x.shape
    tile = lambda w: pl.BlockSpec((seqs, rows, w), lambda i, j: (i, j, 0))
    mod = pl.BlockSpec((seqs, 1, d), lambda i, j: (i, 0, 0))
    return pl.pallas_call(
        _out_ffn_body,
        grid=(ns // seqs, length // rows),
        in_specs=[tile(d), tile(M_W), tile(A_QW), mod, mod, mod, mod,
                  _const_spec((1, d)), _const_spec((d, d)), _const_spec((d, D_FF)),
                  _const_spec((D_FF, d))],
        out_specs=tile(d),
        out_shape=jax.ShapeDtypeStruct((ns, length, d), F32),
        compiler_params=_params(("arbitrary", "arbitrary")),
        name="out_ffn",
    )(x, hm, at, ga1, sc2, sh2, ga2, g2, w_out, w_up, w_down)


def _block_ones(n):
    g = jnp.arange(n, dtype=jnp.int32) // A_DIM
    return (g[:, None] == g[None, :]).astype(BF16)


def kernel(x_prompt, x_sample, c_prompt, c_sample, cache_swa_k, cache_swa_v, state_mlstm_C,
           state_mlstm_n, state_mlstm_m, w_ada, b_ada, g_norm1, w_in, b_gates, g_q, g_k, sinks,
           g_mlstm_out, w_out, g_norm2, w_up, w_down):
    depth = w_ada.shape[0]
    assert depth == 1
    bp, lp, d = x_prompt.shape
    bs, ls, _ = x_sample.shape
    wb = cache_swa_k.shape[2]

    pq = _block_ones(A_QW)
    pk = _block_ones(A_KW)
    xp, xs = x_prompt, x_sample
    for l in range(depth):
        wl = w_in[l]
        g_lo = 4 * M_W
        g_hi = g_lo + 2 * M_HEADS
        w_proj = jnp.concatenate(
            [wl[:, :g_lo], wl[:, g_hi:], wl[:, g_lo:g_hi],
             jnp.zeros((d, LANES - 2 * M_HEADS), wl.dtype)], axis=1).astype(BF16)
        bg = jnp.concatenate([b_gates[l], jnp.zeros((LANES - 2 * M_HEADS,), F32)])[None, :]
        gq_t = (jnp.tile(g_q[l], A_HEADS) * (A_DIM ** -0.5))[None, :]
        gk_t = jnp.tile(g_k[l], A_KV_HEADS)[None, :]
        g1 = g_norm1[l][None, :]
        g2 = g_norm2[l][None, :]
        wo = w_out[l].astype(BF16)
        wu = w_up[l].astype(BF16)
        wd = w_down[l].astype(BF16)

        c_all = jnp.concatenate([c_prompt, c_sample, jnp.zeros((16 - bp - bs, d), F32)], axis=0)
        mod = _ada_mod(c_all, w_ada[l], b_ada[l][None, :])
        mod_p = [mod[:bp, i * d:(i + 1) * d][:, None, :] for i in range(6)]
        mod_s = [mod[bp:bp + bs, i * d:(i + 1) * d][:, None, :] for i in range(6)]

        def run(x, mods, seqs, rows, gate_chunk, mixer):
            sh1, sc1, ga1, sh2, sc2, ga2 = mods
            mq, mk, mv, og, gt, aq, ak, av = _in_proj(
                x, sc1, sh1, g1, w_proj, bg, gq_t, gk_t, pq, pk, seqs=seqs, rows=rows,
                gate_chunk=gate_chunk)
            hm, at, state = mixer(mq, mk, mv, og, gt, aq, ak, av)
            y = _out_ffn(x, hm, at, ga1, sc2, sh2, ga2, g2, wo, wu, wd, seqs=seqs, rows=rows)
            return y, state

        def prompt_mixer(mq, mk, mv, og, gt, aq, ak, av):
            c0 = jnp.zeros((bp, M_HEADS, M_DIM, M_DIM), F32)
            n0 = jnp.zeros((bp, M_HEADS, M_DIM), F32)
            m0 = jnp.zeros((bp, M_HEADS, 1), F32)
            hm, c, n, m = _mlstm(mq, mk, mv, og, gt, c0, n0, m0, g_mlstm_out[l], chunk=MLSTM_CHUNK,
                                 chunks_per_step=2)
            at = _swa_prompt(aq, ak, av, sinks[l])
            win = WIN_CHUNKS * CHUNK
            k_tail = ak[:, -win:].reshape(bp, win, A_KV_HEADS, A_DIM)
            v_tail = av[:, -win:].reshape(bp, win, A_KV_HEADS, A_DIM)
            return hm, at, (k_tail, v_tail, c, n, m[..., 0])

        def sample_mixer(mq, mk, mv, og, gt, aq, ak, av):
            gt3 = gt.reshape(GATE_ROWS, bs, ls).transpose(1, 0, 2)
            hm, c, n, m = _mlstm(mq, mk, mv, og, gt3, state_mlstm_C[l], state_mlstm_n[l],
                                 state_mlstm_m[l][..., None], g_mlstm_out[l], chunk=ls,
                                 chunks_per_step=1)
            ck = cache_swa_k[l].reshape(bs, wb, A_KW)
            cv = cache_swa_v[l].reshape(bs, wb, A_KW)
            at = _swa_sample(aq, ak, av, ck, cv, sinks[l])
            k_all = jnp.concatenate([ck, ak], axis=1)[:, -wb:].reshape(bs, wb, A_KV_HEADS, A_DIM)
            v_all = jnp.concatenate([cv, av], axis=1)[:, -wb:].reshape(bs, wb, A_KV_HEADS, A_DIM)
            return hm, at, (k_all, v_all, c, n, m[..., 0])

        xp, st_p = run(xp, mod_p, 1, ROW_TILE, MLSTM_CHUNK, prompt_mixer)
        xs, st_s = run(xs, mod_s, bs, ls, ls, sample_mixer)

    return (xp, xs) + tuple(s[None] for s in st_p) + tuple(s[None] for s in st_s)
```

```python
import functools

import jax
import jax.numpy as jnp
from jax import lax
from jax.experimental import pallas as pl
from jax.experimental.pallas import tpu as pltpu

F32 = jnp.float32
BF16 = jnp.bfloat16

D_MODEL = 1024
CHUNK = 64
M_HEADS = 4
M_DIM = 128
M_W = M_HEADS * M_DIM
A_HEADS = 8
A_KV_HEADS = 2
A_DIM = 64
A_QW = A_HEADS * A_DIM
A_KW = A_KV_HEADS * A_DIM
WIN_CHUNKS = 2
D_FF = 4 * D_MODEL
EPS = 1e-6
PAST_LEN = 4096

LANES = 128
N_MAIN = 4 * M_W + A_QW + 2 * A_KW
N_PROJ = N_MAIN + LANES
VMEM_LIMIT = 56 * 1024 * 1024
NEG_INF = float("-inf")
ROW_TILE = 512
MLSTM_CHUNK = 256


def _params(sem):
    return pltpu.CompilerParams(dimension_semantics=sem, vmem_limit_bytes=VMEM_LIMIT)


def _const_spec(shape):
    nd = len(shape)
    return pl.BlockSpec(shape, lambda *_: (0,) * nd, pipeline_mode=pl.Buffered(1))


def _ada_body(c_ref, w_ref, b_ref, o_ref):
    c = c_ref[...]
    s = c * jax.nn.sigmoid(c)
    o_ref[...] = jnp.dot(s.astype(BF16), w_ref[...].astype(BF16),
                         preferred_element_type=F32) + b_ref[...]


def _ada_mod(c_all, w_ada, b_ada):
    rows = c_all.shape[0]
    return pl.pallas_call(
        _ada_body,
        grid=(6,),
        in_specs=[pl.BlockSpec((rows, D_MODEL), lambda j: (0, 0)),
                  pl.BlockSpec((D_MODEL, D_MODEL), lambda j: (0, j)),
                  pl.BlockSpec((1, D_MODEL), lambda j: (0, j))],
        out_specs=pl.BlockSpec((rows, D_MODEL), lambda j: (0, j)),
        out_shape=jax.ShapeDtypeStruct((rows, 6 * D_MODEL), F32),
        compiler_params=_params(("arbitrary",)),
        name="ada_mod",
    )(c_all, w_ada, b_ada)


def _group_sumsq(a, p_ref):
    sq = a * a
    hi = sq.astype(BF16)
    lo = (sq - hi.astype(F32)).astype(BF16)
    p = p_ref[...]
    return (jnp.dot(hi, p, preferred_element_type=F32)
            + jnp.dot(lo, p, preferred_element_type=F32))


def _segment_prefix(x, op, identity, seg):
    assert x.shape[1] % LANES == 0 and x.shape[1] % seg == 0
    pos = lax.broadcasted_iota(jnp.int32, x.shape, 1) % seg
    d = 1
    while d < seg:
        x = op(x, jnp.where(pos >= d, pltpu.roll(x, d, axis=1), identity))
        d *= 2
    return x


GATE_ROWS = 24


def _in_proj_body(x_ref, sc_ref, sh_ref, g1_ref, w_ref, bg_ref, gq_ref, gk_ref, pq_ref, pk_ref,
                  mq_ref, mk_ref, mv_ref, og_ref, gt_ref, aq_ref, ak_ref, av_ref, *, gate_chunk):
    s, r, d = x_ref.shape
    m = s * r
    x = x_ref[...]
    ms = jnp.mean(x * x, axis=-1, keepdims=True)
    y = x * lax.rsqrt(ms + EPS) * g1_ref[...]
    h = y * (1.0 + sc_ref[...]) + sh_ref[...]
    hb = h.reshape(m, d).astype(BF16)

    def proj(lo, width):
        return jnp.dot(hb, w_ref[:, lo:lo + width], preferred_element_type=F32)

    zg = proj(N_MAIN, LANES) + bg_ref[...]
    lane = lax.broadcasted_iota(jnp.int32, zg.shape, 1)
    log_sig = jnp.minimum(zg, 0.0) - jnp.log1p(jnp.exp(-jnp.abs(zg)))
    gates = jnp.where(lane < M_HEADS, zg, log_sig)
    gt = gates.T[:2 * M_HEADS, :]
    b8 = _segment_prefix(gt, jnp.add, 0.0, gate_chunk)
    c8 = gt - pltpu.roll(b8, M_HEADS, axis=0)
    cm8 = _segment_prefix(c8, jnp.maximum, NEG_INF, gate_chunk)
    gt_ref[...] = jnp.concatenate([c8, b8, cm8], axis=0)

    mq_ref[...] = proj(0, M_W).astype(BF16).reshape(s, r, M_W)
    mk_ref[...] = (proj(M_W, M_W) * (M_DIM ** -0.5)).astype(BF16).reshape(s, r, M_W)
    mv_ref[...] = proj(2 * M_W, M_W).astype(BF16).reshape(s, r, M_W)
    og_ref[...] = jax.nn.sigmoid(proj(3 * M_W, M_W)).astype(BF16).reshape(s, r, M_W)

    aq = proj(4 * M_W, A_QW)
    aqn = aq * lax.rsqrt(_group_sumsq(aq, pq_ref) * (1.0 / A_DIM) + EPS) * gq_ref[...]
    aq_ref[...] = aqn.astype(BF16).reshape(s, r, A_QW)
    ak = proj(4 * M_W + A_QW, A_KW)
    akn = ak * lax.rsqrt(_group_sumsq(ak, pk_ref) * (1.0 / A_DIM) + EPS) * gk_ref[...]
    ak_ref[...] = akn.reshape(s, r, A_KW)
    av_ref[...] = proj(4 * M_W + A_QW + A_KW, A_KW).reshape(s, r, A_KW)


def _in_proj(x, sc, sh, g1, w_proj, b_gates_pad, gq_t, gk_t, pq, pk, *, seqs, rows, gate_chunk):
    ns, length, d = x.shape
    grid = (ns // seqs, length // rows)
    m = seqs * rows
    n_tok = ns * length

    def tile(width):
        return pl.BlockSpec((seqs, rows, width), lambda i, j: (i, j, 0))

    mod = pl.BlockSpec((seqs, 1, d), lambda i, j: (i, 0, 0))
    steps_per_seq = length // rows
    out_shapes = (
        jax.ShapeDtypeStruct((ns, length, M_W), BF16),
        jax.ShapeDtypeStruct((ns, length, M_W), BF16),
        jax.ShapeDtypeStruct((ns, length, M_W), BF16),
        jax.ShapeDtypeStruct((ns, length, M_W), BF16),
        jax.ShapeDtypeStruct((GATE_ROWS, n_tok), F32),
        jax.ShapeDtypeStruct((ns, length, A_QW), BF16),
        jax.ShapeDtypeStruct((ns, length, A_KW), F32),
        jax.ShapeDtypeStruct((ns, length, A_KW), F32),
    )
    return pl.pallas_call(
        functools.partial(_in_proj_body, gate_chunk=gate_chunk),
        grid=grid,
        in_specs=[tile(d), mod, mod,
                  _const_spec((1, d)), _const_spec((d, N_PROJ)), _const_spec((1, LANES)),
                  _const_spec((1, A_QW)), _const_spec((1, A_KW)),
                  _const_spec((A_QW, A_QW)), _const_spec((A_KW, A_KW))],
        out_specs=(tile(M_W), tile(M_W), tile(M_W), tile(M_W),
                   pl.BlockSpec((GATE_ROWS, m), lambda i, j: (0, i * steps_per_seq + j)),
                   tile(A_QW), tile(A_KW), tile(A_KW)),
        out_shape=out_shapes,
        compiler_params=_params(("arbitrary", "arbitrary")),
        name="in_proj",
    )(x, sc, sh, g1, w_proj, b_gates_pad, gq_t, gk_t, pq, pk)


def _col_bcast(r):
    t = r.shape[1]
    if t % LANES == 0:
        return jnp.broadcast_to(r, (LANES, t)).T
    src = lax.broadcasted_iota(jnp.int32, (t, t), 0)
    dst = lax.broadcasted_iota(jnp.int32, (t, t), 1)
    col = jnp.sum(jnp.where(src == dst, jnp.broadcast_to(r, (t, t)), 0.0), axis=1, keepdims=True)
    return jnp.broadcast_to(col, (t, LANES))


def _eye_dim():
    return (lax.broadcasted_iota(jnp.int32, (M_DIM, M_DIM), 0)
            == lax.broadcasted_iota(jnp.int32, (M_DIM, M_DIM), 1))


def _mlstm_load_state(c0_ref, n0_ref, m0_ref, caug_ref, m_ref):
    for hd in range(M_HEADS):
        caug_ref[hd, :, :M_DIM] = c0_ref[0, hd]
        n_row = jnp.broadcast_to(n0_ref[0, hd:hd + 1, :], (M_DIM, M_DIM))
        n_col = jnp.sum(jnp.where(_eye_dim(), n_row, 0.0), axis=1, keepdims=True)
        caug_ref[hd, :, M_DIM:] = jnp.broadcast_to(n_col, (M_DIM, M_DIM))
    m_ref[...] = m0_ref[0]


def _mlstm_store_state(caug_ref, m_ref, c_out_ref, n_out_ref, m_out_ref):
    for hd in range(M_HEADS):
        c_out_ref[0, hd] = caug_ref[hd, :, :M_DIM]
        n_out_ref[0, hd:hd + 1, :] = jnp.sum(
            jnp.where(_eye_dim(), caug_ref[hd, :, M_DIM:], 0.0), axis=0, keepdims=True)
    m_out_ref[0] = m_ref[...]


def _mlstm_tile(q_ref, k_ref, v_ref, og_ref, gt_ref, gout_ref, caug_ref, m_ref, write_h, chunk):
    t = chunk
    n_chunks = q_ref.shape[1] // t
    tril = (lax.broadcasted_iota(jnp.int32, (t, t), 1) <= lax.broadcasted_iota(jnp.int32, (t, t), 0))
    m_prev = m_ref[...]
    for j in range(n_chunks):
        rows = slice(j * t, (j + 1) * t)
        c = gt_ref[0:M_HEADS, rows]
        b = gt_ref[12:12 + M_HEADS, rows]
        g = jnp.maximum(gt_ref[16:16 + M_HEADS, rows], m_prev)
        g_last = g[:, t - 1:t]
        e_all = jnp.exp(-(b + g))
        w_src_all = jnp.exp(c - g_last)
        w_inter_all = jnp.exp(m_prev - g)
        w_old_all = jnp.exp(m_prev - g_last)
        m_prev = b[:, t - 1:t] + g_last

        for hd in range(M_HEADS):
            sl = slice(hd * M_DIM, (hd + 1) * M_DIM)
            hr = slice(hd, hd + 1)
            g_b = _col_bcast(g[hr])
            e_b = _col_bcast(e_all[hr])
            w_src = _col_bcast(w_src_all[hr])
            w_inter = _col_bcast(w_inter_all[hr])
            g_tt = g_b[:, :t] if t <= LANES else jnp.concatenate([g_b] * (t // LANES), axis=1)
            w_intra = jnp.exp(jnp.where(tril, jnp.broadcast_to(c[hr], (t, t)) - g_tt, NEG_INF))

            q = q_ref[0, rows, sl]
            k = k_ref[0, rows, sl]
            v = v_ref[0, rows, sl]
            s_mat = lax.dot_general(q, k, (((1,), (1,)), ((), ())),
                                    preferred_element_type=F32) * w_intra
            caug = caug_ref[hd]
            inter = jnp.dot(q, caug.astype(BF16), preferred_element_type=F32)
            num = (jnp.dot(s_mat.astype(BF16), v, preferred_element_type=F32)
                   + w_inter * inter[:, :M_DIM])
            den = jnp.sum(s_mat, axis=1, keepdims=True) + w_inter * inter[:, M_DIM:]
            hh = num / jnp.maximum(jnp.abs(den), e_b)

            rhs = jnp.concatenate([v.astype(F32) * w_src, w_src], axis=1).astype(BF16)
            upd = lax.dot_general(k, rhs, (((0,), (0,)), ((), ())), preferred_element_type=F32)
            caug_ref[hd] = w_old_all[hr] * caug + upd

            hn = (hh * lax.rsqrt(jnp.mean(hh * hh, axis=-1, keepdims=True) + EPS)
                  * gout_ref[hr, :])
            write_h(rows, sl, (hn * og_ref[0, rows, sl].astype(F32)).astype(BF16))
    m_ref[...] = m_prev


def _mlstm_body(q_ref, k_ref, v_ref, og_ref, gt_ref, c0_ref, n0_ref, m0_ref, gout_ref,
                h_ref, c_out_ref, n_out_ref, m_out_ref, caug_ref, m_ref, *, chunk):
    ci = pl.program_id(1)

    @pl.when(ci == 0)
    def _():
        _mlstm_load_state(c0_ref, n0_ref, m0_ref, caug_ref, m_ref)

    def write_h(rows, lanes, value):
        h_ref[0, rows, lanes] = value

    _mlstm_tile(q_ref, k_ref, v_ref, og_ref, gt_ref, gout_ref, caug_ref, m_ref, write_h, chunk)

    @pl.when(ci == pl.num_programs(1) - 1)
    def _():
        _mlstm_store_state(caug_ref, m_ref, c_out_ref, n_out_ref, m_out_ref)


def _mlstm(mq, mk, mv, og, gt, c0, n0, m0, gout, *, chunk, chunks_per_step):
    ns, length, _ = mq.shape
    rows = chunk * chunks_per_step
    steps = length // rows
    tile = pl.BlockSpec((1, rows, M_W), lambda b, c: (b, c, 0))
    if gt.ndim == 2:
        gt_spec = pl.BlockSpec((GATE_ROWS, rows), lambda b, c: (0, b * steps + c))
    else:
        gt_spec = pl.BlockSpec((None, GATE_ROWS, rows), lambda b, c: (b, 0, 0))
    st_c = pl.BlockSpec((1, M_HEADS, M_DIM, M_DIM), lambda b, c: (b, 0, 0, 0))
    st_n = pl.BlockSpec((1, M_HEADS, M_DIM), lambda b, c: (b, 0, 0))
    st_m = pl.BlockSpec((1, M_HEADS, 1), lambda b, c: (b, 0, 0))
    return pl.pallas_call(
        functools.partial(_mlstm_body, chunk=chunk),
        grid=(ns, steps),
        in_specs=[tile, tile, tile, tile, gt_spec, st_c, st_n, st_m,
                  _const_spec((M_HEADS, M_DIM))],
        out_specs=(tile, st_c, st_n, st_m),
        out_shape=(jax.ShapeDtypeStruct((ns, length, M_W), BF16),
                   jax.ShapeDtypeStruct((ns, M_HEADS, M_DIM, M_DIM), F32),
                   jax.ShapeDtypeStruct((ns, M_HEADS, M_DIM), F32),
                   jax.ShapeDtypeStruct((ns, M_HEADS, 1), F32)),
        scratch_shapes=[pltpu.VMEM((M_HEADS, M_DIM, 2 * M_DIM), F32),
                        pltpu.VMEM((M_HEADS, 1), F32)],
        compiler_params=_params(("arbitrary", "arbitrary")),
        name="mlstm",
    )(mq, mk, mv, og, gt, c0, n0, m0, gout)


def _kv_halves(a):
    low = lax.broadcasted_iota(jnp.int32, a.shape, 1) < A_DIM
    lo_half = jnp.where(low, a, 0.0)
    hi_half = jnp.where(low, 0.0, a)
    pairs = [(lo_half, pltpu.roll(lo_half, A_DIM, axis=1)),
             (pltpu.roll(hi_half, A_DIM, axis=1), hi_half)]
    return [tuple(h.astype(BF16) for h in pair) for pair in pairs]


def _swa_block(q, kz, vz, q_pos, k_pos, sink_ref, write_o):
    shift = CHUNK.bit_length() - 1
    assert 1 << shift == CHUNK
    qc = lax.shift_right_arithmetic(q_pos, shift)
    kc = lax.shift_right_arithmetic(k_pos, shift)
    allowed = (kc <= qc) & (kc >= qc - WIN_CHUNKS) & (k_pos >= 0)
    dist = jnp.abs(q_pos - k_pos).astype(F32)

    for p in range(A_HEADS // 2):
        j = (2 * p) // (A_HEADS // A_KV_HEADS)
        q_pair = q[:, p * LANES:(p + 1) * LANES]
        acc = None
        for e in range(2):
            head = 2 * p + e
            slope = 2.0 ** (-8.0 * (head + 1) / A_HEADS)
            sink = sink_ref[head]
            logits = lax.dot_general(q_pair, kz[j][e], (((1,), (1,)), ((), ())),
                                     preferred_element_type=F32)
            logits = jnp.where(allowed, logits - slope * dist, NEG_INF)
            mx = jnp.maximum(jnp.max(logits, axis=1, keepdims=True), sink)
            pr = jnp.exp(logits - mx)
            denom = jnp.sum(pr, axis=1, keepdims=True) + jnp.exp(sink - mx)
            part = jnp.dot(pr.astype(BF16), vz[j][e], preferred_element_type=F32) / denom
            acc = part if acc is None else acc + part
        write_o(slice(p * LANES, (p + 1) * LANES), acc.astype(BF16))


SWA_Q_ROWS = WIN_CHUNKS * CHUNK


def _swa_tile(sink_ref, q_ref, kp_ref, kc_ref, vp_ref, vc_ref, tile_start, write_o):
    tq = SWA_Q_ROWS
    kz = _kv_halves(jnp.concatenate([kp_ref[0], kc_ref[0]], axis=0))
    vz = _kv_halves(jnp.concatenate([vp_ref[0], vc_ref[0]], axis=0))
    for r in range(q_ref.shape[1] // tq):
        rows = slice(r * tq, (r + 1) * tq)
        band = slice(r * tq, (r + 2) * tq)
        q_pos = tile_start + r * tq + lax.broadcasted_iota(jnp.int32, (tq, 1), 0)
        k_pos = tile_start + (r - 1) * tq + lax.broadcasted_iota(jnp.int32, (1, 2 * tq), 1)
        _swa_block(q_ref[0, rows, :],
                   [tuple(h[band] for h in pair) for pair in kz],
                   [tuple(h[band] for h in pair) for pair in vz],
                   q_pos, k_pos, sink_ref,
                   lambda lanes, value, rows=rows: write_o(rows, lanes, value))


def _swa_sample_body(sink_ref, q_ref, kn_ref, vn_ref, ck_ref, cv_ref, o_ref):
    tq = q_ref.shape[1]
    wb = ck_ref.shape[1]
    tk = 2 * LANES
    pad = jnp.zeros((tk - wb - tq, A_KW), F32)
    kz = _kv_halves(jnp.concatenate([ck_ref[0], kn_ref[0], pad], axis=0))
    vz = _kv_halves(jnp.concatenate([cv_ref[0], vn_ref[0], pad], axis=0))
    q_pos = PAST_LEN + lax.broadcasted_iota(jnp.int32, (tq, 1), 0)
    idx = lax.broadcasted_iota(jnp.int32, (1, tk), 1)
    k_pos = jnp.where(idx < wb + tq, PAST_LEN - wb + idx, -1)

    def write_o(lanes, value):
        o_ref[0, :, lanes] = value

    _swa_block(q_ref[0], kz, vz, q_pos, k_pos, sink_ref, write_o)


def _swa_sample(aq, ak, av, cache_k, cache_v, sinks):
    ns, tq, _ = aq.shape
    wb = cache_k.shape[1]
    new = lambda w: pl.BlockSpec((1, tq, w), lambda b: (b, 0, 0))
    old = pl.BlockSpec((1, wb, A_KW), lambda b: (b, 0, 0))
    return pl.pallas_call(
        _swa_sample_body,
        grid=(ns,),
        in_specs=[pl.BlockSpec(memory_space=pltpu.SMEM),
                  new(A_QW), new(A_KW), new(A_KW), old, old],
        out_specs=new(A_QW),
        out_shape=jax.ShapeDtypeStruct((ns, tq, A_QW), BF16),
        compiler_params=_params(("arbitrary",)),
        name="swa_sample",
    )(sinks, aq, ak, av, cache_k, cache_v)


FF_BLOCK = 1024


def _out_ffn_tile(x, hm, at, ga1, sc2, sh2, ga2, g2, wo_ref, wu_ref, wd_ref):
    s, r, d = x.shape
    m = s * r
    mix = (jnp.dot(hm, wo_ref[:M_W, :], preferred_element_type=F32)
           + jnp.dot(at, wo_ref[M_W:, :], preferred_element_type=F32))
    x1 = x + ga1 * mix.reshape(s, r, d)
    ms = jnp.mean(x1 * x1, axis=-1, keepdims=True)
    h2 = (x1 * lax.rsqrt(ms + EPS) * g2) * (1.0 + sc2) + sh2
    hb = h2.reshape(m, d).astype(BF16)
    acc = jnp.zeros((m, d), F32)
    for c in range(D_FF // FF_BLOCK):
        u = jnp.dot(hb, wu_ref[:, c * FF_BLOCK:(c + 1) * FF_BLOCK], preferred_element_type=F32)
        u = jnp.square(jnp.maximum(u, 0.0)).astype(BF16)
        acc = acc + jnp.dot(u, wd_ref[c * FF_BLOCK:(c + 1) * FF_BLOCK, :],
                            preferred_element_type=F32)
    return x1 + ga2 * acc.reshape(s, r, d)


def _out_ffn_body(x_ref, hm_ref, at_ref, ga1_ref, sc2_ref, sh2_ref, ga2_ref, g2_ref,
                  wo_ref, wu_ref, wd_ref, y_ref):
    s, r, _ = x_ref.shape
    y_ref[...] = _out_ffn_tile(
        x_ref[...], hm_ref[...].reshape(s * r, M_W), at_ref[...].reshape(s * r, A_QW),
        ga1_ref[...], sc2_ref[...], sh2_ref[...], ga2_ref[...], g2_ref[...],
        wo_ref, wu_ref, wd_ref)


def _out_ffn(x, hm, at, ga1, sc2, sh2, ga2, g2, w_out, w_up, w_down, *, seqs, rows):
    ns, length, d = x.shape
    tile = lambda w: pl.BlockSpec((seqs, rows, w), lambda i, j: (i, j, 0))
    mod = pl.BlockSpec((seqs, 1, d), lambda i, j: (i, 0, 0))
    return pl.pallas_call(
        _out_ffn_body,
        grid=(ns // seqs, length // rows),
        in_specs=[tile(d), tile(M_W), tile(A_QW), mod, mod, mod, mod,
                  _const_spec((1, d)), _const_spec((d, d)), _const_spec((d, D_FF)),
                  _const_spec((D_FF, d))],
        out_specs=tile(d),
        out_shape=jax.ShapeDtypeStruct((ns, length, d), F32),
        compiler_params=_params(("arbitrary", "arbitrary")),
        name="out_ffn",
    )(x, hm, at, ga1, sc2, sh2, ga2, g2, w_out, w_up, w_down)


def _mix_ffn_body(sink_ref, mq_ref, mk_ref, mv_ref, og_ref, gt_ref, aq_ref, kp_ref, kc_ref,
                  vp_ref, vc_ref, c0_ref, n0_ref, m0_ref, gout_ref,
                  x_ref, ga1_ref, sc2_ref, sh2_ref, ga2_ref, g2_ref, wo_ref, wu_ref, wd_ref,
                  y_ref, c_out_ref, n_out_ref, m_out_ref,
                  hm_s, at_s, caug_ref, m_ref, *, n_tiles, tiles_per_seq, chunk):
    i = pl.program_id(0)
    rows_per_tile = mq_ref.shape[1]
    tile = jnp.minimum(i, n_tiles - 1)
    j = tile % tiles_per_seq
    slot = i % 2

    @pl.when(jnp.logical_and(j == 0, i < n_tiles))
    def _():
        _mlstm_load_state(c0_ref, n0_ref, m0_ref, caug_ref, m_ref)

    @pl.when(i == 0)
    def _():
        hm_s[1] = jnp.zeros(hm_s.shape[1:], hm_s.dtype)
        at_s[1] = jnp.zeros(at_s.shape[1:], at_s.dtype)

    def write_h(rows, lanes, value):
        hm_s[slot, rows, lanes] = value

    def write_o(rows, lanes, value):
        at_s[slot, rows, lanes] = value

    _mlstm_tile(mq_ref, mk_ref, mv_ref, og_ref, gt_ref, gout_ref, caug_ref, m_ref, write_h, chunk)
    _swa_tile(sink_ref, aq_ref, kp_ref, kc_ref, vp_ref, vc_ref, j * rows_per_tile, write_o)

    y_ref[...] = _out_ffn_tile(
        x_ref[...], hm_s[1 - slot], at_s[1 - slot],
        ga1_ref[...], sc2_ref[...], sh2_ref[...], ga2_ref[...], g2_ref[...],
        wo_ref, wu_ref, wd_ref)

    @pl.when(jnp.logical_and(j == tiles_per_seq - 1, i < n_tiles))
    def _():
        _mlstm_store_state(caug_ref, m_ref, c_out_ref, n_out_ref, m_out_ref)


def _mix_ffn(mq, mk, mv, og, gt, aq, ak, av, sinks, c0, n0, m0, gout,
             x, ga1, sc2, sh2, ga2, g2, w_out, w_up, w_down, *, rows, chunk):
    ns, length, d = x.shape
    tps = length // rows
    n_tiles = ns * tps
    kv_per_tile = rows // SWA_Q_ROWS

    def mix_idx(i):
        t = jnp.minimum(i, n_tiles - 1)
        return t // tps, t % tps

    def ffn_idx(i):
        t = jnp.maximum(i - 1, 0)
        return t // tps, t % tps

    def mix_tile(w):
        return pl.BlockSpec((1, rows, w), lambda i: (*mix_idx(i), 0))

    def kv_prev(i):
        b, j = mix_idx(i)
        return b, jnp.maximum(j * kv_per_tile - 1, 0), 0

    prev = pl.BlockSpec((1, SWA_Q_ROWS, A_KW), kv_prev)
    st_c = pl.BlockSpec((1, M_HEADS, M_DIM, M_DIM), lambda i: (mix_idx(i)[0], 0, 0, 0))
    st_n = pl.BlockSpec((1, M_HEADS, M_DIM), lambda i: (mix_idx(i)[0], 0, 0))
    st_m = pl.BlockSpec((1, M_HEADS, 1), lambda i: (mix_idx(i)[0], 0, 0))
    ffn_tile = pl.BlockSpec((1, rows, d), lambda i: (*ffn_idx(i), 0))
    mod = pl.BlockSpec((1, 1, d), lambda i: (ffn_idx(i)[0], 0, 0))
    return pl.pallas_call(
        functools.partial(_mix_ffn_body, n_tiles=n_tiles, tiles_per_seq=tps, chunk=chunk),
        grid=(n_tiles + 1,),
        in_specs=[pl.BlockSpec(memory_space=pltpu.SMEM),
                  mix_tile(M_W), mix_tile(M_W), mix_tile(M_W), mix_tile(M_W),
                  pl.BlockSpec((GATE_ROWS, rows), lambda i: (0, jnp.minimum(i, n_tiles - 1))),
                  mix_tile(A_QW), prev, mix_tile(A_KW), prev, mix_tile(A_KW),
                  st_c, st_n, st_m, _const_spec((M_HEADS, M_DIM)),
                  ffn_tile, mod, mod, mod, mod,
                  _const_spec((1, d)), _const_spec((d, d)), _const_spec((d, D_FF)),
                  _const_spec((D_FF, d))],
        out_specs=(ffn_tile, st_c, st_n, st_m),
        out_shape=(jax.ShapeDtypeStruct((ns, length, d), F32),
                   jax.ShapeDtypeStruct((ns, M_HEADS, M_DIM, M_DIM), F32),
                   jax.ShapeDtypeStruct((ns, M_HEADS, M_DIM), F32),
                   jax.ShapeDtypeStruct((ns, M_HEADS, 1), F32)),
        scratch_shapes=[pltpu.VMEM((2, rows, M_W), BF16),
                        pltpu.VMEM((2, rows, A_QW), BF16),
                        pltpu.VMEM((M_HEADS, M_DIM, 2 * M_DIM), F32),
                        pltpu.VMEM((M_HEADS, 1), F32)],
        compiler_params=_params(("arbitrary",)),
        name="mix_ffn",
    )(sinks, mq, mk, mv, og, gt, aq, ak, ak, av, av, c0, n0, m0, gout,
      x, ga1, sc2, sh2, ga2, g2, w_out, w_up, w_down)


def _block_ones(n):
    g = jnp.arange(n, dtype=jnp.int32) // A_DIM
    return (g[:, None] == g[None, :]).astype(BF16)


def kernel(x_prompt, x_sample, c_prompt, c_sample, cache_swa_k, cache_swa_v, state_mlstm_C,
           state_mlstm_n, state_mlstm_m, w_ada, b_ada, g_norm1, w_in, b_gates, g_q, g_k, sinks,
           g_mlstm_out, w_out, g_norm2, w_up, w_down):
    depth = w_ada.shape[0]
    assert depth == 1
    bp, lp, d = x_prompt.shape
    bs, ls, _ = x_sample.shape
    wb = cache_swa_k.shape[2]

    pq = _block_ones(A_QW)
    pk = _block_ones(A_KW)
    xp, xs = x_prompt, x_sample
    for l in range(depth):
        wl = w_in[l]
        g_lo = 4 * M_W
        g_hi = g_lo + 2 * M_HEADS
        w_proj = jnp.concatenate(
            [wl[:, :g_lo], wl[:, g_hi:], wl[:, g_lo:g_hi],
             jnp.zeros((d, LANES - 2 * M_HEADS), wl.dtype)], axis=1).astype(BF16)
        bg = jnp.concatenate([b_gates[l], jnp.zeros((LANES - 2 * M_HEADS,), F32)])[None, :]
        gq_t = (jnp.tile(g_q[l], A_HEADS) * (A_DIM ** -0.5))[None, :]
        gk_t = jnp.tile(g_k[l], A_KV_HEADS)[None, :]
        g1 = g_norm1[l][None, :]
        g2 = g_norm2[l][None, :]
        wo = w_out[l].astype(BF16)
        wu = w_up[l].astype(BF16)
        wd = w_down[l].astype(BF16)

        c_all = jnp.concatenate([c_prompt, c_sample, jnp.zeros((16 - bp - bs, d), F32)], axis=0)
        mod = _ada_mod(c_all, w_ada[l], b_ada[l][None, :])
        mod_p = [mod[:bp, i * d:(i + 1) * d][:, None, :] for i in range(6)]
        mod_s = [mod[bp:bp + bs, i * d:(i + 1) * d][:, None, :] for i in range(6)]

        def run_prompt(x, mods):
            sh1, sc1, ga1, sh2, sc2, ga2 = mods
            mq, mk, mv, og, gt, aq, ak, av = _in_proj(
                x, sc1, sh1, g1, w_proj, bg, gq_t, gk_t, pq, pk, seqs=1, rows=ROW_TILE,
                gate_chunk=MLSTM_CHUNK)
            c0 = jnp.zeros((bp, M_HEADS, M_DIM, M_DIM), F32)
            n0 = jnp.zeros((bp, M_HEADS, M_DIM), F32)
            m0 = jnp.zeros((bp, M_HEADS, 1), F32)
            y, c, n, m = _mix_ffn(mq, mk, mv, og, gt, aq, ak, av, sinks[l], c0, n0, m0,
                                  g_mlstm_out[l], x, ga1, sc2, sh2, ga2, g2, wo, wu, wd,
                                  rows=ROW_TILE, chunk=MLSTM_CHUNK)
            win = WIN_CHUNKS * CHUNK
            k_tail = ak[:, -win:].reshape(bp, win, A_KV_HEADS, A_DIM)
            v_tail = av[:, -win:].reshape(bp, win, A_KV_HEADS, A_DIM)
            return y, (k_tail, v_tail, c, n, m[..., 0])

        def run(x, mods, seqs, rows, gate_chunk, mixer):
            sh1, sc1, ga1, sh2, sc2, ga2 = mods
            mq, mk, mv, og, gt, aq, ak, av = _in_proj(
                x, sc1, sh1, g1, w_proj, bg, gq_t, gk_t, pq, pk, seqs=seqs, rows=rows,
                gate_chunk=gate_chunk)
            hm, at, state = mixer(mq, mk, mv, og, gt, aq, ak, av)
            y = _out_ffn(x, hm, at, ga1, sc2, sh2, ga2, g2, wo, wu, wd, seqs=seqs, rows=rows)
            return y, state

        def sample_mixer(mq, mk, mv, og, gt, aq, ak, av):
            gt3 = gt.reshape(GATE_ROWS, bs, ls).transpose(1, 0, 2)
            hm, c, n, m = _mlstm(mq, mk, mv, og, gt3, state_mlstm_C[l], state_mlstm_n[l],
                                 state_mlstm_m[l][..., None], g_mlstm_out[l], chunk=ls,
                                 chunks_per_step=1)
            ck = cache_swa_k[l].reshape(bs, wb, A_KW)
            cv = cache_swa_v[l].reshape(bs, wb, A_KW)
            at = _swa_sample(aq, ak, av, ck, cv, sinks[l])
            k_all = jnp.concatenate([ck, ak], axis=1)[:, -wb:].reshape(bs, wb, A_KV_HEADS, A_DIM)
            v_all = jnp.concatenate([cv, av], axis=1)[:, -wb:].reshape(bs, wb, A_KV_HEADS, A_DIM)
            return hm, at, (k_all, v_all, c, n, m[..., 0])

        xp, st_p = run_prompt(xp, mod_p)
        xs, st_s = run(xs, mod_s, bs, ls, ls, sample_mixer)

    return (xp, xs) + tuple(s[None] for s in st_p) + tuple(s[None] for s in st_s)
```

```python
import functools

import jax
import jax.numpy as jnp
from jax import lax
from jax.experimental import pallas as pl
from jax.experimental.pallas import tpu as pltpu

F32 = jnp.float32
BF16 = jnp.bfloat16

D_MODEL = 1024
CHUNK = 64
M_HEADS = 4
M_DIM = 128
M_W = M_HEADS * M_DIM
A_HEADS = 8
A_KV_HEADS = 2
A_DIM = 64
A_QW = A_HEADS * A_DIM
A_KW = A_KV_HEADS * A_DIM
WIN_CHUNKS = 2
D_FF = 4 * D_MODEL
EPS = 1e-6
PAST_LEN = 4096

LANES = 128
N_MAIN = 4 * M_W + A_QW + 2 * A_KW
N_PROJ = N_MAIN + LANES
VMEM_LIMIT = 62 * 1024 * 1024
NEG_INF = float("-inf")
LOG2E = 1.4426950408889634
ROW_TILE = 512
MLSTM_CHUNK = 256


def _params(sem):
    return pltpu.CompilerParams(dimension_semantics=sem, vmem_limit_bytes=VMEM_LIMIT)


def _const_spec(shape):
    nd = len(shape)
    return pl.BlockSpec(shape, lambda *_: (0,) * nd, pipeline_mode=pl.Buffered(1))


def _ada_body(c_ref, w_ref, b_ref, o_ref):
    c = c_ref[...]
    s = c * jax.nn.sigmoid(c)
    o_ref[...] = jnp.dot(s.astype(BF16), w_ref[...].astype(BF16),
                         preferred_element_type=F32) + b_ref[...]


def _ada_mod(c_all, w_ada, b_ada):
    rows = c_all.shape[0]
    return pl.pallas_call(
        _ada_body,
        grid=(6,),
        in_specs=[pl.BlockSpec((rows, D_MODEL), lambda j: (0, 0)),
                  pl.BlockSpec((D_MODEL, D_MODEL), lambda j: (0, j)),
                  pl.BlockSpec((1, D_MODEL), lambda j: (0, j))],
        out_specs=pl.BlockSpec((rows, D_MODEL), lambda j: (0, j)),
        out_shape=jax.ShapeDtypeStruct((rows, 6 * D_MODEL), F32),
        compiler_params=_params(("arbitrary",)),
        name="ada_mod",
    )(c_all, w_ada, b_ada)


def _group_sumsq(a, p_ref):
    sq = a * a
    hi = sq.astype(BF16)
    lo = (sq - hi.astype(F32)).astype(BF16)
    p = p_ref[...]
    return (jnp.dot(hi, p, preferred_element_type=F32)
            + jnp.dot(lo, p, preferred_element_type=F32))


def _segment_prefix(x, op, identity, seg):
    assert x.shape[1] % LANES == 0 and x.shape[1] % seg == 0
    pos = lax.broadcasted_iota(jnp.int32, x.shape, 1) % seg
    d = 1
    while d < seg:
        x = op(x, jnp.where(pos >= d, pltpu.roll(x, d, axis=1), identity))
        d *= 2
    return x


GATE_ROWS = 24


PROJ_OUTPUTS = ("mq", "mk", "mv", "og", "gt", "aq", "ak", "av")


def _drain(phases):
    for _ in phases:
        pass


def _interleave(stages):
    order = sorted(((k + 0.5) / n, idx) for idx, (_, n) in enumerate(stages) for k in range(n))
    for _, idx in order:
        next(stages[idx][0], None)
    for gen, _ in stages:
        _drain(gen)


IN_PROJ_PHASES = 7


def _in_proj_tile(x, sc, sh, g1, w_ref, bg, gq, gk, pq_ref, pk_ref, store, gate_chunk):
    s, r, d = x.shape
    m = s * r
    ms = jnp.mean(x * x, axis=-1, keepdims=True)
    y = x * lax.rsqrt(ms + EPS) * g1
    h = y * (1.0 + sc) + sh
    hb = h.reshape(m, d).astype(BF16)

    def proj(lo, width):
        return jnp.dot(hb, w_ref[:, lo:lo + width], preferred_element_type=F32)

    zg = proj(N_MAIN, LANES) + bg
    lane = lax.broadcasted_iota(jnp.int32, zg.shape, 1)
    log_sig = jnp.minimum(zg, 0.0) - jnp.log1p(jnp.exp(-jnp.abs(zg)))
    gates = jnp.where(lane < M_HEADS, zg, log_sig)
    gt = gates.T[:2 * M_HEADS, :]
    b8 = _segment_prefix(gt, jnp.add, 0.0, gate_chunk)
    c8 = gt - pltpu.roll(b8, M_HEADS, axis=0)
    cm8 = _segment_prefix(c8, jnp.maximum, NEG_INF, gate_chunk)
    store("gt", jnp.concatenate([c8, b8, cm8], axis=0))
    yield

    store("mq", proj(0, M_W).astype(BF16))
    yield
    store("mk", (proj(M_W, M_W) * (M_DIM ** -0.5)).astype(BF16))
    yield
    store("mv", proj(2 * M_W, M_W).astype(BF16))
    yield
    store("og", jax.nn.sigmoid(proj(3 * M_W, M_W)).astype(BF16))
    yield

    aq = proj(4 * M_W, A_QW)
    aqn = aq * lax.rsqrt(_group_sumsq(aq, pq_ref) * (1.0 / A_DIM) + EPS) * gq
    store("aq", aqn.astype(BF16))
    yield
    ak = proj(4 * M_W + A_QW, A_KW)
    store("ak", ak * lax.rsqrt(_group_sumsq(ak, pk_ref) * (1.0 / A_DIM) + EPS) * gk)
    store("av", proj(4 * M_W + A_QW + A_KW, A_KW))


def _in_proj_body(x_ref, sc_ref, sh_ref, g1_ref, w_ref, bg_ref, gq_ref, gk_ref, pq_ref, pk_ref,
                  *out_refs, gate_chunk):
    s, r, _ = x_ref.shape
    outs = dict(zip(PROJ_OUTPUTS, out_refs))

    def store(name, value):
        ref = outs[name]
        ref[...] = value if name == "gt" else value.reshape(s, r, value.shape[-1])

    _drain(_in_proj_tile(x_ref[...], sc_ref[...], sh_ref[...], g1_ref[...], w_ref, bg_ref[...],
                         gq_ref[...], gk_ref[...], pq_ref, pk_ref, store, gate_chunk))


def _in_proj(x, sc, sh, g1, w_proj, b_gates_pad, gq_t, gk_t, pq, pk, *, seqs, rows, gate_chunk):
    ns, length, d = x.shape
    grid = (ns // seqs, length // rows)
    m = seqs * rows
    n_tok = ns * length

    def tile(width):
        return pl.BlockSpec((seqs, rows, width), lambda i, j: (i, j, 0))

    mod = pl.BlockSpec((seqs, 1, d), lambda i, j: (i, 0, 0))
    steps_per_seq = length // rows
    out_shapes = (
        jax.ShapeDtypeStruct((ns, length, M_W), BF16),
        jax.ShapeDtypeStruct((ns, length, M_W), BF16),
        jax.ShapeDtypeStruct((ns, length, M_W), BF16),
        jax.ShapeDtypeStruct((ns, length, M_W), BF16),
        jax.ShapeDtypeStruct((GATE_ROWS, n_tok), F32),
        jax.ShapeDtypeStruct((ns, length, A_QW), BF16),
        jax.ShapeDtypeStruct((ns, length, A_KW), F32),
        jax.ShapeDtypeStruct((ns, length, A_KW), F32),
    )
    return pl.pallas_call(
        functools.partial(_in_proj_body, gate_chunk=gate_chunk),
        grid=grid,
        in_specs=[tile(d), mod, mod,
                  _const_spec((1, d)), _const_spec((d, N_PROJ)), _const_spec((1, LANES)),
                  _const_spec((1, A_QW)), _const_spec((1, A_KW)),
                  _const_spec((A_QW, A_QW)), _const_spec((A_KW, A_KW))],
        out_specs=(tile(M_W), tile(M_W), tile(M_W), tile(M_W),
                   pl.BlockSpec((GATE_ROWS, m), lambda i, j: (0, i * steps_per_seq + j)),
                   tile(A_QW), tile(A_KW), tile(A_KW)),
        out_shape=out_shapes,
        compiler_params=_params(("arbitrary", "arbitrary")),
        name="in_proj",
    )(x, sc, sh, g1, w_proj, b_gates_pad, gq_t, gk_t, pq, pk)


def _col_bcast(r):
    t = r.shape[1]
    if t % LANES == 0:
        return jnp.broadcast_to(r, (LANES, t)).T
    src = lax.broadcasted_iota(jnp.int32, (t, t), 0)
    dst = lax.broadcasted_iota(jnp.int32, (t, t), 1)
    col = jnp.sum(jnp.where(src == dst, jnp.broadcast_to(r, (t, t)), 0.0), axis=1, keepdims=True)
    return jnp.broadcast_to(col, (t, LANES))


def _eye_dim():
    return (lax.broadcasted_iota(jnp.int32, (M_DIM, M_DIM), 0)
            == lax.broadcasted_iota(jnp.int32, (M_DIM, M_DIM), 1))


def _mlstm_load_state(c0_ref, n0_ref, m0_ref, caug_ref, m_ref):
    for hd in range(M_HEADS):
        caug_ref[hd, :, :M_DIM] = c0_ref[0, hd]
        n_row = jnp.broadcast_to(n0_ref[0, hd:hd + 1, :], (M_DIM, M_DIM))
        n_col = jnp.sum(jnp.where(_eye_dim(), n_row, 0.0), axis=1, keepdims=True)
        caug_ref[hd, :, M_DIM:] = jnp.broadcast_to(n_col, (M_DIM, M_DIM))
    m_ref[...] = m0_ref[0]


def _mlstm_store_state(caug_ref, m_ref, c_out_ref, n_out_ref, m_out_ref):
    for hd in range(M_HEADS):
        c_out_ref[0, hd] = caug_ref[hd, :, :M_DIM]
        n_out_ref[0, hd:hd + 1, :] = jnp.sum(
            jnp.where(_eye_dim(), caug_ref[hd, :, M_DIM:], 0.0), axis=0, keepdims=True)
    m_out_ref[0] = m_ref[...]


def _mlstm_tile(q_ref, k_ref, v_ref, og_ref, gt_ref, gout_ref, caug_ref, m_ref, write_h, chunk):
    t = chunk
    n_chunks = q_ref.shape[1] // t
    bs = LANES if t % LANES == 0 else t
    tril = (lax.broadcasted_iota(jnp.int32, (bs, bs), 1)
            <= lax.broadcasted_iota(jnp.int32, (bs, bs), 0))
    m_prev = m_ref[...]
    for j in range(n_chunks):
        rows = slice(j * t, (j + 1) * t)
        c = gt_ref[0:M_HEADS, rows]
        b = gt_ref[12:12 + M_HEADS, rows]
        g = jnp.maximum(gt_ref[16:16 + M_HEADS, rows], m_prev)
        g_last = g[:, t - 1:t]
        c2 = c * LOG2E
        g2 = g * LOG2E
        e_all = jnp.exp2(-LOG2E * (b + g))
        w_src_all = jnp.exp2(c2 - LOG2E * g_last)
        w_inter_all = jnp.exp2(LOG2E * m_prev - g2)
        w_old_all = jnp.exp2(LOG2E * (m_prev - g_last))
        m_prev = b[:, t - 1:t] + g_last

        for hd in range(M_HEADS):
            sl = slice(hd * M_DIM, (hd + 1) * M_DIM)
            hr = slice(hd, hd + 1)
            g_b = _col_bcast(g2[hr])
            e_b = _col_bcast(e_all[hr])
            w_src = _col_bcast(w_src_all[hr])
            w_inter = _col_bcast(w_inter_all[hr])

            q = q_ref[0, rows, sl]
            k = k_ref[0, rows, sl]
            v = v_ref[0, rows, sl]
            caug = caug_ref[hd]
            inter = jnp.dot(q, caug.astype(BF16), preferred_element_type=F32)
            gain = gout_ref[hr, :]
            for rb in range(t // bs):
                rsl = slice(rb * bs, (rb + 1) * bs)
                ncols = (rb + 1) * bs
                s_mat = lax.dot_general(q[rsl], k[:ncols], (((1,), (1,)), ((), ())),
                                        preferred_element_type=F32)
                decay = []
                for cb in range(rb + 1):
                    arg = c2[hr, cb * bs:(cb + 1) * bs] - g_b[rsl, :bs]
                    decay.append(jnp.exp2(jnp.where(tril, arg, NEG_INF) if cb == rb else arg))
                s_mat = s_mat * (decay[0] if rb == 0 else jnp.concatenate(decay, axis=1))
                num = (jnp.dot(s_mat.astype(BF16), v[:ncols], preferred_element_type=F32)
                       + w_inter[rsl] * inter[rsl, :M_DIM])
                den = jnp.sum(s_mat, axis=1, keepdims=True) + w_inter[rsl] * inter[rsl, M_DIM:]
                hh = num / jnp.maximum(jnp.abs(den), e_b[rsl])
                hn = hh * lax.rsqrt(jnp.mean(hh * hh, axis=-1, keepdims=True) + EPS) * gain
                out_rows = slice(j * t + rb * bs, j * t + (rb + 1) * bs)
                write_h(out_rows, sl, (hn * og_ref[0, out_rows, sl].astype(F32)).astype(BF16))

            rhs = jnp.concatenate([v.astype(F32) * w_src, w_src], axis=1).astype(BF16)
            upd = lax.dot_general(k, rhs, (((0,), (0,)), ((), ())), preferred_element_type=F32)
            caug_ref[hd] = w_old_all[hr] * caug + upd
            yield
    m_ref[...] = m_prev


def _mlstm_body(q_ref, k_ref, v_ref, og_ref, gt_ref, c0_ref, n0_ref, m0_ref, gout_ref,
                h_ref, c_out_ref, n_out_ref, m_out_ref, caug_ref, m_ref, *, chunk):
    ci = pl.program_id(1)

    @pl.when(ci == 0)
    def _():
        _mlstm_load_state(c0_ref, n0_ref, m0_ref, caug_ref, m_ref)

    def write_h(rows, lanes, value):
        h_ref[0, rows, lanes] = value

    _drain(_mlstm_tile(q_ref, k_ref, v_ref, og_ref, gt_ref, gout_ref, caug_ref, m_ref, write_h,
                       chunk))

    @pl.when(ci == pl.num_programs(1) - 1)
    def _():
        _mlstm_store_state(caug_ref, m_ref, c_out_ref, n_out_ref, m_out_ref)


def _mlstm(mq, mk, mv, og, gt, c0, n0, m0, gout, *, chunk, chunks_per_step):
    ns, length, _ = mq.shape
    rows = chunk * chunks_per_step
    steps = length // rows
    tile = pl.BlockSpec((1, rows, M_W), lambda b, c: (b, c, 0))
    if gt.ndim == 2:
        gt_spec = pl.BlockSpec((GATE_ROWS, rows), lambda b, c: (0, b * steps + c))
    else:
        gt_spec = pl.BlockSpec((None, GATE_ROWS, rows), lambda b, c: (b, 0, 0))
    st_c = pl.BlockSpec((1, M_HEADS, M_DIM, M_DIM), lambda b, c: (b, 0, 0, 0))
    st_n = pl.BlockSpec((1, M_HEADS, M_DIM), lambda b, c: (b, 0, 0))
    st_m = pl.BlockSpec((1, M_HEADS, 1), lambda b, c: (b, 0, 0))
    return pl.pallas_call(
        functools.partial(_mlstm_body, chunk=chunk),
        grid=(ns, steps),
        in_specs=[tile, tile, tile, tile, gt_spec, st_c, st_n, st_m,
                  _const_spec((M_HEADS, M_DIM))],
        out_specs=(tile, st_c, st_n, st_m),
        out_shape=(jax.ShapeDtypeStruct((ns, length, M_W), BF16),
                   jax.ShapeDtypeStruct((ns, M_HEADS, M_DIM, M_DIM), F32),
                   jax.ShapeDtypeStruct((ns, M_HEADS, M_DIM), F32),
                   jax.ShapeDtypeStruct((ns, M_HEADS, 1), F32)),
        scratch_shapes=[pltpu.VMEM((M_HEADS, M_DIM, 2 * M_DIM), F32),
                        pltpu.VMEM((M_HEADS, 1), F32)],
        compiler_params=_params(("arbitrary", "arbitrary")),
        name="mlstm",
    )(mq, mk, mv, og, gt, c0, n0, m0, gout)


def _kv_halves(a):
    low = lax.broadcasted_iota(jnp.int32, a.shape, 1) < A_DIM
    lo_half = jnp.where(low, a, 0.0)
    hi_half = jnp.where(low, 0.0, a)
    pairs = [(lo_half, pltpu.roll(lo_half, A_DIM, axis=1)),
             (pltpu.roll(hi_half, A_DIM, axis=1), hi_half)]
    return [tuple(h.astype(BF16) for h in pair) for pair in pairs]


def _swa_biases(q_pos, k_pos):
    shift = CHUNK.bit_length() - 1
    assert 1 << shift == CHUNK
    qc = lax.shift_right_arithmetic(q_pos, shift)
    kc = lax.shift_right_arithmetic(k_pos, shift)
    allowed = (kc <= qc) & (kc >= qc - WIN_CHUNKS) & (k_pos >= 0)
    dist = jnp.abs(q_pos - k_pos).astype(F32)
    slopes = [2.0 ** (-8.0 * (head + 1) / A_HEADS) for head in range(A_HEADS)]
    return [jnp.where(allowed, (-LOG2E * slope) * dist, NEG_INF) for slope in slopes]


def _swa_block(q, kz, vz, biases, sink_ref, write_o, valid=None):
    for p in range(A_HEADS // 2):
        j = (2 * p) // (A_HEADS // A_KV_HEADS)
        q_pair = q[:, p * LANES:(p + 1) * LANES]
        acc = None
        for e in range(2):
            head = 2 * p + e
            sink = sink_ref[head] * LOG2E
            logits = lax.dot_general(q_pair, kz[j][e], (((1,), (1,)), ((), ())),
                                     preferred_element_type=F32) + biases[head]
            if valid is not None:
                logits = jnp.where(valid, logits, NEG_INF)
            mx = jnp.maximum(jnp.max(logits, axis=1, keepdims=True), sink)
            pr = jnp.exp2(logits - mx)
            denom = jnp.sum(pr, axis=1, keepdims=True) + jnp.exp2(sink - mx)
            part = jnp.dot(pr.astype(BF16), vz[j][e], preferred_element_type=F32) / denom
            acc = part if acc is None else acc + part
        write_o(slice(p * LANES, (p + 1) * LANES), acc.astype(BF16))


SWA_Q_ROWS = WIN_CHUNKS * CHUNK


def _swa_tile(sink_ref, q_ref, k_all, v_all, tile_start, write_o):
    tq = SWA_Q_ROWS
    kz = _kv_halves(k_all)
    vz = _kv_halves(v_all)
    rel_k = lax.broadcasted_iota(jnp.int32, (1, 2 * tq), 1)
    biases = _swa_biases(tq + lax.broadcasted_iota(jnp.int32, (tq, 1), 0), rel_k)
    for r in range(q_ref.shape[1] // tq):
        rows = slice(r * tq, (r + 1) * tq)
        band = slice(r * tq, (r + 2) * tq)
        valid = (tile_start - tq + rel_k >= 0) if r == 0 else None
        _swa_block(q_ref[0, rows, :],
                   [tuple(h[band] for h in pair) for pair in kz],
                   [tuple(h[band] for h in pair) for pair in vz],
                   biases, sink_ref,
                   lambda lanes, value, rows=rows: write_o(rows, lanes, value), valid=valid)
        yield


def _swa_sample_body(sink_ref, q_ref, kn_ref, vn_ref, ck_ref, cv_ref, o_ref):
    tq = q_ref.shape[1]
    wb = ck_ref.shape[1]
    tk = 2 * LANES
    pad = jnp.zeros((tk - wb - tq, A_KW), F32)
    kz = _kv_halves(jnp.concatenate([ck_ref[0], kn_ref[0], pad], axis=0))
    vz = _kv_halves(jnp.concatenate([cv_ref[0], vn_ref[0], pad], axis=0))
    q_pos = PAST_LEN + lax.broadcasted_iota(jnp.int32, (tq, 1), 0)
    idx = lax.broadcasted_iota(jnp.int32, (1, tk), 1)
    k_pos = jnp.where(idx < wb + tq, PAST_LEN - wb + idx, -1)

    def write_o(lanes, value):
        o_ref[0, :, lanes] = value

    _swa_block(q_ref[0], kz, vz, _swa_biases(q_pos, k_pos), sink_ref, write_o)


def _swa_sample(aq, ak, av, cache_k, cache_v, sinks):
    ns, tq, _ = aq.shape
    wb = cache_k.shape[1]
    new = lambda w: pl.BlockSpec((1, tq, w), lambda b: (b, 0, 0))
    old = pl.BlockSpec((1, wb, A_KW), lambda b: (b, 0, 0))
    return pl.pallas_call(
        _swa_sample_body,
        grid=(ns,),
        in_specs=[pl.BlockSpec(memory_space=pltpu.SMEM),
                  new(A_QW), new(A_KW), new(A_KW), old, old],
        out_specs=new(A_QW),
        out_shape=jax.ShapeDtypeStruct((ns, tq, A_QW), BF16),
        compiler_params=_params(("arbitrary",)),
        name="swa_sample",
    )(sinks, aq, ak, av, cache_k, cache_v)


FF_BLOCK = 1024


OUT_FFN_PHASES = 1 + D_FF // FF_BLOCK


def _out_ffn_tile(x, mixed, ga1, sc2, sh2, ga2, g2, wo_ref, wu_ref, wd_ref, write_y):
    s, r, d = x.shape
    mix = jnp.dot(mixed, wo_ref[...], preferred_element_type=F32)
    x1 = x + ga1 * mix.reshape(s, r, d)
    ms = jnp.mean(x1 * x1, axis=-1, keepdims=True)
    h2 = (x1 * lax.rsqrt(ms + EPS) * g2) * (1.0 + sc2) + sh2
    hb = h2.reshape(s * r, d).astype(BF16)
    yield
    ffn = jnp.zeros((s * r, d), F32)
    for c in range(D_FF // FF_BLOCK):
        u = jnp.dot(hb, wu_ref[:, c * FF_BLOCK:(c + 1) * FF_BLOCK], preferred_element_type=F32)
        u = jnp.square(jnp.maximum(u, 0.0)).astype(BF16)
        ffn = ffn + jnp.dot(u, wd_ref[c * FF_BLOCK:(c + 1) * FF_BLOCK, :],
                            preferred_element_type=F32)
        if c + 1 == D_FF // FF_BLOCK:
            write_y(x1 + ga2 * ffn.reshape(s, r, d))
        yield


def _out_ffn_body(x_ref, hm_ref, at_ref, ga1_ref, sc2_ref, sh2_ref, ga2_ref, g2_ref,
                  wo_ref, wu_ref, wd_ref, y_ref):
    s, r, _ = x_ref.shape
    mixed = jnp.concatenate([hm_ref[...].reshape(s * r, M_W), at_ref[...].reshape(s * r, A_QW)],
                            axis=1)

    def write_y(value):
        y_ref[...] = value

    _drain(_out_ffn_tile(
        x_ref[...], mixed, ga1_ref[...], sc2_ref[...], sh2_ref[...], ga2_ref[...], g2_ref[...],
        wo_ref, wu_ref, wd_ref, write_y))


def _out_ffn(x, hm, at, ga1, sc2, sh2, ga2, g2, w_out, w_up, w_down, *, seqs, rows):
    ns, length, d = x.shape
    tile = lambda w: pl.BlockSpec((seqs, rows, w), lambda i, j: (i, j, 0))
    mod = pl.BlockSpec((seqs, 1, d), lambda i, j: (i, 0, 0))
    return pl.pallas_call(
        _out_ffn_body,
        grid=(ns // seqs, length // rows),
        in_specs=[tile(d), tile(M_W), tile(A_QW), mod, mod, mod, mod,
                  _const_spec((1, d)), _const_spec((d, d)), _const_spec((d, D_FF)),
                  _const_spec((D_FF, d))],
        out_specs=tile(d),
        out_shape=jax.ShapeDtypeStruct((ns, length, d), F32),
        compiler_params=_params(("arbitrary", "arbitrary")),
        name="out_ffn",
    )(x, hm, at, ga1, sc2, sh2, ga2, g2, w_out, w_up, w_down)


def _mix_ffn_body(sink_ref, mq_ref, mk_ref, mv_ref, og_ref, gt_ref, aq_ref, kp_ref, kc_ref,
                  vp_ref, vc_ref, c0_ref, n0_ref, m0_ref, gout_ref,
                  x_ref, ga1_ref, sc2_ref, sh2_ref, ga2_ref, g2_ref, wo_ref, wu_ref, wd_ref,
                  y_ref, c_out_ref, n_out_ref, m_out_ref,
                  mix_s, caug_ref, m_ref, *, n_tiles, tiles_per_seq, chunk):
    i = pl.program_id(0)
    rows_per_tile = mq_ref.shape[1]
    tile = jnp.minimum(i, n_tiles - 1)
    j = tile % tiles_per_seq
    cur = i % 2

    @pl.when(jnp.logical_and(j == 0, i < n_tiles))
    def _():
        _mlstm_load_state(c0_ref, n0_ref, m0_ref, caug_ref, m_ref)

    @pl.when(i == 0)
    def _():
        mix_s[1] = jnp.zeros(mix_s.shape[1:], mix_s.dtype)

    def write_h(rows, lanes, value):
        mix_s[cur, rows, lanes] = value

    def write_o(rows, lanes, value):
        mix_s[cur, rows, slice(M_W + lanes.start, M_W + lanes.stop)] = value

    y_ref[...] = _out_ffn_tile(
        x_ref[...], mix_s[1 - cur], ga1_ref[...], sc2_ref[...], sh2_ref[...], ga2_ref[...],
        g2_ref[...], wo_ref, wu_ref, wd_ref)

    _mlstm_tile(mq_ref, mk_ref, mv_ref, og_ref, gt_ref, gout_ref, caug_ref, m_ref, write_h, chunk)
    _swa_tile(sink_ref, aq_ref, kp_ref, kc_ref, vp_ref, vc_ref, j * rows_per_tile, write_o)

    @pl.when(jnp.logical_and(j == tiles_per_seq - 1, i < n_tiles))
    def _():
        _mlstm_store_state(caug_ref, m_ref, c_out_ref, n_out_ref, m_out_ref)


def _mix_ffn(mq, mk, mv, og, gt, aq, ak, av, sinks, c0, n0, m0, gout,
             x, ga1, sc2, sh2, ga2, g2, w_out, w_up, w_down, *, rows, chunk):
    ns, length, d = x.shape
    tps = length // rows
    n_tiles = ns * tps
    kv_per_tile = rows // SWA_Q_ROWS

    def stage_idx(lag):
        def idx(i):
            t = jnp.clip(i - lag, 0, n_tiles - 1)
            return t // tps, t % tps
        return idx

    mix_idx, ffn_idx = stage_idx(0), stage_idx(1)

    def mix_tile(w):
        return pl.BlockSpec((1, rows, w), lambda i: (*mix_idx(i), 0))

    def kv_prev(i):
        b, j = mix_idx(i)
        return b, jnp.maximum(j * kv_per_tile - 1, 0), 0

    prev = pl.BlockSpec((1, SWA_Q_ROWS, A_KW), kv_prev)
    st_c = pl.BlockSpec((1, M_HEADS, M_DIM, M_DIM), lambda i: (mix_idx(i)[0], 0, 0, 0))
    st_n = pl.BlockSpec((1, M_HEADS, M_DIM), lambda i: (mix_idx(i)[0], 0, 0))
    st_m = pl.BlockSpec((1, M_HEADS, 1), lambda i: (mix_idx(i)[0], 0, 0))
    ffn_tile = pl.BlockSpec((1, rows, d), lambda i: (*ffn_idx(i), 0))
    mod = pl.BlockSpec((1, 1, d), lambda i: (ffn_idx(i)[0], 0, 0))
    return pl.pallas_call(
        functools.partial(_mix_ffn_body, n_tiles=n_tiles, tiles_per_seq=tps, chunk=chunk),
        grid=(n_tiles + 1,),
        in_specs=[pl.BlockSpec(memory_space=pltpu.SMEM),
                  mix_tile(M_W), mix_tile(M_W), mix_tile(M_W), mix_tile(M_W),
                  pl.BlockSpec((GATE_ROWS, rows), lambda i: (0, jnp.minimum(i, n_tiles - 1))),
                  mix_tile(A_QW), prev, mix_tile(A_KW), prev, mix_tile(A_KW),
                  st_c, st_n, st_m, _const_spec((M_HEADS, M_DIM)),
                  ffn_tile, mod, mod, mod, mod,
                  _const_spec((1, d)), _const_spec((d, d)), _const_spec((d, D_FF)),
                  _const_spec((D_FF, d))],
        out_specs=(ffn_tile, st_c, st_n, st_m),
        out_shape=(jax.ShapeDtypeStruct((ns, length, d), F32),
                   jax.ShapeDtypeStruct((ns, M_HEADS, M_DIM, M_DIM), F32),
                   jax.ShapeDtypeStruct((ns, M_HEADS, M_DIM), F32),
                   jax.ShapeDtypeStruct((ns, M_HEADS, 1), F32)),
        scratch_shapes=[pltpu.VMEM((2, rows, M_W + A_QW), BF16),
                        pltpu.VMEM((M_HEADS, M_DIM, 2 * M_DIM), F32),
                        pltpu.VMEM((M_HEADS, 1), F32)],
        compiler_params=_params(("arbitrary",)),
        name="mix_ffn",
    )(sinks, mq, mk, mv, og, gt, aq, ak, ak, av, av, c0, n0, m0, gout,
      x, ga1, sc2, sh2, ga2, g2, w_out, w_up, w_down)


PROJ_WIDTH = {"mq": M_W, "mk": M_W, "mv": M_W, "og": M_W, "aq": A_QW, "ak": A_KW, "av": A_KW}


def _layer_body(sink_ref, xp_ref, sc1_ref, sh1_ref, g1_ref, w_ref, bg_ref, gq_ref, gk_ref,
                pq_ref, pk_ref, c0_ref, n0_ref, m0_ref, gout_ref,
                xf_ref, ga1_ref, sc2_ref, sh2_ref, ga2_ref, g2_ref, wo_ref, wu_ref, wd_ref,
                y_ref, kt_ref, vt_ref, c_out_ref, n_out_ref, m_out_ref,
                mq_s, mk_s, mv_s, og_s, gt_s, aq_s, ak_s, av_s, kprev_s, vprev_s,
                mix_s, caug_ref, m_ref, *, n_tiles, tiles_per_seq, chunk):
    s = pl.program_id(0)
    rows = xp_ref.shape[1]
    cur = s % 2
    prv = 1 - cur
    tile_p = jnp.minimum(s, n_tiles - 1)
    tile_a = jnp.clip(s - 1, 0, n_tiles - 1)
    ja = tile_a % tiles_per_seq
    a_valid = jnp.logical_and(s >= 1, s <= n_tiles)
    proj_s = {"mq": mq_s, "mk": mk_s, "mv": mv_s, "og": og_s, "gt": gt_s,
              "aq": aq_s, "ak": ak_s, "av": av_s}

    @pl.when(s == 0)
    def _():
        for ref in proj_s.values():
            ref[1] = jnp.zeros(ref.shape[1:], ref.dtype)
        kprev_s[...] = jnp.zeros(kprev_s.shape, kprev_s.dtype)
        vprev_s[...] = jnp.zeros(vprev_s.shape, vprev_s.dtype)
        mix_s[0] = jnp.zeros(mix_s.shape[1:], mix_s.dtype)

    @pl.when(jnp.logical_and(ja == 0, a_valid))
    def _():
        _mlstm_load_state(c0_ref, n0_ref, m0_ref, caug_ref, m_ref)

    def write_y(value):
        y_ref[...] = value

    def write_h(r, lanes, value):
        mix_s[prv, r, lanes] = value

    def write_o(r, lanes, value):
        mix_s[prv, r, slice(M_W + lanes.start, M_W + lanes.stop)] = value

    def store(name, value):
        proj_s[name][cur] = value

    view = lambda ref: ref.at[pl.ds(prv, 1)]
    k_cur = ak_s[prv]
    v_cur = av_s[prv]
    k_all = jnp.concatenate([kprev_s[...], k_cur], axis=0)
    v_all = jnp.concatenate([vprev_s[...], v_cur], axis=0)
    kprev_s[...] = k_cur[rows - SWA_Q_ROWS:]
    vprev_s[...] = v_cur[rows - SWA_Q_ROWS:]
    _drain(_out_ffn_tile(xf_ref[...], mix_s[cur], ga1_ref[...], sc2_ref[...], sh2_ref[...],
                         ga2_ref[...], g2_ref[...], wo_ref, wu_ref, wd_ref, write_y))
    _drain(_mlstm_tile(view(mq_s), view(mk_s), view(mv_s), view(og_s), gt_s.at[prv], gout_ref,
                       caug_ref, m_ref, write_h, chunk))
    _drain(_swa_tile(sink_ref, view(aq_s), k_all, v_all, ja * rows, write_o))
    _drain(_in_proj_tile(xp_ref[...], sc1_ref[...], sh1_ref[...], g1_ref[...], w_ref, bg_ref[...],
                         gq_ref[...], gk_ref[...], pq_ref, pk_ref, store, chunk))

    @pl.when(jnp.logical_and(tile_p % tiles_per_seq == tiles_per_seq - 1, s < n_tiles))
    def _():
        kt_ref[0] = ak_s[cur, rows - SWA_Q_ROWS:, :]
        vt_ref[0] = av_s[cur, rows - SWA_Q_ROWS:, :]

    @pl.when(jnp.logical_and(ja == tiles_per_seq - 1, a_valid))
    def _():
        _mlstm_store_state(caug_ref, m_ref, c_out_ref, n_out_ref, m_out_ref)


def _layer(x, mods, sinks, c0, n0, m0, gout, g1, w_proj, bg, gq_t, gk_t, pq, pk,
           g2, w_out, w_up, w_down, *, rows, chunk):
    sh1, sc1, ga1, sh2, sc2, ga2 = mods
    ns, length, d = x.shape
    tps = length // rows
    n_tiles = ns * tps

    def stage(lag):
        def tile_idx(i):
            t = jnp.clip(i - lag, 0, n_tiles - 1)
            return t // tps, t % tps
        x_tile = pl.BlockSpec((1, rows, d), lambda i: (*tile_idx(i), 0))
        mod = pl.BlockSpec((1, 1, d), lambda i: (tile_idx(i)[0], 0, 0))
        per_seq = lambda shape: pl.BlockSpec(
            (1,) + shape, lambda i: (tile_idx(i)[0],) + (0,) * len(shape))
        return x_tile, mod, per_seq

    xp_tile, p_mod, p_seq = stage(0)
    _, _, a_seq = stage(1)
    xf_tile, f_mod, _ = stage(2)
    st_c, st_n, st_m = a_seq((M_HEADS, M_DIM, M_DIM)), a_seq((M_HEADS, M_DIM)), a_seq((M_HEADS, 1))
    tail = p_seq((SWA_Q_ROWS, A_KW))
    slots = lambda shape, dtype: pltpu.VMEM((2,) + shape, dtype)
    return pl.pallas_call(
        functools.partial(_layer_body, n_tiles=n_tiles, tiles_per_seq=tps, chunk=chunk),
        grid=(n_tiles + 2,),
        in_specs=[pl.BlockSpec(memory_space=pltpu.SMEM),
                  xp_tile, p_mod, p_mod,
                  _const_spec((1, d)), _const_spec((d, N_PROJ)), _const_spec((1, LANES)),
                  _const_spec((1, A_QW)), _const_spec((1, A_KW)),
                  _const_spec((A_QW, A_QW)), _const_spec((A_KW, A_KW)),
                  st_c, st_n, st_m, _const_spec((M_HEADS, M_DIM)),
                  xf_tile, f_mod, f_mod, f_mod, f_mod,
                  _const_spec((1, d)), _const_spec((d, d)), _const_spec((d, D_FF)),
                  _const_spec((D_FF, d))],
        out_specs=(xf_tile, tail, tail, st_c, st_n, st_m),
        out_shape=(jax.ShapeDtypeStruct((ns, length, d), F32),
                   jax.ShapeDtypeStruct((ns, SWA_Q_ROWS, A_KW), F32),
                   jax.ShapeDtypeStruct((ns, SWA_Q_ROWS, A_KW), F32),
                   jax.ShapeDtypeStruct((ns, M_HEADS, M_DIM, M_DIM), F32),
                   jax.ShapeDtypeStruct((ns, M_HEADS, M_DIM), F32),
                   jax.ShapeDtypeStruct((ns, M_HEADS, 1), F32)),
        scratch_shapes=[slots((rows, M_W), BF16), slots((rows, M_W), BF16),
                        slots((rows, M_W), BF16), slots((rows, M_W), BF16),
                        slots((GATE_ROWS, rows), F32), slots((rows, A_QW), BF16),
                        slots((rows, A_KW), F32), slots((rows, A_KW), F32),
                        pltpu.VMEM((SWA_Q_ROWS, A_KW), F32), pltpu.VMEM((SWA_Q_ROWS, A_KW), F32),
                        slots((rows, M_W + A_QW), BF16),
                        pltpu.VMEM((M_HEADS, M_DIM, 2 * M_DIM), F32),
                        pltpu.VMEM((M_HEADS, 1), F32)],
        compiler_params=_params(("arbitrary",)),
        name="layer",
    )(sinks, x, sc1, sh1, g1, w_proj, bg, gq_t, gk_t, pq, pk, c0, n0, m0, gout,
      x, ga1, sc2, sh2, ga2, g2, w_out, w_up, w_down)


def _block_ones(n):
    g = jnp.arange(n, dtype=jnp.int32) // A_DIM
    return (g[:, None] == g[None, :]).astype(BF16)


def kernel(x_prompt, x_sample, c_prompt, c_sample, cache_swa_k, cache_swa_v, state_mlstm_C,
           state_mlstm_n, state_mlstm_m, w_ada, b_ada, g_norm1, w_in, b_gates, g_q, g_k, sinks,
           g_mlstm_out, w_out, g_norm2, w_up, w_down):
    depth = w_ada.shape[0]
    assert depth == 1
    bp, lp, d = x_prompt.shape
    bs, ls, _ = x_sample.shape
    wb = cache_swa_k.shape[2]

    pq = _block_ones(A_QW)
    pk = _block_ones(A_KW)
    xp, xs = x_prompt, x_sample
    for l in range(depth):
        wl = w_in[l]
        g_lo = 4 * M_W
        g_hi = g_lo + 2 * M_HEADS
        w_proj = jnp.concatenate(
            [wl[:, :g_lo], wl[:, g_hi:], wl[:, g_lo:g_hi],
             jnp.zeros((d, LANES - 2 * M_HEADS), wl.dtype)], axis=1).astype(BF16)
        bg = jnp.concatenate([b_gates[l], jnp.zeros((LANES - 2 * M_HEADS,), F32)])[None, :]
        gq_t = (jnp.tile(g_q[l], A_HEADS) * (LOG2E * A_DIM ** -0.5))[None, :]
        gk_t = jnp.tile(g_k[l], A_KV_HEADS)[None, :]
        g1 = g_norm1[l][None, :]
        g2 = g_norm2[l][None, :]
        wo = w_out[l].astype(BF16)
        wu = w_up[l].astype(BF16)
        wd = w_down[l].astype(BF16)

        c_all = jnp.concatenate([c_prompt, c_sample, jnp.zeros((16 - bp - bs, d), F32)], axis=0)
        mod = _ada_mod(c_all, w_ada[l], b_ada[l][None, :])
        mod_p = [mod[:bp, i * d:(i + 1) * d][:, None, :] for i in range(6)]
        mod_s = [mod[bp:bp + bs, i * d:(i + 1) * d][:, None, :] for i in range(6)]

        def run_prompt(x, mods):
            c0 = jnp.zeros((bp, M_HEADS, M_DIM, M_DIM), F32)
            n0 = jnp.zeros((bp, M_HEADS, M_DIM), F32)
            m0 = jnp.zeros((bp, M_HEADS, 1), F32)
            y, kt, vt, c, n, m = _layer(
                x, mods, sinks[l], c0, n0, m0, g_mlstm_out[l], g1, w_proj, bg, gq_t, gk_t, pq, pk,
                g2, wo, wu, wd, rows=ROW_TILE, chunk=MLSTM_CHUNK)
            win = WIN_CHUNKS * CHUNK
            k_tail = kt.reshape(bp, win, A_KV_HEADS, A_DIM)
            v_tail = vt.reshape(bp, win, A_KV_HEADS, A_DIM)
            return y, (k_tail, v_tail, c, n, m[..., 0])

        def run(x, mods, seqs, rows, gate_chunk, mixer):
            sh1, sc1, ga1, sh2, sc2, ga2 = mods
            mq, mk, mv, og, gt, aq, ak, av = _in_proj(
                x, sc1, sh1, g1, w_proj, bg, gq_t, gk_t, pq, pk, seqs=seqs, rows=rows,
                gate_chunk=gate_chunk)
            hm, at, state = mixer(mq, mk, mv, og, gt, aq, ak, av)
            y = _out_ffn(x, hm, at, ga1, sc2, sh2, ga2, g2, wo, wu, wd, seqs=seqs, rows=rows)
            return y, state

        def sample_mixer(mq, mk, mv, og, gt, aq, ak, av):
            gt3 = gt.reshape(GATE_ROWS, bs, ls).transpose(1, 0, 2)
            hm, c, n, m = _mlstm(mq, mk, mv, og, gt3, state_mlstm_C[l], state_mlstm_n[l],
                                 state_mlstm_m[l][..., None], g_mlstm_out[l], chunk=ls,
                                 chunks_per_step=1)
            ck = cache_swa_k[l].reshape(bs, wb, A_KW)
            cv = cache_swa_v[l].reshape(bs, wb, A_KW)
            at = _swa_sample(aq, ak, av, ck, cv, sinks[l])
            k_all = jnp.concatenate([ck, ak], axis=1)[:, -wb:].reshape(bs, wb, A_KV_HEADS, A_DIM)
            v_all = jnp.concatenate([cv, av], axis=1)[:, -wb:].reshape(bs, wb, A_KV_HEADS, A_DIM)
            return hm, at, (k_all, v_all, c, n, m[..., 0])

        xp, st_p = run_prompt(xp, mod_p)
        xs, st_s = run(xs, mod_s, bs, ls, ls, sample_mixer)

    return (xp, xs) + tuple(s[None] for s in st_p) + tuple(s[None] for s in st_s)
```

```python
import functools

import jax
import jax.numpy as jnp
from jax import lax
from jax.experimental import pallas as pl
from jax.experimental.pallas import tpu as pltpu

F32 = jnp.float32
BF16 = jnp.bfloat16

D_MODEL = 1024
CHUNK = 64
M_HEADS = 4
M_DIM = 128
M_W = M_HEADS * M_DIM
A_HEADS = 8
A_KV_HEADS = 2
A_DIM = 64
A_QW = A_HEADS * A_DIM
A_KW = A_KV_HEADS * A_DIM
WIN_CHUNKS = 2
D_FF = 4 * D_MODEL
EPS = 1e-6
PAST_LEN = 4096

LANES = 128
BF16_ROWS = 16
VMEM_LIMIT = 62 * 1024 * 1024
NEG_INF = float("-inf")
LOG2E = 1.4426950408889634
ROW_TILE = 512
MLSTM_CHUNK = 256
FF_BLOCK = 1024
NT_DIMS = (((1,), (1,)), ((), ()))


def _params(sem):
    return pltpu.CompilerParams(dimension_semantics=sem, vmem_limit_bytes=VMEM_LIMIT)


def _const_spec(shape):
    nd = len(shape)
    return pl.BlockSpec(shape, lambda *_: (0,) * nd, pipeline_mode=pl.Buffered(1))


def _ada_body(c_ref, w_ref, b_ref, o_ref):
    c = c_ref[...]
    s = c * jax.nn.sigmoid(c)
    o_ref[...] = jnp.dot(s.astype(BF16), w_ref[...].astype(BF16),
                         preferred_element_type=F32) + b_ref[...]


def _ada_mod(c_all, w_ada, b_ada):
    rows = c_all.shape[0]
    return pl.pallas_call(
        _ada_body,
        grid=(6,),
        in_specs=[pl.BlockSpec((rows, D_MODEL), lambda j: (0, 0)),
                  pl.BlockSpec((D_MODEL, D_MODEL), lambda j: (0, j)),
                  pl.BlockSpec((1, D_MODEL), lambda j: (0, j))],
        out_specs=pl.BlockSpec((rows, D_MODEL), lambda j: (0, j)),
        out_shape=jax.ShapeDtypeStruct((rows, 6 * D_MODEL), F32),
        compiler_params=_params(("arbitrary",)),
        name="ada_mod",
    )(c_all, w_ada, b_ada)


def _segment_prefix(x, op, identity, seg):
    assert x.shape[1] % LANES == 0 and x.shape[1] % seg == 0
    pos = lax.broadcasted_iota(jnp.int32, x.shape, 1) % seg
    d = 1
    while d < seg:
        x = op(x, jnp.where(pos >= d, pltpu.roll(x, d, axis=1), identity))
        d *= 2
    return x


GATE_ROWS = 24
PROJ_OUTPUTS = ("mq", "mk", "mv", "og", "gt", "aq", "ak", "av")


def _in_proj_tile(x, sc, sh, g1, wa_ref, wb_ref, wg_ref, bg, gq, gk, pq_ref, pk_ref, store,
                  gate_chunk):
    s, r, d = x.shape
    m = s * r
    ms = jnp.mean(x * x, axis=-1, keepdims=True)
    y = x * lax.rsqrt(ms + EPS) * g1
    h = y * (1.0 + sc) + sh
    hb = h.reshape(m, d).astype(BF16)

    def proj(w_ref, lo, width):
        return lax.dot_general(hb, w_ref[lo:lo + width, :], NT_DIMS, preferred_element_type=F32)

    zg = lax.dot_general(wg_ref[...], hb, NT_DIMS,
                         preferred_element_type=F32)[:2 * M_HEADS] + bg
    row = lax.broadcasted_iota(jnp.int32, zg.shape, 0)
    log_sig = jnp.minimum(zg, 0.0) - jnp.log1p(jnp.exp(-jnp.abs(zg)))
    gt = jnp.where(row < M_HEADS, zg, log_sig)
    b8 = _segment_prefix(gt, jnp.add, 0.0, gate_chunk)
    c8 = gt - pltpu.roll(b8, M_HEADS, axis=0)
    cm8 = _segment_prefix(c8, jnp.maximum, NEG_INF, gate_chunk)
    store("gt", jnp.concatenate([c8, b8, cm8], axis=0))

    store("mq", proj(wa_ref, 0, M_W).astype(BF16))
    store("mk", (proj(wa_ref, M_W, M_W) * (M_DIM ** -0.5)).astype(BF16))
    store("mv", proj(wa_ref, 2 * M_W, M_W).astype(BF16))
    store("og", jax.nn.sigmoid(proj(wa_ref, 3 * M_W, M_W)).astype(BF16))

    aq = proj(wb_ref, 0, A_QW)
    ss_q = jnp.dot((aq * aq).astype(BF16), pq_ref[...], preferred_element_type=F32)
    store("aq", (aq * lax.rsqrt(ss_q * (1.0 / A_DIM) + EPS) * gq).astype(BF16))
    ak = proj(wb_ref, A_QW, A_KW)
    sq = ak * ak
    hi = sq.astype(BF16)
    lo = (sq - hi.astype(F32)).astype(BF16)
    ss_k = (jnp.dot(hi, pk_ref[...], preferred_element_type=F32)
            + jnp.dot(lo, pk_ref[...], preferred_element_type=F32))
    store("ak", ak * lax.rsqrt(ss_k * (1.0 / A_DIM) + EPS) * gk)
    store("av", proj(wb_ref, A_QW + A_KW, A_KW))


def _in_proj_body(x_ref, sc_ref, sh_ref, g1_ref, wa_ref, wb_ref, wg_ref, bg_ref, gq_ref, gk_ref,
                  pq_ref, pk_ref, *out_refs, gate_chunk):
    s, r, _ = x_ref.shape
    outs = dict(zip(PROJ_OUTPUTS, out_refs))

    def store(name, value):
        ref = outs[name]
        ref[...] = value if name == "gt" else value.reshape(s, r, value.shape[-1])

    _in_proj_tile(x_ref[...], sc_ref[...], sh_ref[...], g1_ref[...], wa_ref, wb_ref, wg_ref,
                  bg_ref[...], gq_ref[...], gk_ref[...], pq_ref, pk_ref, store, gate_chunk)


def _proj_weight_specs(d):
    return [_const_spec((1, d)), _const_spec((4 * M_W, d)), _const_spec((A_QW + 2 * A_KW, d)),
            _const_spec((BF16_ROWS, d)), _const_spec((2 * M_HEADS, 1)),
            _const_spec((1, A_QW)), _const_spec((1, A_KW)),
            _const_spec((A_QW, A_QW)), _const_spec((A_KW, A_KW))]


def _in_proj(x, sc, sh, proj_w, *, seqs, rows, gate_chunk):
    ns, length, d = x.shape
    grid = (ns // seqs, length // rows)
    m = seqs * rows
    n_tok = ns * length

    def tile(width):
        return pl.BlockSpec((seqs, rows, width), lambda i, j: (i, j, 0))

    mod = pl.BlockSpec((seqs, 1, d), lambda i, j: (i, 0, 0))
    steps_per_seq = length // rows
    out_shapes = (
        jax.ShapeDtypeStruct((ns, length, M_W), BF16),
        jax.ShapeDtypeStruct((ns, length, M_W), BF16),
        jax.ShapeDtypeStruct((ns, length, M_W), BF16),
        jax.ShapeDtypeStruct((ns, length, M_W), BF16),
        jax.ShapeDtypeStruct((GATE_ROWS, n_tok), F32),
        jax.ShapeDtypeStruct((ns, length, A_QW), BF16),
        jax.ShapeDtypeStruct((ns, length, A_KW), F32),
        jax.ShapeDtypeStruct((ns, length, A_KW), F32),
    )
    return pl.pallas_call(
        functools.partial(_in_proj_body, gate_chunk=gate_chunk),
        grid=grid,
        in_specs=[tile(d), mod, mod] + _proj_weight_specs(d),
        out_specs=(tile(M_W), tile(M_W), tile(M_W), tile(M_W),
                   pl.BlockSpec((GATE_ROWS, m), lambda i, j: (0, i * steps_per_seq + j)),
                   tile(A_QW), tile(A_KW), tile(A_KW)),
        out_shape=out_shapes,
        compiler_params=_params(("arbitrary", "arbitrary")),
        name="in_proj",
    )(x, sc, sh, *proj_w)


def _col_bcast(r):
    t = r.shape[1]
    if t % LANES == 0:
        return jnp.broadcast_to(r, (LANES, t)).T
    src = lax.broadcasted_iota(jnp.int32, (t, t), 0)
    dst = lax.broadcasted_iota(jnp.int32, (t, t), 1)
    col = jnp.sum(jnp.where(src == dst, jnp.broadcast_to(r, (t, t)), 0.0), axis=1, keepdims=True)
    return jnp.broadcast_to(col, (t, LANES))


def _eye_dim():
    return (lax.broadcasted_iota(jnp.int32, (M_DIM, M_DIM), 0)
            == lax.broadcasted_iota(jnp.int32, (M_DIM, M_DIM), 1))


def _mlstm_load_state(c0_ref, n0_ref, m0_ref, caug_ref, m_ref):
    for hd in range(M_HEADS):
        caug_ref[hd, :, :M_DIM] = c0_ref[0, hd]
        n_row = jnp.broadcast_to(n0_ref[0, hd:hd + 1, :], (M_DIM, M_DIM))
        n_col = jnp.sum(jnp.where(_eye_dim(), n_row, 0.0), axis=1, keepdims=True)
        caug_ref[hd, :, M_DIM:] = jnp.broadcast_to(n_col, (M_DIM, M_DIM))
    m_ref[...] = m0_ref[0]


def _mlstm_store_state(caug_ref, m_ref, c_out_ref, n_out_ref, m_out_ref):
    for hd in range(M_HEADS):
        c_out_ref[0, hd] = caug_ref[hd, :, :M_DIM]
        n_out_ref[0, hd:hd + 1, :] = jnp.sum(
            jnp.where(_eye_dim(), caug_ref[hd, :, M_DIM:], 0.0), axis=0, keepdims=True)
    m_out_ref[0] = m_ref[...]


def _mlstm_tile(q_ref, k_ref, v_ref, og_ref, gt_ref, gout_ref, caug_ref, m_ref, write_h, chunk):
    t = chunk
    n_chunks = q_ref.shape[1] // t
    bs = LANES if t % LANES == 0 else t
    tril = (lax.broadcasted_iota(jnp.int32, (bs, bs), 1)
            <= lax.broadcasted_iota(jnp.int32, (bs, bs), 0))
    m_prev = m_ref[...]
    for j in range(n_chunks):
        rows = slice(j * t, (j + 1) * t)
        c = gt_ref[0:M_HEADS, rows]
        b = gt_ref[12:12 + M_HEADS, rows]
        g = jnp.maximum(gt_ref[16:16 + M_HEADS, rows], m_prev)
        g_last = g[:, t - 1:t]
        c2 = c * LOG2E
        g2 = g * LOG2E
        e_all = jnp.exp2(-LOG2E * (b + g))
        w_src_all = jnp.exp2(c2 - LOG2E * g_last)
        w_inter_all = jnp.exp2(LOG2E * m_prev - g2)
        w_old_all = jnp.exp2(LOG2E * (m_prev - g_last))
        m_prev = b[:, t - 1:t] + g_last

        for hd in range(M_HEADS):
            sl = slice(hd * M_DIM, (hd + 1) * M_DIM)
            hr = slice(hd, hd + 1)
            g_b = _col_bcast(g2[hr])
            e_b = _col_bcast(e_all[hr])
            w_src = _col_bcast(w_src_all[hr])
            w_inter = _col_bcast(w_inter_all[hr])

            q = q_ref[0, rows, sl]
            k = k_ref[0, rows, sl]
            v = v_ref[0, rows, sl]
            caug = caug_ref[hd]
            inter = jnp.dot(q, caug.astype(BF16), preferred_element_type=F32)
            gain = gout_ref[hr, :]
            for rb in range(t // bs):
                rsl = slice(rb * bs, (rb + 1) * bs)
                ncols = (rb + 1) * bs
                s_mat = lax.dot_general(q[rsl], k[:ncols], NT_DIMS,
                                        preferred_element_type=F32)
                decay = []
                for cb in range(rb + 1):
                    arg = c2[hr, cb * bs:(cb + 1) * bs] - g_b[rsl, :bs]
                    decay.append(jnp.exp2(jnp.where(tril, arg, NEG_INF) if cb == rb else arg))
                s_mat = s_mat * (decay[0] if rb == 0 else jnp.concatenate(decay, axis=1))
                num = (jnp.dot(s_mat.astype(BF16), v[:ncols], preferred_element_type=F32)
                       + w_inter[rsl] * inter[rsl, :M_DIM])
                den = jnp.sum(s_mat, axis=1, keepdims=True) + w_inter[rsl] * inter[rsl, M_DIM:]
                hh = num / jnp.maximum(jnp.abs(den), e_b[rsl])
                hn = hh * lax.rsqrt(jnp.mean(hh * hh, axis=-1, keepdims=True) + EPS) * gain
                out_rows = slice(j * t + rb * bs, j * t + (rb + 1) * bs)
                write_h(out_rows, sl, (hn * og_ref[0, out_rows, sl].astype(F32)).astype(BF16))

            rhs = jnp.concatenate([v.astype(F32) * w_src, w_src], axis=1).astype(BF16)
            upd = lax.dot_general(k, rhs, (((0,), (0,)), ((), ())), preferred_element_type=F32)
            caug_ref[hd] = w_old_all[hr] * caug + upd
    m_ref[...] = m_prev


def _mlstm_body(q_ref, k_ref, v_ref, og_ref, gt_ref, c0_ref, n0_ref, m0_ref, gout_ref,
                h_ref, c_out_ref, n_out_ref, m_out_ref, caug_ref, m_ref, *, chunk):
    ci = pl.program_id(1)

    @pl.when(ci == 0)
    def _():
        _mlstm_load_state(c0_ref, n0_ref, m0_ref, caug_ref, m_ref)

    def write_h(rows, lanes, value):
        h_ref[0, rows, lanes] = value

    _mlstm_tile(q_ref, k_ref, v_ref, og_ref, gt_ref, gout_ref, caug_ref, m_ref, write_h, chunk)

    @pl.when(ci == pl.num_programs(1) - 1)
    def _():
        _mlstm_store_state(caug_ref, m_ref, c_out_ref, n_out_ref, m_out_ref)


def _mlstm(mq, mk, mv, og, gt, c0, n0, m0, gout, *, chunk):
    ns, length, _ = mq.shape
    assert length == chunk
    tile = pl.BlockSpec((1, chunk, M_W), lambda b, c: (b, c, 0))
    gt_spec = pl.BlockSpec((None, GATE_ROWS, chunk), lambda b, c: (b, 0, 0))
    st_c = pl.BlockSpec((1, M_HEADS, M_DIM, M_DIM), lambda b, c: (b, 0, 0, 0))
    st_n = pl.BlockSpec((1, M_HEADS, M_DIM), lambda b, c: (b, 0, 0))
    st_m = pl.BlockSpec((1, M_HEADS, 1), lambda b, c: (b, 0, 0))
    return pl.pallas_call(
        functools.partial(_mlstm_body, chunk=chunk),
        grid=(ns, 1),
        in_specs=[tile, tile, tile, tile, gt_spec, st_c, st_n, st_m,
                  _const_spec((M_HEADS, M_DIM))],
        out_specs=(tile, st_c, st_n, st_m),
        out_shape=(jax.ShapeDtypeStruct((ns, length, M_W), BF16),
                   jax.ShapeDtypeStruct((ns, M_HEADS, M_DIM, M_DIM), F32),
                   jax.ShapeDtypeStruct((ns, M_HEADS, M_DIM), F32),
                   jax.ShapeDtypeStruct((ns, M_HEADS, 1), F32)),
        scratch_shapes=[pltpu.VMEM((M_HEADS, M_DIM, 2 * M_DIM), F32),
                        pltpu.VMEM((M_HEADS, 1), F32)],
        compiler_params=_params(("arbitrary", "arbitrary")),
        name="mlstm",
    )(mq, mk, mv, og, gt, c0, n0, m0, gout)


def _kv_halves(a):
    low = lax.broadcasted_iota(jnp.int32, a.shape, 1) < A_DIM
    lo_half = jnp.where(low, a, 0.0)
    hi_half = jnp.where(low, 0.0, a)
    pairs = [(lo_half, pltpu.roll(lo_half, A_DIM, axis=1)),
             (pltpu.roll(hi_half, A_DIM, axis=1), hi_half)]
    return [tuple(h.astype(BF16) for h in pair) for pair in pairs]


def _swa_biases(q_pos, k_pos):
    shift = CHUNK.bit_length() - 1
    assert 1 << shift == CHUNK
    qc = lax.shift_right_arithmetic(q_pos, shift)
    kc = lax.shift_right_arithmetic(k_pos, shift)
    allowed = (kc <= qc) & (kc >= qc - WIN_CHUNKS) & (k_pos >= 0)
    dist = jnp.abs(q_pos - k_pos).astype(F32)
    slopes = [2.0 ** (-8.0 * (head + 1) / A_HEADS) for head in range(A_HEADS)]
    return [jnp.where(allowed, (-LOG2E * slope) * dist, NEG_INF) for slope in slopes]


def _swa_block(q, kz, vz, biases, sink_ref, write_o, valid=None):
    for p in range(A_HEADS // 2):
        j = (2 * p) // (A_HEADS // A_KV_HEADS)
        q_pair = q[:, p * LANES:(p + 1) * LANES]
        acc = None
        for e in range(2):
            head = 2 * p + e
            sink = sink_ref[head] * LOG2E
            logits = lax.dot_general(q_pair, kz[j][e], NT_DIMS,
                                     preferred_element_type=F32) + biases[head]
            if valid is not None:
                logits = jnp.where(valid, logits, NEG_INF)
            mx = jnp.maximum(jnp.max(logits, axis=1, keepdims=True), sink)
            pr = jnp.exp2(logits - mx)
            denom = jnp.sum(pr, axis=1, keepdims=True) + jnp.exp2(sink - mx)
            part = jnp.dot(pr.astype(BF16), vz[j][e], preferred_element_type=F32) / denom
            acc = part if acc is None else acc + part
        write_o(slice(p * LANES, (p + 1) * LANES), acc.astype(BF16))


SWA_Q_ROWS = WIN_CHUNKS * CHUNK


def _swa_tile(sink_ref, q_ref, k_all, v_all, tile_start, write_o):
    tq = SWA_Q_ROWS
    kz = _kv_halves(k_all)
    vz = _kv_halves(v_all)
    rel_k = lax.broadcasted_iota(jnp.int32, (1, 2 * tq), 1)
    biases = _swa_biases(tq + lax.broadcasted_iota(jnp.int32, (tq, 1), 0), rel_k)
    for r in range(q_ref.shape[1] // tq):
        rows = slice(r * tq, (r + 1) * tq)
        band = slice(r * tq, (r + 2) * tq)
        valid = (tile_start - tq + rel_k >= 0) if r == 0 else None
        _swa_block(q_ref[0, rows, :],
                   [tuple(h[band] for h in pair) for pair in kz],
                   [tuple(h[band] for h in pair) for pair in vz],
                   biases, sink_ref,
                   lambda lanes, value, rows=rows: write_o(rows, lanes, value), valid=valid)


def _swa_sample_body(sink_ref, q_ref, kn_ref, vn_ref, ck_ref, cv_ref, o_ref):
    tq = q_ref.shape[1]
    wb = ck_ref.shape[1]
    tk = 2 * LANES
    pad = jnp.zeros((tk - wb - tq, A_KW), F32)
    kz = _kv_halves(jnp.concatenate([ck_ref[0], kn_ref[0], pad], axis=0))
    vz = _kv_halves(jnp.concatenate([cv_ref[0], vn_ref[0], pad], axis=0))
    q_pos = PAST_LEN + lax.broadcasted_iota(jnp.int32, (tq, 1), 0)
    idx = lax.broadcasted_iota(jnp.int32, (1, tk), 1)
    k_pos = jnp.where(idx < wb + tq, PAST_LEN - wb + idx, -1)

    def write_o(lanes, value):
        o_ref[0, :, lanes] = value

    _swa_block(q_ref[0], kz, vz, _swa_biases(q_pos, k_pos), sink_ref, write_o)


def _swa_sample(aq, ak, av, cache_k, cache_v, sinks):
    ns, tq, _ = aq.shape
    wb = cache_k.shape[1]
    new = lambda w: pl.BlockSpec((1, tq, w), lambda b: (b, 0, 0))
    old = pl.BlockSpec((1, wb, A_KW), lambda b: (b, 0, 0))
    return pl.pallas_call(
        _swa_sample_body,
        grid=(ns,),
        in_specs=[pl.BlockSpec(memory_space=pltpu.SMEM),
                  new(A_QW), new(A_KW), new(A_KW), old, old],
        out_specs=new(A_QW),
        out_shape=jax.ShapeDtypeStruct((ns, tq, A_QW), BF16),
        compiler_params=_params(("arbitrary",)),
        name="swa_sample",
    )(sinks, aq, ak, av, cache_k, cache_v)


def _out_ffn_tile(x, mixed, ga1, sc2, sh2, ga2, g2, wo_ref, wu_ref, wd_ref):
    s, r, d = x.shape
    mix = jnp.dot(mixed, wo_ref[...], preferred_element_type=F32)
    x1 = x + ga1 * mix.reshape(s, r, d)
    ms = jnp.mean(x1 * x1, axis=-1, keepdims=True)
    h2 = (x1 * lax.rsqrt(ms + EPS) * g2) * (1.0 + sc2) + sh2
    hb = h2.reshape(s * r, d).astype(BF16)
    ffn = jnp.zeros((s * r, d), F32)
    for c in range(D_FF // FF_BLOCK):
        u = jnp.dot(hb, wu_ref[:, c * FF_BLOCK:(c + 1) * FF_BLOCK], preferred_element_type=F32)
        u = jnp.square(jnp.maximum(u, 0.0)).astype(BF16)
        ffn = ffn + jnp.dot(u, wd_ref[c * FF_BLOCK:(c + 1) * FF_BLOCK, :],
                            preferred_element_type=F32)
    return x1 + ga2 * ffn.reshape(s, r, d)


def _out_ffn_body(x_ref, hm_ref, at_ref, ga1_ref, sc2_ref, sh2_ref, ga2_ref, g2_ref,
                  wo_ref, wu_ref, wd_ref, y_ref):
    s, r, _ = x_ref.shape
    mixed = jnp.concatenate([hm_ref[...].reshape(s * r, M_W), at_ref[...].reshape(s * r, A_QW)],
                            axis=1)
    y_ref[...] = _out_ffn_tile(
        x_ref[...], mixed, ga1_ref[...], sc2_ref[...], sh2_ref[...], ga2_ref[...], g2_ref[...],
        wo_ref, wu_ref, wd_ref)


def _ffn_weight_specs(d):
    return [_const_spec((1, d)), _const_spec((d, d)), _const_spec((d, D_FF)),
            _const_spec((D_FF, d))]


def _out_ffn(x, hm, at, ga1, sc2, sh2, ga2, ffn_w, *, seqs, rows):
    ns, length, d = x.shape
    tile = lambda w: pl.BlockSpec((seqs, rows, w), lambda i, j: (i, j, 0))
    mod = pl.BlockSpec((seqs, 1, d), lambda i, j: (i, 0, 0))
    return pl.pallas_call(
        _out_ffn_body,
        grid=(ns // seqs, length // rows),
        in_specs=[tile(d), tile(M_W), tile(A_QW), mod, mod, mod, mod] + _ffn_weight_specs(d),
        out_specs=tile(d),
        out_shape=jax.ShapeDtypeStruct((ns, length, d), F32),
        compiler_params=_params(("arbitrary", "arbitrary")),
        name="out_ffn",
    )(x, hm, at, ga1, sc2, sh2, ga2, *ffn_w)


def _layer_body(sink_ref, xp_ref, sc1_ref, sh1_ref, g1_ref, wa_ref, wb_ref, wg_ref, bg_ref,
                gq_ref, gk_ref, pq_ref, pk_ref, c0_ref, n0_ref, m0_ref, gout_ref,
                xf_ref, ga1_ref, sc2_ref, sh2_ref, ga2_ref, g2_ref, wo_ref, wu_ref, wd_ref,
                y_ref, kt_ref, vt_ref, c_out_ref, n_out_ref, m_out_ref,
                mq_s, mk_s, mv_s, og_s, gt_s, aq_s, ak_s, av_s, kprev_s, vprev_s,
                mix_s, caug_ref, m_ref, *, n_tiles, tiles_per_seq, chunk):
    s = pl.program_id(0)
    rows = xp_ref.shape[1]
    cur = s % 2
    prv = 1 - cur
    tile_p = jnp.minimum(s, n_tiles - 1)
    tile_a = jnp.clip(s - 1, 0, n_tiles - 1)
    ja = tile_a % tiles_per_seq
    a_valid = jnp.logical_and(s >= 1, s <= n_tiles)
    proj_s = {"mq": mq_s, "mk": mk_s, "mv": mv_s, "og": og_s, "gt": gt_s,
              "aq": aq_s, "ak": ak_s, "av": av_s}

    @pl.when(s == 0)
    def _():
        for ref in proj_s.values():
            ref[1] = jnp.zeros(ref.shape[1:], ref.dtype)
        kprev_s[...] = jnp.zeros(kprev_s.shape, kprev_s.dtype)
        vprev_s[...] = jnp.zeros(vprev_s.shape, vprev_s.dtype)
        mix_s[0] = jnp.zeros(mix_s.shape[1:], mix_s.dtype)

    @pl.when(jnp.logical_and(ja == 0, a_valid))
    def _():
        _mlstm_load_state(c0_ref, n0_ref, m0_ref, caug_ref, m_ref)

    y_ref[...] = _out_ffn_tile(
        xf_ref[...], mix_s[cur], ga1_ref[...], sc2_ref[...], sh2_ref[...], ga2_ref[...],
        g2_ref[...], wo_ref, wu_ref, wd_ref)

    def write_h(r, lanes, value):
        mix_s[prv, r, lanes] = value

    def write_o(r, lanes, value):
        mix_s[prv, r, slice(M_W + lanes.start, M_W + lanes.stop)] = value

    view = lambda ref: ref.at[pl.ds(prv, 1)]
    _mlstm_tile(view(mq_s), view(mk_s), view(mv_s), view(og_s), gt_s.at[prv], gout_ref,
                caug_ref, m_ref, write_h, chunk)
    k_cur = ak_s[prv]
    v_cur = av_s[prv]
    _swa_tile(sink_ref, view(aq_s),
              jnp.concatenate([kprev_s[...], k_cur], axis=0),
              jnp.concatenate([vprev_s[...], v_cur], axis=0), ja * rows, write_o)
    kprev_s[...] = k_cur[rows - SWA_Q_ROWS:]
    vprev_s[...] = v_cur[rows - SWA_Q_ROWS:]

    def store(name, value):
        proj_s[name][cur] = value

    _in_proj_tile(xp_ref[...], sc1_ref[...], sh1_ref[...], g1_ref[...], wa_ref, wb_ref, wg_ref,
                  bg_ref[...], gq_ref[...], gk_ref[...], pq_ref, pk_ref, store, chunk)

    @pl.when(jnp.logical_and(tile_p % tiles_per_seq == tiles_per_seq - 1, s < n_tiles))
    def _():
        kt_ref[0] = ak_s[cur, rows - SWA_Q_ROWS:, :]
        vt_ref[0] = av_s[cur, rows - SWA_Q_ROWS:, :]

    @pl.when(jnp.logical_and(ja == tiles_per_seq - 1, a_valid))
    def _():
        _mlstm_store_state(caug_ref, m_ref, c_out_ref, n_out_ref, m_out_ref)


def _layer(x, mods, sinks, c0, n0, m0, gout, proj_w, ffn_w, *, rows, chunk):
    sh1, sc1, ga1, sh2, sc2, ga2 = mods
    ns, length, d = x.shape
    tps = length // rows
    n_tiles = ns * tps

    def stage(lag):
        def tile_idx(i):
            t = jnp.clip(i - lag, 0, n_tiles - 1)
            return t // tps, t % tps
        x_tile = pl.BlockSpec((1, rows, d), lambda i: (*tile_idx(i), 0))
        mod = pl.BlockSpec((1, 1, d), lambda i: (tile_idx(i)[0], 0, 0))
        per_seq = lambda shape: pl.BlockSpec(
            (1,) + shape, lambda i: (tile_idx(i)[0],) + (0,) * len(shape))
        return x_tile, mod, per_seq

    xp_tile, p_mod, p_seq = stage(0)
    _, _, a_seq = stage(1)
    xf_tile, f_mod, _ = stage(2)
    st_c, st_n, st_m = a_seq((M_HEADS, M_DIM, M_DIM)), a_seq((M_HEADS, M_DIM)), a_seq((M_HEADS, 1))
    tail = p_seq((SWA_Q_ROWS, A_KW))
    slots = lambda shape, dtype: pltpu.VMEM((2,) + shape, dtype)
    return pl.pallas_call(
        functools.partial(_layer_body, n_tiles=n_tiles, tiles_per_seq=tps, chunk=chunk),
        grid=(n_tiles + 2,),
        in_specs=([pl.BlockSpec(memory_space=pltpu.SMEM), xp_tile, p_mod, p_mod]
                  + _proj_weight_specs(d)
                  + [st_c, st_n, st_m, _const_spec((M_HEADS, M_DIM)),
                     xf_tile, f_mod, f_mod, f_mod, f_mod]
                  + _ffn_weight_specs(d)),
        out_specs=(xf_tile, tail, tail, st_c, st_n, st_m),
        out_shape=(jax.ShapeDtypeStruct((ns, length, d), F32),
                   jax.ShapeDtypeStruct((ns, SWA_Q_ROWS, A_KW), F32),
                   jax.ShapeDtypeStruct((ns, SWA_Q_ROWS, A_KW), F32),
                   jax.ShapeDtypeStruct((ns, M_HEADS, M_DIM, M_DIM), F32),
                   jax.ShapeDtypeStruct((ns, M_HEADS, M_DIM), F32),
                   jax.ShapeDtypeStruct((ns, M_HEADS, 1), F32)),
        scratch_shapes=[slots((rows, M_W), BF16), slots((rows, M_W), BF16),
                        slots((rows, M_W), BF16), slots((rows, M_W), BF16),
                        slots((GATE_ROWS, rows), F32), slots((rows, A_QW), BF16),
                        slots((rows, A_KW), F32), slots((rows, A_KW), F32),
                        pltpu.VMEM((SWA_Q_ROWS, A_KW), F32), pltpu.VMEM((SWA_Q_ROWS, A_KW), F32),
                        slots((rows, M_W + A_QW), BF16),
                        pltpu.VMEM((M_HEADS, M_DIM, 2 * M_DIM), F32),
                        pltpu.VMEM((M_HEADS, 1), F32)],
        compiler_params=_params(("arbitrary",)),
        name="layer",
    )(sinks, x, sc1, sh1, *proj_w, c0, n0, m0, gout, x, ga1, sc2, sh2, ga2, *ffn_w)


def _block_ones(n):
    g = jnp.arange(n, dtype=jnp.int32) // A_DIM
    return (g[:, None] == g[None, :]).astype(BF16)


def kernel(x_prompt, x_sample, c_prompt, c_sample, cache_swa_k, cache_swa_v, state_mlstm_C,
           state_mlstm_n, state_mlstm_m, w_ada, b_ada, g_norm1, w_in, b_gates, g_q, g_k, sinks,
           g_mlstm_out, w_out, g_norm2, w_up, w_down):
    depth = w_ada.shape[0]
    assert depth == 1
    bp, lp, d = x_prompt.shape
    bs, ls, _ = x_sample.shape
    wb = cache_swa_k.shape[2]

    xp, xs = x_prompt, x_sample
    for l in range(depth):
        wt = w_in[l].T
        g_lo = 4 * M_W
        g_hi = g_lo + 2 * M_HEADS
        wg = jnp.concatenate([wt[g_lo:g_hi], jnp.zeros((BF16_ROWS - 2 * M_HEADS, d), wt.dtype)])
        gq_t = (jnp.tile(g_q[l], A_HEADS) * (LOG2E * A_DIM ** -0.5))[None, :]
        proj_w = (g_norm1[l][None, :], wt[:g_lo].astype(BF16), wt[g_hi:].astype(BF16),
                  wg.astype(BF16), b_gates[l][:, None], gq_t,
                  jnp.tile(g_k[l], A_KV_HEADS)[None, :], _block_ones(A_QW), _block_ones(A_KW))
        ffn_w = (g_norm2[l][None, :], w_out[l].astype(BF16), w_up[l].astype(BF16),
                 w_down[l].astype(BF16))

        c_all = jnp.concatenate([c_prompt, c_sample, jnp.zeros((16 - bp - bs, d), F32)], axis=0)
        mod = _ada_mod(c_all, w_ada[l], b_ada[l][None, :])
        mod_p = [mod[:bp, i * d:(i + 1) * d][:, None, :] for i in range(6)]
        mod_s = [mod[bp:bp + bs, i * d:(i + 1) * d][:, None, :] for i in range(6)]

        c0 = jnp.zeros((bp, M_HEADS, M_DIM, M_DIM), F32)
        n0 = jnp.zeros((bp, M_HEADS, M_DIM), F32)
        m0 = jnp.zeros((bp, M_HEADS, 1), F32)
        xp, kt, vt, c, n, m = _layer(xp, mod_p, sinks[l], c0, n0, m0, g_mlstm_out[l], proj_w, ffn_w,
                                     rows=ROW_TILE, chunk=MLSTM_CHUNK)
        win = WIN_CHUNKS * CHUNK
        st_p = (kt.reshape(bp, win, A_KV_HEADS, A_DIM), vt.reshape(bp, win, A_KV_HEADS, A_DIM),
                c, n, m[..., 0])

        sh1, sc1, ga1, sh2, sc2, ga2 = mod_s
        mq, mk, mv, og, gt, aq, ak, av = _in_proj(xs, sc1, sh1, proj_w, seqs=bs, rows=ls,
                                                  gate_chunk=ls)
        gt3 = gt.reshape(GATE_ROWS, bs, ls).transpose(1, 0, 2)
        hm, c, n, m = _mlstm(mq, mk, mv, og, gt3, state_mlstm_C[l], state_mlstm_n[l],
                             state_mlstm_m[l][..., None], g_mlstm_out[l], chunk=ls)
        ck = cache_swa_k[l].reshape(bs, wb, A_KW)
        cv = cache_swa_v[l].reshape(bs, wb, A_KW)
        at = _swa_sample(aq, ak, av, ck, cv, sinks[l])
        k_all = jnp.concatenate([ck, ak], axis=1)[:, -wb:].reshape(bs, wb, A_KV_HEADS, A_DIM)
        v_all = jnp.concatenate([cv, av], axis=1)[:, -wb:].reshape(bs, wb, A_KV_HEADS, A_DIM)
        xs = _out_ffn(xs, hm, at, ga1, sc2, sh2, ga2, ffn_w, seqs=bs, rows=ls)
        st_s = (k_all, v_all, c, n, m[..., 0])

    return (xp, xs) + tuple(s[None] for s in st_p) + tuple(s[None] for s in st_s)
```

```python
import functools

import jax
import jax.numpy as jnp
from jax import lax
from jax.experimental import pallas as pl
from jax.experimental.pallas import tpu as pltpu

F32 = jnp.float32
BF16 = jnp.bfloat16

D_MODEL = 1024
CHUNK = 64
M_HEADS = 4
M_DIM = 128
M_W = M_HEADS * M_DIM
A_HEADS = 8
A_KV_HEADS = 2
A_DIM = 64
A_QW = A_HEADS * A_DIM
A_KW = A_KV_HEADS * A_DIM
WIN_CHUNKS = 2
D_FF = 4 * D_MODEL
EPS = 1e-6
PAST_LEN = 4096

LANES = 128
BF16_ROWS = 16
VMEM_LIMIT = 62 * 1024 * 1024
NEG_INF = float("-inf")
LOG2E = 1.4426950408889634
ROW_TILE = 512
MLSTM_CHUNK = 256
FF_BLOCK = 1024
NT_DIMS = (((1,), (1,)), ((), ()))
MOD_SH1, MOD_SC1, MOD_GA1, MOD_SH2, MOD_SC2, MOD_GA2 = range(6)


def _params(sem):
    return pltpu.CompilerParams(dimension_semantics=sem, vmem_limit_bytes=VMEM_LIMIT)


def _const_spec(shape):
    nd = len(shape)
    return pl.BlockSpec(shape, lambda *_: (0,) * nd, pipeline_mode=pl.Buffered(1))


def _ada_body(c_ref, w_ref, b_ref, o_ref):
    c = c_ref[...]
    s = c * jax.nn.sigmoid(c)
    o_ref[...] = jnp.dot(s.astype(BF16), w_ref[...].astype(BF16),
                         preferred_element_type=F32) + b_ref[...]


def _ada_mod(c_all, w_ada, b_ada):
    rows = c_all.shape[0]
    return pl.pallas_call(
        _ada_body,
        grid=(6,),
        in_specs=[pl.BlockSpec((rows, D_MODEL), lambda j: (0, 0)),
                  pl.BlockSpec((D_MODEL, D_MODEL), lambda j: (0, j)),
                  pl.BlockSpec((1, D_MODEL), lambda j: (0, j))],
        out_specs=pl.BlockSpec((rows, D_MODEL), lambda j: (0, j)),
        out_shape=jax.ShapeDtypeStruct((rows, 6 * D_MODEL), F32),
        compiler_params=_params(("arbitrary",)),
        name="ada_mod",
    )(c_all, w_ada, b_ada)


def _segment_prefix(x, op, identity, seg):
    assert x.shape[1] % LANES == 0 and x.shape[1] % seg == 0
    pos = lax.broadcasted_iota(jnp.int32, x.shape, 1) % seg
    d = 1
    while d < seg:
        x = op(x, jnp.where(pos >= d, pltpu.roll(x, d, axis=1), identity))
        d *= 2
    return x


GATE_ROWS = 24
PROJ_OUTPUTS = ("mq", "mk", "mv", "og", "gt", "aq", "ak", "av")


def _in_proj_tile(x, sc, sh, g1, wa_ref, wb_ref, wg_ref, bg, gq, gk, pq_ref, pk_ref, store,
                  gate_chunk):
    s, r, d = x.shape
    m = s * r
    ms = jnp.mean(x * x, axis=-1, keepdims=True)
    y = x * lax.rsqrt(ms + EPS) * g1
    h = y * (1.0 + sc) + sh
    hb = h.reshape(m, d).astype(BF16)

    def proj(w_ref, lo, width):
        return lax.dot_general(hb, w_ref[lo:lo + width, :], NT_DIMS, preferred_element_type=F32)

    zg = lax.dot_general(wg_ref[...], hb, NT_DIMS,
                         preferred_element_type=F32)[:2 * M_HEADS] + bg
    row = lax.broadcasted_iota(jnp.int32, zg.shape, 0)
    log_sig = jnp.minimum(zg, 0.0) - jnp.log1p(jnp.exp(-jnp.abs(zg)))
    gt = jnp.where(row < M_HEADS, zg, log_sig)
    b8 = _segment_prefix(gt, jnp.add, 0.0, gate_chunk)
    c8 = gt - pltpu.roll(b8, M_HEADS, axis=0)
    cm8 = _segment_prefix(c8, jnp.maximum, NEG_INF, gate_chunk)
    store("gt", jnp.concatenate([c8, b8, cm8], axis=0))

    store("mq", proj(wa_ref, 0, M_W).astype(BF16))
    store("mk", (proj(wa_ref, M_W, M_W) * (M_DIM ** -0.5)).astype(BF16))
    store("mv", proj(wa_ref, 2 * M_W, M_W).astype(BF16))
    store("og", jax.nn.sigmoid(proj(wa_ref, 3 * M_W, M_W)).astype(BF16))

    aq = proj(wb_ref, 0, A_QW)
    ss_q = jnp.dot((aq * aq).astype(BF16), pq_ref[...], preferred_element_type=F32)
    store("aq", (aq * lax.rsqrt(ss_q * (1.0 / A_DIM) + EPS) * gq).astype(BF16))
    ak = proj(wb_ref, A_QW, A_KW)
    sq = ak * ak
    hi = sq.astype(BF16)
    lo = (sq - hi.astype(F32)).astype(BF16)
    ss_k = (jnp.dot(hi, pk_ref[...], preferred_element_type=F32)
            + jnp.dot(lo, pk_ref[...], preferred_element_type=F32))
    store("ak", ak * lax.rsqrt(ss_k * (1.0 / A_DIM) + EPS) * gk)
    store("av", proj(wb_ref, A_QW + A_KW, A_KW))


def _in_proj_body(x_ref, sc_ref, sh_ref, g1_ref, wa_ref, wb_ref, wg_ref, bg_ref, gq_ref, gk_ref,
                  pq_ref, pk_ref, *out_refs, gate_chunk):
    s, r, _ = x_ref.shape
    outs = dict(zip(PROJ_OUTPUTS, out_refs))

    def store(name, value):
        ref = outs[name]
        ref[...] = value if name == "gt" else value.reshape(s, r, value.shape[-1])

    _in_proj_tile(x_ref[...], sc_ref[...], sh_ref[...], g1_ref[...], wa_ref, wb_ref, wg_ref,
                  bg_ref[...], gq_ref[...], gk_ref[...], pq_ref, pk_ref, store, gate_chunk)


def _proj_weight_specs(d):
    return [_const_spec((1, d)), _const_spec((4 * M_W, d)), _const_spec((A_QW + 2 * A_KW, d)),
            _const_spec((BF16_ROWS, d)), _const_spec((2 * M_HEADS, 1)),
            _const_spec((1, A_QW)), _const_spec((1, A_KW)),
            _const_spec((A_QW, A_QW)), _const_spec((A_KW, A_KW))]


def _in_proj(x, sc, sh, proj_w, *, seqs, rows, gate_chunk):
    ns, length, d = x.shape
    grid = (ns // seqs, length // rows)
    m = seqs * rows
    n_tok = ns * length

    def tile(width):
        return pl.BlockSpec((seqs, rows, width), lambda i, j: (i, j, 0))

    mod = pl.BlockSpec((seqs, 1, d), lambda i, j: (i, 0, 0))
    steps_per_seq = length // rows
    out_shapes = (
        jax.ShapeDtypeStruct((ns, length, M_W), BF16),
        jax.ShapeDtypeStruct((ns, length, M_W), BF16),
        jax.ShapeDtypeStruct((ns, length, M_W), BF16),
        jax.ShapeDtypeStruct((ns, length, M_W), BF16),
        jax.ShapeDtypeStruct((GATE_ROWS, n_tok), F32),
        jax.ShapeDtypeStruct((ns, length, A_QW), BF16),
        jax.ShapeDtypeStruct((ns, length, A_KW), F32),
        jax.ShapeDtypeStruct((ns, length, A_KW), F32),
    )
    return pl.pallas_call(
        functools.partial(_in_proj_body, gate_chunk=gate_chunk),
        grid=grid,
        in_specs=[tile(d), mod, mod] + _proj_weight_specs(d),
        out_specs=(tile(M_W), tile(M_W), tile(M_W), tile(M_W),
                   pl.BlockSpec((GATE_ROWS, m), lambda i, j: (0, i * steps_per_seq + j)),
                   tile(A_QW), tile(A_KW), tile(A_KW)),
        out_shape=out_shapes,
        compiler_params=_params(("arbitrary", "arbitrary")),
        name="in_proj",
    )(x, sc, sh, *proj_w)


def _col_bcast(r):
    t = r.shape[1]
    if t % LANES == 0:
        return jnp.broadcast_to(r, (LANES, t)).T
    src = lax.broadcasted_iota(jnp.int32, (t, t), 0)
    dst = lax.broadcasted_iota(jnp.int32, (t, t), 1)
    col = jnp.sum(jnp.where(src == dst, jnp.broadcast_to(r, (t, t)), 0.0), axis=1, keepdims=True)
    return jnp.broadcast_to(col, (t, LANES))


def _eye_dim():
    return (lax.broadcasted_iota(jnp.int32, (M_DIM, M_DIM), 0)
            == lax.broadcasted_iota(jnp.int32, (M_DIM, M_DIM), 1))


def _mlstm_load_state(c0_ref, n0_ref, m0_ref, caug_ref, m_ref):
    for hd in range(M_HEADS):
        caug_ref[hd, :, :M_DIM] = c0_ref[0, hd]
        n_row = jnp.broadcast_to(n0_ref[0, hd:hd + 1, :], (M_DIM, M_DIM))
        n_col = jnp.sum(jnp.where(_eye_dim(), n_row, 0.0), axis=1, keepdims=True)
        caug_ref[hd, :, M_DIM:] = jnp.broadcast_to(n_col, (M_DIM, M_DIM))
    m_ref[...] = m0_ref[0]


def _mlstm_store_state(caug_ref, m_ref, c_out_ref, n_out_ref, m_out_ref):
    for hd in range(M_HEADS):
        c_out_ref[0, hd] = caug_ref[hd, :, :M_DIM]
        n_out_ref[0, hd:hd + 1, :] = jnp.sum(
            jnp.where(_eye_dim(), caug_ref[hd, :, M_DIM:], 0.0), axis=0, keepdims=True)
    m_out_ref[0] = m_ref[...]


def _mlstm_tile(q_ref, k_ref, v_ref, og_ref, gt_ref, gout_ref, caug_ref, m_ref, write_h, chunk):
    t = chunk
    n_chunks = q_ref.shape[1] // t
    bs = LANES if t % LANES == 0 else t
    tril = (lax.broadcasted_iota(jnp.int32, (bs, bs), 1)
            <= lax.broadcasted_iota(jnp.int32, (bs, bs), 0))
    m_prev = m_ref[...]
    for j in range(n_chunks):
        rows = slice(j * t, (j + 1) * t)
        c = gt_ref[0:M_HEADS, rows]
        b = gt_ref[12:12 + M_HEADS, rows]
        g = jnp.maximum(gt_ref[16:16 + M_HEADS, rows], m_prev)
        g_last = g[:, t - 1:t]
        c2 = c * LOG2E
        g2 = g * LOG2E
        e_all = jnp.exp2(-LOG2E * (b + g))
        w_src_all = jnp.exp2(c2 - LOG2E * g_last)
        w_inter_all = jnp.exp2(LOG2E * m_prev - g2)
        w_old_all = jnp.exp2(LOG2E * (m_prev - g_last))
        m_prev = b[:, t - 1:t] + g_last

        for hd in range(M_HEADS):
            sl = slice(hd * M_DIM, (hd + 1) * M_DIM)
            hr = slice(hd, hd + 1)
            g_b = _col_bcast(g2[hr])
            e_b = _col_bcast(e_all[hr])
            w_src = _col_bcast(w_src_all[hr])
            w_inter = _col_bcast(w_inter_all[hr])

            q = q_ref[0, rows, sl]
            k = k_ref[0, rows, sl]
            v = v_ref[0, rows, sl]
            caug = caug_ref[hd]
            inter = jnp.dot(q, caug.astype(BF16), preferred_element_type=F32)
            gain = gout_ref[hr, :]
            for rb in range(t // bs):
                rsl = slice(rb * bs, (rb + 1) * bs)
                ncols = (rb + 1) * bs
                s_mat = lax.dot_general(q[rsl], k[:ncols], NT_DIMS,
                                        preferred_element_type=F32)
                decay = []
                for cb in range(rb + 1):
                    arg = c2[hr, cb * bs:(cb + 1) * bs] - g_b[rsl, :bs]
                    decay.append(jnp.exp2(jnp.where(tril, arg, NEG_INF) if cb == rb else arg))
                s_mat = s_mat * (decay[0] if rb == 0 else jnp.concatenate(decay, axis=1))
                num = (jnp.dot(s_mat.astype(BF16), v[:ncols], preferred_element_type=F32)
                       + w_inter[rsl] * inter[rsl, :M_DIM])
                den = jnp.sum(s_mat, axis=1, keepdims=True) + w_inter[rsl] * inter[rsl, M_DIM:]
                hh = num / jnp.maximum(jnp.abs(den), e_b[rsl])
                hn = hh * lax.rsqrt(jnp.mean(hh * hh, axis=-1, keepdims=True) + EPS) * gain
                out_rows = slice(j * t + rb * bs, j * t + (rb + 1) * bs)
                write_h(out_rows, sl, (hn * og_ref[0, out_rows, sl].astype(F32)).astype(BF16))

            rhs = jnp.concatenate([v.astype(F32) * w_src, w_src], axis=1).astype(BF16)
            upd = lax.dot_general(k, rhs, (((0,), (0,)), ((), ())), preferred_element_type=F32)
            caug_ref[hd] = w_old_all[hr] * caug + upd
    m_ref[...] = m_prev


def _mlstm_body(q_ref, k_ref, v_ref, og_ref, gt_ref, c0_ref, n0_ref, m0_ref, gout_ref,
                h_ref, c_out_ref, n_out_ref, m_out_ref, caug_ref, m_ref, *, chunk):
    ci = pl.program_id(1)

    @pl.when(ci == 0)
    def _():
        _mlstm_load_state(c0_ref, n0_ref, m0_ref, caug_ref, m_ref)

    def write_h(rows, lanes, value):
        h_ref[0, rows, lanes] = value

    _mlstm_tile(q_ref, k_ref, v_ref, og_ref, gt_ref, gout_ref, caug_ref, m_ref, write_h, chunk)

    @pl.when(ci == pl.num_programs(1) - 1)
    def _():
        _mlstm_store_state(caug_ref, m_ref, c_out_ref, n_out_ref, m_out_ref)


def _mlstm(mq, mk, mv, og, gt, c0, n0, m0, gout, *, chunk):
    ns, length, _ = mq.shape
    assert length == chunk
    tile = pl.BlockSpec((1, chunk, M_W), lambda b, c: (b, c, 0))
    gt_spec = pl.BlockSpec((None, GATE_ROWS, chunk), lambda b, c: (b, 0, 0))
    st_c = pl.BlockSpec((1, M_HEADS, M_DIM, M_DIM), lambda b, c: (b, 0, 0, 0))
    st_n = pl.BlockSpec((1, M_HEADS, M_DIM), lambda b, c: (b, 0, 0))
    st_m = pl.BlockSpec((1, M_HEADS, 1), lambda b, c: (b, 0, 0))
    return pl.pallas_call(
        functools.partial(_mlstm_body, chunk=chunk),
        grid=(ns, 1),
        in_specs=[tile, tile, tile, tile, gt_spec, st_c, st_n, st_m,
                  _const_spec((M_HEADS, M_DIM))],
        out_specs=(tile, st_c, st_n, st_m),
        out_shape=(jax.ShapeDtypeStruct((ns, length, M_W), BF16),
                   jax.ShapeDtypeStruct((ns, M_HEADS, M_DIM, M_DIM), F32),
                   jax.ShapeDtypeStruct((ns, M_HEADS, M_DIM), F32),
                   jax.ShapeDtypeStruct((ns, M_HEADS, 1), F32)),
        scratch_shapes=[pltpu.VMEM((M_HEADS, M_DIM, 2 * M_DIM), F32),
                        pltpu.VMEM((M_HEADS, 1), F32)],
        compiler_params=_params(("arbitrary", "arbitrary")),
        name="mlstm",
    )(mq, mk, mv, og, gt, c0, n0, m0, gout)


def _kv_halves(a):
    low = lax.broadcasted_iota(jnp.int32, a.shape, 1) < A_DIM
    lo_half = jnp.where(low, a, 0.0)
    hi_half = jnp.where(low, 0.0, a)
    pairs = [(lo_half, pltpu.roll(lo_half, A_DIM, axis=1)),
             (pltpu.roll(hi_half, A_DIM, axis=1), hi_half)]
    return [tuple(h.astype(BF16) for h in pair) for pair in pairs]


def _swa_biases(q_pos, k_pos):
    shift = CHUNK.bit_length() - 1
    assert 1 << shift == CHUNK
    qc = lax.shift_right_arithmetic(q_pos, shift)
    kc = lax.shift_right_arithmetic(k_pos, shift)
    allowed = (kc <= qc) & (kc >= qc - WIN_CHUNKS) & (k_pos >= 0)
    dist = jnp.abs(q_pos - k_pos).astype(F32)
    slopes = [2.0 ** (-8.0 * (head + 1) / A_HEADS) for head in range(A_HEADS)]
    return [jnp.where(allowed, (-LOG2E * slope) * dist, NEG_INF) for slope in slopes]


def _swa_block(q, kz, vz, biases, sink_ref, write_o, valid=None):
    tq = q.shape[0]
    pairs_per_kv = A_HEADS // A_KV_HEADS // 2
    for j in range(A_KV_HEADS):
        pairs = [j * pairs_per_kv + i for i in range(pairs_per_kv)]
        q_stack = jnp.concatenate([q[:, p * LANES:(p + 1) * LANES] for p in pairs], axis=0)
        acc = [None] * pairs_per_kv
        for e in range(2):
            logits_all = lax.dot_general(q_stack, kz[j][e], NT_DIMS, preferred_element_type=F32)
            probs, denoms = [], []
            for i, p in enumerate(pairs):
                head = 2 * p + e
                sink = sink_ref[head] * LOG2E
                logits = logits_all[i * tq:(i + 1) * tq] + biases[head]
                if valid is not None:
                    logits = jnp.where(valid, logits, NEG_INF)
                mx = jnp.maximum(jnp.max(logits, axis=1, keepdims=True), sink)
                pr = jnp.exp2(logits - mx)
                denoms.append(jnp.sum(pr, axis=1, keepdims=True) + jnp.exp2(sink - mx))
                probs.append(pr.astype(BF16))
            out_all = jnp.dot(jnp.concatenate(probs, axis=0), vz[j][e],
                              preferred_element_type=F32)
            for i in range(pairs_per_kv):
                part = out_all[i * tq:(i + 1) * tq] / denoms[i]
                acc[i] = part if acc[i] is None else acc[i] + part
        for i, p in enumerate(pairs):
            write_o(slice(p * LANES, (p + 1) * LANES), acc[i].astype(BF16))


SWA_Q_ROWS = WIN_CHUNKS * CHUNK


def _swa_tile(sink_ref, q_ref, k_all, v_all, tile_start, write_o):
    tq = SWA_Q_ROWS
    kz = _kv_halves(k_all)
    vz = _kv_halves(v_all)
    rel_k = lax.broadcasted_iota(jnp.int32, (1, 2 * tq), 1)
    biases = _swa_biases(tq + lax.broadcasted_iota(jnp.int32, (tq, 1), 0), rel_k)
    for r in range(q_ref.shape[1] // tq):
        rows = slice(r * tq, (r + 1) * tq)
        band = slice(r * tq, (r + 2) * tq)
        valid = (tile_start - tq + rel_k >= 0) if r == 0 else None
        _swa_block(q_ref[0, rows, :],
                   [tuple(h[band] for h in pair) for pair in kz],
                   [tuple(h[band] for h in pair) for pair in vz],
                   biases, sink_ref,
                   lambda lanes, value, rows=rows: write_o(rows, lanes, value), valid=valid)


def _swa_sample_body(sink_ref, q_ref, kn_ref, vn_ref, ck_ref, cv_ref, o_ref):
    tq = q_ref.shape[1]
    wb = ck_ref.shape[1]
    tk = 2 * LANES
    pad = jnp.zeros((tk - wb - tq, A_KW), F32)
    kz = _kv_halves(jnp.concatenate([ck_ref[0], kn_ref[0], pad], axis=0))
    vz = _kv_halves(jnp.concatenate([cv_ref[0], vn_ref[0], pad], axis=0))
    q_pos = PAST_LEN + lax.broadcasted_iota(jnp.int32, (tq, 1), 0)
    idx = lax.broadcasted_iota(jnp.int32, (1, tk), 1)
    k_pos = jnp.where(idx < wb + tq, PAST_LEN - wb + idx, -1)

    def write_o(lanes, value):
        o_ref[0, :, lanes] = value

    _swa_block(q_ref[0], kz, vz, _swa_biases(q_pos, k_pos), sink_ref, write_o)


def _swa_sample(aq, ak, av, cache_k, cache_v, sinks):
    ns, tq, _ = aq.shape
    wb = cache_k.shape[1]
    new = lambda w: pl.BlockSpec((1, tq, w), lambda b: (b, 0, 0))
    old = pl.BlockSpec((1, wb, A_KW), lambda b: (b, 0, 0))
    return pl.pallas_call(
        _swa_sample_body,
        grid=(ns,),
        in_specs=[pl.BlockSpec(memory_space=pltpu.SMEM),
                  new(A_QW), new(A_KW), new(A_KW), old, old],
        out_specs=new(A_QW),
        out_shape=jax.ShapeDtypeStruct((ns, tq, A_QW), BF16),
        compiler_params=_params(("arbitrary",)),
        name="swa_sample",
    )(sinks, aq, ak, av, cache_k, cache_v)


def _out_ffn_tile(x, mixed, ga1, sc2, sh2, ga2, g2, wo_ref, wu_ref, wd_ref):
    s, r, d = x.shape
    mix = jnp.dot(mixed, wo_ref[...], preferred_element_type=F32)
    x1 = x + ga1 * mix.reshape(s, r, d)
    ms = jnp.mean(x1 * x1, axis=-1, keepdims=True)
    h2 = (x1 * lax.rsqrt(ms + EPS) * g2) * (1.0 + sc2) + sh2
    hb = h2.reshape(s * r, d).astype(BF16)
    ffn = jnp.zeros((s * r, d), F32)
    for c in range(D_FF // FF_BLOCK):
        u = jnp.dot(hb, wu_ref[:, c * FF_BLOCK:(c + 1) * FF_BLOCK], preferred_element_type=F32)
        u = jnp.square(jnp.maximum(u, 0.0)).astype(BF16)
        ffn = ffn + jnp.dot(u, wd_ref[c * FF_BLOCK:(c + 1) * FF_BLOCK, :],
                            preferred_element_type=F32)
    return x1 + ga2 * ffn.reshape(s, r, d)


def _out_ffn_body(x_ref, hm_ref, at_ref, ga1_ref, sc2_ref, sh2_ref, ga2_ref, g2_ref,
                  wo_ref, wu_ref, wd_ref, y_ref):
    s, r, _ = x_ref.shape
    mixed = jnp.concatenate([hm_ref[...].reshape(s * r, M_W), at_ref[...].reshape(s * r, A_QW)],
                            axis=1)
    y_ref[...] = _out_ffn_tile(
        x_ref[...], mixed, ga1_ref[...], sc2_ref[...], sh2_ref[...], ga2_ref[...], g2_ref[...],
        wo_ref, wu_ref, wd_ref)


def _ffn_weight_specs(d):
    return [_const_spec((1, d)), _const_spec((d, d)), _const_spec((d, D_FF)),
            _const_spec((D_FF, d))]


def _out_ffn(x, hm, at, ga1, sc2, sh2, ga2, ffn_w, *, seqs, rows):
    ns, length, d = x.shape
    tile = lambda w: pl.BlockSpec((seqs, rows, w), lambda i, j: (i, j, 0))
    mod = pl.BlockSpec((seqs, 1, d), lambda i, j: (i, 0, 0))
    return pl.pallas_call(
        _out_ffn_body,
        grid=(ns // seqs, length // rows),
        in_specs=[tile(d), tile(M_W), tile(A_QW), mod, mod, mod, mod] + _ffn_weight_specs(d),
        out_specs=tile(d),
        out_shape=jax.ShapeDtypeStruct((ns, length, d), F32),
        compiler_params=_params(("arbitrary", "arbitrary")),
        name="out_ffn",
    )(x, hm, at, ga1, sc2, sh2, ga2, *ffn_w)


def _layer_body(sink_ref, xp_ref, sc1_ref, sh1_ref, g1_ref, wa_ref, wb_ref, wg_ref, bg_ref,
                gq_ref, gk_ref, pq_ref, pk_ref, gout_ref,
                xf_ref, ga1_ref, sc2_ref, sh2_ref, ga2_ref, g2_ref, wo_ref, wu_ref, wd_ref,
                y_ref, kt_ref, vt_ref, c_out_ref, n_out_ref, m_out_ref,
                mq_s, mk_s, mv_s, og_s, gt_s, aq_s, ak_s, av_s, kprev_s, vprev_s,
                mix_s, caug_ref, m_ref, *, n_tiles, tiles_per_seq, chunk):
    s = pl.program_id(0)
    rows = xp_ref.shape[1]
    cur = s % 2
    prv = 1 - cur
    tile_p = jnp.minimum(s, n_tiles - 1)
    tile_a = jnp.clip(s - 1, 0, n_tiles - 1)
    ja = tile_a % tiles_per_seq
    a_valid = jnp.logical_and(s >= 1, s <= n_tiles)
    proj_s = {"mq": mq_s, "mk": mk_s, "mv": mv_s, "og": og_s, "gt": gt_s,
              "aq": aq_s, "ak": ak_s, "av": av_s}

    @pl.when(s == 0)
    def _():
        for ref in proj_s.values():
            ref[1] = jnp.zeros(ref.shape[1:], ref.dtype)
        kprev_s[...] = jnp.zeros(kprev_s.shape, kprev_s.dtype)
        vprev_s[...] = jnp.zeros(vprev_s.shape, vprev_s.dtype)
        mix_s[0] = jnp.zeros(mix_s.shape[1:], mix_s.dtype)

    @pl.when(jnp.logical_and(ja == 0, s <= n_tiles))
    def _():
        caug_ref[...] = jnp.zeros(caug_ref.shape, caug_ref.dtype)
        m_ref[...] = jnp.zeros(m_ref.shape, m_ref.dtype)

    y_ref[...] = _out_ffn_tile(
        xf_ref[...], mix_s[cur], ga1_ref[...], sc2_ref[...], sh2_ref[...], ga2_ref[...],
        g2_ref[...], wo_ref, wu_ref, wd_ref)

    def write_h(r, lanes, value):
        mix_s[prv, r, lanes] = value

    def write_o(r, lanes, value):
        mix_s[prv, r, slice(M_W + lanes.start, M_W + lanes.stop)] = value

    view = lambda ref: ref.at[pl.ds(prv, 1)]
    _mlstm_tile(view(mq_s), view(mk_s), view(mv_s), view(og_s), gt_s.at[prv], gout_ref,
                caug_ref, m_ref, write_h, chunk)
    k_cur = ak_s[prv]
    v_cur = av_s[prv]
    _swa_tile(sink_ref, view(aq_s),
              jnp.concatenate([kprev_s[...], k_cur], axis=0),
              jnp.concatenate([vprev_s[...], v_cur], axis=0), ja * rows, write_o)
    kprev_s[...] = k_cur[rows - SWA_Q_ROWS:]
    vprev_s[...] = v_cur[rows - SWA_Q_ROWS:]

    def store(name, value):
        proj_s[name][cur] = value

    _in_proj_tile(xp_ref[...], sc1_ref[...], sh1_ref[...], g1_ref[...], wa_ref, wb_ref, wg_ref,
                  bg_ref[...], gq_ref[...], gk_ref[...], pq_ref, pk_ref, store, chunk)

    @pl.when(jnp.logical_and(tile_p % tiles_per_seq == tiles_per_seq - 1, s < n_tiles))
    def _():
        kt_ref[0] = ak_s[cur, rows - SWA_Q_ROWS:, :]
        vt_ref[0] = av_s[cur, rows - SWA_Q_ROWS:, :]

    @pl.when(jnp.logical_and(ja == tiles_per_seq - 1, a_valid))
    def _():
        _mlstm_store_state(caug_ref, m_ref, c_out_ref, n_out_ref, m_out_ref)


def _layer(x, mod4, mod_row0, sinks, gout, proj_w, ffn_w, *, rows, chunk):
    ns, length, d = x.shape
    tps = length // rows
    n_tiles = ns * tps

    def stage(lag):
        def tile_idx(i):
            t = jnp.clip(i - lag, 0, n_tiles - 1)
            return t // tps, t % tps
        x_tile = pl.BlockSpec((1, rows, d), lambda i: (*tile_idx(i), 0))
        mod = lambda k: pl.BlockSpec((1, None, 1, d),
                                     lambda i: (mod_row0 + tile_idx(i)[0], k, 0, 0))
        per_seq = lambda shape: pl.BlockSpec(
            (1,) + shape, lambda i: (tile_idx(i)[0],) + (0,) * len(shape))
        return x_tile, mod, per_seq

    xp_tile, p_mod, p_seq = stage(0)
    _, _, a_seq = stage(1)
    xf_tile, f_mod, _ = stage(2)
    st_c, st_n, st_m = a_seq((M_HEADS, M_DIM, M_DIM)), a_seq((M_HEADS, M_DIM)), a_seq((M_HEADS, 1))
    tail = p_seq((SWA_Q_ROWS, A_KW))
    slots = lambda shape, dtype: pltpu.VMEM((2,) + shape, dtype)
    return pl.pallas_call(
        functools.partial(_layer_body, n_tiles=n_tiles, tiles_per_seq=tps, chunk=chunk),
        grid=(n_tiles + 2,),
        in_specs=([pl.BlockSpec(memory_space=pltpu.SMEM), xp_tile, p_mod(MOD_SC1), p_mod(MOD_SH1)]
                  + _proj_weight_specs(d)
                  + [_const_spec((M_HEADS, M_DIM)), xf_tile,
                     f_mod(MOD_GA1), f_mod(MOD_SC2), f_mod(MOD_SH2), f_mod(MOD_GA2)]
                  + _ffn_weight_specs(d)),
        out_specs=(xf_tile, tail, tail, st_c, st_n, st_m),
        out_shape=(jax.ShapeDtypeStruct((ns, length, d), F32),
                   jax.ShapeDtypeStruct((ns, SWA_Q_ROWS, A_KW), F32),
                   jax.ShapeDtypeStruct((ns, SWA_Q_ROWS, A_KW), F32),
                   jax.ShapeDtypeStruct((ns, M_HEADS, M_DIM, M_DIM), F32),
                   jax.ShapeDtypeStruct((ns, M_HEADS, M_DIM), F32),
                   jax.ShapeDtypeStruct((ns, M_HEADS, 1), F32)),
        scratch_shapes=[slots((rows, M_W), BF16), slots((rows, M_W), BF16),
                        slots((rows, M_W), BF16), slots((rows, M_W), BF16),
                        slots((GATE_ROWS, rows), F32), slots((rows, A_QW), BF16),
                        slots((rows, A_KW), F32), slots((rows, A_KW), F32),
                        pltpu.VMEM((SWA_Q_ROWS, A_KW), F32), pltpu.VMEM((SWA_Q_ROWS, A_KW), F32),
                        slots((rows, M_W + A_QW), BF16),
                        pltpu.VMEM((M_HEADS, M_DIM, 2 * M_DIM), F32),
                        pltpu.VMEM((M_HEADS, 1), F32)],
        compiler_params=_params(("arbitrary",)),
        name="layer",
    )(sinks, x, mod4, mod4, *proj_w, gout, x, mod4, mod4, mod4, mod4, *ffn_w)


class _LaneWindow:
    def __init__(self, ref, offset):
        self.ref, self.offset = ref, offset

    def __getitem__(self, idx):
        rows, cols = idx
        return self.ref[rows, slice(cols.start + self.offset, cols.stop + self.offset)]


def _sample_body(sink_ref, x_ref, sc1_ref, sh1_ref, g1_ref, wa_ref, wb_ref, wg_ref, bg_ref,
                 gq_ref, gk_ref, pq_ref, pk_ref, c0_ref, n0_ref, m0_ref, gout_ref,
                 ck_ref, cv_ref, ga1_ref, sc2_ref, sh2_ref, ga2_ref, g2_ref, wo_ref, wu_ref, wd_ref,
                 y_ref, ko_ref, vo_ref, c_out_ref, n_out_ref, m_out_ref,
                 mq_s, mk_s, mv_s, og_s, gt_s, aq_s, ak_s, av_s, mix_s, caug_s, m_s):
    ns, t, _ = x_ref.shape
    cached = ck_ref.shape[1]
    proj_s = {"mq": mq_s, "mk": mk_s, "mv": mv_s, "og": og_s, "gt": gt_s,
              "aq": aq_s, "ak": ak_s, "av": av_s}

    def store(name, value):
        proj_s[name][...] = value if name == "gt" else value.reshape(ns, t, value.shape[-1])

    _in_proj_tile(x_ref[...], sc1_ref[...], sh1_ref[...], g1_ref[...], wa_ref, wb_ref, wg_ref,
                  bg_ref[...], gq_ref[...], gk_ref[...], pq_ref, pk_ref, store, t)

    seq = lambda ref, b: ref.at[pl.ds(b, 1)]
    for b in range(ns):
        caug_b, m_b = caug_s.at[b], m_s.at[b]
        _mlstm_load_state(seq(c0_ref, b), seq(n0_ref, b), seq(m0_ref, b), caug_b, m_b)

        def write_h(rows, lanes, value, b=b):
            mix_s[b * t + rows.start:b * t + rows.stop, lanes] = value

        _mlstm_tile(seq(mq_s, b), seq(mk_s, b), seq(mv_s, b), seq(og_s, b),
                    _LaneWindow(gt_s, b * t), gout_ref, caug_b, m_b, write_h, t)
        _mlstm_store_state(caug_b, m_b, seq(c_out_ref, b), seq(n_out_ref, b), seq(m_out_ref, b))

    tk = 2 * LANES
    pad = jnp.zeros((tk - cached - t, A_KW), F32)
    q_pos = PAST_LEN + lax.broadcasted_iota(jnp.int32, (t, 1), 0)
    idx = lax.broadcasted_iota(jnp.int32, (1, tk), 1)
    k_pos = jnp.where(idx < cached + t, PAST_LEN - cached + idx, -1)
    biases = _swa_biases(q_pos, k_pos)
    for b in range(ns):
        k_new, v_new = ak_s[b], av_s[b]

        def write_o(lanes, value, b=b):
            mix_s[b * t:(b + 1) * t, M_W + lanes.start:M_W + lanes.stop] = value

        _swa_block(aq_s[b], _kv_halves(jnp.concatenate([ck_ref[b], k_new, pad], axis=0)),
                   _kv_halves(jnp.concatenate([cv_ref[b], v_new, pad], axis=0)),
                   biases, sink_ref, write_o)
        ko_ref[b] = jnp.concatenate([ck_ref[b, t:, :], k_new], axis=0)
        vo_ref[b] = jnp.concatenate([cv_ref[b, t:, :], v_new], axis=0)

    y_ref[...] = _out_ffn_tile(
        x_ref[...], mix_s[...], ga1_ref[...], sc2_ref[...], sh2_ref[...], ga2_ref[...],
        g2_ref[...], wo_ref, wu_ref, wd_ref)


def _sample_layer(x, mod4, sinks, c0, n0, m0, gout, cache_k, cache_v, proj_w, ffn_w):
    ns, t, d = x.shape
    cached = cache_k.shape[1]
    assert cached >= t
    whole = lambda shape: pl.BlockSpec(shape, lambda i: (0,) * len(shape))
    mod = lambda k: pl.BlockSpec((ns, None, 1, d), lambda i: (0, k, 0, 0))
    st_c, st_n, st_m = (whole((ns, M_HEADS, M_DIM, M_DIM)), whole((ns, M_HEADS, M_DIM)),
                        whole((ns, M_HEADS, 1)))
    kv = whole((ns, cached, A_KW))
    return pl.pallas_call(
        _sample_body,
        grid=(1,),
        in_specs=([pl.BlockSpec(memory_space=pltpu.SMEM), whole((ns, t, d)),
                   mod(MOD_SC1), mod(MOD_SH1)]
                  + _proj_weight_specs(d)
                  + [st_c, st_n, st_m, _const_spec((M_HEADS, M_DIM)), kv, kv,
                     mod(MOD_GA1), mod(MOD_SC2), mod(MOD_SH2), mod(MOD_GA2)]
                  + _ffn_weight_specs(d)),
        out_specs=(whole((ns, t, d)), kv, kv, st_c, st_n, st_m),
        out_shape=(jax.ShapeDtypeStruct((ns, t, d), F32),
                   jax.ShapeDtypeStruct((ns, cached, A_KW), F32),
                   jax.ShapeDtypeStruct((ns, cached, A_KW), F32),
                   jax.ShapeDtypeStruct((ns, M_HEADS, M_DIM, M_DIM), F32),
                   jax.ShapeDtypeStruct((ns, M_HEADS, M_DIM), F32),
                   jax.ShapeDtypeStruct((ns, M_HEADS, 1), F32)),
        scratch_shapes=[pltpu.VMEM((ns, t, M_W), BF16), pltpu.VMEM((ns, t, M_W), BF16),
                        pltpu.VMEM((ns, t, M_W), BF16), pltpu.VMEM((ns, t, M_W), BF16),
                        pltpu.VMEM((GATE_ROWS, ns * t), F32), pltpu.VMEM((ns, t, A_QW), BF16),
                        pltpu.VMEM((ns, t, A_KW), F32), pltpu.VMEM((ns, t, A_KW), F32),
                        pltpu.VMEM((ns * t, M_W + A_QW), BF16),
                        pltpu.VMEM((ns, M_HEADS, M_DIM, 2 * M_DIM), F32),
                        pltpu.VMEM((ns, M_HEADS, 1), F32)],
        compiler_params=_params(("arbitrary",)),
        name="sample_layer",
    )(sinks, x, mod4, mod4, *proj_w, c0, n0, m0, gout, cache_k, cache_v,
      mod4, mod4, mod4, mod4, *ffn_w)


def _block_ones(n):
    g = jnp.arange(n, dtype=jnp.int32) // A_DIM
    return (g[:, None] == g[None, :]).astype(BF16)


def kernel(x_prompt, x_sample, c_prompt, c_sample, cache_swa_k, cache_swa_v, state_mlstm_C,
           state_mlstm_n, state_mlstm_m, w_ada, b_ada, g_norm1, w_in, b_gates, g_q, g_k, sinks,
           g_mlstm_out, w_out, g_norm2, w_up, w_down):
    depth = w_ada.shape[0]
    assert depth == 1
    bp, lp, d = x_prompt.shape
    bs, ls, _ = x_sample.shape
    wb = cache_swa_k.shape[2]

    xp, xs = x_prompt, x_sample
    for l in range(depth):
        wt = w_in[l].T
        g_lo = 4 * M_W
        g_hi = g_lo + 2 * M_HEADS
        wg = jnp.concatenate([wt[g_lo:g_hi], jnp.zeros((BF16_ROWS - 2 * M_HEADS, d), wt.dtype)])
        gq_t = (jnp.tile(g_q[l], A_HEADS) * (LOG2E * A_DIM ** -0.5))[None, :]
        proj_w = (g_norm1[l][None, :], wt[:g_lo].astype(BF16), wt[g_hi:].astype(BF16),
                  wg.astype(BF16), b_gates[l][:, None], gq_t,
                  jnp.tile(g_k[l], A_KV_HEADS)[None, :], _block_ones(A_QW), _block_ones(A_KW))
        ffn_w = (g_norm2[l][None, :], w_out[l].astype(BF16), w_up[l].astype(BF16),
                 w_down[l].astype(BF16))

        mod_rows = -(-(bp + bs) // 8) * 8
        c_all = jnp.concatenate([c_sample, c_prompt, jnp.zeros((mod_rows - bp - bs, d), F32)],
                                axis=0)
        mod4 = _ada_mod(c_all, w_ada[l], b_ada[l][None, :]).reshape(mod_rows, 6, 1, d)

        xp, kt, vt, c, n, m = _layer(xp, mod4, bs, sinks[l], g_mlstm_out[l], proj_w, ffn_w,
                                     rows=ROW_TILE, chunk=MLSTM_CHUNK)
        win = WIN_CHUNKS * CHUNK
        st_p = (kt.reshape(bp, win, A_KV_HEADS, A_DIM), vt.reshape(bp, win, A_KV_HEADS, A_DIM),
                c, n, m[..., 0])

        xs, ko, vo, c, n, m = _sample_layer(
            xs, mod4, sinks[l], state_mlstm_C[l], state_mlstm_n[l], state_mlstm_m[l][..., None],
            g_mlstm_out[l], cache_swa_k[l].reshape(bs, wb, A_KW),
            cache_swa_v[l].reshape(bs, wb, A_KW), proj_w, ffn_w)
        st_s = (ko.reshape(bs, wb, A_KV_HEADS, A_DIM), vo.reshape(bs, wb, A_KV_HEADS, A_DIM),
                c, n, m[..., 0])

    return (xp, xs) + tuple(s[None] for s in st_p) + tuple(s[None] for s in st_s)
```

```python
import functools

import jax
import jax.numpy as jnp
from jax import lax
from jax.experimental import pallas as pl
from jax.experimental.pallas import tpu as pltpu

F32 = jnp.float32
BF16 = jnp.bfloat16

D_MODEL = 1024
CHUNK = 64
M_HEADS = 4
M_DIM = 128
M_W = M_HEADS * M_DIM
A_HEADS = 8
A_KV_HEADS = 2
A_DIM = 64
A_QW = A_HEADS * A_DIM
A_KW = A_KV_HEADS * A_DIM
WIN_CHUNKS = 2
D_FF = 4 * D_MODEL
EPS = 1e-6
PAST_LEN = 4096

LANES = 128
BF16_ROWS = 16
VMEM_LIMIT = 62 * 1024 * 1024
NEG_INF = float("-inf")
LOG2E = 1.4426950408889634
ROW_TILE = 512
MLSTM_CHUNK = 256
FF_BLOCK = 1024
NT_DIMS = (((1,), (1,)), ((), ()))
MOD_SH1, MOD_SC1, MOD_GA1, MOD_SH2, MOD_SC2, MOD_GA2 = range(6)


def _params(sem):
    return pltpu.CompilerParams(dimension_semantics=sem, vmem_limit_bytes=VMEM_LIMIT)


def _const_spec(shape):
    nd = len(shape)
    return pl.BlockSpec(shape, lambda *_: (0,) * nd, pipeline_mode=pl.Buffered(1))


def _ada_body(c_ref, w_ref, b_ref, o_ref):
    c = c_ref[...]
    s = c * jax.nn.sigmoid(c)
    o_ref[...] = jnp.dot(s.astype(BF16), w_ref[...].astype(BF16),
                         preferred_element_type=F32) + b_ref[...]


def _ada_mod(c_all, w_ada, b_ada):
    rows = c_all.shape[0]
    return pl.pallas_call(
        _ada_body,
        grid=(6,),
        in_specs=[pl.BlockSpec((rows, D_MODEL), lambda j: (0, 0)),
                  pl.BlockSpec((D_MODEL, D_MODEL), lambda j: (0, j)),
                  pl.BlockSpec((1, D_MODEL), lambda j: (0, j))],
        out_specs=pl.BlockSpec((rows, D_MODEL), lambda j: (0, j)),
        out_shape=jax.ShapeDtypeStruct((rows, 6 * D_MODEL), F32),
        compiler_params=_params(("arbitrary",)),
        name="ada_mod",
    )(c_all, w_ada, b_ada)


def _segment_prefix(x, op, identity, seg):
    assert x.shape[1] % LANES == 0 and x.shape[1] % seg == 0
    pos = lax.broadcasted_iota(jnp.int32, x.shape, 1) % seg
    d = 1
    while d < seg:
        x = op(x, jnp.where(pos >= d, pltpu.roll(x, d, axis=1), identity))
        d *= 2
    return x


GATE_ROWS = 24
PROJ_OUTPUTS = ("mq", "mk", "mv", "og", "gt", "aq", "ak", "av")


def _in_proj_tile(x, sc, sh, g1, wa_ref, wb_ref, wg_ref, bg, gq, gk, pq_ref, pk_ref, store,
                  gate_chunk):
    s, r, d = x.shape
    m = s * r
    ms = jnp.mean(x * x, axis=-1, keepdims=True)
    y = x * lax.rsqrt(ms + EPS) * g1
    h = y * (1.0 + sc) + sh
    hb = h.reshape(m, d).astype(BF16)

    def proj(w_ref, lo, width):
        return lax.dot_general(hb, w_ref[lo:lo + width, :], NT_DIMS, preferred_element_type=F32)

    zg = lax.dot_general(wg_ref[...], hb, NT_DIMS,
                         preferred_element_type=F32)[:2 * M_HEADS] + bg
    row = lax.broadcasted_iota(jnp.int32, zg.shape, 0)
    log_sig = jnp.minimum(zg, 0.0) - jnp.log1p(jnp.exp(-jnp.abs(zg)))
    gt = jnp.where(row < M_HEADS, zg, log_sig)
    b8 = _segment_prefix(gt, jnp.add, 0.0, gate_chunk)
    c8 = gt - pltpu.roll(b8, M_HEADS, axis=0)
    cm8 = _segment_prefix(c8, jnp.maximum, NEG_INF, gate_chunk)
    store("gt", jnp.concatenate([c8, b8, cm8], axis=0))

    store("mq", proj(wa_ref, 0, M_W).astype(BF16))
    store("mk", (proj(wa_ref, M_W, M_W) * (M_DIM ** -0.5)).astype(BF16))
    store("mv", proj(wa_ref, 2 * M_W, M_W).astype(BF16))
    store("og", jax.nn.sigmoid(proj(wa_ref, 3 * M_W, M_W)).astype(BF16))

    aq = proj(wb_ref, 0, A_QW)
    ss_q = jnp.dot((aq * aq).astype(BF16), pq_ref[...], preferred_element_type=F32)
    store("aq", (aq * lax.rsqrt(ss_q * (1.0 / A_DIM) + EPS) * gq).astype(BF16))
    ak = proj(wb_ref, A_QW, A_KW)
    sq = ak * ak
    hi = sq.astype(BF16)
    lo = (sq - hi.astype(F32)).astype(BF16)
    ss_k = (jnp.dot(hi, pk_ref[...], preferred_element_type=F32)
            + jnp.dot(lo, pk_ref[...], preferred_element_type=F32))
    store("ak", ak * lax.rsqrt(ss_k * (1.0 / A_DIM) + EPS) * gk)
    store("av", proj(wb_ref, A_QW + A_KW, A_KW))


def _in_proj_body(x_ref, sc_ref, sh_ref, g1_ref, wa_ref, wb_ref, wg_ref, bg_ref, gq_ref, gk_ref,
                  pq_ref, pk_ref, *out_refs, gate_chunk):
    s, r, _ = x_ref.shape
    outs = dict(zip(PROJ_OUTPUTS, out_refs))

    def store(name, value):
        ref = outs[name]
        ref[...] = value if name == "gt" else value.reshape(s, r, value.shape[-1])

    _in_proj_tile(x_ref[...], sc_ref[...], sh_ref[...], g1_ref[...], wa_ref, wb_ref, wg_ref,
                  bg_ref[...], gq_ref[...], gk_ref[...], pq_ref, pk_ref, store, gate_chunk)


def _proj_weight_specs(d):
    return [_const_spec((1, d)), _const_spec((4 * M_W, d)), _const_spec((A_QW + 2 * A_KW, d)),
            _const_spec((BF16_ROWS, d)), _const_spec((2 * M_HEADS, 1)),
            _const_spec((1, A_QW)), _const_spec((1, A_KW)),
            _const_spec((A_QW, A_QW)), _const_spec((A_KW, A_KW))]


def _in_proj(x, sc, sh, proj_w, *, seqs, rows, gate_chunk):
    ns, length, d = x.shape
    grid = (ns // seqs, length // rows)
    m = seqs * rows
    n_tok = ns * length

    def tile(width):
        return pl.BlockSpec((seqs, rows, width), lambda i, j: (i, j, 0))

    mod = pl.BlockSpec((seqs, 1, d), lambda i, j: (i, 0, 0))
    steps_per_seq = length // rows
    out_shapes = (
        jax.ShapeDtypeStruct((ns, length, M_W), BF16),
        jax.ShapeDtypeStruct((ns, length, M_W), BF16),
        jax.ShapeDtypeStruct((ns, length, M_W), BF16),
        jax.ShapeDtypeStruct((ns, length, M_W), BF16),
        jax.ShapeDtypeStruct((GATE_ROWS, n_tok), F32),
        jax.ShapeDtypeStruct((ns, length, A_QW), BF16),
        jax.ShapeDtypeStruct((ns, length, A_KW), F32),
        jax.ShapeDtypeStruct((ns, length, A_KW), F32),
    )
    return pl.pallas_call(
        functools.partial(_in_proj_body, gate_chunk=gate_chunk),
        grid=grid,
        in_specs=[tile(d), mod, mod] + _proj_weight_specs(d),
        out_specs=(tile(M_W), tile(M_W), tile(M_W), tile(M_W),
                   pl.BlockSpec((GATE_ROWS, m), lambda i, j: (0, i * steps_per_seq + j)),
                   tile(A_QW), tile(A_KW), tile(A_KW)),
        out_shape=out_shapes,
        compiler_params=_params(("arbitrary", "arbitrary")),
        name="in_proj",
    )(x, sc, sh, *proj_w)


def _col_bcast(r):
    t = r.shape[1]
    if t % LANES == 0:
        return jnp.broadcast_to(r, (LANES, t)).T
    src = lax.broadcasted_iota(jnp.int32, (t, t), 0)
    dst = lax.broadcasted_iota(jnp.int32, (t, t), 1)
    col = jnp.sum(jnp.where(src == dst, jnp.broadcast_to(r, (t, t)), 0.0), axis=1, keepdims=True)
    return jnp.broadcast_to(col, (t, LANES))


def _eye_dim():
    return (lax.broadcasted_iota(jnp.int32, (M_DIM, M_DIM), 0)
            == lax.broadcasted_iota(jnp.int32, (M_DIM, M_DIM), 1))


def _mlstm_load_state(c0_ref, n0_ref, m0_ref, caug_ref, m_ref):
    for hd in range(M_HEADS):
        caug_ref[hd, :, :M_DIM] = c0_ref[0, hd]
        n_row = jnp.broadcast_to(n0_ref[0, hd:hd + 1, :], (M_DIM, M_DIM))
        n_col = jnp.sum(jnp.where(_eye_dim(), n_row, 0.0), axis=1, keepdims=True)
        caug_ref[hd, :, M_DIM:] = jnp.broadcast_to(n_col, (M_DIM, M_DIM))
    m_ref[...] = m0_ref[0]


def _mlstm_store_state(caug_ref, m_ref, c_out_ref, n_out_ref, m_out_ref):
    for hd in range(M_HEADS):
        c_out_ref[0, hd] = caug_ref[hd, :, :M_DIM]
        n_out_ref[0, hd:hd + 1, :] = jnp.sum(
            jnp.where(_eye_dim(), caug_ref[hd, :, M_DIM:], 0.0), axis=0, keepdims=True)
    m_out_ref[0] = m_ref[...]


def _mlstm_tile(q_ref, k_ref, v_ref, og_ref, gt_ref, gout_ref, caug_ref, m_ref, write_h, chunk):
    t = chunk
    n_chunks = q_ref.shape[1] // t
    bs = LANES if t % LANES == 0 else t
    tril = (lax.broadcasted_iota(jnp.int32, (bs, bs), 1)
            <= lax.broadcasted_iota(jnp.int32, (bs, bs), 0))
    m_prev = m_ref[...]
    for j in range(n_chunks):
        rows = slice(j * t, (j + 1) * t)
        c = gt_ref[0:M_HEADS, rows]
        b = gt_ref[12:12 + M_HEADS, rows]
        g = jnp.maximum(gt_ref[16:16 + M_HEADS, rows], m_prev)
        g_last = g[:, t - 1:t]
        c2 = c * LOG2E
        g2 = g * LOG2E
        e_all = jnp.exp2(-LOG2E * (b + g))
        w_src_all = jnp.exp2(c2 - LOG2E * g_last)
        w_inter_all = jnp.exp2(LOG2E * m_prev - g2)
        w_old_all = jnp.exp2(LOG2E * (m_prev - g_last))
        m_prev = b[:, t - 1:t] + g_last

        for hd in range(M_HEADS):
            sl = slice(hd * M_DIM, (hd + 1) * M_DIM)
            hr = slice(hd, hd + 1)
            g_b = _col_bcast(g2[hr])
            e_b = _col_bcast(e_all[hr])
            w_src = _col_bcast(w_src_all[hr])
            w_inter = _col_bcast(w_inter_all[hr])

            q = q_ref[0, rows, sl]
            k = k_ref[0, rows, sl]
            v = v_ref[0, rows, sl]
            caug = caug_ref[hd]
            inter = jnp.dot(q, caug.astype(BF16), preferred_element_type=F32)
            gain = gout_ref[hr, :]
            for rb in range(t // bs):
                rsl = slice(rb * bs, (rb + 1) * bs)
                ncols = (rb + 1) * bs
                s_mat = lax.dot_general(q[rsl], k[:ncols], NT_DIMS,
                                        preferred_element_type=F32)
                decay = []
                for cb in range(rb + 1):
                    arg = c2[hr, cb * bs:(cb + 1) * bs] - g_b[rsl, :bs]
                    decay.append(jnp.exp2(jnp.where(tril, arg, NEG_INF) if cb == rb else arg))
                s_mat = s_mat * (decay[0] if rb == 0 else jnp.concatenate(decay, axis=1))
                num = (jnp.dot(s_mat.astype(BF16), v[:ncols], preferred_element_type=F32)
                       + w_inter[rsl] * inter[rsl, :M_DIM])
                den = jnp.sum(s_mat, axis=1, keepdims=True) + w_inter[rsl] * inter[rsl, M_DIM:]
                hh = num / jnp.maximum(jnp.abs(den), e_b[rsl])
                hn = hh * lax.rsqrt(jnp.mean(hh * hh, axis=-1, keepdims=True) + EPS) * gain
                out_rows = slice(j * t + rb * bs, j * t + (rb + 1) * bs)
                write_h(out_rows, sl, (hn * og_ref[0, out_rows, sl].astype(F32)).astype(BF16))

            rhs = jnp.concatenate([v.astype(F32) * w_src, w_src], axis=1).astype(BF16)
            upd = lax.dot_general(k, rhs, (((0,), (0,)), ((), ())), preferred_element_type=F32)
            caug_ref[hd] = w_old_all[hr] * caug + upd
    m_ref[...] = m_prev


def _mlstm_body(q_ref, k_ref, v_ref, og_ref, gt_ref, c0_ref, n0_ref, m0_ref, gout_ref,
                h_ref, c_out_ref, n_out_ref, m_out_ref, caug_ref, m_ref, *, chunk):
    ci = pl.program_id(1)

    @pl.when(ci == 0)
    def _():
        _mlstm_load_state(c0_ref, n0_ref, m0_ref, caug_ref, m_ref)

    def write_h(rows, lanes, value):
        h_ref[0, rows, lanes] = value

    _mlstm_tile(q_ref, k_ref, v_ref, og_ref, gt_ref, gout_ref, caug_ref, m_ref, write_h, chunk)

    @pl.when(ci == pl.num_programs(1) - 1)
    def _():
        _mlstm_store_state(caug_ref, m_ref, c_out_ref, n_out_ref, m_out_ref)


def _mlstm(mq, mk, mv, og, gt, c0, n0, m0, gout, *, chunk):
    ns, length, _ = mq.shape
    assert length == chunk
    tile = pl.BlockSpec((1, chunk, M_W), lambda b, c: (b, c, 0))
    gt_spec = pl.BlockSpec((None, GATE_ROWS, chunk), lambda b, c: (b, 0, 0))
    st_c = pl.BlockSpec((1, M_HEADS, M_DIM, M_DIM), lambda b, c: (b, 0, 0, 0))
    st_n = pl.BlockSpec((1, M_HEADS, M_DIM), lambda b, c: (b, 0, 0))
    st_m = pl.BlockSpec((1, M_HEADS, 1), lambda b, c: (b, 0, 0))
    return pl.pallas_call(
        functools.partial(_mlstm_body, chunk=chunk),
        grid=(ns, 1),
        in_specs=[tile, tile, tile, tile, gt_spec, st_c, st_n, st_m,
                  _const_spec((M_HEADS, M_DIM))],
        out_specs=(tile, st_c, st_n, st_m),
        out_shape=(jax.ShapeDtypeStruct((ns, length, M_W), BF16),
                   jax.ShapeDtypeStruct((ns, M_HEADS, M_DIM, M_DIM), F32),
                   jax.ShapeDtypeStruct((ns, M_HEADS, M_DIM), F32),
                   jax.ShapeDtypeStruct((ns, M_HEADS, 1), F32)),
        scratch_shapes=[pltpu.VMEM((M_HEADS, M_DIM, 2 * M_DIM), F32),
                        pltpu.VMEM((M_HEADS, 1), F32)],
        compiler_params=_params(("arbitrary", "arbitrary")),
        name="mlstm",
    )(mq, mk, mv, og, gt, c0, n0, m0, gout)


def _kv_halves(a):
    low = lax.broadcasted_iota(jnp.int32, a.shape, 1) < A_DIM
    lo_half = jnp.where(low, a, 0.0)
    hi_half = jnp.where(low, 0.0, a)
    pairs = [(lo_half, pltpu.roll(lo_half, A_DIM, axis=1)),
             (pltpu.roll(hi_half, A_DIM, axis=1), hi_half)]
    return [tuple(h.astype(BF16) for h in pair) for pair in pairs]


def _swa_biases(q_pos, k_pos):
    shift = CHUNK.bit_length() - 1
    assert 1 << shift == CHUNK
    qc = lax.shift_right_arithmetic(q_pos, shift)
    kc = lax.shift_right_arithmetic(k_pos, shift)
    allowed = (kc <= qc) & (kc >= qc - WIN_CHUNKS) & (k_pos >= 0)
    dist = jnp.abs(q_pos - k_pos).astype(F32)
    slopes = [2.0 ** (-8.0 * (head + 1) / A_HEADS) for head in range(A_HEADS)]
    return [jnp.where(allowed, (-LOG2E * slope) * dist, NEG_INF) for slope in slopes]


def _swa_block(q, kz, vz, biases, sink_ref, write_o, valid=None):
    tq = q.shape[0]
    pairs_per_kv = A_HEADS // A_KV_HEADS // 2
    for j in range(A_KV_HEADS):
        pairs = [j * pairs_per_kv + i for i in range(pairs_per_kv)]
        q_stack = jnp.concatenate([q[:, p * LANES:(p + 1) * LANES] for p in pairs], axis=0)
        acc = [None] * pairs_per_kv
        for e in range(2):
            logits_all = lax.dot_general(q_stack, kz[j][e], NT_DIMS, preferred_element_type=F32)
            probs, denoms = [], []
            for i, p in enumerate(pairs):
                head = 2 * p + e
                sink = sink_ref[head] * LOG2E
                logits = logits_all[i * tq:(i + 1) * tq] + biases[head]
                if valid is not None:
                    logits = jnp.where(valid, logits, NEG_INF)
                mx = jnp.maximum(jnp.max(logits, axis=1, keepdims=True), sink)
                pr = jnp.exp2(logits - mx)
                denoms.append(jnp.sum(pr, axis=1, keepdims=True) + jnp.exp2(sink - mx))
                probs.append(pr.astype(BF16))
            out_all = jnp.dot(jnp.concatenate(probs, axis=0), vz[j][e],
                              preferred_element_type=F32)
            for i in range(pairs_per_kv):
                part = out_all[i * tq:(i + 1) * tq] / denoms[i]
                acc[i] = part if acc[i] is None else acc[i] + part
        for i, p in enumerate(pairs):
            write_o(slice(p * LANES, (p + 1) * LANES), acc[i].astype(BF16))


SWA_Q_ROWS = WIN_CHUNKS * CHUNK


def _swa_tile(sink_ref, q_ref, k_all, v_all, tile_start, write_o):
    tq = SWA_Q_ROWS
    kz = _kv_halves(k_all)
    vz = _kv_halves(v_all)
    rel_k = lax.broadcasted_iota(jnp.int32, (1, 2 * tq), 1)
    biases = _swa_biases(tq + lax.broadcasted_iota(jnp.int32, (tq, 1), 0), rel_k)
    for r in range(q_ref.shape[1] // tq):
        rows = slice(r * tq, (r + 1) * tq)
        band = slice(r * tq, (r + 2) * tq)
        valid = (tile_start - tq + rel_k >= 0) if r == 0 else None
        _swa_block(q_ref[0, rows, :],
                   [tuple(h[band] for h in pair) for pair in kz],
                   [tuple(h[band] for h in pair) for pair in vz],
                   biases, sink_ref,
                   lambda lanes, value, rows=rows: write_o(rows, lanes, value), valid=valid)


def _swa_sample_body(sink_ref, q_ref, kn_ref, vn_ref, ck_ref, cv_ref, o_ref):
    tq = q_ref.shape[1]
    wb = ck_ref.shape[1]
    tk = 2 * LANES
    pad = jnp.zeros((tk - wb - tq, A_KW), F32)
    kz = _kv_halves(jnp.concatenate([ck_ref[0], kn_ref[0], pad], axis=0))
    vz = _kv_halves(jnp.concatenate([cv_ref[0], vn_ref[0], pad], axis=0))
    q_pos = PAST_LEN + lax.broadcasted_iota(jnp.int32, (tq, 1), 0)
    idx = lax.broadcasted_iota(jnp.int32, (1, tk), 1)
    k_pos = jnp.where(idx < wb + tq, PAST_LEN - wb + idx, -1)

    def write_o(lanes, value):
        o_ref[0, :, lanes] = value

    _swa_block(q_ref[0], kz, vz, _swa_biases(q_pos, k_pos), sink_ref, write_o)


def _swa_sample(aq, ak, av, cache_k, cache_v, sinks):
    ns, tq, _ = aq.shape
    wb = cache_k.shape[1]
    new = lambda w: pl.BlockSpec((1, tq, w), lambda b: (b, 0, 0))
    old = pl.BlockSpec((1, wb, A_KW), lambda b: (b, 0, 0))
    return pl.pallas_call(
        _swa_sample_body,
        grid=(ns,),
        in_specs=[pl.BlockSpec(memory_space=pltpu.SMEM),
                  new(A_QW), new(A_KW), new(A_KW), old, old],
        out_specs=new(A_QW),
        out_shape=jax.ShapeDtypeStruct((ns, tq, A_QW), BF16),
        compiler_params=_params(("arbitrary",)),
        name="swa_sample",
    )(sinks, aq, ak, av, cache_k, cache_v)


def _out_ffn_tile(x, mixed, ga1, sc2, sh2, ga2, g2, wo_ref, wu_ref, wd_ref):
    s, r, d = x.shape
    mix = jnp.dot(mixed, wo_ref[...], preferred_element_type=F32)
    x1 = x + ga1 * mix.reshape(s, r, d)
    ms = jnp.mean(x1 * x1, axis=-1, keepdims=True)
    h2 = (x1 * lax.rsqrt(ms + EPS) * g2) * (1.0 + sc2) + sh2
    hb = h2.reshape(s * r, d).astype(BF16)
    ffn = jnp.zeros((s * r, d), F32)
    for c in range(D_FF // FF_BLOCK):
        u = jnp.dot(hb, wu_ref[:, c * FF_BLOCK:(c + 1) * FF_BLOCK], preferred_element_type=F32)
        u = jnp.square(jnp.maximum(u, 0.0)).astype(BF16)
        ffn = ffn + jnp.dot(u, wd_ref[c * FF_BLOCK:(c + 1) * FF_BLOCK, :],
                            preferred_element_type=F32)
    return x1 + ga2 * ffn.reshape(s, r, d)


def _out_ffn_body(x_ref, hm_ref, at_ref, ga1_ref, sc2_ref, sh2_ref, ga2_ref, g2_ref,
                  wo_ref, wu_ref, wd_ref, y_ref):
    s, r, _ = x_ref.shape
    mixed = jnp.concatenate([hm_ref[...].reshape(s * r, M_W), at_ref[...].reshape(s * r, A_QW)],
                            axis=1)
    y_ref[...] = _out_ffn_tile(
        x_ref[...], mixed, ga1_ref[...], sc2_ref[...], sh2_ref[...], ga2_ref[...], g2_ref[...],
        wo_ref, wu_ref, wd_ref)


def _ffn_weight_specs(d):
    return [_const_spec((1, d)), _const_spec((d, d)), _const_spec((d, D_FF)),
            _const_spec((D_FF, d))]


def _out_ffn(x, hm, at, ga1, sc2, sh2, ga2, ffn_w, *, seqs, rows):
    ns, length, d = x.shape
    tile = lambda w: pl.BlockSpec((seqs, rows, w), lambda i, j: (i, j, 0))
    mod = pl.BlockSpec((seqs, 1, d), lambda i, j: (i, 0, 0))
    return pl.pallas_call(
        _out_ffn_body,
        grid=(ns // seqs, length // rows),
        in_specs=[tile(d), tile(M_W), tile(A_QW), mod, mod, mod, mod] + _ffn_weight_specs(d),
        out_specs=tile(d),
        out_shape=jax.ShapeDtypeStruct((ns, length, d), F32),
        compiler_params=_params(("arbitrary", "arbitrary")),
        name="out_ffn",
    )(x, hm, at, ga1, sc2, sh2, ga2, *ffn_w)


def _layer_body(sink_ref, xp_ref, sc1_ref, sh1_ref, g1_ref, wa_ref, wb_ref, wg_ref, bg_ref,
                gq_ref, gk_ref, pq_ref, pk_ref, gout_ref,
                xf_ref, ga1_ref, sc2_ref, sh2_ref, ga2_ref, g2_ref, wo_ref, wu_ref, wd_ref,
                y_ref, kt_ref, vt_ref, c_out_ref, n_out_ref, m_out_ref,
                mq_s, mk_s, mv_s, og_s, gt_s, aq_s, ak_s, av_s, kprev_s, vprev_s,
                mix_s, caug_ref, m_ref, *, n_tiles, tiles_per_seq, chunk):
    s = pl.program_id(0)
    rows = xp_ref.shape[1]
    cur = s % 2
    prv = 1 - cur
    tile_p = jnp.minimum(s, n_tiles - 1)
    tile_a = jnp.clip(s - 1, 0, n_tiles - 1)
    ja = tile_a % tiles_per_seq
    a_valid = jnp.logical_and(s >= 1, s <= n_tiles)
    proj_s = {"mq": mq_s, "mk": mk_s, "mv": mv_s, "og": og_s, "gt": gt_s,
              "aq": aq_s, "ak": ak_s, "av": av_s}

    @pl.when(s == 0)
    def _():
        for ref in proj_s.values():
            ref[1] = jnp.zeros(ref.shape[1:], ref.dtype)
        kprev_s[...] = jnp.zeros(kprev_s.shape, kprev_s.dtype)
        vprev_s[...] = jnp.zeros(vprev_s.shape, vprev_s.dtype)
        mix_s[0] = jnp.zeros(mix_s.shape[1:], mix_s.dtype)

    @pl.when(jnp.logical_and(ja == 0, s <= n_tiles))
    def _():
        caug_ref[...] = jnp.zeros(caug_ref.shape, caug_ref.dtype)
        m_ref[...] = jnp.zeros(m_ref.shape, m_ref.dtype)

    def stages(cur, prv):
        y_ref[...] = _out_ffn_tile(
            xf_ref[...], mix_s[cur], ga1_ref[...], sc2_ref[...], sh2_ref[...], ga2_ref[...],
            g2_ref[...], wo_ref, wu_ref, wd_ref)

        def write_h(r, lanes, value):
            mix_s[prv, r, lanes] = value

        def write_o(r, lanes, value):
            mix_s[prv, r, slice(M_W + lanes.start, M_W + lanes.stop)] = value

        view = lambda ref: ref.at[pl.ds(prv, 1)]
        _mlstm_tile(view(mq_s), view(mk_s), view(mv_s), view(og_s), gt_s.at[prv], gout_ref,
                    caug_ref, m_ref, write_h, chunk)
        k_cur = ak_s[prv]
        v_cur = av_s[prv]
        _swa_tile(sink_ref, view(aq_s),
                  jnp.concatenate([kprev_s[...], k_cur], axis=0),
                  jnp.concatenate([vprev_s[...], v_cur], axis=0), ja * rows, write_o)
        kprev_s[...] = k_cur[rows - SWA_Q_ROWS:]
        vprev_s[...] = v_cur[rows - SWA_Q_ROWS:]

        def store(name, value):
            proj_s[name][cur] = value

        _in_proj_tile(xp_ref[...], sc1_ref[...], sh1_ref[...], g1_ref[...], wa_ref, wb_ref,
                      wg_ref, bg_ref[...], gq_ref[...], gk_ref[...], pq_ref, pk_ref, store, chunk)

    for parity in range(2):
        pl.when(cur == parity)(functools.partial(stages, parity, 1 - parity))

    @pl.when(jnp.logical_and(tile_p % tiles_per_seq == tiles_per_seq - 1, s < n_tiles))
    def _():
        kt_ref[0] = ak_s[cur, rows - SWA_Q_ROWS:, :]
        vt_ref[0] = av_s[cur, rows - SWA_Q_ROWS:, :]

    @pl.when(jnp.logical_and(ja == tiles_per_seq - 1, a_valid))
    def _():
        _mlstm_store_state(caug_ref, m_ref, c_out_ref, n_out_ref, m_out_ref)


def _layer(x, mod4, mod_row0, sinks, gout, proj_w, ffn_w, *, rows, chunk):
    ns, length, d = x.shape
    tps = length // rows
    n_tiles = ns * tps

    def stage(lag):
        def tile_idx(i):
            t = jnp.clip(i - lag, 0, n_tiles - 1)
            return t // tps, t % tps
        x_tile = pl.BlockSpec((1, rows, d), lambda i: (*tile_idx(i), 0))
        mod = lambda k: pl.BlockSpec((1, None, 1, d),
                                     lambda i: (mod_row0 + tile_idx(i)[0], k, 0, 0))
        per_seq = lambda shape: pl.BlockSpec(
            (1,) + shape, lambda i: (tile_idx(i)[0],) + (0,) * len(shape))
        return x_tile, mod, per_seq

    xp_tile, p_mod, p_seq = stage(0)
    _, _, a_seq = stage(1)
    xf_tile, f_mod, _ = stage(2)
    st_c, st_n, st_m = a_seq((M_HEADS, M_DIM, M_DIM)), a_seq((M_HEADS, M_DIM)), a_seq((M_HEADS, 1))
    tail = p_seq((SWA_Q_ROWS, A_KW))
    slots = lambda shape, dtype: pltpu.VMEM((2,) + shape, dtype)
    return pl.pallas_call(
        functools.partial(_layer_body, n_tiles=n_tiles, tiles_per_seq=tps, chunk=chunk),
        grid=(n_tiles + 2,),
        in_specs=([pl.BlockSpec(memory_space=pltpu.SMEM), xp_tile, p_mod(MOD_SC1), p_mod(MOD_SH1)]
                  + _proj_weight_specs(d)
                  + [_const_spec((M_HEADS, M_DIM)), xf_tile,
                     f_mod(MOD_GA1), f_mod(MOD_SC2), f_mod(MOD_SH2), f_mod(MOD_GA2)]
                  + _ffn_weight_specs(d)),
        out_specs=(xf_tile, tail, tail, st_c, st_n, st_m),
        out_shape=(jax.ShapeDtypeStruct((ns, length, d), F32),
                   jax.ShapeDtypeStruct((ns, SWA_Q_ROWS, A_KW), F32),
                   jax.ShapeDtypeStruct((ns, SWA_Q_ROWS, A_KW), F32),
                   jax.ShapeDtypeStruct((ns, M_HEADS, M_DIM, M_DIM), F32),
                   jax.ShapeDtypeStruct((ns, M_HEADS, M_DIM), F32),
                   jax.ShapeDtypeStruct((ns, M_HEADS, 1), F32)),
        scratch_shapes=[slots((rows, M_W), BF16), slots((rows, M_W), BF16),
                        slots((rows, M_W), BF16), slots((rows, M_W), BF16),
                        slots((GATE_ROWS, rows), F32), slots((rows, A_QW), BF16),
                        slots((rows, A_KW), F32), slots((rows, A_KW), F32),
                        pltpu.VMEM((SWA_Q_ROWS, A_KW), F32), pltpu.VMEM((SWA_Q_ROWS, A_KW), F32),
                        slots((rows, M_W + A_QW), BF16),
                        pltpu.VMEM((M_HEADS, M_DIM, 2 * M_DIM), F32),
                        pltpu.VMEM((M_HEADS, 1), F32)],
        compiler_params=_params(("arbitrary",)),
        name="layer",
    )(sinks, x, mod4, mod4, *proj_w, gout, x, mod4, mod4, mod4, mod4, *ffn_w)


class _LaneWindow:
    def __init__(self, ref, offset):
        self.ref, self.offset = ref, offset

    def __getitem__(self, idx):
        rows, cols = idx
        return self.ref[rows, slice(cols.start + self.offset, cols.stop + self.offset)]


def _sample_body(sink_ref, x_ref, sc1_ref, sh1_ref, g1_ref, wa_ref, wb_ref, wg_ref, bg_ref,
                 gq_ref, gk_ref, pq_ref, pk_ref, c0_ref, n0_ref, m0_ref, gout_ref,
                 ck_ref, cv_ref, ga1_ref, sc2_ref, sh2_ref, ga2_ref, g2_ref, wo_ref, wu_ref, wd_ref,
                 y_ref, ko_ref, vo_ref, c_out_ref, n_out_ref, m_out_ref,
                 mq_s, mk_s, mv_s, og_s, gt_s, aq_s, ak_s, av_s, mix_s, caug_s, m_s):
    ns, t, _ = x_ref.shape
    cached = ck_ref.shape[1]
    proj_s = {"mq": mq_s, "mk": mk_s, "mv": mv_s, "og": og_s, "gt": gt_s,
              "aq": aq_s, "ak": ak_s, "av": av_s}

    def store(name, value):
        proj_s[name][...] = value if name == "gt" else value.reshape(ns, t, value.shape[-1])

    _in_proj_tile(x_ref[...], sc1_ref[...], sh1_ref[...], g1_ref[...], wa_ref, wb_ref, wg_ref,
                  bg_ref[...], gq_ref[...], gk_ref[...], pq_ref, pk_ref, store, t)

    seq = lambda ref, b: ref.at[pl.ds(b, 1)]
    for b in range(ns):
        caug_b, m_b = caug_s.at[b], m_s.at[b]
        _mlstm_load_state(seq(c0_ref, b), seq(n0_ref, b), seq(m0_ref, b), caug_b, m_b)

        def write_h(rows, lanes, value, b=b):
            mix_s[b * t + rows.start:b * t + rows.stop, lanes] = value

        _mlstm_tile(seq(mq_s, b), seq(mk_s, b), seq(mv_s, b), seq(og_s, b),
                    _LaneWindow(gt_s, b * t), gout_ref, caug_b, m_b, write_h, t)
        _mlstm_store_state(caug_b, m_b, seq(c_out_ref, b), seq(n_out_ref, b), seq(m_out_ref, b))

    tk = 2 * LANES
    pad = jnp.zeros((tk - cached - t, A_KW), F32)
    q_pos = PAST_LEN + lax.broadcasted_iota(jnp.int32, (t, 1), 0)
    idx = lax.broadcasted_iota(jnp.int32, (1, tk), 1)
    k_pos = jnp.where(idx < cached + t, PAST_LEN - cached + idx, -1)
    biases = _swa_biases(q_pos, k_pos)
    for b in range(ns):
        k_new, v_new = ak_s[b], av_s[b]

        def write_o(lanes, value, b=b):
            mix_s[b * t:(b + 1) * t, M_W + lanes.start:M_W + lanes.stop] = value

        _swa_block(aq_s[b], _kv_halves(jnp.concatenate([ck_ref[b], k_new, pad], axis=0)),
                   _kv_halves(jnp.concatenate([cv_ref[b], v_new, pad], axis=0)),
                   biases, sink_ref, write_o)
        ko_ref[b] = jnp.concatenate([ck_ref[b, t:, :], k_new], axis=0)
        vo_ref[b] = jnp.concatenate([cv_ref[b, t:, :], v_new], axis=0)

    y_ref[...] = _out_ffn_tile(
        x_ref[...], mix_s[...], ga1_ref[...], sc2_ref[...], sh2_ref[...], ga2_ref[...],
        g2_ref[...], wo_ref, wu_ref, wd_ref)


def _sample_layer(x, mod4, sinks, c0, n0, m0, gout, cache_k, cache_v, proj_w, ffn_w):
    ns, t, d = x.shape
    cached = cache_k.shape[1]
    assert cached >= t
    whole = lambda shape: pl.BlockSpec(shape, lambda i: (0,) * len(shape))
    mod = lambda k: pl.BlockSpec((ns, None, 1, d), lambda i: (0, k, 0, 0))
    st_c, st_n, st_m = (whole((ns, M_HEADS, M_DIM, M_DIM)), whole((ns, M_HEADS, M_DIM)),
                        whole((ns, M_HEADS, 1)))
    kv = whole((ns, cached, A_KW))
    return pl.pallas_call(
        _sample_body,
        grid=(1,),
        in_specs=([pl.BlockSpec(memory_space=pltpu.SMEM), whole((ns, t, d)),
                   mod(MOD_SC1), mod(MOD_SH1)]
                  + _proj_weight_specs(d)
                  + [st_c, st_n, st_m, _const_spec((M_HEADS, M_DIM)), kv, kv,
                     mod(MOD_GA1), mod(MOD_SC2), mod(MOD_SH2), mod(MOD_GA2)]
                  + _ffn_weight_specs(d)),
        out_specs=(whole((ns, t, d)), kv, kv, st_c, st_n, st_m),
        out_shape=(jax.ShapeDtypeStruct((ns, t, d), F32),
                   jax.ShapeDtypeStruct((ns, cached, A_KW), F32),
                   jax.ShapeDtypeStruct((ns, cached, A_KW), F32),
                   jax.ShapeDtypeStruct((ns, M_HEADS, M_DIM, M_DIM), F32),
                   jax.ShapeDtypeStruct((ns, M_HEADS, M_DIM), F32),
                   jax.ShapeDtypeStruct((ns, M_HEADS, 1), F32)),
        scratch_shapes=[pltpu.VMEM((ns, t, M_W), BF16), pltpu.VMEM((ns, t, M_W), BF16),
                        pltpu.VMEM((ns, t, M_W), BF16), pltpu.VMEM((ns, t, M_W), BF16),
                        pltpu.VMEM((GATE_ROWS, ns * t), F32), pltpu.VMEM((ns, t, A_QW), BF16),
                        pltpu.VMEM((ns, t, A_KW), F32), pltpu.VMEM((ns, t, A_KW), F32),
                        pltpu.VMEM((ns * t, M_W + A_QW), BF16),
                        pltpu.VMEM((ns, M_HEADS, M_DIM, 2 * M_DIM), F32),
                        pltpu.VMEM((ns, M_HEADS, 1), F32)],
        compiler_params=_params(("arbitrary",)),
        name="sample_layer",
    )(sinks, x, mod4, mod4, *proj_w, c0, n0, m0, gout, cache_k, cache_v,
      mod4, mod4, mod4, mod4, *ffn_w)


def _block_ones(n):
    g = jnp.arange(n, dtype=jnp.int32) // A_DIM
    return (g[:, None] == g[None, :]).astype(BF16)


def kernel(x_prompt, x_sample, c_prompt, c_sample, cache_swa_k, cache_swa_v, state_mlstm_C,
           state_mlstm_n, state_mlstm_m, w_ada, b_ada, g_norm1, w_in, b_gates, g_q, g_k, sinks,
           g_mlstm_out, w_out, g_norm2, w_up, w_down):
    depth = w_ada.shape[0]
    assert depth == 1
    bp, lp, d = x_prompt.shape
    bs, ls, _ = x_sample.shape
    wb = cache_swa_k.shape[2]

    xp, xs = x_prompt, x_sample
    for l in range(depth):
        wt = w_in[l].T
        g_lo = 4 * M_W
        g_hi = g_lo + 2 * M_HEADS
        wg = jnp.concatenate([wt[g_lo:g_hi], jnp.zeros((BF16_ROWS - 2 * M_HEADS, d), wt.dtype)])
        gq_t = (jnp.tile(g_q[l], A_HEADS) * (LOG2E * A_DIM ** -0.5))[None, :]
        proj_w = (g_norm1[l][None, :], wt[:g_lo].astype(BF16), wt[g_hi:].astype(BF16),
                  wg.astype(BF16), b_gates[l][:, None], gq_t,
                  jnp.tile(g_k[l], A_KV_HEADS)[None, :], _block_ones(A_QW), _block_ones(A_KW))
        ffn_w = (g_norm2[l][None, :], w_out[l].astype(BF16), w_up[l].astype(BF16),
                 w_down[l].astype(BF16))

        mod_rows = -(-(bp + bs) // 8) * 8
        c_all = jnp.concatenate([c_sample, c_prompt, jnp.zeros((mod_rows - bp - bs, d), F32)],
                                axis=0)
        mod4 = _ada_mod(c_all, w_ada[l], b_ada[l][None, :]).reshape(mod_rows, 6, 1, d)

        xp, kt, vt, c, n, m = _layer(xp, mod4, bs, sinks[l], g_mlstm_out[l], proj_w, ffn_w,
                                     rows=ROW_TILE, chunk=MLSTM_CHUNK)
        win = WIN_CHUNKS * CHUNK
        st_p = (kt.reshape(bp, win, A_KV_HEADS, A_DIM), vt.reshape(bp, win, A_KV_HEADS, A_DIM),
                c, n, m[..., 0])

        xs, ko, vo, c, n, m = _sample_layer(
            xs, mod4, sinks[l], state_mlstm_C[l], state_mlstm_n[l], state_mlstm_m[l][..., None],
            g_mlstm_out[l], cache_swa_k[l].reshape(bs, wb, A_KW),
            cache_swa_v[l].reshape(bs, wb, A_KW), proj_w, ffn_w)
        st_s = (ko.reshape(bs, wb, A_KV_HEADS, A_DIM), vo.reshape(bs, wb, A_KV_HEADS, A_DIM),
                c, n, m[..., 0])

    return (xp, xs) + tuple(s[None] for s in st_p) + tuple(s[None] for s in st_s)
```

```python
import functools

import jax
import jax.numpy as jnp
from jax import lax
from jax.experimental import pallas as pl
from jax.experimental.pallas import tpu as pltpu

F32 = jnp.float32
BF16 = jnp.bfloat16

D_MODEL = 1024
CHUNK = 64
M_HEADS = 4
M_DIM = 128
M_W = M_HEADS * M_DIM
A_HEADS = 8
A_KV_HEADS = 2
A_DIM = 64
A_QW = A_HEADS * A_DIM
A_KW = A_KV_HEADS * A_DIM
WIN_CHUNKS = 2
D_FF = 4 * D_MODEL
EPS = 1e-6
PAST_LEN = 4096

LANES = 128
BF16_ROWS = 16
VMEM_LIMIT = 62 * 1024 * 1024
NEG_INF = float("-inf")
LOG2E = 1.4426950408889634
ROW_TILE = 512
MLSTM_CHUNK = 128
FF_BLOCK = 1024
NT_DIMS = (((1,), (1,)), ((), ()))
MOD_SH1, MOD_SC1, MOD_GA1, MOD_SH2, MOD_SC2, MOD_GA2 = range(6)


def _params(sem):
    return pltpu.CompilerParams(dimension_semantics=sem, vmem_limit_bytes=VMEM_LIMIT)


def _const_spec(shape):
    nd = len(shape)
    return pl.BlockSpec(shape, lambda *_: (0,) * nd, pipeline_mode=pl.Buffered(1))


def _ada_body(c_ref, w_ref, b_ref, o_ref):
    c = c_ref[...]
    s = c * jax.nn.sigmoid(c)
    o_ref[...] = jnp.dot(s.astype(BF16), w_ref[...].astype(BF16),
                         preferred_element_type=F32) + b_ref[...]


def _ada_mod(c_all, w_ada, b_ada):
    rows = c_all.shape[0]
    return pl.pallas_call(
        _ada_body,
        grid=(6,),
        in_specs=[pl.BlockSpec((rows, D_MODEL), lambda j: (0, 0)),
                  pl.BlockSpec((D_MODEL, D_MODEL), lambda j: (0, j)),
                  pl.BlockSpec((1, D_MODEL), lambda j: (0, j))],
        out_specs=pl.BlockSpec((rows, D_MODEL), lambda j: (0, j)),
        out_shape=jax.ShapeDtypeStruct((rows, 6 * D_MODEL), F32),
        compiler_params=_params(("arbitrary",)),
        name="ada_mod",
    )(c_all, w_ada, b_ada)


def _segment_prefix(x, op, identity, seg):
    assert x.shape[1] % LANES == 0 and x.shape[1] % seg == 0
    pos = lax.broadcasted_iota(jnp.int32, x.shape, 1) % seg
    d = 1
    while d < seg:
        x = op(x, jnp.where(pos >= d, pltpu.roll(x, d, axis=1), identity))
        d *= 2
    return x


GATE_ROWS = 24
PROJ_OUTPUTS = ("mq", "mk", "mv", "og", "gt", "aq", "ak", "av")


def _in_proj_tile(x, sc, sh, g1, wa_ref, wb_ref, wg_ref, bg, gq, gk, pq_ref, pk_ref, store,
                  gate_chunk):
    s, r, d = x.shape
    m = s * r
    ms = jnp.mean(x * x, axis=-1, keepdims=True)
    y = x * lax.rsqrt(ms + EPS) * g1
    h = y * (1.0 + sc) + sh
    hb = h.reshape(m, d).astype(BF16)

    def proj(w_ref, lo, width):
        return lax.dot_general(hb, w_ref[lo:lo + width, :], NT_DIMS, preferred_element_type=F32)

    zg = lax.dot_general(wg_ref[...], hb, NT_DIMS,
                         preferred_element_type=F32)[:2 * M_HEADS] + bg
    row = lax.broadcasted_iota(jnp.int32, zg.shape, 0)
    log_sig = jnp.minimum(zg, 0.0) - jnp.log1p(jnp.exp(-jnp.abs(zg)))
    gt = jnp.where(row < M_HEADS, zg, log_sig)
    b8 = _segment_prefix(gt, jnp.add, 0.0, gate_chunk)
    c8 = gt - pltpu.roll(b8, M_HEADS, axis=0)
    cm8 = _segment_prefix(c8, jnp.maximum, NEG_INF, gate_chunk)
    store("gt", jnp.concatenate([c8, b8, cm8], axis=0))

    store("mq", proj(wa_ref, 0, M_W).astype(BF16))
    store("mk", (proj(wa_ref, M_W, M_W) * (M_DIM ** -0.5)).astype(BF16))
    store("mv", proj(wa_ref, 2 * M_W, M_W).astype(BF16))
    store("og", jax.nn.sigmoid(proj(wa_ref, 3 * M_W, M_W)).astype(BF16))

    aq = proj(wb_ref, 0, A_QW)
    ss_q = jnp.dot((aq * aq).astype(BF16), pq_ref[...], preferred_element_type=F32)
    store("aq", (aq * lax.rsqrt(ss_q * (1.0 / A_DIM) + EPS) * gq).astype(BF16))
    ak = proj(wb_ref, A_QW, A_KW)
    sq = ak * ak
    hi = sq.astype(BF16)
    lo = (sq - hi.astype(F32)).astype(BF16)
    ss_k = (jnp.dot(hi, pk_ref[...], preferred_element_type=F32)
            + jnp.dot(lo, pk_ref[...], preferred_element_type=F32))
    store("ak", ak * lax.rsqrt(ss_k * (1.0 / A_DIM) + EPS) * gk)
    store("av", proj(wb_ref, A_QW + A_KW, A_KW))


def _in_proj_body(x_ref, sc_ref, sh_ref, g1_ref, wa_ref, wb_ref, wg_ref, bg_ref, gq_ref, gk_ref,
                  pq_ref, pk_ref, *out_refs, gate_chunk):
    s, r, _ = x_ref.shape
    outs = dict(zip(PROJ_OUTPUTS, out_refs))

    def store(name, value):
        ref = outs[name]
        ref[...] = value if name == "gt" else value.reshape(s, r, value.shape[-1])

    _in_proj_tile(x_ref[...], sc_ref[...], sh_ref[...], g1_ref[...], wa_ref, wb_ref, wg_ref,
                  bg_ref[...], gq_ref[...], gk_ref[...], pq_ref, pk_ref, store, gate_chunk)


def _proj_weight_specs(d):
    return [_const_spec((1, d)), _const_spec((4 * M_W, d)), _const_spec((A_QW + 2 * A_KW, d)),
            _const_spec((BF16_ROWS, d)), _const_spec((2 * M_HEADS, 1)),
            _const_spec((1, A_QW)), _const_spec((1, A_KW)),
            _const_spec((A_QW, A_QW)), _const_spec((A_KW, A_KW))]


def _in_proj(x, sc, sh, proj_w, *, seqs, rows, gate_chunk):
    ns, length, d = x.shape
    grid = (ns // seqs, length // rows)
    m = seqs * rows
    n_tok = ns * length

    def tile(width):
        return pl.BlockSpec((seqs, rows, width), lambda i, j: (i, j, 0))

    mod = pl.BlockSpec((seqs, 1, d), lambda i, j: (i, 0, 0))
    steps_per_seq = length // rows
    out_shapes = (
        jax.ShapeDtypeStruct((ns, length, M_W), BF16),
        jax.ShapeDtypeStruct((ns, length, M_W), BF16),
        jax.ShapeDtypeStruct((ns, length, M_W), BF16),
        jax.ShapeDtypeStruct((ns, length, M_W), BF16),
        jax.ShapeDtypeStruct((GATE_ROWS, n_tok), F32),
        jax.ShapeDtypeStruct((ns, length, A_QW), BF16),
        jax.ShapeDtypeStruct((ns, length, A_KW), F32),
        jax.ShapeDtypeStruct((ns, length, A_KW), F32),
    )
    return pl.pallas_call(
        functools.partial(_in_proj_body, gate_chunk=gate_chunk),
        grid=grid,
        in_specs=[tile(d), mod, mod] + _proj_weight_specs(d),
        out_specs=(tile(M_W), tile(M_W), tile(M_W), tile(M_W),
                   pl.BlockSpec((GATE_ROWS, m), lambda i, j: (0, i * steps_per_seq + j)),
                   tile(A_QW), tile(A_KW), tile(A_KW)),
        out_shape=out_shapes,
        compiler_params=_params(("arbitrary", "arbitrary")),
        name="in_proj",
    )(x, sc, sh, *proj_w)


def _col_bcast(r):
    t = r.shape[1]
    if t % LANES == 0:
        return jnp.broadcast_to(r, (LANES, t)).T
    src = lax.broadcasted_iota(jnp.int32, (t, t), 0)
    dst = lax.broadcasted_iota(jnp.int32, (t, t), 1)
    col = jnp.sum(jnp.where(src == dst, jnp.broadcast_to(r, (t, t)), 0.0), axis=1, keepdims=True)
    return jnp.broadcast_to(col, (t, LANES))


def _eye_dim():
    return (lax.broadcasted_iota(jnp.int32, (M_DIM, M_DIM), 0)
            == lax.broadcasted_iota(jnp.int32, (M_DIM, M_DIM), 1))


def _mlstm_load_state(c0_ref, n0_ref, m0_ref, caug_ref, m_ref):
    for hd in range(M_HEADS):
        caug_ref[hd, :, :M_DIM] = c0_ref[0, hd]
        n_row = jnp.broadcast_to(n0_ref[0, hd:hd + 1, :], (M_DIM, M_DIM))
        n_col = jnp.sum(jnp.where(_eye_dim(), n_row, 0.0), axis=1, keepdims=True)
        caug_ref[hd, :, M_DIM:] = jnp.broadcast_to(n_col, (M_DIM, M_DIM))
    m_ref[...] = m0_ref[0]


def _mlstm_store_state(caug_ref, m_ref, c_out_ref, n_out_ref, m_out_ref):
    for hd in range(M_HEADS):
        c_out_ref[0, hd] = caug_ref[hd, :, :M_DIM]
        n_out_ref[0, hd:hd + 1, :] = jnp.sum(
            jnp.where(_eye_dim(), caug_ref[hd, :, M_DIM:], 0.0), axis=0, keepdims=True)
    m_out_ref[0] = m_ref[...]


def _mlstm_tile(q_ref, k_ref, v_ref, og_ref, gt_ref, gout_ref, caug_ref, m_ref, write_h, chunk):
    t = chunk
    n_chunks = q_ref.shape[1] // t
    bs = LANES if t % LANES == 0 else t
    tril = (lax.broadcasted_iota(jnp.int32, (bs, bs), 1)
            <= lax.broadcasted_iota(jnp.int32, (bs, bs), 0))
    m_prev = m_ref[...]
    for j in range(n_chunks):
        rows = slice(j * t, (j + 1) * t)
        c = gt_ref[0:M_HEADS, rows]
        b = gt_ref[12:12 + M_HEADS, rows]
        g = jnp.maximum(gt_ref[16:16 + M_HEADS, rows], m_prev)
        g_last = g[:, t - 1:t]
        c2 = c * LOG2E
        g2 = g * LOG2E
        e_all = jnp.exp2(-LOG2E * (b + g))
        w_src_all = jnp.exp2(c2 - LOG2E * g_last)
        w_inter_all = jnp.exp2(LOG2E * m_prev - g2)
        w_old_all = jnp.exp2(LOG2E * (m_prev - g_last))
        m_prev = b[:, t - 1:t] + g_last

        for hd in range(M_HEADS):
            sl = slice(hd * M_DIM, (hd + 1) * M_DIM)
            hr = slice(hd, hd + 1)
            g_b = _col_bcast(g2[hr])
            e_b = _col_bcast(e_all[hr])
            w_src = _col_bcast(w_src_all[hr])
            w_inter = _col_bcast(w_inter_all[hr])

            q = q_ref[0, rows, sl]
            k = k_ref[0, rows, sl]
            v = v_ref[0, rows, sl]
            caug = caug_ref[hd]
            inter = jnp.dot(q, caug.astype(BF16), preferred_element_type=F32)
            gain = gout_ref[hr, :]
            for rb in range(t // bs):
                rsl = slice(rb * bs, (rb + 1) * bs)
                ncols = (rb + 1) * bs
                s_mat = lax.dot_general(q[rsl], k[:ncols], NT_DIMS,
                                        preferred_element_type=F32)
                decay = []
                for cb in range(rb + 1):
                    arg = c2[hr, cb * bs:(cb + 1) * bs] - g_b[rsl, :bs]
                    decay.append(jnp.exp2(jnp.where(tril, arg, NEG_INF) if cb == rb else arg))
                s_mat = s_mat * (decay[0] if rb == 0 else jnp.concatenate(decay, axis=1))
                num = (jnp.dot(s_mat.astype(BF16), v[:ncols], preferred_element_type=F32)
                       + w_inter[rsl] * inter[rsl, :M_DIM])
                den = jnp.sum(s_mat, axis=1, keepdims=True) + w_inter[rsl] * inter[rsl, M_DIM:]
                hh = num / jnp.maximum(jnp.abs(den), e_b[rsl])
                hn = hh * lax.rsqrt(jnp.mean(hh * hh, axis=-1, keepdims=True) + EPS) * gain
                out_rows = slice(j * t + rb * bs, j * t + (rb + 1) * bs)
                write_h(out_rows, sl, (hn * og_ref[0, out_rows, sl].astype(F32)).astype(BF16))

            rhs = jnp.concatenate([v.astype(F32) * w_src, w_src], axis=1).astype(BF16)
            upd = lax.dot_general(k, rhs, (((0,), (0,)), ((), ())), preferred_element_type=F32)
            caug_ref[hd] = w_old_all[hr] * caug + upd
    m_ref[...] = m_prev


def _mlstm_body(q_ref, k_ref, v_ref, og_ref, gt_ref, c0_ref, n0_ref, m0_ref, gout_ref,
                h_ref, c_out_ref, n_out_ref, m_out_ref, caug_ref, m_ref, *, chunk):
    ci = pl.program_id(1)

    @pl.when(ci == 0)
    def _():
        _mlstm_load_state(c0_ref, n0_ref, m0_ref, caug_ref, m_ref)

    def write_h(rows, lanes, value):
        h_ref[0, rows, lanes] = value

    _mlstm_tile(q_ref, k_ref, v_ref, og_ref, gt_ref, gout_ref, caug_ref, m_ref, write_h, chunk)

    @pl.when(ci == pl.num_programs(1) - 1)
    def _():
        _mlstm_store_state(caug_ref, m_ref, c_out_ref, n_out_ref, m_out_ref)


def _mlstm(mq, mk, mv, og, gt, c0, n0, m0, gout, *, chunk):
    ns, length, _ = mq.shape
    assert length == chunk
    tile = pl.BlockSpec((1, chunk, M_W), lambda b, c: (b, c, 0))
    gt_spec = pl.BlockSpec((None, GATE_ROWS, chunk), lambda b, c: (b, 0, 0))
    st_c = pl.BlockSpec((1, M_HEADS, M_DIM, M_DIM), lambda b, c: (b, 0, 0, 0))
    st_n = pl.BlockSpec((1, M_HEADS, M_DIM), lambda b, c: (b, 0, 0))
    st_m = pl.BlockSpec((1, M_HEADS, 1), lambda b, c: (b, 0, 0))
    return pl.pallas_call(
        functools.partial(_mlstm_body, chunk=chunk),
        grid=(ns, 1),
        in_specs=[tile, tile, tile, tile, gt_spec, st_c, st_n, st_m,
                  _const_spec((M_HEADS, M_DIM))],
        out_specs=(tile, st_c, st_n, st_m),
        out_shape=(jax.ShapeDtypeStruct((ns, length, M_W), BF16),
                   jax.ShapeDtypeStruct((ns, M_HEADS, M_DIM, M_DIM), F32),
                   jax.ShapeDtypeStruct((ns, M_HEADS, M_DIM), F32),
                   jax.ShapeDtypeStruct((ns, M_HEADS, 1), F32)),
        scratch_shapes=[pltpu.VMEM((M_HEADS, M_DIM, 2 * M_DIM), F32),
                        pltpu.VMEM((M_HEADS, 1), F32)],
        compiler_params=_params(("arbitrary", "arbitrary")),
        name="mlstm",
    )(mq, mk, mv, og, gt, c0, n0, m0, gout)


def _kv_halves(a):
    low = lax.broadcasted_iota(jnp.int32, a.shape, 1) < A_DIM
    lo_half = jnp.where(low, a, 0.0)
    hi_half = jnp.where(low, 0.0, a)
    pairs = [(lo_half, pltpu.roll(lo_half, A_DIM, axis=1)),
             (pltpu.roll(hi_half, A_DIM, axis=1), hi_half)]
    return [tuple(h.astype(BF16) for h in pair) for pair in pairs]


def _swa_biases(q_pos, k_pos):
    shift = CHUNK.bit_length() - 1
    assert 1 << shift == CHUNK
    qc = lax.shift_right_arithmetic(q_pos, shift)
    kc = lax.shift_right_arithmetic(k_pos, shift)
    allowed = (kc <= qc) & (kc >= qc - WIN_CHUNKS) & (k_pos >= 0)
    dist = jnp.abs(q_pos - k_pos).astype(F32)
    slopes = [2.0 ** (-8.0 * (head + 1) / A_HEADS) for head in range(A_HEADS)]
    return [jnp.where(allowed, (-LOG2E * slope) * dist, NEG_INF) for slope in slopes]


def _swa_block(q, kz, vz, biases, sink_ref, write_o, valid=None):
    tq = q.shape[0]
    pairs_per_kv = A_HEADS // A_KV_HEADS // 2
    for j in range(A_KV_HEADS):
        pairs = [j * pairs_per_kv + i for i in range(pairs_per_kv)]
        q_stack = jnp.concatenate([q[:, p * LANES:(p + 1) * LANES] for p in pairs], axis=0)
        acc = [None] * pairs_per_kv
        for e in range(2):
            logits_all = lax.dot_general(q_stack, kz[j][e], NT_DIMS, preferred_element_type=F32)
            probs, denoms = [], []
            for i, p in enumerate(pairs):
                head = 2 * p + e
                sink = sink_ref[head] * LOG2E
                logits = logits_all[i * tq:(i + 1) * tq] + biases[head]
                if valid is not None:
                    logits = jnp.where(valid, logits, NEG_INF)
                mx = jnp.maximum(jnp.max(logits, axis=1, keepdims=True), sink)
                pr = jnp.exp2(logits - mx)
                denoms.append(jnp.sum(pr, axis=1, keepdims=True) + jnp.exp2(sink - mx))
                probs.append(pr.astype(BF16))
            out_all = jnp.dot(jnp.concatenate(probs, axis=0), vz[j][e],
                              preferred_element_type=F32)
            for i in range(pairs_per_kv):
                part = out_all[i * tq:(i + 1) * tq] / denoms[i]
                acc[i] = part if acc[i] is None else acc[i] + part
        for i, p in enumerate(pairs):
            write_o(slice(p * LANES, (p + 1) * LANES), acc[i].astype(BF16))


SWA_Q_ROWS = WIN_CHUNKS * CHUNK


def _swa_tile(sink_ref, q_ref, k_all, v_all, tile_start, write_o):
    tq = SWA_Q_ROWS
    kz = _kv_halves(k_all)
    vz = _kv_halves(v_all)
    rel_k = lax.broadcasted_iota(jnp.int32, (1, 2 * tq), 1)
    biases = _swa_biases(tq + lax.broadcasted_iota(jnp.int32, (tq, 1), 0), rel_k)
    for r in range(q_ref.shape[1] // tq):
        rows = slice(r * tq, (r + 1) * tq)
        band = slice(r * tq, (r + 2) * tq)
        valid = (tile_start - tq + rel_k >= 0) if r == 0 else None
        _swa_block(q_ref[0, rows, :],
                   [tuple(h[band] for h in pair) for pair in kz],
                   [tuple(h[band] for h in pair) for pair in vz],
                   biases, sink_ref,
                   lambda lanes, value, rows=rows: write_o(rows, lanes, value), valid=valid)


def _swa_sample_body(sink_ref, q_ref, kn_ref, vn_ref, ck_ref, cv_ref, o_ref):
    tq = q_ref.shape[1]
    wb = ck_ref.shape[1]
    tk = 2 * LANES
    pad = jnp.zeros((tk - wb - tq, A_KW), F32)
    kz = _kv_halves(jnp.concatenate([ck_ref[0], kn_ref[0], pad], axis=0))
    vz = _kv_halves(jnp.concatenate([cv_ref[0], vn_ref[0], pad], axis=0))
    q_pos = PAST_LEN + lax.broadcasted_iota(jnp.int32, (tq, 1), 0)
    idx = lax.broadcasted_iota(jnp.int32, (1, tk), 1)
    k_pos = jnp.where(idx < wb + tq, PAST_LEN - wb + idx, -1)

    def write_o(lanes, value):
        o_ref[0, :, lanes] = value

    _swa_block(q_ref[0], kz, vz, _swa_biases(q_pos, k_pos), sink_ref, write_o)


def _swa_sample(aq, ak, av, cache_k, cache_v, sinks):
    ns, tq, _ = aq.shape
    wb = cache_k.shape[1]
    new = lambda w: pl.BlockSpec((1, tq, w), lambda b: (b, 0, 0))
    old = pl.BlockSpec((1, wb, A_KW), lambda b: (b, 0, 0))
    return pl.pallas_call(
        _swa_sample_body,
        grid=(ns,),
        in_specs=[pl.BlockSpec(memory_space=pltpu.SMEM),
                  new(A_QW), new(A_KW), new(A_KW), old, old],
        out_specs=new(A_QW),
        out_shape=jax.ShapeDtypeStruct((ns, tq, A_QW), BF16),
        compiler_params=_params(("arbitrary",)),
        name="swa_sample",
    )(sinks, aq, ak, av, cache_k, cache_v)


def _out_ffn_tile(x, mixed, ga1, sc2, sh2, ga2, g2, wo_ref, wu_ref, wd_ref):
    s, r, d = x.shape
    mix = jnp.dot(mixed, wo_ref[...], preferred_element_type=F32)
    x1 = x + ga1 * mix.reshape(s, r, d)
    ms = jnp.mean(x1 * x1, axis=-1, keepdims=True)
    h2 = (x1 * lax.rsqrt(ms + EPS) * g2) * (1.0 + sc2) + sh2
    hb = h2.reshape(s * r, d).astype(BF16)
    ffn = jnp.zeros((s * r, d), F32)
    for c in range(D_FF // FF_BLOCK):
        u = jnp.dot(hb, wu_ref[:, c * FF_BLOCK:(c + 1) * FF_BLOCK], preferred_element_type=F32)
        u = jnp.square(jnp.maximum(u, 0.0)).astype(BF16)
        ffn = ffn + jnp.dot(u, wd_ref[c * FF_BLOCK:(c + 1) * FF_BLOCK, :],
                            preferred_element_type=F32)
    return x1 + ga2 * ffn.reshape(s, r, d)


def _out_ffn_body(x_ref, hm_ref, at_ref, ga1_ref, sc2_ref, sh2_ref, ga2_ref, g2_ref,
                  wo_ref, wu_ref, wd_ref, y_ref):
    s, r, _ = x_ref.shape
    mixed = jnp.concatenate([hm_ref[...].reshape(s * r, M_W), at_ref[...].reshape(s * r, A_QW)],
                            axis=1)
    y_ref[...] = _out_ffn_tile(
        x_ref[...], mixed, ga1_ref[...], sc2_ref[...], sh2_ref[...], ga2_ref[...], g2_ref[...],
        wo_ref, wu_ref, wd_ref)


def _ffn_weight_specs(d):
    return [_const_spec((1, d)), _const_spec((d, d)), _const_spec((d, D_FF)),
            _const_spec((D_FF, d))]


def _out_ffn(x, hm, at, ga1, sc2, sh2, ga2, ffn_w, *, seqs, rows):
    ns, length, d = x.shape
    tile = lambda w: pl.BlockSpec((seqs, rows, w), lambda i, j: (i, j, 0))
    mod = pl.BlockSpec((seqs, 1, d), lambda i, j: (i, 0, 0))
    return pl.pallas_call(
        _out_ffn_body,
        grid=(ns // seqs, length // rows),
        in_specs=[tile(d), tile(M_W), tile(A_QW), mod, mod, mod, mod] + _ffn_weight_specs(d),
        out_specs=tile(d),
        out_shape=jax.ShapeDtypeStruct((ns, length, d), F32),
        compiler_params=_params(("arbitrary", "arbitrary")),
        name="out_ffn",
    )(x, hm, at, ga1, sc2, sh2, ga2, *ffn_w)


def _layer_body(sink_ref, xp_ref, sc1_ref, sh1_ref, g1_ref, wa_ref, wb_ref, wg_ref, bg_ref,
                gq_ref, gk_ref, pq_ref, pk_ref, gout_ref,
                xf_ref, ga1_ref, sc2_ref, sh2_ref, ga2_ref, g2_ref, wo_ref, wu_ref, wd_ref,
                y_ref, kt_ref, vt_ref, c_out_ref, n_out_ref, m_out_ref,
                mq_s, mk_s, mv_s, og_s, gt_s, aq_s, ak_s, av_s, kprev_s, vprev_s,
                mix_s, caug_ref, m_ref, *, n_tiles, tiles_per_seq, chunk):
    s = pl.program_id(0)
    rows = xp_ref.shape[1]
    cur = s % 2
    prv = 1 - cur
    tile_p = jnp.minimum(s, n_tiles - 1)
    tile_a = jnp.clip(s - 1, 0, n_tiles - 1)
    ja = tile_a % tiles_per_seq
    a_valid = jnp.logical_and(s >= 1, s <= n_tiles)
    proj_s = {"mq": mq_s, "mk": mk_s, "mv": mv_s, "og": og_s, "gt": gt_s,
              "aq": aq_s, "ak": ak_s, "av": av_s}

    @pl.when(s == 0)
    def _():
        for ref in proj_s.values():
            ref[1] = jnp.zeros(ref.shape[1:], ref.dtype)
        kprev_s[...] = jnp.zeros(kprev_s.shape, kprev_s.dtype)
        vprev_s[...] = jnp.zeros(vprev_s.shape, vprev_s.dtype)
        mix_s[0] = jnp.zeros(mix_s.shape[1:], mix_s.dtype)

    @pl.when(jnp.logical_and(ja == 0, s <= n_tiles))
    def _():
        caug_ref[...] = jnp.zeros(caug_ref.shape, caug_ref.dtype)
        m_ref[...] = jnp.zeros(m_ref.shape, m_ref.dtype)

    y_ref[...] = _out_ffn_tile(
        xf_ref[...], mix_s[cur], ga1_ref[...], sc2_ref[...], sh2_ref[...], ga2_ref[...],
        g2_ref[...], wo_ref, wu_ref, wd_ref)

    def write_h(r, lanes, value):
        mix_s[prv, r, lanes] = value

    def write_o(r, lanes, value):
        mix_s[prv, r, slice(M_W + lanes.start, M_W + lanes.stop)] = value

    view = lambda ref: ref.at[pl.ds(prv, 1)]
    _mlstm_tile(view(mq_s), view(mk_s), view(mv_s), view(og_s), gt_s.at[prv], gout_ref,
                caug_ref, m_ref, write_h, chunk)
    k_cur = ak_s[prv]
    v_cur = av_s[prv]
    _swa_tile(sink_ref, view(aq_s),
              jnp.concatenate([kprev_s[...], k_cur], axis=0),
              jnp.concatenate([vprev_s[...], v_cur], axis=0), ja * rows, write_o)
    kprev_s[...] = k_cur[rows - SWA_Q_ROWS:]
    vprev_s[...] = v_cur[rows - SWA_Q_ROWS:]

    def store(name, value):
        proj_s[name][cur] = value

    _in_proj_tile(xp_ref[...], sc1_ref[...], sh1_ref[...], g1_ref[...], wa_ref, wb_ref, wg_ref,
                  bg_ref[...], gq_ref[...], gk_ref[...], pq_ref, pk_ref, store, chunk)

    @pl.when(jnp.logical_and(tile_p % tiles_per_seq == tiles_per_seq - 1, s < n_tiles))
    def _():
        kt_ref[0] = ak_s[cur, rows - SWA_Q_ROWS:, :]
        vt_ref[0] = av_s[cur, rows - SWA_Q_ROWS:, :]

    @pl.when(jnp.logical_and(ja == tiles_per_seq - 1, a_valid))
    def _():
        _mlstm_store_state(caug_ref, m_ref, c_out_ref, n_out_ref, m_out_ref)


def _layer(x, mod4, mod_row0, sinks, gout, proj_w, ffn_w, *, rows, chunk):
    ns, length, d = x.shape
    tps = length // rows
    n_tiles = ns * tps

    def stage(lag):
        def tile_idx(i):
            t = jnp.clip(i - lag, 0, n_tiles - 1)
            return t // tps, t % tps
        x_tile = pl.BlockSpec((1, rows, d), lambda i: (*tile_idx(i), 0))
        mod = lambda k: pl.BlockSpec((1, None, 1, d),
                                     lambda i: (mod_row0 + tile_idx(i)[0], k, 0, 0))
        per_seq = lambda shape: pl.BlockSpec(
            (1,) + shape, lambda i: (tile_idx(i)[0],) + (0,) * len(shape))
        return x_tile, mod, per_seq

    xp_tile, p_mod, p_seq = stage(0)
    _, _, a_seq = stage(1)
    xf_tile, f_mod, _ = stage(2)
    st_c, st_n, st_m = a_seq((M_HEADS, M_DIM, M_DIM)), a_seq((M_HEADS, M_DIM)), a_seq((M_HEADS, 1))
    tail = p_seq((SWA_Q_ROWS, A_KW))
    slots = lambda shape, dtype: pltpu.VMEM((2,) + shape, dtype)
    return pl.pallas_call(
        functools.partial(_layer_body, n_tiles=n_tiles, tiles_per_seq=tps, chunk=chunk),
        grid=(n_tiles + 2,),
        in_specs=([pl.BlockSpec(memory_space=pltpu.SMEM), xp_tile, p_mod(MOD_SC1), p_mod(MOD_SH1)]
                  + _proj_weight_specs(d)
                  + [_const_spec((M_HEADS, M_DIM)), xf_tile,
                     f_mod(MOD_GA1), f_mod(MOD_SC2), f_mod(MOD_SH2), f_mod(MOD_GA2)]
                  + _ffn_weight_specs(d)),
        out_specs=(xf_tile, tail, tail, st_c, st_n, st_m),
        out_shape=(jax.ShapeDtypeStruct((ns, length, d), F32),
                   jax.ShapeDtypeStruct((ns, SWA_Q_ROWS, A_KW), F32),
                   jax.ShapeDtypeStruct((ns, SWA_Q_ROWS, A_KW), F32),
                   jax.ShapeDtypeStruct((ns, M_HEADS, M_DIM, M_DIM), F32),
                   jax.ShapeDtypeStruct((ns, M_HEADS, M_DIM), F32),
                   jax.ShapeDtypeStruct((ns, M_HEADS, 1), F32)),
        scratch_shapes=[slots((rows, M_W), BF16), slots((rows, M_W), BF16),
                        slots((rows, M_W), BF16), slots((rows, M_W), BF16),
                        slots((GATE_ROWS, rows), F32), slots((rows, A_QW), BF16),
                        slots((rows, A_KW), F32), slots((rows, A_KW), F32),
                        pltpu.VMEM((SWA_Q_ROWS, A_KW), F32), pltpu.VMEM((SWA_Q_ROWS, A_KW), F32),
                        slots((rows, M_W + A_QW), BF16),
                        pltpu.VMEM((M_HEADS, M_DIM, 2 * M_DIM), F32),
                        pltpu.VMEM((M_HEADS, 1), F32)],
        compiler_params=_params(("arbitrary",)),
        name="layer",
    )(sinks, x, mod4, mod4, *proj_w, gout, x, mod4, mod4, mod4, mod4, *ffn_w)


class _LaneWindow:
    def __init__(self, ref, offset):
        self.ref, self.offset = ref, offset

    def __getitem__(self, idx):
        rows, cols = idx
        return self.ref[rows, slice(cols.start + self.offset, cols.stop + self.offset)]


def _sample_body(sink_ref, x_ref, sc1_ref, sh1_ref, g1_ref, wa_ref, wb_ref, wg_ref, bg_ref,
                 gq_ref, gk_ref, pq_ref, pk_ref, c0_ref, n0_ref, m0_ref, gout_ref,
                 ck_ref, cv_ref, ga1_ref, sc2_ref, sh2_ref, ga2_ref, g2_ref, wo_ref, wu_ref, wd_ref,
                 y_ref, ko_ref, vo_ref, c_out_ref, n_out_ref, m_out_ref,
                 mq_s, mk_s, mv_s, og_s, gt_s, aq_s, ak_s, av_s, mix_s, caug_s, m_s):
    ns, t, _ = x_ref.shape
    cached = ck_ref.shape[1]
    proj_s = {"mq": mq_s, "mk": mk_s, "mv": mv_s, "og": og_s, "gt": gt_s,
              "aq": aq_s, "ak": ak_s, "av": av_s}

    def store(name, value):
        proj_s[name][...] = value if name == "gt" else value.reshape(ns, t, value.shape[-1])

    _in_proj_tile(x_ref[...], sc1_ref[...], sh1_ref[...], g1_ref[...], wa_ref, wb_ref, wg_ref,
                  bg_ref[...], gq_ref[...], gk_ref[...], pq_ref, pk_ref, store, t)

    seq = lambda ref, b: ref.at[pl.ds(b, 1)]
    for b in range(ns):
        caug_b, m_b = caug_s.at[b], m_s.at[b]
        _mlstm_load_state(seq(c0_ref, b), seq(n0_ref, b), seq(m0_ref, b), caug_b, m_b)

        def write_h(rows, lanes, value, b=b):
            mix_s[b * t + rows.start:b * t + rows.stop, lanes] = value

        _mlstm_tile(seq(mq_s, b), seq(mk_s, b), seq(mv_s, b), seq(og_s, b),
                    _LaneWindow(gt_s, b * t), gout_ref, caug_b, m_b, write_h, t)
        _mlstm_store_state(caug_b, m_b, seq(c_out_ref, b), seq(n_out_ref, b), seq(m_out_ref, b))

    tk = 2 * LANES
    pad = jnp.zeros((tk - cached - t, A_KW), F32)
    q_pos = PAST_LEN + lax.broadcasted_iota(jnp.int32, (t, 1), 0)
    idx = lax.broadcasted_iota(jnp.int32, (1, tk), 1)
    k_pos = jnp.where(idx < cached + t, PAST_LEN - cached + idx, -1)
    biases = _swa_biases(q_pos, k_pos)
    for b in range(ns):
        k_new, v_new = ak_s[b], av_s[b]

        def write_o(lanes, value, b=b):
            mix_s[b * t:(b + 1) * t, M_W + lanes.start:M_W + lanes.stop] = value

        _swa_block(aq_s[b], _kv_halves(jnp.concatenate([ck_ref[b], k_new, pad], axis=0)),
                   _kv_halves(jnp.concatenate([cv_ref[b], v_new, pad], axis=0)),
                   biases, sink_ref, write_o)
        ko_ref[b] = jnp.concatenate([ck_ref[b, t:, :], k_new], axis=0)
        vo_ref[b] = jnp.concatenate([cv_ref[b, t:, :], v_new], axis=0)

    y_ref[...] = _out_ffn_tile(
        x_ref[...], mix_s[...], ga1_ref[...], sc2_ref[...], sh2_ref[...], ga2_ref[...],
        g2_ref[...], wo_ref, wu_ref, wd_ref)


def _sample_layer(x, mod4, sinks, c0, n0, m0, gout, cache_k, cache_v, proj_w, ffn_w):
    ns, t, d = x.shape
    cached = cache_k.shape[1]
    assert cached >= t
    whole = lambda shape: pl.BlockSpec(shape, lambda i: (0,) * len(shape))
    mod = lambda k: pl.BlockSpec((ns, None, 1, d), lambda i: (0, k, 0, 0))
    st_c, st_n, st_m = (whole((ns, M_HEADS, M_DIM, M_DIM)), whole((ns, M_HEADS, M_DIM)),
                        whole((ns, M_HEADS, 1)))
    kv = whole((ns, cached, A_KW))
    return pl.pallas_call(
        _sample_body,
        grid=(1,),
        in_specs=([pl.BlockSpec(memory_space=pltpu.SMEM), whole((ns, t, d)),
                   mod(MOD_SC1), mod(MOD_SH1)]
                  + _proj_weight_specs(d)
                  + [st_c, st_n, st_m, _const_spec((M_HEADS, M_DIM)), kv, kv,
                     mod(MOD_GA1), mod(MOD_SC2), mod(MOD_SH2), mod(MOD_GA2)]
                  + _ffn_weight_specs(d)),
        out_specs=(whole((ns, t, d)), kv, kv, st_c, st_n, st_m),
        out_shape=(jax.ShapeDtypeStruct((ns, t, d), F32),
                   jax.ShapeDtypeStruct((ns, cached, A_KW), F32),
                   jax.ShapeDtypeStruct((ns, cached, A_KW), F32),
                   jax.ShapeDtypeStruct((ns, M_HEADS, M_DIM, M_DIM), F32),
                   jax.ShapeDtypeStruct((ns, M_HEADS, M_DIM), F32),
                   jax.ShapeDtypeStruct((ns, M_HEADS, 1), F32)),
        scratch_shapes=[pltpu.VMEM((ns, t, M_W), BF16), pltpu.VMEM((ns, t, M_W), BF16),
                        pltpu.VMEM((ns, t, M_W), BF16), pltpu.VMEM((ns, t, M_W), BF16),
                        pltpu.VMEM((GATE_ROWS, ns * t), F32), pltpu.VMEM((ns, t, A_QW), BF16),
                        pltpu.VMEM((ns, t, A_KW), F32), pltpu.VMEM((ns, t, A_KW), F32),
                        pltpu.VMEM((ns * t, M_W + A_QW), BF16),
                        pltpu.VMEM((ns, M_HEADS, M_DIM, 2 * M_DIM), F32),
                        pltpu.VMEM((ns, M_HEADS, 1), F32)],
        compiler_params=_params(("arbitrary",)),
        name="sample_layer",
    )(sinks, x, mod4, mod4, *proj_w, c0, n0, m0, gout, cache_k, cache_v,
      mod4, mod4, mod4, mod4, *ffn_w)


def _block_ones(n):
    g = jnp.arange(n, dtype=jnp.int32) // A_DIM
    return (g[:, None] == g[None, :]).astype(BF16)


def kernel(x_prompt, x_sample, c_prompt, c_sample, cache_swa_k, cache_swa_v, state_mlstm_C,
           state_mlstm_n, state_mlstm_m, w_ada, b_ada, g_norm1, w_in, b_gates, g_q, g_k, sinks,
           g_mlstm_out, w_out, g_norm2, w_up, w_down):
    depth = w_ada.shape[0]
    assert depth == 1
    bp, lp, d = x_prompt.shape
    bs, ls, _ = x_sample.shape
    wb = cache_swa_k.shape[2]

    xp, xs = x_prompt, x_sample
    for l in range(depth):
        wt = w_in[l].T
        g_lo = 4 * M_W
        g_hi = g_lo + 2 * M_HEADS
        wg = jnp.concatenate([wt[g_lo:g_hi], jnp.zeros((BF16_ROWS - 2 * M_HEADS, d), wt.dtype)])
        gq_t = (jnp.tile(g_q[l], A_HEADS) * (LOG2E * A_DIM ** -0.5))[None, :]
        proj_w = (g_norm1[l][None, :], wt[:g_lo].astype(BF16), wt[g_hi:].astype(BF16),
                  wg.astype(BF16), b_gates[l][:, None], gq_t,
                  jnp.tile(g_k[l], A_KV_HEADS)[None, :], _block_ones(A_QW), _block_ones(A_KW))
        ffn_w = (g_norm2[l][None, :], w_out[l].astype(BF16), w_up[l].astype(BF16),
                 w_down[l].astype(BF16))

        mod_rows = -(-(bp + bs) // 8) * 8
        c_all = jnp.concatenate([c_sample, c_prompt, jnp.zeros((mod_rows - bp - bs, d), F32)],
                                axis=0)
        mod4 = _ada_mod(c_all, w_ada[l], b_ada[l][None, :]).reshape(mod_rows, 6, 1, d)

        xp, kt, vt, c, n, m = _layer(xp, mod4, bs, sinks[l], g_mlstm_out[l], proj_w, ffn_w,
                                     rows=ROW_TILE, chunk=MLSTM_CHUNK)
        win = WIN_CHUNKS * CHUNK
        st_p = (kt.reshape(bp, win, A_KV_HEADS, A_DIM), vt.reshape(bp, win, A_KV_HEADS, A_DIM),
                c, n, m[..., 0])

        xs, ko, vo, c, n, m = _sample_layer(
            xs, mod4, sinks[l], state_mlstm_C[l], state_mlstm_n[l], state_mlstm_m[l][..., None],
            g_mlstm_out[l], cache_swa_k[l].reshape(bs, wb, A_KW),
            cache_swa_v[l].reshape(bs, wb, A_KW), proj_w, ffn_w)
        st_s = (ko.reshape(bs, wb, A_KV_HEADS, A_DIM), vo.reshape(bs, wb, A_KV_HEADS, A_DIM),
                c, n, m[..., 0])

    return (xp, xs) + tuple(s[None] for s in st_p) + tuple(s[None] for s in st_s)
```

```python
import functools

import jax
import jax.numpy as jnp
from jax import lax
from jax.experimental import pallas as pl
from jax.experimental.pallas import tpu as pltpu

F32 = jnp.float32
BF16 = jnp.bfloat16

D_MODEL = 1024
CHUNK = 64
M_HEADS = 4
M_DIM = 128
M_W = M_HEADS * M_DIM
A_HEADS = 8
A_KV_HEADS = 2
A_DIM = 64
A_QW = A_HEADS * A_DIM
A_KW = A_KV_HEADS * A_DIM
WIN_CHUNKS = 2
D_FF = 4 * D_MODEL
EPS = 1e-6
PAST_LEN = 4096

LANES = 128
BF16_ROWS = 16
VMEM_LIMIT = 62 * 1024 * 1024
NEG_INF = float("-inf")
LOG2E = 1.4426950408889634
ROW_TILE = 512
MLSTM_CHUNK = 128
FF_BLOCK = 1024
NT_DIMS = (((1,), (1,)), ((), ()))
MOD_SH1, MOD_SC1, MOD_GA1, MOD_SH2, MOD_SC2, MOD_GA2 = range(6)


def _params(sem):
    return pltpu.CompilerParams(dimension_semantics=sem, vmem_limit_bytes=VMEM_LIMIT)


def _const_spec(shape):
    nd = len(shape)
    return pl.BlockSpec(shape, lambda *_: (0,) * nd, pipeline_mode=pl.Buffered(1))


def _ada_body(c_ref, w_ref, b_ref, o_ref):
    c = c_ref[...]
    s = c * jax.nn.sigmoid(c)
    o_ref[...] = jnp.dot(s.astype(BF16), w_ref[...].astype(BF16),
                         preferred_element_type=F32) + b_ref[...]


def _ada_mod(c_all, w_ada, b_ada):
    rows = c_all.shape[0]
    return pl.pallas_call(
        _ada_body,
        grid=(6,),
        in_specs=[pl.BlockSpec((rows, D_MODEL), lambda j: (0, 0)),
                  pl.BlockSpec((D_MODEL, D_MODEL), lambda j: (0, j)),
                  pl.BlockSpec((1, D_MODEL), lambda j: (0, j))],
        out_specs=pl.BlockSpec((rows, D_MODEL), lambda j: (0, j)),
        out_shape=jax.ShapeDtypeStruct((rows, 6 * D_MODEL), F32),
        compiler_params=_params(("arbitrary",)),
        name="ada_mod",
    )(c_all, w_ada, b_ada)


def _segment_prefix(x, op, identity, seg):
    assert x.shape[1] % LANES == 0 and x.shape[1] % seg == 0
    pos = lax.broadcasted_iota(jnp.int32, x.shape, 1) % seg
    d = 1
    while d < seg:
        x = op(x, jnp.where(pos >= d, pltpu.roll(x, d, axis=1), identity))
        d *= 2
    return x


GATE_ROWS = 24
PROJ_OUTPUTS = ("mq", "mk", "mv", "og", "gt", "aq", "ak", "av")


def _in_proj_tile(x, sc, sh, g1, wa_ref, wb_ref, wg_ref, bg, gq, gk, pq_ref, pk_ref, store,
                  gate_chunk):
    s, r, d = x.shape
    m = s * r
    ms = jnp.mean(x * x, axis=-1, keepdims=True)
    y = x * lax.rsqrt(ms + EPS) * g1
    h = y * (1.0 + sc) + sh
    hb = h.reshape(m, d).astype(BF16)

    def proj(w_ref, lo, width):
        return lax.dot_general(hb, w_ref[lo:lo + width, :], NT_DIMS, preferred_element_type=F32)

    zg = lax.dot_general(wg_ref[...], hb, NT_DIMS,
                         preferred_element_type=F32)[:2 * M_HEADS] + bg
    row = lax.broadcasted_iota(jnp.int32, zg.shape, 0)
    log_sig = jnp.minimum(zg, 0.0) - jnp.log1p(jnp.exp(-jnp.abs(zg)))
    gt = jnp.where(row < M_HEADS, zg, log_sig)
    b8 = _segment_prefix(gt, jnp.add, 0.0, gate_chunk)
    c8 = gt - pltpu.roll(b8, M_HEADS, axis=0)
    cm8 = _segment_prefix(c8, jnp.maximum, NEG_INF, gate_chunk)
    store("gt", jnp.concatenate([c8, b8, cm8], axis=0))

    store("mq", proj(wa_ref, 0, M_W).astype(BF16))
    store("mk", (proj(wa_ref, M_W, M_W) * (M_DIM ** -0.5)).astype(BF16))
    store("mv", proj(wa_ref, 2 * M_W, M_W).astype(BF16))
    store("og", jax.nn.sigmoid(proj(wa_ref, 3 * M_W, M_W)).astype(BF16))

    aq = proj(wb_ref, 0, A_QW)
    ss_q = jnp.dot((aq * aq).astype(BF16), pq_ref[...], preferred_element_type=F32)
    store("aq", (aq * lax.rsqrt(ss_q * (1.0 / A_DIM) + EPS) * gq).astype(BF16))
    ak = proj(wb_ref, A_QW, A_KW)
    sq = ak * ak
    hi = sq.astype(BF16)
    lo = (sq - hi.astype(F32)).astype(BF16)
    ss_k = (jnp.dot(hi, pk_ref[...], preferred_element_type=F32)
            + jnp.dot(lo, pk_ref[...], preferred_element_type=F32))
    store("ak", ak * lax.rsqrt(ss_k * (1.0 / A_DIM) + EPS) * gk)
    store("av", proj(wb_ref, A_QW + A_KW, A_KW))


def _in_proj_body(x_ref, sc_ref, sh_ref, g1_ref, wa_ref, wb_ref, wg_ref, bg_ref, gq_ref, gk_ref,
                  pq_ref, pk_ref, *out_refs, gate_chunk):
    s, r, _ = x_ref.shape
    outs = dict(zip(PROJ_OUTPUTS, out_refs))

    def store(name, value):
        ref = outs[name]
        ref[...] = value if name == "gt" else value.reshape(s, r, value.shape[-1])

    _in_proj_tile(x_ref[...], sc_ref[...], sh_ref[...], g1_ref[...], wa_ref, wb_ref, wg_ref,
                  bg_ref[...], gq_ref[...], gk_ref[...], pq_ref, pk_ref, store, gate_chunk)


def _proj_weight_specs(d):
    return [_const_spec((1, d)), _const_spec((4 * M_W, d)), _const_spec((A_QW + 2 * A_KW, d)),
            _const_spec((BF16_ROWS, d)), _const_spec((2 * M_HEADS, 1)),
            _const_spec((1, A_QW)), _const_spec((1, A_KW)),
            _const_spec((A_QW, A_QW)), _const_spec((A_KW, A_KW))]


def _in_proj(x, sc, sh, proj_w, *, seqs, rows, gate_chunk):
    ns, length, d = x.shape
    grid = (ns // seqs, length // rows)
    m = seqs * rows
    n_tok = ns * length

    def tile(width):
        return pl.BlockSpec((seqs, rows, width), lambda i, j: (i, j, 0))

    mod = pl.BlockSpec((seqs, 1, d), lambda i, j: (i, 0, 0))
    steps_per_seq = length // rows
    out_shapes = (
        jax.ShapeDtypeStruct((ns, length, M_W), BF16),
        jax.ShapeDtypeStruct((ns, length, M_W), BF16),
        jax.ShapeDtypeStruct((ns, length, M_W), BF16),
        jax.ShapeDtypeStruct((ns, length, M_W), BF16),
        jax.ShapeDtypeStruct((GATE_ROWS, n_tok), F32),
        jax.ShapeDtypeStruct((ns, length, A_QW), BF16),
        jax.ShapeDtypeStruct((ns, length, A_KW), F32),
        jax.ShapeDtypeStruct((ns, length, A_KW), F32),
    )
    return pl.pallas_call(
        functools.partial(_in_proj_body, gate_chunk=gate_chunk),
        grid=grid,
        in_specs=[tile(d), mod, mod] + _proj_weight_specs(d),
        out_specs=(tile(M_W), tile(M_W), tile(M_W), tile(M_W),
                   pl.BlockSpec((GATE_ROWS, m), lambda i, j: (0, i * steps_per_seq + j)),
                   tile(A_QW), tile(A_KW), tile(A_KW)),
        out_shape=out_shapes,
        compiler_params=_params(("arbitrary", "arbitrary")),
        name="in_proj",
    )(x, sc, sh, *proj_w)


def _col_bcast(r):
    t = r.shape[1]
    if t % LANES == 0:
        return jnp.broadcast_to(r, (LANES, t)).T
    src = lax.broadcasted_iota(jnp.int32, (t, t), 0)
    dst = lax.broadcasted_iota(jnp.int32, (t, t), 1)
    col = jnp.sum(jnp.where(src == dst, jnp.broadcast_to(r, (t, t)), 0.0), axis=1, keepdims=True)
    return jnp.broadcast_to(col, (t, LANES))


def _eye_dim():
    return (lax.broadcasted_iota(jnp.int32, (M_DIM, M_DIM), 0)
            == lax.broadcasted_iota(jnp.int32, (M_DIM, M_DIM), 1))


def _mlstm_load_state(c0_ref, n0_ref, m0_ref, caug_ref, m_ref):
    for hd in range(M_HEADS):
        caug_ref[hd, :, :M_DIM] = c0_ref[0, hd]
        n_row = jnp.broadcast_to(n0_ref[0, hd:hd + 1, :], (M_DIM, M_DIM))
        n_col = jnp.sum(jnp.where(_eye_dim(), n_row, 0.0), axis=1, keepdims=True)
        caug_ref[hd, :, M_DIM:] = jnp.broadcast_to(n_col, (M_DIM, M_DIM))
    m_ref[...] = m0_ref[0]


def _mlstm_store_state(caug_ref, m_ref, c_out_ref, n_out_ref, m_out_ref):
    for hd in range(M_HEADS):
        c_out_ref[0, hd] = caug_ref[hd, :, :M_DIM]
        n_out_ref[0, hd:hd + 1, :] = jnp.sum(
            jnp.where(_eye_dim(), caug_ref[hd, :, M_DIM:], 0.0), axis=0, keepdims=True)
    m_out_ref[0] = m_ref[...]


def _mlstm_tile(q_ref, k_ref, v_ref, og_ref, gt_ref, gout_ref, caug_ref, m_ref, write_h, chunk):
    t = chunk
    n_chunks = q_ref.shape[1] // t
    bs = LANES if t % LANES == 0 else t
    tril = (lax.broadcasted_iota(jnp.int32, (bs, bs), 1)
            <= lax.broadcasted_iota(jnp.int32, (bs, bs), 0))
    m_prev = m_ref[...]
    for j in range(n_chunks):
        rows = slice(j * t, (j + 1) * t)
        c = gt_ref[0:M_HEADS, rows]
        b = gt_ref[12:12 + M_HEADS, rows]
        g = jnp.maximum(gt_ref[16:16 + M_HEADS, rows], m_prev)
        g_last = g[:, t - 1:t]
        c2 = c * LOG2E
        g2 = g * LOG2E
        e_all = jnp.exp2(-LOG2E * (b + g))
        w_src_all = jnp.exp2(c2 - LOG2E * g_last)
        w_inter_all = jnp.exp2(LOG2E * m_prev - g2)
        w_old_all = jnp.exp2(LOG2E * (m_prev - g_last))
        m_prev = b[:, t - 1:t] + g_last

        for hd in range(M_HEADS):
            sl = slice(hd * M_DIM, (hd + 1) * M_DIM)
            hr = slice(hd, hd + 1)
            g_b = _col_bcast(g2[hr])
            e_b = _col_bcast(e_all[hr])
            w_src = _col_bcast(w_src_all[hr])
            w_inter = _col_bcast(w_inter_all[hr])

            q = q_ref[0, rows, sl]
            k = k_ref[0, rows, sl]
            v = v_ref[0, rows, sl]
            caug = caug_ref[hd]
            inter = jnp.dot(q, caug.astype(BF16), preferred_element_type=F32)
            gain = gout_ref[hr, :]
            for rb in range(t // bs):
                rsl = slice(rb * bs, (rb + 1) * bs)
                ncols = (rb + 1) * bs
                s_mat = lax.dot_general(q[rsl], k[:ncols], NT_DIMS,
                                        preferred_element_type=F32)
                decay = []
                for cb in range(rb + 1):
                    arg = c2[hr, cb * bs:(cb + 1) * bs] - g_b[rsl, :bs]
                    decay.append(jnp.exp2(jnp.where(tril, arg, NEG_INF) if cb == rb else arg))
                s_mat = s_mat * (decay[0] if rb == 0 else jnp.concatenate(decay, axis=1))
                num = (jnp.dot(s_mat.astype(BF16), v[:ncols], preferred_element_type=F32)
                       + w_inter[rsl] * inter[rsl, :M_DIM])
                den = jnp.sum(s_mat, axis=1, keepdims=True) + w_inter[rsl] * inter[rsl, M_DIM:]
                hh = num / jnp.maximum(jnp.abs(den), e_b[rsl])
                hn = hh * lax.rsqrt(jnp.mean(hh * hh, axis=-1, keepdims=True) + EPS) * gain
                out_rows = slice(j * t + rb * bs, j * t + (rb + 1) * bs)
                write_h(out_rows, sl, (hn * og_ref[0, out_rows, sl].astype(F32)).astype(BF16))

            rhs = jnp.concatenate([v.astype(F32) * w_src, w_src], axis=1).astype(BF16)
            upd = lax.dot_general(k, rhs, (((0,), (0,)), ((), ())), preferred_element_type=F32)
            caug_ref[hd] = w_old_all[hr] * caug + upd
    m_ref[...] = m_prev


def _mlstm_body(q_ref, k_ref, v_ref, og_ref, gt_ref, c0_ref, n0_ref, m0_ref, gout_ref,
                h_ref, c_out_ref, n_out_ref, m_out_ref, caug_ref, m_ref, *, chunk):
    ci = pl.program_id(1)

    @pl.when(ci == 0)
    def _():
        _mlstm_load_state(c0_ref, n0_ref, m0_ref, caug_ref, m_ref)

    def write_h(rows, lanes, value):
        h_ref[0, rows, lanes] = value

    _mlstm_tile(q_ref, k_ref, v_ref, og_ref, gt_ref, gout_ref, caug_ref, m_ref, write_h, chunk)

    @pl.when(ci == pl.num_programs(1) - 1)
    def _():
        _mlstm_store_state(caug_ref, m_ref, c_out_ref, n_out_ref, m_out_ref)


def _mlstm(mq, mk, mv, og, gt, c0, n0, m0, gout, *, chunk):
    ns, length, _ = mq.shape
    assert length == chunk
    tile = pl.BlockSpec((1, chunk, M_W), lambda b, c: (b, c, 0))
    gt_spec = pl.BlockSpec((None, GATE_ROWS, chunk), lambda b, c: (b, 0, 0))
    st_c = pl.BlockSpec((1, M_HEADS, M_DIM, M_DIM), lambda b, c: (b, 0, 0, 0))
    st_n = pl.BlockSpec((1, M_HEADS, M_DIM), lambda b, c: (b, 0, 0))
    st_m = pl.BlockSpec((1, M_HEADS, 1), lambda b, c: (b, 0, 0))
    return pl.pallas_call(
        functools.partial(_mlstm_body, chunk=chunk),
        grid=(ns, 1),
        in_specs=[tile, tile, tile, tile, gt_spec, st_c, st_n, st_m,
                  _const_spec((M_HEADS, M_DIM))],
        out_specs=(tile, st_c, st_n, st_m),
        out_shape=(jax.ShapeDtypeStruct((ns, length, M_W), BF16),
                   jax.ShapeDtypeStruct((ns, M_HEADS, M_DIM, M_DIM), F32),
                   jax.ShapeDtypeStruct((ns, M_HEADS, M_DIM), F32),
                   jax.ShapeDtypeStruct((ns, M_HEADS, 1), F32)),
        scratch_shapes=[pltpu.VMEM((M_HEADS, M_DIM, 2 * M_DIM), F32),
                        pltpu.VMEM((M_HEADS, 1), F32)],
        compiler_params=_params(("arbitrary", "arbitrary")),
        name="mlstm",
    )(mq, mk, mv, og, gt, c0, n0, m0, gout)


def _kv_halves(a):
    low = lax.broadcasted_iota(jnp.int32, a.shape, 1) < A_DIM
    lo_half = jnp.where(low, a, 0.0)
    hi_half = jnp.where(low, 0.0, a)
    pairs = [(lo_half, pltpu.roll(lo_half, A_DIM, axis=1)),
             (pltpu.roll(hi_half, A_DIM, axis=1), hi_half)]
    return [tuple(h.astype(BF16) for h in pair) for pair in pairs]


def _swa_biases(q_pos, k_pos):
    shift = CHUNK.bit_length() - 1
    assert 1 << shift == CHUNK
    qc = lax.shift_right_arithmetic(q_pos, shift)
    kc = lax.shift_right_arithmetic(k_pos, shift)
    allowed = (kc <= qc) & (kc >= qc - WIN_CHUNKS) & (k_pos >= 0)
    dist = jnp.abs(q_pos - k_pos).astype(F32)
    slopes = [2.0 ** (-8.0 * (head + 1) / A_HEADS) for head in range(A_HEADS)]
    return [jnp.where(allowed, (-LOG2E * slope) * dist, NEG_INF) for slope in slopes]


def _swa_block(q, kz, vz, biases, sink_ref, write_o, valid=None):
    tq = q.shape[0]
    pairs_per_kv = A_HEADS // A_KV_HEADS // 2
    for j in range(A_KV_HEADS):
        pairs = [j * pairs_per_kv + i for i in range(pairs_per_kv)]
        q_stack = jnp.concatenate([q[:, p * LANES:(p + 1) * LANES] for p in pairs], axis=0)
        acc = [None] * pairs_per_kv
        for e in range(2):
            logits_all = lax.dot_general(q_stack, kz[j][e], NT_DIMS, preferred_element_type=F32)
            probs, denoms = [], []
            for i, p in enumerate(pairs):
                head = 2 * p + e
                sink = sink_ref[head] * LOG2E
                logits = logits_all[i * tq:(i + 1) * tq] + biases[head]
                if valid is not None:
                    logits = jnp.where(valid, logits, NEG_INF)
                mx = jnp.maximum(jnp.max(logits, axis=1, keepdims=True), sink)
                pr = jnp.exp2(logits - mx)
                denoms.append(jnp.sum(pr, axis=1, keepdims=True) + jnp.exp2(sink - mx))
                probs.append(pr.astype(BF16))
            out_all = jnp.dot(jnp.concatenate(probs, axis=0), vz[j][e],
                              preferred_element_type=F32)
            for i in range(pairs_per_kv):
                part = out_all[i * tq:(i + 1) * tq] / denoms[i]
                acc[i] = part if acc[i] is None else acc[i] + part
        for i, p in enumerate(pairs):
            write_o(slice(p * LANES, (p + 1) * LANES), acc[i].astype(BF16))


SWA_Q_ROWS = WIN_CHUNKS * CHUNK


def _swa_tile(sink_ref, q_ref, k_all, v_all, tile_start, write_o):
    tq = SWA_Q_ROWS
    kz = _kv_halves(k_all)
    vz = _kv_halves(v_all)
    rel_k = lax.broadcasted_iota(jnp.int32, (1, 2 * tq), 1)
    biases = _swa_biases(tq + lax.broadcasted_iota(jnp.int32, (tq, 1), 0), rel_k)
    for r in range(q_ref.shape[1] // tq):
        rows = slice(r * tq, (r + 1) * tq)
        band = slice(r * tq, (r + 2) * tq)
        valid = (tile_start - tq + rel_k >= 0) if r == 0 else None
        _swa_block(q_ref[0, rows, :],
                   [tuple(h[band] for h in pair) for pair in kz],
                   [tuple(h[band] for h in pair) for pair in vz],
                   biases, sink_ref,
                   lambda lanes, value, rows=rows: write_o(rows, lanes, value), valid=valid)


def _swa_sample_body(sink_ref, q_ref, kn_ref, vn_ref, ck_ref, cv_ref, o_ref):
    tq = q_ref.shape[1]
    wb = ck_ref.shape[1]
    tk = 2 * LANES
    pad = jnp.zeros((tk - wb - tq, A_KW), F32)
    kz = _kv_halves(jnp.concatenate([ck_ref[0], kn_ref[0], pad], axis=0))
    vz = _kv_halves(jnp.concatenate([cv_ref[0], vn_ref[0], pad], axis=0))
    q_pos = PAST_LEN + lax.broadcasted_iota(jnp.int32, (tq, 1), 0)
    idx = lax.broadcasted_iota(jnp.int32, (1, tk), 1)
    k_pos = jnp.where(idx < wb + tq, PAST_LEN - wb + idx, -1)

    def write_o(lanes, value):
        o_ref[0, :, lanes] = value

    _swa_block(q_ref[0], kz, vz, _swa_biases(q_pos, k_pos), sink_ref, write_o)


def _swa_sample(aq, ak, av, cache_k, cache_v, sinks):
    ns, tq, _ = aq.shape
    wb = cache_k.shape[1]
    new = lambda w: pl.BlockSpec((1, tq, w), lambda b: (b, 0, 0))
    old = pl.BlockSpec((1, wb, A_KW), lambda b: (b, 0, 0))
    return pl.pallas_call(
        _swa_sample_body,
        grid=(ns,),
        in_specs=[pl.BlockSpec(memory_space=pltpu.SMEM),
                  new(A_QW), new(A_KW), new(A_KW), old, old],
        out_specs=new(A_QW),
        out_shape=jax.ShapeDtypeStruct((ns, tq, A_QW), BF16),
        compiler_params=_params(("arbitrary",)),
        name="swa_sample",
    )(sinks, aq, ak, av, cache_k, cache_v)


def _out_ffn_tile(x, mixed, ga1, sc2, sh2, ga2, g2, wo_ref, wu_ref, wd_ref):
    s, r, d = x.shape
    mix = jnp.dot(mixed, wo_ref[...], preferred_element_type=F32)
    x1 = x + ga1 * mix.reshape(s, r, d)
    ms = jnp.mean(x1 * x1, axis=-1, keepdims=True)
    h2 = (x1 * lax.rsqrt(ms + EPS) * g2) * (1.0 + sc2) + sh2
    hb = h2.reshape(s * r, d).astype(BF16)
    ffn = jnp.zeros((s * r, d), F32)
    for c in range(D_FF // FF_BLOCK):
        u = jnp.dot(hb, wu_ref[:, c * FF_BLOCK:(c + 1) * FF_BLOCK], preferred_element_type=F32)
        u = jnp.square(jnp.maximum(u, 0.0)).astype(BF16)
        ffn = ffn + jnp.dot(u, wd_ref[c * FF_BLOCK:(c + 1) * FF_BLOCK, :],
                            preferred_element_type=F32)
    return x1 + ga2 * ffn.reshape(s, r, d)


def _out_ffn_body(x_ref, hm_ref, at_ref, ga1_ref, sc2_ref, sh2_ref, ga2_ref, g2_ref,
                  wo_ref, wu_ref, wd_ref, y_ref):
    s, r, _ = x_ref.shape
    mixed = jnp.concatenate([hm_ref[...].reshape(s * r, M_W), at_ref[...].reshape(s * r, A_QW)],
                            axis=1)
    y_ref[...] = _out_ffn_tile(
        x_ref[...], mixed, ga1_ref[...], sc2_ref[...], sh2_ref[...], ga2_ref[...], g2_ref[...],
        wo_ref, wu_ref, wd_ref)


def _ffn_weight_specs(d):
    return [_const_spec((1, d)), _const_spec((d, d)), _const_spec((d, D_FF)),
            _const_spec((D_FF, d))]


def _out_ffn(x, hm, at, ga1, sc2, sh2, ga2, ffn_w, *, seqs, rows):
    ns, length, d = x.shape
    tile = lambda w: pl.BlockSpec((seqs, rows, w), lambda i, j: (i, j, 0))
    mod = pl.BlockSpec((seqs, 1, d), lambda i, j: (i, 0, 0))
    return pl.pallas_call(
        _out_ffn_body,
        grid=(ns // seqs, length // rows),
        in_specs=[tile(d), tile(M_W), tile(A_QW), mod, mod, mod, mod] + _ffn_weight_specs(d),
        out_specs=tile(d),
        out_shape=jax.ShapeDtypeStruct((ns, length, d), F32),
        compiler_params=_params(("arbitrary", "arbitrary")),
        name="out_ffn",
    )(x, hm, at, ga1, sc2, sh2, ga2, *ffn_w)


def _layer_body(sink_ref, xp_ref, sc1_ref, sh1_ref, g1_ref, wa_ref, wb_ref, wg_ref, bg_ref,
                gq_ref, gk_ref, pq_ref, pk_ref, gout_ref,
                xf_ref, ga1_ref, sc2_ref, sh2_ref, ga2_ref, g2_ref, wo_ref, wu_ref, wd_ref,
                y_ref, kt_ref, vt_ref, c_out_ref, n_out_ref, m_out_ref,
                mq_s, mk_s, mv_s, og_s, gt_s, aq_s, ak_s, av_s, kprev_s, vprev_s,
                mix_s, caug_ref, m_ref, *, n_tiles, tiles_per_seq, chunk):
    s = pl.program_id(0)
    rows = xp_ref.shape[1]
    cur = s % 2
    prv = 1 - cur
    tile = jnp.minimum(s, n_tiles - 1)
    j = tile % tiles_per_seq
    valid = s < n_tiles
    proj_s = {"mq": mq_s, "mk": mk_s, "mv": mv_s, "og": og_s, "gt": gt_s,
              "aq": aq_s, "ak": ak_s, "av": av_s}

    @pl.when(s == 0)
    def _():
        kprev_s[...] = jnp.zeros(kprev_s.shape, kprev_s.dtype)
        vprev_s[...] = jnp.zeros(vprev_s.shape, vprev_s.dtype)
        mix_s[1] = jnp.zeros(mix_s.shape[1:], mix_s.dtype)

    @pl.when(jnp.logical_and(j == 0, valid))
    def _():
        caug_ref[...] = jnp.zeros(caug_ref.shape, caug_ref.dtype)
        m_ref[...] = jnp.zeros(m_ref.shape, m_ref.dtype)

    y_ref[...] = _out_ffn_tile(
        xf_ref[...], mix_s[prv], ga1_ref[...], sc2_ref[...], sh2_ref[...], ga2_ref[...],
        g2_ref[...], wo_ref, wu_ref, wd_ref)

    def store(name, value):
        proj_s[name][...] = value if name == "gt" else value[None]

    _in_proj_tile(xp_ref[...], sc1_ref[...], sh1_ref[...], g1_ref[...], wa_ref, wb_ref, wg_ref,
                  bg_ref[...], gq_ref[...], gk_ref[...], pq_ref, pk_ref, store, chunk)

    def write_h(r, lanes, value):
        mix_s[cur, r, lanes] = value

    def write_o(r, lanes, value):
        mix_s[cur, r, slice(M_W + lanes.start, M_W + lanes.stop)] = value

    _mlstm_tile(mq_s, mk_s, mv_s, og_s, gt_s, gout_ref, caug_ref, m_ref, write_h, chunk)
    k_cur = ak_s[0]
    v_cur = av_s[0]
    _swa_tile(sink_ref, aq_s,
              jnp.concatenate([kprev_s[...], k_cur], axis=0),
              jnp.concatenate([vprev_s[...], v_cur], axis=0), j * rows, write_o)
    kprev_s[...] = k_cur[rows - SWA_Q_ROWS:]
    vprev_s[...] = v_cur[rows - SWA_Q_ROWS:]

    @pl.when(jnp.logical_and(j == tiles_per_seq - 1, valid))
    def _():
        kt_ref[0] = k_cur[rows - SWA_Q_ROWS:]
        vt_ref[0] = v_cur[rows - SWA_Q_ROWS:]
        _mlstm_store_state(caug_ref, m_ref, c_out_ref, n_out_ref, m_out_ref)


def _layer(x, mod4, mod_row0, sinks, gout, proj_w, ffn_w, *, rows, chunk):
    ns, length, d = x.shape
    tps = length // rows
    n_tiles = ns * tps

    def stage(lag):
        def tile_idx(i):
            t = jnp.clip(i - lag, 0, n_tiles - 1)
            return t // tps, t % tps
        x_tile = pl.BlockSpec((1, rows, d), lambda i: (*tile_idx(i), 0))
        mod = lambda k: pl.BlockSpec((1, None, 1, d),
                                     lambda i: (mod_row0 + tile_idx(i)[0], k, 0, 0))
        per_seq = lambda shape: pl.BlockSpec(
            (1,) + shape, lambda i: (tile_idx(i)[0],) + (0,) * len(shape))
        return x_tile, mod, per_seq

    xp_tile, p_mod, p_seq = stage(0)
    xf_tile, f_mod, _ = stage(1)
    st_c, st_n, st_m = p_seq((M_HEADS, M_DIM, M_DIM)), p_seq((M_HEADS, M_DIM)), p_seq((M_HEADS, 1))
    tail = p_seq((SWA_Q_ROWS, A_KW))
    one = lambda shape, dtype: pltpu.VMEM((1,) + shape, dtype)
    return pl.pallas_call(
        functools.partial(_layer_body, n_tiles=n_tiles, tiles_per_seq=tps, chunk=chunk),
        grid=(n_tiles + 1,),
        in_specs=([pl.BlockSpec(memory_space=pltpu.SMEM), xp_tile, p_mod(MOD_SC1), p_mod(MOD_SH1)]
                  + _proj_weight_specs(d)
                  + [_const_spec((M_HEADS, M_DIM)), xf_tile,
                     f_mod(MOD_GA1), f_mod(MOD_SC2), f_mod(MOD_SH2), f_mod(MOD_GA2)]
                  + _ffn_weight_specs(d)),
        out_specs=(xf_tile, tail, tail, st_c, st_n, st_m),
        out_shape=(jax.ShapeDtypeStruct((ns, length, d), F32),
                   jax.ShapeDtypeStruct((ns, SWA_Q_ROWS, A_KW), F32),
                   jax.ShapeDtypeStruct((ns, SWA_Q_ROWS, A_KW), F32),
                   jax.ShapeDtypeStruct((ns, M_HEADS, M_DIM, M_DIM), F32),
                   jax.ShapeDtypeStruct((ns, M_HEADS, M_DIM), F32),
                   jax.ShapeDtypeStruct((ns, M_HEADS, 1), F32)),
        scratch_shapes=[one((rows, M_W), BF16), one((rows, M_W), BF16),
                        one((rows, M_W), BF16), one((rows, M_W), BF16),
                        pltpu.VMEM((GATE_ROWS, rows), F32), one((rows, A_QW), BF16),
                        one((rows, A_KW), F32), one((rows, A_KW), F32),
                        pltpu.VMEM((SWA_Q_ROWS, A_KW), F32), pltpu.VMEM((SWA_Q_ROWS, A_KW), F32),
                        pltpu.VMEM((2, rows, M_W + A_QW), BF16),
                        pltpu.VMEM((M_HEADS, M_DIM, 2 * M_DIM), F32),
                        pltpu.VMEM((M_HEADS, 1), F32)],
        compiler_params=_params(("arbitrary",)),
        name="layer",
    )(sinks, x, mod4, mod4, *proj_w, gout, x, mod4, mod4, mod4, mod4, *ffn_w)


class _LaneWindow:
    def __init__(self, ref, offset):
        self.ref, self.offset = ref, offset

    def __getitem__(self, idx):
        rows, cols = idx
        return self.ref[rows, slice(cols.start + self.offset, cols.stop + self.offset)]


def _sample_body(sink_ref, x_ref, sc1_ref, sh1_ref, g1_ref, wa_ref, wb_ref, wg_ref, bg_ref,
                 gq_ref, gk_ref, pq_ref, pk_ref, c0_ref, n0_ref, m0_ref, gout_ref,
                 ck_ref, cv_ref, ga1_ref, sc2_ref, sh2_ref, ga2_ref, g2_ref, wo_ref, wu_ref, wd_ref,
                 y_ref, ko_ref, vo_ref, c_out_ref, n_out_ref, m_out_ref,
                 mq_s, mk_s, mv_s, og_s, gt_s, aq_s, ak_s, av_s, mix_s, caug_s, m_s):
    ns, t, _ = x_ref.shape
    cached = ck_ref.shape[1]
    proj_s = {"mq": mq_s, "mk": mk_s, "mv": mv_s, "og": og_s, "gt": gt_s,
              "aq": aq_s, "ak": ak_s, "av": av_s}

    def store(name, value):
        proj_s[name][...] = value if name == "gt" else value.reshape(ns, t, value.shape[-1])

    _in_proj_tile(x_ref[...], sc1_ref[...], sh1_ref[...], g1_ref[...], wa_ref, wb_ref, wg_ref,
                  bg_ref[...], gq_ref[...], gk_ref[...], pq_ref, pk_ref, store, t)

    seq = lambda ref, b: ref.at[pl.ds(b, 1)]
    for b in range(ns):
        caug_b, m_b = caug_s.at[b], m_s.at[b]
        _mlstm_load_state(seq(c0_ref, b), seq(n0_ref, b), seq(m0_ref, b), caug_b, m_b)

        def write_h(rows, lanes, value, b=b):
            mix_s[b * t + rows.start:b * t + rows.stop, lanes] = value

        _mlstm_tile(seq(mq_s, b), seq(mk_s, b), seq(mv_s, b), seq(og_s, b),
                    _LaneWindow(gt_s, b * t), gout_ref, caug_b, m_b, write_h, t)
        _mlstm_store_state(caug_b, m_b, seq(c_out_ref, b), seq(n_out_ref, b), seq(m_out_ref, b))

    tk = 2 * LANES
    pad = jnp.zeros((tk - cached - t, A_KW), F32)
    q_pos = PAST_LEN + lax.broadcasted_iota(jnp.int32, (t, 1), 0)
    idx = lax.broadcasted_iota(jnp.int32, (1, tk), 1)
    k_pos = jnp.where(idx < cached + t, PAST_LEN - cached + idx, -1)
    biases = _swa_biases(q_pos, k_pos)
    for b in range(ns):
        k_new, v_new = ak_s[b], av_s[b]

        def write_o(lanes, value, b=b):
            mix_s[b * t:(b + 1) * t, M_W + lanes.start:M_W + lanes.stop] = value

        _swa_block(aq_s[b], _kv_halves(jnp.concatenate([ck_ref[b], k_new, pad], axis=0)),
                   _kv_halves(jnp.concatenate([cv_ref[b], v_new, pad], axis=0)),
                   biases, sink_ref, write_o)
        ko_ref[b] = jnp.concatenate([ck_ref[b, t:, :], k_new], axis=0)
        vo_ref[b] = jnp.concatenate([cv_ref[b, t:, :], v_new], axis=0)

    y_ref[...] = _out_ffn_tile(
        x_ref[...], mix_s[...], ga1_ref[...], sc2_ref[...], sh2_ref[...], ga2_ref[...],
        g2_ref[...], wo_ref, wu_ref, wd_ref)


def _sample_layer(x, mod4, sinks, c0, n0, m0, gout, cache_k, cache_v, proj_w, ffn_w):
    ns, t, d = x.shape
    cached = cache_k.shape[1]
    assert cached >= t
    whole = lambda shape: pl.BlockSpec(shape, lambda i: (0,) * len(shape))
    mod = lambda k: pl.BlockSpec((ns, None, 1, d), lambda i: (0, k, 0, 0))
    st_c, st_n, st_m = (whole((ns, M_HEADS, M_DIM, M_DIM)), whole((ns, M_HEADS, M_DIM)),
                        whole((ns, M_HEADS, 1)))
    kv = whole((ns, cached, A_KW))
    return pl.pallas_call(
        _sample_body,
        grid=(1,),
        in_specs=([pl.BlockSpec(memory_space=pltpu.SMEM), whole((ns, t, d)),
                   mod(MOD_SC1), mod(MOD_SH1)]
                  + _proj_weight_specs(d)
                  + [st_c, st_n, st_m, _const_spec((M_HEADS, M_DIM)), kv, kv,
                     mod(MOD_GA1), mod(MOD_SC2), mod(MOD_SH2), mod(MOD_GA2)]
                  + _ffn_weight_specs(d)),
        out_specs=(whole((ns, t, d)), kv, kv, st_c, st_n, st_m),
        out_shape=(jax.ShapeDtypeStruct((ns, t, d), F32),
                   jax.ShapeDtypeStruct((ns, cached, A_KW), F32),
                   jax.ShapeDtypeStruct((ns, cached, A_KW), F32),
                   jax.ShapeDtypeStruct((ns, M_HEADS, M_DIM, M_DIM), F32),
                   jax.ShapeDtypeStruct((ns, M_HEADS, M_DIM), F32),
                   jax.ShapeDtypeStruct((ns, M_HEADS, 1), F32)),
        scratch_shapes=[pltpu.VMEM((ns, t, M_W), BF16), pltpu.VMEM((ns, t, M_W), BF16),
                        pltpu.VMEM((ns, t, M_W), BF16), pltpu.VMEM((ns, t, M_W), BF16),
                        pltpu.VMEM((GATE_ROWS, ns * t), F32), pltpu.VMEM((ns, t, A_QW), BF16),
                        pltpu.VMEM((ns, t, A_KW), F32), pltpu.VMEM((ns, t, A_KW), F32),
                        pltpu.VMEM((ns * t, M_W + A_QW), BF16),
                        pltpu.VMEM((ns, M_HEADS, M_DIM, 2 * M_DIM), F32),
                        pltpu.VMEM((ns, M_HEADS, 1), F32)],
        compiler_params=_params(("arbitrary",)),
        name="sample_layer",
    )(sinks, x, mod4, mod4, *proj_w, c0, n0, m0, gout, cache_k, cache_v,
      mod4, mod4, mod4, mod4, *ffn_w)


def _block_ones(n):
    g = jnp.arange(n, dtype=jnp.int32) // A_DIM
    return (g[:, None] == g[None, :]).astype(BF16)


def kernel(x_prompt, x_sample, c_prompt, c_sample, cache_swa_k, cache_swa_v, state_mlstm_C,
           state_mlstm_n, state_mlstm_m, w_ada, b_ada, g_norm1, w_in, b_gates, g_q, g_k, sinks,
           g_mlstm_out, w_out, g_norm2, w_up, w_down):
    depth = w_ada.shape[0]
    assert depth == 1
    bp, lp, d = x_prompt.shape
    bs, ls, _ = x_sample.shape
    wb = cache_swa_k.shape[2]

    xp, xs = x_prompt, x_sample
    for l in range(depth):
        wt = w_in[l].T
        g_lo = 4 * M_W
        g_hi = g_lo + 2 * M_HEADS
        wg = jnp.concatenate([wt[g_lo:g_hi], jnp.zeros((BF16_ROWS - 2 * M_HEADS, d), wt.dtype)])
        gq_t = (jnp.tile(g_q[l], A_HEADS) * (LOG2E * A_DIM ** -0.5))[None, :]
        proj_w = (g_norm1[l][None, :], wt[:g_lo].astype(BF16), wt[g_hi:].astype(BF16),
                  wg.astype(BF16), b_gates[l][:, None], gq_t,
                  jnp.tile(g_k[l], A_KV_HEADS)[None, :], _block_ones(A_QW), _block_ones(A_KW))
        ffn_w = (g_norm2[l][None, :], w_out[l].astype(BF16), w_up[l].astype(BF16),
                 w_down[l].astype(BF16))

        mod_rows = -(-(bp + bs) // 8) * 8
        c_all = jnp.concatenate([c_sample, c_prompt, jnp.zeros((mod_rows - bp - bs, d), F32)],
                                axis=0)
        mod4 = _ada_mod(c_all, w_ada[l], b_ada[l][None, :]).reshape(mod_rows, 6, 1, d)

        xp, kt, vt, c, n, m = _layer(xp, mod4, bs, sinks[l], g_mlstm_out[l], proj_w, ffn_w,
                                     rows=ROW_TILE, chunk=MLSTM_CHUNK)
        win = WIN_CHUNKS * CHUNK
        st_p = (kt.reshape(bp, win, A_KV_HEADS, A_DIM), vt.reshape(bp, win, A_KV_HEADS, A_DIM),
                c, n, m[..., 0])

        xs, ko, vo, c, n, m = _sample_layer(
            xs, mod4, sinks[l], state_mlstm_C[l], state_mlstm_n[l], state_mlstm_m[l][..., None],
            g_mlstm_out[l], cache_swa_k[l].reshape(bs, wb, A_KW),
            cache_swa_v[l].reshape(bs, wb, A_KW), proj_w, ffn_w)
        st_s = (ko.reshape(bs, wb, A_KV_HEADS, A_DIM), vo.reshape(bs, wb, A_KV_HEADS, A_DIM),
                c, n, m[..., 0])

    return (xp, xs) + tuple(s[None] for s in st_p) + tuple(s[None] for s in st_s)
```

```python
import functools

import jax
import jax.numpy as jnp
from jax import lax
from jax.experimental import pallas as pl
from jax.experimental.pallas import tpu as pltpu

F32 = jnp.float32
BF16 = jnp.bfloat16

D_MODEL = 1024
CHUNK = 64
M_HEADS = 4
M_DIM = 128
M_W = M_HEADS * M_DIM
A_HEADS = 8
A_KV_HEADS = 2
A_DIM = 64
A_QW = A_HEADS * A_DIM
A_KW = A_KV_HEADS * A_DIM
WIN_CHUNKS = 2
D_FF = 4 * D_MODEL
EPS = 1e-6
PAST_LEN = 4096

LANES = 128
BF16_ROWS = 16
VMEM_LIMIT = 56 * 1024 * 1024
NEG_INF = float("-inf")
LOG2E = 1.4426950408889634
ROW_TILE = 512
MLSTM_CHUNK = 128
FF_BLOCK = 1024
NT_DIMS = (((1,), (1,)), ((), ()))
MOD_SH1, MOD_SC1, MOD_GA1, MOD_SH2, MOD_SC2, MOD_GA2 = range(6)


def _params(sem):
    return pltpu.CompilerParams(dimension_semantics=sem, vmem_limit_bytes=VMEM_LIMIT)


def _const_spec(shape):
    nd = len(shape)
    return pl.BlockSpec(shape, lambda *_: (0,) * nd, pipeline_mode=pl.Buffered(1))


def _ada_body(c_ref, w_ref, b_ref, o_ref):
    c = c_ref[...]
    s = c * jax.nn.sigmoid(c)
    o_ref[...] = jnp.dot(s.astype(BF16), w_ref[...].astype(BF16),
                         preferred_element_type=F32) + b_ref[...]


def _ada_mod(c_all, w_ada, b_ada):
    rows = c_all.shape[0]
    return pl.pallas_call(
        _ada_body,
        grid=(6,),
        in_specs=[pl.BlockSpec((rows, D_MODEL), lambda j: (0, 0)),
                  pl.BlockSpec((D_MODEL, D_MODEL), lambda j: (0, j)),
                  pl.BlockSpec((1, D_MODEL), lambda j: (0, j))],
        out_specs=pl.BlockSpec((rows, D_MODEL), lambda j: (0, j)),
        out_shape=jax.ShapeDtypeStruct((rows, 6 * D_MODEL), F32),
        compiler_params=_params(("arbitrary",)),
        name="ada_mod",
    )(c_all, w_ada, b_ada)


def _segment_prefix(x, op, identity, seg):
    assert x.shape[1] % LANES == 0 and x.shape[1] % seg == 0
    pos = lax.broadcasted_iota(jnp.int32, x.shape, 1) % seg
    d = 1
    while d < seg:
        x = op(x, jnp.where(pos >= d, pltpu.roll(x, d, axis=1), identity))
        d *= 2
    return x


GATE_ROWS = 24
PROJ_OUTPUTS = ("mq", "mk", "mv", "og", "gt", "aq", "ak", "av")


def _in_proj_tile(x, sc, sh, g1, wa_ref, wb_ref, wg_ref, bg, gq, gk, pq_ref, pk_ref, store,
                  gate_chunk):
    s, r, d = x.shape
    m = s * r
    ms = jnp.mean(x * x, axis=-1, keepdims=True)
    y = x * lax.rsqrt(ms + EPS) * g1
    h = y * (1.0 + sc) + sh
    hb = h.reshape(m, d).astype(BF16)

    def proj(w_ref, lo, width):
        return lax.dot_general(hb, w_ref[lo:lo + width, :], NT_DIMS, preferred_element_type=F32)

    zg = lax.dot_general(wg_ref[...], hb, NT_DIMS,
                         preferred_element_type=F32)[:2 * M_HEADS] + bg
    row = lax.broadcasted_iota(jnp.int32, zg.shape, 0)
    log_sig = jnp.minimum(zg, 0.0) - jnp.log1p(jnp.exp(-jnp.abs(zg)))
    gt = jnp.where(row < M_HEADS, zg, log_sig)
    b8 = _segment_prefix(gt, jnp.add, 0.0, gate_chunk)
    c8 = gt - pltpu.roll(b8, M_HEADS, axis=0)
    cm8 = _segment_prefix(c8, jnp.maximum, NEG_INF, gate_chunk)
    store("gt", jnp.concatenate([c8, b8, cm8], axis=0))

    store("mq", proj(wa_ref, 0, M_W).astype(BF16))
    store("mk", (proj(wa_ref, M_W, M_W) * (M_DIM ** -0.5)).astype(BF16))
    store("mv", proj(wa_ref, 2 * M_W, M_W).astype(BF16))
    store("og", jax.nn.sigmoid(proj(wa_ref, 3 * M_W, M_W)).astype(BF16))

    aq = proj(wb_ref, 0, A_QW)
    ss_q = jnp.dot((aq * aq).astype(BF16), pq_ref[...], preferred_element_type=F32)
    store("aq", (aq * lax.rsqrt(ss_q * (1.0 / A_DIM) + EPS) * gq).astype(BF16))
    ak = proj(wb_ref, A_QW, A_KW)
    sq = ak * ak
    hi = sq.astype(BF16)
    lo = (sq - hi.astype(F32)).astype(BF16)
    ss_k = (jnp.dot(hi, pk_ref[...], preferred_element_type=F32)
            + jnp.dot(lo, pk_ref[...], preferred_element_type=F32))
    store("ak", ak * lax.rsqrt(ss_k * (1.0 / A_DIM) + EPS) * gk)
    store("av", proj(wb_ref, A_QW + A_KW, A_KW))


def _proj_weight_specs(d):
    return [_const_spec((1, d)), _const_spec((4 * M_W, d)), _const_spec((A_QW + 2 * A_KW, d)),
            _const_spec((BF16_ROWS, d)), _const_spec((2 * M_HEADS, 1)),
            _const_spec((1, A_QW)), _const_spec((1, A_KW)),
            _const_spec((A_QW, A_QW)), _const_spec((A_KW, A_KW))]


def _col_bcast(r):
    t = r.shape[1]
    if t % LANES == 0:
        return jnp.broadcast_to(r, (LANES, t)).T
    src = lax.broadcasted_iota(jnp.int32, (t, t), 0)
    dst = lax.broadcasted_iota(jnp.int32, (t, t), 1)
    col = jnp.sum(jnp.where(src == dst, jnp.broadcast_to(r, (t, t)), 0.0), axis=1, keepdims=True)
    return jnp.broadcast_to(col, (t, LANES))


def _eye_dim():
    return (lax.broadcasted_iota(jnp.int32, (M_DIM, M_DIM), 0)
            == lax.broadcasted_iota(jnp.int32, (M_DIM, M_DIM), 1))


def _mlstm_load_state(c0_ref, n0_ref, m0_ref, caug_ref, m_ref):
    for hd in range(M_HEADS):
        caug_ref[hd, :, :M_DIM] = c0_ref[0, hd]
        n_row = jnp.broadcast_to(n0_ref[0, hd:hd + 1, :], (M_DIM, M_DIM))
        n_col = jnp.sum(jnp.where(_eye_dim(), n_row, 0.0), axis=1, keepdims=True)
        caug_ref[hd, :, M_DIM:] = jnp.broadcast_to(n_col, (M_DIM, M_DIM))
    m_ref[...] = m0_ref[0]


def _mlstm_store_state(caug_ref, m_ref, c_out_ref, n_out_ref, m_out_ref):
    for hd in range(M_HEADS):
        c_out_ref[0, hd] = caug_ref[hd, :, :M_DIM]
        n_out_ref[0, hd:hd + 1, :] = jnp.sum(
            jnp.where(_eye_dim(), caug_ref[hd, :, M_DIM:], 0.0), axis=0, keepdims=True)
    m_out_ref[0] = m_ref[...]


def _mlstm_tile(q_ref, k_ref, v_ref, og_ref, gt_ref, gout_ref, caug_ref, m_ref, write_h, chunk):
    t = chunk
    n_chunks = q_ref.shape[1] // t
    bs = LANES if t % LANES == 0 else t
    tril = (lax.broadcasted_iota(jnp.int32, (bs, bs), 1)
            <= lax.broadcasted_iota(jnp.int32, (bs, bs), 0))
    m_prev = m_ref[...]
    for j in range(n_chunks):
        rows = slice(j * t, (j + 1) * t)
        c = gt_ref[0:M_HEADS, rows]
        b = gt_ref[12:12 + M_HEADS, rows]
        g = jnp.maximum(gt_ref[16:16 + M_HEADS, rows], m_prev)
        g_last = g[:, t - 1:t]
        c2 = c * LOG2E
        g2 = g * LOG2E
        e_all = jnp.exp2(-LOG2E * (b + g))
        w_src_all = jnp.exp2(c2 - LOG2E * g_last)
        w_inter_all = jnp.exp2(LOG2E * m_prev - g2)
        w_old_all = jnp.exp2(LOG2E * (m_prev - g_last))
        m_prev = b[:, t - 1:t] + g_last

        for hd in range(M_HEADS):
            sl = slice(hd * M_DIM, (hd + 1) * M_DIM)
            hr = slice(hd, hd + 1)
            g_b = _col_bcast(g2[hr])
            e_b = _col_bcast(e_all[hr])
            w_src = _col_bcast(w_src_all[hr])
            w_inter = _col_bcast(w_inter_all[hr])

            q = q_ref[0, rows, sl]
            k = k_ref[0, rows, sl]
            v = v_ref[0, rows, sl]
            caug = caug_ref[hd]
            inter = jnp.dot(q, caug.astype(BF16), preferred_element_type=F32)
            gain = gout_ref[hr, :]
            for rb in range(t // bs):
                rsl = slice(rb * bs, (rb + 1) * bs)
                ncols = (rb + 1) * bs
                s_mat = lax.dot_general(q[rsl], k[:ncols], NT_DIMS,
                                        preferred_element_type=F32)
                decay = []
                for cb in range(rb + 1):
                    arg = c2[hr, cb * bs:(cb + 1) * bs] - g_b[rsl, :bs]
                    decay.append(jnp.exp2(jnp.where(tril, arg, NEG_INF) if cb == rb else arg))
                s_mat = s_mat * (decay[0] if rb == 0 else jnp.concatenate(decay, axis=1))
                num = (jnp.dot(s_mat.astype(BF16), v[:ncols], preferred_element_type=F32)
                       + w_inter[rsl] * inter[rsl, :M_DIM])
                den = jnp.sum(s_mat, axis=1, keepdims=True) + w_inter[rsl] * inter[rsl, M_DIM:]
                hh = num / jnp.maximum(jnp.abs(den), e_b[rsl])
                hn = hh * lax.rsqrt(jnp.mean(hh * hh, axis=-1, keepdims=True) + EPS) * gain
                out_rows = slice(j * t + rb * bs, j * t + (rb + 1) * bs)
                write_h(out_rows, sl, (hn * og_ref[0, out_rows, sl].astype(F32)).astype(BF16))

            rhs = jnp.concatenate([v.astype(F32) * w_src, w_src], axis=1).astype(BF16)
            upd = lax.dot_general(k, rhs, (((0,), (0,)), ((), ())), preferred_element_type=F32)
            caug_ref[hd] = w_old_all[hr] * caug + upd
    m_ref[...] = m_prev


def _kv_halves(a):
    low = lax.broadcasted_iota(jnp.int32, a.shape, 1) < A_DIM
    lo_half = jnp.where(low, a, 0.0)
    hi_half = jnp.where(low, 0.0, a)
    pairs = [(lo_half, pltpu.roll(lo_half, A_DIM, axis=1)),
             (pltpu.roll(hi_half, A_DIM, axis=1), hi_half)]
    return [tuple(h.astype(BF16) for h in pair) for pair in pairs]


def _swa_biases(q_pos, k_pos):
    shift = CHUNK.bit_length() - 1
    assert 1 << shift == CHUNK
    qc = lax.shift_right_arithmetic(q_pos, shift)
    kc = lax.shift_right_arithmetic(k_pos, shift)
    allowed = (kc <= qc) & (kc >= qc - WIN_CHUNKS) & (k_pos >= 0)
    dist = jnp.abs(q_pos - k_pos).astype(F32)
    slopes = [2.0 ** (-8.0 * (head + 1) / A_HEADS) for head in range(A_HEADS)]
    return [jnp.where(allowed, (-LOG2E * slope) * dist, NEG_INF) for slope in slopes]


def _swa_block(q, kz, vz, biases, sink_ref, write_o, valid=None):
    tq = q.shape[0]
    pairs_per_kv = A_HEADS // A_KV_HEADS // 2
    for j in range(A_KV_HEADS):
        pairs = [j * pairs_per_kv + i for i in range(pairs_per_kv)]
        q_stack = jnp.concatenate([q[:, p * LANES:(p + 1) * LANES] for p in pairs], axis=0)
        acc = [None] * pairs_per_kv
        for e in range(2):
            logits_all = lax.dot_general(q_stack, kz[j][e], NT_DIMS, preferred_element_type=F32)
            probs, inv_denoms = [], []
            for i, p in enumerate(pairs):
                head = 2 * p + e
                sink = sink_ref[head] * LOG2E
                logits = logits_all[i * tq:(i + 1) * tq] + biases[head]
                if valid is not None:
                    logits = jnp.where(valid, logits, NEG_INF)
                mx = jnp.maximum(jnp.max(logits, axis=1, keepdims=True), sink)
                pr = jnp.exp2(logits - mx)
                inv_denoms.append(
                    1.0 / (jnp.sum(pr, axis=1, keepdims=True) + jnp.exp2(sink - mx)))
                probs.append(pr.astype(BF16))
            out_all = jnp.dot(jnp.concatenate(probs, axis=0), vz[j][e],
                              preferred_element_type=F32)
            for i in range(pairs_per_kv):
                part = out_all[i * tq:(i + 1) * tq] * inv_denoms[i]
                acc[i] = part if acc[i] is None else acc[i] + part
        for i, p in enumerate(pairs):
            write_o(slice(p * LANES, (p + 1) * LANES), acc[i].astype(BF16))


SWA_Q_ROWS = WIN_CHUNKS * CHUNK


def _swa_tile(sink_ref, q_ref, k_all, v_all, tile_start, write_o):
    tq = SWA_Q_ROWS
    kz = _kv_halves(k_all)
    vz = _kv_halves(v_all)
    rel_k = lax.broadcasted_iota(jnp.int32, (1, 2 * tq), 1)
    biases = _swa_biases(tq + lax.broadcasted_iota(jnp.int32, (tq, 1), 0), rel_k)
    for r in range(q_ref.shape[1] // tq):
        rows = slice(r * tq, (r + 1) * tq)
        band = slice(r * tq, (r + 2) * tq)
        valid = (tile_start - tq + rel_k >= 0) if r == 0 else None
        _swa_block(q_ref[0, rows, :],
                   [tuple(h[band] for h in pair) for pair in kz],
                   [tuple(h[band] for h in pair) for pair in vz],
                   biases, sink_ref,
                   lambda lanes, value, rows=rows: write_o(rows, lanes, value), valid=valid)


def _out_ffn_tile(x, mixed, ga1, sc2, sh2, ga2, g2, wo_ref, wu_ref, wd_ref):
    s, r, d = x.shape
    mix = jnp.dot(mixed, wo_ref[...], preferred_element_type=F32)
    x1 = x + ga1 * mix.reshape(s, r, d)
    ms = jnp.mean(x1 * x1, axis=-1, keepdims=True)
    h2 = (x1 * lax.rsqrt(ms + EPS) * g2) * (1.0 + sc2) + sh2
    hb = h2.reshape(s * r, d).astype(BF16)
    ffn = jnp.zeros((s * r, d), F32)
    for c in range(D_FF // FF_BLOCK):
        u = jnp.dot(hb, wu_ref[:, c * FF_BLOCK:(c + 1) * FF_BLOCK], preferred_element_type=F32)
        u = jnp.square(jnp.maximum(u, 0.0)).astype(BF16)
        ffn = ffn + jnp.dot(u, wd_ref[c * FF_BLOCK:(c + 1) * FF_BLOCK, :],
                            preferred_element_type=F32)
    return x1 + ga2 * ffn.reshape(s, r, d)


def _ffn_weight_specs(d):
    return [_const_spec((1, d)), _const_spec((d, d)), _const_spec((d, D_FF)),
            _const_spec((D_FF, d))]


def _layer_body(sink_ref, xp_ref, sc1_ref, sh1_ref, g1_ref, wa_ref, wb_ref, wg_ref, bg_ref,
                gq_ref, gk_ref, pq_ref, pk_ref, gout_ref,
                xf_ref, ga1_ref, sc2_ref, sh2_ref, ga2_ref, g2_ref, wo_ref, wu_ref, wd_ref,
                y_ref, kt_ref, vt_ref, c_out_ref, n_out_ref, m_out_ref,
                mq_s, mk_s, mv_s, og_s, gt_s, aq_s, ak_s, av_s, kprev_s, vprev_s,
                mix_s, caug_ref, m_ref, *, n_tiles, tiles_per_seq, chunk):
    s = pl.program_id(0)
    rows = xp_ref.shape[1]
    tile = jnp.minimum(s, n_tiles - 1)
    j = tile % tiles_per_seq
    valid = s < n_tiles
    proj_s = {"mq": mq_s, "mk": mk_s, "mv": mv_s, "og": og_s, "gt": gt_s,
              "aq": aq_s, "ak": ak_s, "av": av_s}

    @pl.when(s == 0)
    def _():
        kprev_s[...] = jnp.zeros(kprev_s.shape, kprev_s.dtype)
        vprev_s[...] = jnp.zeros(vprev_s.shape, vprev_s.dtype)
        mix_s[...] = jnp.zeros(mix_s.shape, mix_s.dtype)

    @pl.when(jnp.logical_and(j == 0, valid))
    def _():
        caug_ref[...] = jnp.zeros(caug_ref.shape, caug_ref.dtype)
        m_ref[...] = jnp.zeros(m_ref.shape, m_ref.dtype)

    y_ref[...] = _out_ffn_tile(
        xf_ref[...], mix_s[...], ga1_ref[...], sc2_ref[...], sh2_ref[...], ga2_ref[...],
        g2_ref[...], wo_ref, wu_ref, wd_ref)

    def store(name, value):
        proj_s[name][...] = value if name == "gt" else value[None]

    _in_proj_tile(xp_ref[...], sc1_ref[...], sh1_ref[...], g1_ref[...], wa_ref, wb_ref, wg_ref,
                  bg_ref[...], gq_ref[...], gk_ref[...], pq_ref, pk_ref, store, chunk)

    def write_h(r, lanes, value):
        mix_s[r, lanes] = value

    def write_o(r, lanes, value):
        mix_s[r, slice(M_W + lanes.start, M_W + lanes.stop)] = value

    _mlstm_tile(mq_s, mk_s, mv_s, og_s, gt_s, gout_ref, caug_ref, m_ref, write_h, chunk)
    k_cur = ak_s[0]
    v_cur = av_s[0]
    _swa_tile(sink_ref, aq_s,
              jnp.concatenate([kprev_s[...], k_cur], axis=0),
              jnp.concatenate([vprev_s[...], v_cur], axis=0), j * rows, write_o)
    kprev_s[...] = k_cur[rows - SWA_Q_ROWS:]
    vprev_s[...] = v_cur[rows - SWA_Q_ROWS:]

    @pl.when(jnp.logical_and(j == tiles_per_seq - 1, valid))
    def _():
        kt_ref[0] = k_cur[rows - SWA_Q_ROWS:]
        vt_ref[0] = v_cur[rows - SWA_Q_ROWS:]
        _mlstm_store_state(caug_ref, m_ref, c_out_ref, n_out_ref, m_out_ref)


def _layer(x, mod4, mod_row0, sinks, gout, proj_w, ffn_w, *, rows, chunk):
    ns, length, d = x.shape
    tps = length // rows
    n_tiles = ns * tps

    def stage(lag):
        def tile_idx(i):
            t = jnp.clip(i - lag, 0, n_tiles - 1)
            return t // tps, t % tps
        x_tile = pl.BlockSpec((1, rows, d), lambda i: (*tile_idx(i), 0))
        mod = lambda k: pl.BlockSpec((1, None, 1, d),
                                     lambda i: (mod_row0 + tile_idx(i)[0], k, 0, 0))
        per_seq = lambda shape: pl.BlockSpec(
            (1,) + shape, lambda i: (tile_idx(i)[0],) + (0,) * len(shape))
        return x_tile, mod, per_seq

    xp_tile, p_mod, p_seq = stage(0)
    xf_tile, f_mod, _ = stage(1)
    st_c, st_n, st_m = p_seq((M_HEADS, M_DIM, M_DIM)), p_seq((M_HEADS, M_DIM)), p_seq((M_HEADS, 1))
    tail = p_seq((SWA_Q_ROWS, A_KW))
    one = lambda shape, dtype: pltpu.VMEM((1,) + shape, dtype)
    return pl.pallas_call(
        functools.partial(_layer_body, n_tiles=n_tiles, tiles_per_seq=tps, chunk=chunk),
        grid=(n_tiles + 1,),
        in_specs=([pl.BlockSpec(memory_space=pltpu.SMEM), xp_tile, p_mod(MOD_SC1), p_mod(MOD_SH1)]
                  + _proj_weight_specs(d)
                  + [_const_spec((M_HEADS, M_DIM)), xf_tile,
                     f_mod(MOD_GA1), f_mod(MOD_SC2), f_mod(MOD_SH2), f_mod(MOD_GA2)]
                  + _ffn_weight_specs(d)),
        out_specs=(xf_tile, tail, tail, st_c, st_n, st_m),
        out_shape=(jax.ShapeDtypeStruct((ns, length, d), F32),
                   jax.ShapeDtypeStruct((ns, SWA_Q_ROWS, A_KW), F32),
                   jax.ShapeDtypeStruct((ns, SWA_Q_ROWS, A_KW), F32),
                   jax.ShapeDtypeStruct((ns, M_HEADS, M_DIM, M_DIM), F32),
                   jax.ShapeDtypeStruct((ns, M_HEADS, M_DIM), F32),
                   jax.ShapeDtypeStruct((ns, M_HEADS, 1), F32)),
        scratch_shapes=[one((rows, M_W), BF16), one((rows, M_W), BF16),
                        one((rows, M_W), BF16), one((rows, M_W), BF16),
                        pltpu.VMEM((GATE_ROWS, rows), F32), one((rows, A_QW), BF16),
                        one((rows, A_KW), F32), one((rows, A_KW), F32),
                        pltpu.VMEM((SWA_Q_ROWS, A_KW), F32), pltpu.VMEM((SWA_Q_ROWS, A_KW), F32),
                        pltpu.VMEM((rows, M_W + A_QW), BF16),
                        pltpu.VMEM((M_HEADS, M_DIM, 2 * M_DIM), F32),
                        pltpu.VMEM((M_HEADS, 1), F32)],
        compiler_params=_params(("arbitrary",)),
        name="layer",
    )(sinks, x, mod4, mod4, *proj_w, gout, x, mod4, mod4, mod4, mod4, *ffn_w)


class _LaneWindow:
    def __init__(self, ref, offset):
        self.ref, self.offset = ref, offset

    def __getitem__(self, idx):
        rows, cols = idx
        return self.ref[rows, slice(cols.start + self.offset, cols.stop + self.offset)]


def _sample_body(sink_ref, x_ref, sc1_ref, sh1_ref, g1_ref, wa_ref, wb_ref, wg_ref, bg_ref,
                 gq_ref, gk_ref, pq_ref, pk_ref, c0_ref, n0_ref, m0_ref, gout_ref,
                 ck_ref, cv_ref, ga1_ref, sc2_ref, sh2_ref, ga2_ref, g2_ref, wo_ref, wu_ref, wd_ref,
                 y_ref, ko_ref, vo_ref, c_out_ref, n_out_ref, m_out_ref,
                 mq_s, mk_s, mv_s, og_s, gt_s, aq_s, ak_s, av_s, mix_s, caug_s, m_s):
    ns, t, _ = x_ref.shape
    cached = ck_ref.shape[1]
    proj_s = {"mq": mq_s, "mk": mk_s, "mv": mv_s, "og": og_s, "gt": gt_s,
              "aq": aq_s, "ak": ak_s, "av": av_s}

    def store(name, value):
        proj_s[name][...] = value if name == "gt" else value.reshape(ns, t, value.shape[-1])

    _in_proj_tile(x_ref[...], sc1_ref[...], sh1_ref[...], g1_ref[...], wa_ref, wb_ref, wg_ref,
                  bg_ref[...], gq_ref[...], gk_ref[...], pq_ref, pk_ref, store, t)

    seq = lambda ref, b: ref.at[pl.ds(b, 1)]
    for b in range(ns):
        caug_b, m_b = caug_s.at[b], m_s.at[b]
        _mlstm_load_state(seq(c0_ref, b), seq(n0_ref, b), seq(m0_ref, b), caug_b, m_b)

        def write_h(rows, lanes, value, b=b):
            mix_s[b * t + rows.start:b * t + rows.stop, lanes] = value

        _mlstm_tile(seq(mq_s, b), seq(mk_s, b), seq(mv_s, b), seq(og_s, b),
                    _LaneWindow(gt_s, b * t), gout_ref, caug_b, m_b, write_h, t)
        _mlstm_store_state(caug_b, m_b, seq(c_out_ref, b), seq(n_out_ref, b), seq(m_out_ref, b))

    tk = 2 * LANES
    pad = jnp.zeros((tk - cached - t, A_KW), F32)
    q_pos = PAST_LEN + lax.broadcasted_iota(jnp.int32, (t, 1), 0)
    idx = lax.broadcasted_iota(jnp.int32, (1, tk), 1)
    k_pos = jnp.where(idx < cached + t, PAST_LEN - cached + idx, -1)
    biases = _swa_biases(q_pos, k_pos)
    for b in range(ns):
        k_new, v_new = ak_s[b], av_s[b]

        def write_o(lanes, value, b=b):
            mix_s[b * t:(b + 1) * t, M_W + lanes.start:M_W + lanes.stop] = value

        _swa_block(aq_s[b], _kv_halves(jnp.concatenate([ck_ref[b], k_new, pad], axis=0)),
                   _kv_halves(jnp.concatenate([cv_ref[b], v_new, pad], axis=0)),
                   biases, sink_ref, write_o)
        ko_ref[b] = jnp.concatenate([ck_ref[b, t:, :], k_new], axis=0)
        vo_ref[b] = jnp.concatenate([cv_ref[b, t:, :], v_new], axis=0)

    y_ref[...] = _out_ffn_tile(
        x_ref[...], mix_s[...], ga1_ref[...], sc2_ref[...], sh2_ref[...], ga2_ref[...],
        g2_ref[...], wo_ref, wu_ref, wd_ref)


def _sample_layer(x, mod4, sinks, c0, n0, m0, gout, cache_k, cache_v, proj_w, ffn_w):
    ns, t, d = x.shape
    cached = cache_k.shape[1]
    assert cached >= t
    whole = lambda shape: pl.BlockSpec(shape, lambda i: (0,) * len(shape))
    mod = lambda k: pl.BlockSpec((ns, None, 1, d), lambda i: (0, k, 0, 0))
    st_c, st_n, st_m = (whole((ns, M_HEADS, M_DIM, M_DIM)), whole((ns, M_HEADS, M_DIM)),
                        whole((ns, M_HEADS, 1)))
    kv = whole((ns, cached, A_KW))
    return pl.pallas_call(
        _sample_body,
        grid=(1,),
        in_specs=([pl.BlockSpec(memory_space=pltpu.SMEM), whole((ns, t, d)),
                   mod(MOD_SC1), mod(MOD_SH1)]
                  + _proj_weight_specs(d)
                  + [st_c, st_n, st_m, _const_spec((M_HEADS, M_DIM)), kv, kv,
                     mod(MOD_GA1), mod(MOD_SC2), mod(MOD_SH2), mod(MOD_GA2)]
                  + _ffn_weight_specs(d)),
        out_specs=(whole((ns, t, d)), kv, kv, st_c, st_n, st_m),
        out_shape=(jax.ShapeDtypeStruct((ns, t, d), F32),
                   jax.ShapeDtypeStruct((ns, cached, A_KW), F32),
                   jax.ShapeDtypeStruct((ns, cached, A_KW), F32),
                   jax.ShapeDtypeStruct((ns, M_HEADS, M_DIM, M_DIM), F32),
                   jax.ShapeDtypeStruct((ns, M_HEADS, M_DIM), F32),
                   jax.ShapeDtypeStruct((ns, M_HEADS, 1), F32)),
        scratch_shapes=[pltpu.VMEM((ns, t, M_W), BF16), pltpu.VMEM((ns, t, M_W), BF16),
                        pltpu.VMEM((ns, t, M_W), BF16), pltpu.VMEM((ns, t, M_W), BF16),
                        pltpu.VMEM((GATE_ROWS, ns * t), F32), pltpu.VMEM((ns, t, A_QW), BF16),
                        pltpu.VMEM((ns, t, A_KW), F32), pltpu.VMEM((ns, t, A_KW), F32),
                        pltpu.VMEM((ns * t, M_W + A_QW), BF16),
                        pltpu.VMEM((ns, M_HEADS, M_DIM, 2 * M_DIM), F32),
                        pltpu.VMEM((ns, M_HEADS, 1), F32)],
        compiler_params=_params(("arbitrary",)),
        name="sample_layer",
    )(sinks, x, mod4, mod4, *proj_w, c0, n0, m0, gout, cache_k, cache_v,
      mod4, mod4, mod4, mod4, *ffn_w)


def _block_ones(n):
    g = jnp.arange(n, dtype=jnp.int32) // A_DIM
    return (g[:, None] == g[None, :]).astype(BF16)


def kernel(x_prompt, x_sample, c_prompt, c_sample, cache_swa_k, cache_swa_v, state_mlstm_C,
           state_mlstm_n, state_mlstm_m, w_ada, b_ada, g_norm1, w_in, b_gates, g_q, g_k, sinks,
           g_mlstm_out, w_out, g_norm2, w_up, w_down):
    depth = w_ada.shape[0]
    assert depth == 1
    bp, lp, d = x_prompt.shape
    bs, ls, _ = x_sample.shape
    wb = cache_swa_k.shape[2]

    xp, xs = x_prompt, x_sample
    for l in range(depth):
        wt = w_in[l].T
        g_lo = 4 * M_W
        g_hi = g_lo + 2 * M_HEADS
        wg = jnp.concatenate([wt[g_lo:g_hi], jnp.zeros((BF16_ROWS - 2 * M_HEADS, d), wt.dtype)])
        gq_t = (jnp.tile(g_q[l], A_HEADS) * (LOG2E * A_DIM ** -0.5))[None, :]
        proj_w = (g_norm1[l][None, :], wt[:g_lo].astype(BF16), wt[g_hi:].astype(BF16),
                  wg.astype(BF16), b_gates[l][:, None], gq_t,
                  jnp.tile(g_k[l], A_KV_HEADS)[None, :], _block_ones(A_QW), _block_ones(A_KW))
        ffn_w = (g_norm2[l][None, :], w_out[l].astype(BF16), w_up[l].astype(BF16),
                 w_down[l].astype(BF16))

        mod_rows = -(-(bp + bs) // 8) * 8
        c_all = jnp.concatenate([c_sample, c_prompt, jnp.zeros((mod_rows - bp - bs, d), F32)],
                                axis=0)
        mod4 = _ada_mod(c_all, w_ada[l], b_ada[l][None, :]).reshape(mod_rows, 6, 1, d)

        xp, kt, vt, c, n, m = _layer(xp, mod4, bs, sinks[l], g_mlstm_out[l], proj_w, ffn_w,
                                     rows=ROW_TILE, chunk=MLSTM_CHUNK)
        win = WIN_CHUNKS * CHUNK
        st_p = (kt.reshape(bp, win, A_KV_HEADS, A_DIM), vt.reshape(bp, win, A_KV_HEADS, A_DIM),
                c, n, m[..., 0])

        xs, ko, vo, c, n, m = _sample_layer(
            xs, mod4, sinks[l], state_mlstm_C[l], state_mlstm_n[l], state_mlstm_m[l][..., None],
            g_mlstm_out[l], cache_swa_k[l].reshape(bs, wb, A_KW),
            cache_swa_v[l].reshape(bs, wb, A_KW), proj_w, ffn_w)
        st_s = (ko.reshape(bs, wb, A_KV_HEADS, A_DIM), vo.reshape(bs, wb, A_KV_HEADS, A_DIM),
                c, n, m[..., 0])

    return (xp, xs) + tuple(s[None] for s in st_p) + tuple(s[None] for s in st_s)
```

```python
import functools

import jax
import jax.numpy as jnp
from jax import lax
from jax.experimental import pallas as pl
from jax.experimental.pallas import tpu as pltpu

F32 = jnp.float32
BF16 = jnp.bfloat16

D_MODEL = 1024
CHUNK = 64
M_HEADS = 4
M_DIM = 128
M_W = M_HEADS * M_DIM
A_HEADS = 8
A_KV_HEADS = 2
A_DIM = 64
A_QW = A_HEADS * A_DIM
A_KW = A_KV_HEADS * A_DIM
WIN_CHUNKS = 2
D_FF = 4 * D_MODEL
EPS = 1e-6
PAST_LEN = 4096

LANES = 128
BF16_ROWS = 16
VMEM_LIMIT = 56 * 1024 * 1024
NEG_INF = float("-inf")
LOG2E = 1.4426950408889634
ROW_TILE = 512
MLSTM_CHUNK = 128
FF_BLOCK = 1024
NT_DIMS = (((1,), (1,)), ((), ()))
MOD_SH1, MOD_SC1, MOD_GA1, MOD_SH2, MOD_SC2, MOD_GA2 = range(6)


def _params(sem, vmem_limit=VMEM_LIMIT):
    return pltpu.CompilerParams(dimension_semantics=sem, vmem_limit_bytes=vmem_limit)


def _const_spec(shape):
    nd = len(shape)
    return pl.BlockSpec(shape, lambda *_: (0,) * nd, pipeline_mode=pl.Buffered(1))


def _ada_body(c_ref, w_ref, b_ref, o_ref):
    c = c_ref[...]
    s = c * jax.nn.sigmoid(c)
    o_ref[...] = jnp.dot(s.astype(BF16), w_ref[...].astype(BF16),
                         preferred_element_type=F32) + b_ref[...]


def _ada_mod(c_all, w_ada, b_ada):
    rows = c_all.shape[0]
    return pl.pallas_call(
        _ada_body,
        grid=(6,),
        in_specs=[pl.BlockSpec((rows, D_MODEL), lambda j: (0, 0)),
                  pl.BlockSpec((D_MODEL, D_MODEL), lambda j: (0, j)),
                  pl.BlockSpec((1, D_MODEL), lambda j: (0, j))],
        out_specs=pl.BlockSpec((rows, D_MODEL), lambda j: (0, j)),
        out_shape=jax.ShapeDtypeStruct((rows, 6 * D_MODEL), F32),
        compiler_params=_params(("arbitrary",)),
        name="ada_mod",
    )(c_all, w_ada, b_ada)


def _segment_prefix(x, op, identity, seg):
    assert x.shape[1] % LANES == 0 and x.shape[1] % seg == 0
    pos = lax.broadcasted_iota(jnp.int32, x.shape, 1) % seg
    d = 1
    while d < seg:
        x = op(x, jnp.where(pos >= d, pltpu.roll(x, d, axis=1), identity))
        d *= 2
    return x


GATE_ROWS = 24
PROJ_OUTPUTS = ("mq", "mk", "mv", "og", "gt", "aq", "ak", "av")


def _in_proj_tile(x, sc, sh, g1, wa_ref, wb_ref, wg_ref, bg, gq, gk, pq_ref, pk_ref, store,
                  gate_chunk):
    s, r, d = x.shape
    m = s * r
    ms = jnp.mean(x * x, axis=-1, keepdims=True)
    y = x * lax.rsqrt(ms + EPS) * g1
    h = y * (1.0 + sc) + sh
    hb = h.reshape(m, d).astype(BF16)

    def proj(w_ref, lo, width):
        return lax.dot_general(hb, w_ref[lo:lo + width, :], NT_DIMS, preferred_element_type=F32)

    zg = lax.dot_general(wg_ref[...], hb, NT_DIMS,
                         preferred_element_type=F32)[:2 * M_HEADS] + bg
    row = lax.broadcasted_iota(jnp.int32, zg.shape, 0)
    log_sig = jnp.minimum(zg, 0.0) - jnp.log1p(jnp.exp(-jnp.abs(zg)))
    gt = jnp.where(row < M_HEADS, zg, log_sig)
    b8 = _segment_prefix(gt, jnp.add, 0.0, gate_chunk)
    c8 = gt - pltpu.roll(b8, M_HEADS, axis=0)
    cm8 = _segment_prefix(c8, jnp.maximum, NEG_INF, gate_chunk)
    store("gt", jnp.concatenate([c8, b8, cm8], axis=0))

    store("mq", proj(wa_ref, 0, M_W).astype(BF16))
    store("mk", (proj(wa_ref, M_W, M_W) * (M_DIM ** -0.5)).astype(BF16))
    store("mv", proj(wa_ref, 2 * M_W, M_W).astype(BF16))
    store("og", jax.nn.sigmoid(proj(wa_ref, 3 * M_W, M_W)).astype(BF16))

    aq = proj(wb_ref, 0, A_QW)
    ss_q = jnp.dot((aq * aq).astype(BF16), pq_ref[...], preferred_element_type=F32)
    store("aq", (aq * lax.rsqrt(ss_q * (1.0 / A_DIM) + EPS) * gq).astype(BF16))
    ak = proj(wb_ref, A_QW, A_KW)
    sq = ak * ak
    hi = sq.astype(BF16)
    lo = (sq - hi.astype(F32)).astype(BF16)
    ss_k = (jnp.dot(hi, pk_ref[...], preferred_element_type=F32)
            + jnp.dot(lo, pk_ref[...], preferred_element_type=F32))
    store("ak", ak * lax.rsqrt(ss_k * (1.0 / A_DIM) + EPS) * gk)
    store("av", proj(wb_ref, A_QW + A_KW, A_KW))


def _proj_weight_specs(d):
    return [_const_spec((1, d)), _const_spec((4 * M_W, d)), _const_spec((A_QW + 2 * A_KW, d)),
            _const_spec((BF16_ROWS, d)), _const_spec((2 * M_HEADS, 1)),
            _const_spec((1, A_QW)), _const_spec((1, A_KW)),
            _const_spec((A_QW, A_QW)), _const_spec((A_KW, A_KW))]


def _col_bcast(r):
    t = r.shape[1]
    if t % LANES == 0:
        return jnp.broadcast_to(r, (LANES, t)).T
    src = lax.broadcasted_iota(jnp.int32, (t, t), 0)
    dst = lax.broadcasted_iota(jnp.int32, (t, t), 1)
    col = jnp.sum(jnp.where(src == dst, jnp.broadcast_to(r, (t, t)), 0.0), axis=1, keepdims=True)
    return jnp.broadcast_to(col, (t, LANES))


def _eye_dim():
    return (lax.broadcasted_iota(jnp.int32, (M_DIM, M_DIM), 0)
            == lax.broadcasted_iota(jnp.int32, (M_DIM, M_DIM), 1))


def _mlstm_load_state(c0_ref, n0_ref, m0_ref, caug_ref, m_ref):
    for hd in range(M_HEADS):
        caug_ref[hd, :, :M_DIM] = c0_ref[0, hd]
        n_row = jnp.broadcast_to(n0_ref[0, hd:hd + 1, :], (M_DIM, M_DIM))
        n_col = jnp.sum(jnp.where(_eye_dim(), n_row, 0.0), axis=1, keepdims=True)
        caug_ref[hd, :, M_DIM:] = jnp.broadcast_to(n_col, (M_DIM, M_DIM))
    m_ref[...] = m0_ref[0]


def _mlstm_store_state(caug_ref, m_ref, c_out_ref, n_out_ref, m_out_ref):
    for hd in range(M_HEADS):
        c_out_ref[0, hd] = caug_ref[hd, :, :M_DIM]
        n_out_ref[0, hd:hd + 1, :] = jnp.sum(
            jnp.where(_eye_dim(), caug_ref[hd, :, M_DIM:], 0.0), axis=0, keepdims=True)
    m_out_ref[0] = m_ref[...]


def _mlstm_tile(q_ref, k_ref, v_ref, og_ref, gt_ref, gout_ref, caug_ref, m_ref, write_h, chunk):
    t = chunk
    n_chunks = q_ref.shape[1] // t
    bs = LANES if t % LANES == 0 else t
    tril = (lax.broadcasted_iota(jnp.int32, (bs, bs), 1)
            <= lax.broadcasted_iota(jnp.int32, (bs, bs), 0))
    m_prev = m_ref[...]
    for j in range(n_chunks):
        rows = slice(j * t, (j + 1) * t)
        c = gt_ref[0:M_HEADS, rows]
        b = gt_ref[12:12 + M_HEADS, rows]
        g = jnp.maximum(gt_ref[16:16 + M_HEADS, rows], m_prev)
        g_last = g[:, t - 1:t]
        c2 = c * LOG2E
        g2 = g * LOG2E
        e_all = jnp.exp2(-LOG2E * (b + g))
        w_src_all = jnp.exp2(c2 - LOG2E * g_last)
        w_inter_all = jnp.exp2(LOG2E * m_prev - g2)
        w_old_all = jnp.exp2(LOG2E * (m_prev - g_last))
        m_prev = b[:, t - 1:t] + g_last

        for hd in range(M_HEADS):
            sl = slice(hd * M_DIM, (hd + 1) * M_DIM)
            hr = slice(hd, hd + 1)
            g_b = _col_bcast(g2[hr])
            e_b = _col_bcast(e_all[hr])
            w_src = _col_bcast(w_src_all[hr])
            w_inter = _col_bcast(w_inter_all[hr])

            q = q_ref[0, rows, sl]
            k = k_ref[0, rows, sl]
            v = v_ref[0, rows, sl]
            caug = caug_ref[hd]
            inter = jnp.dot(q, caug.astype(BF16), preferred_element_type=F32)
            gain = gout_ref[hr, :]
            for rb in range(t // bs):
                rsl = slice(rb * bs, (rb + 1) * bs)
                ncols = (rb + 1) * bs
                s_mat = lax.dot_general(q[rsl], k[:ncols], NT_DIMS,
                                        preferred_element_type=F32)
                decay = []
                for cb in range(rb + 1):
                    arg = c2[hr, cb * bs:(cb + 1) * bs] - g_b[rsl, :bs]
                    decay.append(jnp.exp2(jnp.where(tril, arg, NEG_INF) if cb == rb else arg))
                s_mat = s_mat * (decay[0] if rb == 0 else jnp.concatenate(decay, axis=1))
                num = (jnp.dot(s_mat.astype(BF16), v[:ncols], preferred_element_type=F32)
                       + w_inter[rsl] * inter[rsl, :M_DIM])
                den = jnp.sum(s_mat, axis=1, keepdims=True) + w_inter[rsl] * inter[rsl, M_DIM:]
                hh = num / jnp.maximum(jnp.abs(den), e_b[rsl])
                hn = hh * lax.rsqrt(jnp.mean(hh * hh, axis=-1, keepdims=True) + EPS) * gain
                out_rows = slice(j * t + rb * bs, j * t + (rb + 1) * bs)
                write_h(out_rows, sl, (hn * og_ref[0, out_rows, sl].astype(F32)).astype(BF16))

            rhs = jnp.concatenate([v.astype(F32) * w_src, w_src], axis=1).astype(BF16)
            upd = lax.dot_general(k, rhs, (((0,), (0,)), ((), ())), preferred_element_type=F32)
            caug_ref[hd] = w_old_all[hr] * caug + upd
    m_ref[...] = m_prev


def _kv_halves(a):
    low = lax.broadcasted_iota(jnp.int32, a.shape, 1) < A_DIM
    lo_half = jnp.where(low, a, 0.0)
    hi_half = jnp.where(low, 0.0, a)
    pairs = [(lo_half, pltpu.roll(lo_half, A_DIM, axis=1)),
             (pltpu.roll(hi_half, A_DIM, axis=1), hi_half)]
    return [tuple(h.astype(BF16) for h in pair) for pair in pairs]


def _swa_biases(q_pos, k_pos):
    shift = CHUNK.bit_length() - 1
    assert 1 << shift == CHUNK
    qc = lax.shift_right_arithmetic(q_pos, shift)
    kc = lax.shift_right_arithmetic(k_pos, shift)
    allowed = (kc <= qc) & (kc >= qc - WIN_CHUNKS) & (k_pos >= 0)
    dist = jnp.abs(q_pos - k_pos).astype(F32)
    slopes = [2.0 ** (-8.0 * (head + 1) / A_HEADS) for head in range(A_HEADS)]
    return [jnp.where(allowed, (-LOG2E * slope) * dist, NEG_INF) for slope in slopes]


def _swa_block(q, kz, vz, biases, sink_ref, write_o, valid=None):
    tq = q.shape[0]
    pairs_per_kv = A_HEADS // A_KV_HEADS // 2
    for j in range(A_KV_HEADS):
        pairs = [j * pairs_per_kv + i for i in range(pairs_per_kv)]
        q_stack = jnp.concatenate([q[:, p * LANES:(p + 1) * LANES] for p in pairs], axis=0)
        acc = [None] * pairs_per_kv
        for e in range(2):
            logits_all = lax.dot_general(q_stack, kz[j][e], NT_DIMS, preferred_element_type=F32)
            probs, inv_denoms = [], []
            for i, p in enumerate(pairs):
                head = 2 * p + e
                sink = sink_ref[head] * LOG2E
                logits = logits_all[i * tq:(i + 1) * tq] + biases[head]
                if valid is not None:
                    logits = jnp.where(valid, logits, NEG_INF)
                mx = jnp.maximum(jnp.max(logits, axis=1, keepdims=True), sink)
                pr = jnp.exp2(logits - mx)
                inv_denoms.append(
                    1.0 / (jnp.sum(pr, axis=1, keepdims=True) + jnp.exp2(sink - mx)))
                probs.append(pr.astype(BF16))
            out_all = jnp.dot(jnp.concatenate(probs, axis=0), vz[j][e],
                              preferred_element_type=F32)
            for i in range(pairs_per_kv):
                part = out_all[i * tq:(i + 1) * tq] * inv_denoms[i]
                acc[i] = part if acc[i] is None else acc[i] + part
        for i, p in enumerate(pairs):
            write_o(slice(p * LANES, (p + 1) * LANES), acc[i].astype(BF16))


SWA_Q_ROWS = WIN_CHUNKS * CHUNK


def _swa_tile(sink_ref, q_ref, k_all, v_all, tile_start, write_o):
    tq = SWA_Q_ROWS
    kz = _kv_halves(k_all)
    vz = _kv_halves(v_all)
    rel_k = lax.broadcasted_iota(jnp.int32, (1, 2 * tq), 1)
    biases = _swa_biases(tq + lax.broadcasted_iota(jnp.int32, (tq, 1), 0), rel_k)
    for r in range(q_ref.shape[1] // tq):
        rows = slice(r * tq, (r + 1) * tq)
        band = slice(r * tq, (r + 2) * tq)
        valid = (tile_start - tq + rel_k >= 0) if r == 0 else None
        _swa_block(q_ref[0, rows, :],
                   [tuple(h[band] for h in pair) for pair in kz],
                   [tuple(h[band] for h in pair) for pair in vz],
                   biases, sink_ref,
                   lambda lanes, value, rows=rows: write_o(rows, lanes, value), valid=valid)


def _mix_residual_norm(x, mixed, ga1, sc2, sh2, g2, w_out):
    s, r, d = x.shape
    mix = jnp.dot(mixed, w_out, preferred_element_type=F32)
    x1 = x + ga1 * mix.reshape(s, r, d)
    ms = jnp.mean(x1 * x1, axis=-1, keepdims=True)
    h2 = (x1 * lax.rsqrt(ms + EPS) * g2) * (1.0 + sc2) + sh2
    return x1, h2.reshape(s * r, d).astype(BF16)


def _ffn_block(hb, w_up, w_down):
    u = jnp.dot(hb, w_up, preferred_element_type=F32)
    u = jnp.square(jnp.maximum(u, 0.0)).astype(BF16)
    return jnp.dot(u, w_down, preferred_element_type=F32)


def _out_ffn_tile(x, mixed, ga1, sc2, sh2, ga2, g2, wo_ref, wu_ref, wd_ref):
    s, r, d = x.shape
    x1, hb = _mix_residual_norm(x, mixed, ga1, sc2, sh2, g2, wo_ref[...])
    ffn = jnp.zeros((s * r, d), F32)
    for c in range(D_FF // FF_BLOCK):
        cols = slice(c * FF_BLOCK, (c + 1) * FF_BLOCK)
        ffn = ffn + _ffn_block(hb, wu_ref[:, cols], wd_ref[cols, :])
    return x1 + ga2 * ffn.reshape(s, r, d)


def _ffn_weight_specs(d):
    return [_const_spec((1, d)), _const_spec((d, d)), _const_spec((d, D_FF)),
            _const_spec((D_FF, d))]


def _layer_body(sink_ref, xp_ref, sc1_ref, sh1_ref, g1_ref, wa_ref, wb_ref, wg_ref, bg_ref,
                gq_ref, gk_ref, pq_ref, pk_ref, gout_ref,
                xf_ref, ga1_ref, sc2_ref, sh2_ref, ga2_ref, g2_ref, wo_ref, wu_ref, wd_ref,
                y_ref, kt_ref, vt_ref, c_out_ref, n_out_ref, m_out_ref,
                mq_s, mk_s, mv_s, og_s, gt_s, aq_s, ak_s, av_s, kprev_s, vprev_s,
                mix_s, caug_ref, m_ref, *, n_tiles, tiles_per_seq, chunk):
    s = pl.program_id(0)
    rows = xp_ref.shape[1]
    tile = jnp.minimum(s, n_tiles - 1)
    j = tile % tiles_per_seq
    valid = s < n_tiles
    proj_s = {"mq": mq_s, "mk": mk_s, "mv": mv_s, "og": og_s, "gt": gt_s,
              "aq": aq_s, "ak": ak_s, "av": av_s}

    @pl.when(s == 0)
    def _():
        kprev_s[...] = jnp.zeros(kprev_s.shape, kprev_s.dtype)
        vprev_s[...] = jnp.zeros(vprev_s.shape, vprev_s.dtype)
        mix_s[...] = jnp.zeros(mix_s.shape, mix_s.dtype)

    @pl.when(jnp.logical_and(j == 0, valid))
    def _():
        caug_ref[...] = jnp.zeros(caug_ref.shape, caug_ref.dtype)
        m_ref[...] = jnp.zeros(m_ref.shape, m_ref.dtype)

    y_ref[...] = _out_ffn_tile(
        xf_ref[...], mix_s[...], ga1_ref[...], sc2_ref[...], sh2_ref[...], ga2_ref[...],
        g2_ref[...], wo_ref, wu_ref, wd_ref)

    def store(name, value):
        proj_s[name][...] = value if name == "gt" else value[None]

    _in_proj_tile(xp_ref[...], sc1_ref[...], sh1_ref[...], g1_ref[...], wa_ref, wb_ref, wg_ref,
                  bg_ref[...], gq_ref[...], gk_ref[...], pq_ref, pk_ref, store, chunk)

    def write_h(r, lanes, value):
        mix_s[r, lanes] = value

    def write_o(r, lanes, value):
        mix_s[r, slice(M_W + lanes.start, M_W + lanes.stop)] = value

    _mlstm_tile(mq_s, mk_s, mv_s, og_s, gt_s, gout_ref, caug_ref, m_ref, write_h, chunk)
    k_cur = ak_s[0]
    v_cur = av_s[0]
    _swa_tile(sink_ref, aq_s,
              jnp.concatenate([kprev_s[...], k_cur], axis=0),
              jnp.concatenate([vprev_s[...], v_cur], axis=0), j * rows, write_o)
    kprev_s[...] = k_cur[rows - SWA_Q_ROWS:]
    vprev_s[...] = v_cur[rows - SWA_Q_ROWS:]

    @pl.when(jnp.logical_and(j == tiles_per_seq - 1, valid))
    def _():
        kt_ref[0] = k_cur[rows - SWA_Q_ROWS:]
        vt_ref[0] = v_cur[rows - SWA_Q_ROWS:]
        _mlstm_store_state(caug_ref, m_ref, c_out_ref, n_out_ref, m_out_ref)


def _layer(x, mod4, mod_row0, sinks, gout, proj_w, ffn_w, *, rows, chunk):
    ns, length, d = x.shape
    tps = length // rows
    n_tiles = ns * tps

    def stage(lag):
        def tile_idx(i):
            t = jnp.clip(i - lag, 0, n_tiles - 1)
            return t // tps, t % tps
        x_tile = pl.BlockSpec((1, rows, d), lambda i: (*tile_idx(i), 0))
        mod = lambda k: pl.BlockSpec((1, None, 1, d),
                                     lambda i: (mod_row0 + tile_idx(i)[0], k, 0, 0))
        per_seq = lambda shape: pl.BlockSpec(
            (1,) + shape, lambda i: (tile_idx(i)[0],) + (0,) * len(shape))
        return x_tile, mod, per_seq

    xp_tile, p_mod, p_seq = stage(0)
    xf_tile, f_mod, _ = stage(1)
    st_c, st_n, st_m = p_seq((M_HEADS, M_DIM, M_DIM)), p_seq((M_HEADS, M_DIM)), p_seq((M_HEADS, 1))
    tail = p_seq((SWA_Q_ROWS, A_KW))
    one = lambda shape, dtype: pltpu.VMEM((1,) + shape, dtype)
    return pl.pallas_call(
        functools.partial(_layer_body, n_tiles=n_tiles, tiles_per_seq=tps, chunk=chunk),
        grid=(n_tiles + 1,),
        in_specs=([pl.BlockSpec(memory_space=pltpu.SMEM), xp_tile, p_mod(MOD_SC1), p_mod(MOD_SH1)]
                  + _proj_weight_specs(d)
                  + [_const_spec((M_HEADS, M_DIM)), xf_tile,
                     f_mod(MOD_GA1), f_mod(MOD_SC2), f_mod(MOD_SH2), f_mod(MOD_GA2)]
                  + _ffn_weight_specs(d)),
        out_specs=(xf_tile, tail, tail, st_c, st_n, st_m),
        out_shape=(jax.ShapeDtypeStruct((ns, length, d), F32),
                   jax.ShapeDtypeStruct((ns, SWA_Q_ROWS, A_KW), F32),
                   jax.ShapeDtypeStruct((ns, SWA_Q_ROWS, A_KW), F32),
                   jax.ShapeDtypeStruct((ns, M_HEADS, M_DIM, M_DIM), F32),
                   jax.ShapeDtypeStruct((ns, M_HEADS, M_DIM), F32),
                   jax.ShapeDtypeStruct((ns, M_HEADS, 1), F32)),
        scratch_shapes=[one((rows, M_W), BF16), one((rows, M_W), BF16),
                        one((rows, M_W), BF16), one((rows, M_W), BF16),
                        pltpu.VMEM((GATE_ROWS, rows), F32), one((rows, A_QW), BF16),
                        one((rows, A_KW), F32), one((rows, A_KW), F32),
                        pltpu.VMEM((SWA_Q_ROWS, A_KW), F32), pltpu.VMEM((SWA_Q_ROWS, A_KW), F32),
                        pltpu.VMEM((rows, M_W + A_QW), BF16),
                        pltpu.VMEM((M_HEADS, M_DIM, 2 * M_DIM), F32),
                        pltpu.VMEM((M_HEADS, 1), F32)],
        compiler_params=_params(("arbitrary",)),
        name="layer",
    )(sinks, x, mod4, mod4, *proj_w, gout, x, mod4, mod4, mod4, mod4, *ffn_w)


class _LaneWindow:
    def __init__(self, ref, offset):
        self.ref, self.offset = ref, offset

    def __getitem__(self, idx):
        rows, cols = idx
        return self.ref[rows, slice(cols.start + self.offset, cols.stop + self.offset)]


def _sample_mixers(sink_ref, x_ref, sc1_ref, sh1_ref, g1_ref, wa_ref, wb_ref, wg_ref, bg_ref,
                   gq_ref, gk_ref, pq_ref, pk_ref, c0_ref, n0_ref, m0_ref, gout_ref,
                   ck_ref, cv_ref, ko_ref, vo_ref, c_out_ref, n_out_ref, m_out_ref,
                   mq_s, mk_s, mv_s, og_s, gt_s, aq_s, ak_s, av_s, mix_s, caug_s, m_s):
    ns, t, _ = x_ref.shape
    cached = ck_ref.shape[1]
    proj_s = {"mq": mq_s, "mk": mk_s, "mv": mv_s, "og": og_s, "gt": gt_s,
              "aq": aq_s, "ak": ak_s, "av": av_s}

    def store(name, value):
        proj_s[name][...] = value if name == "gt" else value.reshape(ns, t, value.shape[-1])

    _in_proj_tile(x_ref[...], sc1_ref[...], sh1_ref[...], g1_ref[...], wa_ref, wb_ref, wg_ref,
                  bg_ref[...], gq_ref[...], gk_ref[...], pq_ref, pk_ref, store, t)

    seq = lambda ref, b: ref.at[pl.ds(b, 1)]
    for b in range(ns):
        caug_b, m_b = caug_s.at[b], m_s.at[b]
        _mlstm_load_state(seq(c0_ref, b), seq(n0_ref, b), seq(m0_ref, b), caug_b, m_b)

        def write_h(rows, lanes, value, b=b):
            mix_s[b * t + rows.start:b * t + rows.stop, lanes] = value

        _mlstm_tile(seq(mq_s, b), seq(mk_s, b), seq(mv_s, b), seq(og_s, b),
                    _LaneWindow(gt_s, b * t), gout_ref, caug_b, m_b, write_h, t)
        _mlstm_store_state(caug_b, m_b, seq(c_out_ref, b), seq(n_out_ref, b), seq(m_out_ref, b))

    tk = 2 * LANES
    pad = jnp.zeros((tk - cached - t, A_KW), F32)
    q_pos = PAST_LEN + lax.broadcasted_iota(jnp.int32, (t, 1), 0)
    idx = lax.broadcasted_iota(jnp.int32, (1, tk), 1)
    k_pos = jnp.where(idx < cached + t, PAST_LEN - cached + idx, -1)
    biases = _swa_biases(q_pos, k_pos)
    for b in range(ns):
        k_new, v_new = ak_s[b], av_s[b]

        def write_o(lanes, value, b=b):
            mix_s[b * t:(b + 1) * t, M_W + lanes.start:M_W + lanes.stop] = value

        _swa_block(aq_s[b], _kv_halves(jnp.concatenate([ck_ref[b], k_new, pad], axis=0)),
                   _kv_halves(jnp.concatenate([cv_ref[b], v_new, pad], axis=0)),
                   biases, sink_ref, write_o)
        ko_ref[b] = jnp.concatenate([ck_ref[b, t:, :], k_new], axis=0)
        vo_ref[b] = jnp.concatenate([cv_ref[b, t:, :], v_new], axis=0)


SAMPLE_FF_BLOCK = 512
SAMPLE_VMEM_LIMIT = 60 * 1024 * 1024


def _sample_body(sink_ref, x_ref, sc1_ref, sh1_ref, g1_ref, wt_ref, bg_ref, gq_ref, gk_ref,
                 pq_ref, pk_ref, c0_ref, n0_ref, m0_ref, gout_ref, ck_ref, cv_ref,
                 ga1_ref, sc2_ref, sh2_ref, ga2_ref, g2_ref, wo_ref, wu_ref, wd_ref,
                 y_ref, ko_ref, vo_ref, c_out_ref, n_out_ref, m_out_ref,
                 wa_o, wb_o, wg_o, wo_o, wu_o, wd_o,
                 mq_s, mk_s, mv_s, og_s, gt_s, aq_s, ak_s, av_s, mix_s, caug_s, m_s,
                 x1_s, hb_s, acc_s):
    step = pl.program_id(0)
    ns, t, d = x_ref.shape

    @pl.when(step == 0)
    def _():
        g_lo = 4 * M_W
        g_hi = g_lo + 2 * M_HEADS
        wa_o[...] = wt_ref[:g_lo, :].astype(BF16)
        wg_o[...] = jnp.concatenate(
            [wt_ref[g_lo:g_hi, :], jnp.zeros((BF16_ROWS - 2 * M_HEADS, d), F32)],
            axis=0).astype(BF16)
        wb_o[...] = wt_ref[g_hi:, :].astype(BF16)
        wo_o[...] = wo_ref[...].astype(BF16)
        _sample_mixers(sink_ref, x_ref, sc1_ref, sh1_ref, g1_ref, wa_o, wb_o, wg_o, bg_ref,
                       gq_ref, gk_ref, pq_ref, pk_ref, c0_ref, n0_ref, m0_ref, gout_ref,
                       ck_ref, cv_ref, ko_ref, vo_ref, c_out_ref, n_out_ref, m_out_ref,
                       mq_s, mk_s, mv_s, og_s, gt_s, aq_s, ak_s, av_s, mix_s, caug_s, m_s)
        x1, hb = _mix_residual_norm(x_ref[...], mix_s[...], ga1_ref[...], sc2_ref[...],
                                    sh2_ref[...], g2_ref[...], wo_o[...])
        x1_s[...] = x1
        hb_s[...] = hb
        acc_s[...] = jnp.zeros(acc_s.shape, acc_s.dtype)

    @pl.when(step >= 1)
    def _():
        w_up = wu_ref[...].astype(BF16)
        w_down = wd_ref[...].astype(BF16)
        wu_o[...] = w_up
        wd_o[...] = w_down
        acc_s[...] += _ffn_block(hb_s[...], w_up, w_down)

    @pl.when(step == pl.num_programs(0) - 1)
    def _():
        y_ref[...] = x1_s[...] + ga2_ref[...] * acc_s[...].reshape(ns, t, d)


def _sample_layer(x, mod4, sinks, c0, n0, m0, gout, cache_k, cache_v, proj_small, wt, g2,
                  w_out, w_up, w_down):
    ns, t, d = x.shape
    cached = cache_k.shape[1]
    assert cached >= t
    g1, bg, gq_t, gk_t, pq, pk = proj_small
    fb = SAMPLE_FF_BLOCK
    n_blocks = D_FF // fb
    blk = lambda i: jnp.maximum(i - 1, 0)
    whole = lambda shape: pl.BlockSpec(shape, lambda i: (0,) * len(shape))
    mod = lambda k: pl.BlockSpec((ns, None, 1, d), lambda i: (0, k, 0, 0))
    st_c, st_n, st_m = (whole((ns, M_HEADS, M_DIM, M_DIM)), whole((ns, M_HEADS, M_DIM)),
                        whole((ns, M_HEADS, 1)))
    kv = whole((ns, cached, A_KW))
    up_blk = pl.BlockSpec((d, fb), lambda i: (0, blk(i)))
    down_blk = pl.BlockSpec((fb, d), lambda i: (blk(i), 0))
    n_in = wt.shape[0]
    n_b = n_in - 4 * M_W - 2 * M_HEADS
    return pl.pallas_call(
        _sample_body,
        grid=(1 + n_blocks,),
        in_specs=[pl.BlockSpec(memory_space=pltpu.SMEM), whole((ns, t, d)),
                  mod(MOD_SC1), mod(MOD_SH1),
                  _const_spec((1, d)), _const_spec((n_in, d)), _const_spec((2 * M_HEADS, 1)),
                  _const_spec((1, A_QW)), _const_spec((1, A_KW)),
                  _const_spec((A_QW, A_QW)), _const_spec((A_KW, A_KW)),
                  st_c, st_n, st_m, _const_spec((M_HEADS, M_DIM)), kv, kv,
                  mod(MOD_GA1), mod(MOD_SC2), mod(MOD_SH2), mod(MOD_GA2),
                  _const_spec((1, d)), _const_spec((d, d)), up_blk, down_blk],
        out_specs=(whole((ns, t, d)), kv, kv, st_c, st_n, st_m,
                   _const_spec((4 * M_W, d)), _const_spec((n_b, d)), _const_spec((BF16_ROWS, d)),
                   _const_spec((d, d)), up_blk, down_blk),
        out_shape=(jax.ShapeDtypeStruct((ns, t, d), F32),
                   jax.ShapeDtypeStruct((ns, cached, A_KW), F32),
                   jax.ShapeDtypeStruct((ns, cached, A_KW), F32),
                   jax.ShapeDtypeStruct((ns, M_HEADS, M_DIM, M_DIM), F32),
                   jax.ShapeDtypeStruct((ns, M_HEADS, M_DIM), F32),
                   jax.ShapeDtypeStruct((ns, M_HEADS, 1), F32),
                   jax.ShapeDtypeStruct((4 * M_W, d), BF16),
                   jax.ShapeDtypeStruct((n_b, d), BF16),
                   jax.ShapeDtypeStruct((BF16_ROWS, d), BF16),
                   jax.ShapeDtypeStruct((d, d), BF16),
                   jax.ShapeDtypeStruct((d, D_FF), BF16),
                   jax.ShapeDtypeStruct((D_FF, d), BF16)),
        scratch_shapes=[pltpu.VMEM((ns, t, M_W), BF16), pltpu.VMEM((ns, t, M_W), BF16),
                        pltpu.VMEM((ns, t, M_W), BF16), pltpu.VMEM((ns, t, M_W), BF16),
                        pltpu.VMEM((GATE_ROWS, ns * t), F32), pltpu.VMEM((ns, t, A_QW), BF16),
                        pltpu.VMEM((ns, t, A_KW), F32), pltpu.VMEM((ns, t, A_KW), F32),
                        pltpu.VMEM((ns * t, M_W + A_QW), BF16),
                        pltpu.VMEM((ns, M_HEADS, M_DIM, 2 * M_DIM), F32),
                        pltpu.VMEM((ns, M_HEADS, 1), F32),
                        pltpu.VMEM((ns, t, d), F32), pltpu.VMEM((ns * t, d), BF16),
                        pltpu.VMEM((ns * t, d), F32)],
        compiler_params=_params(("arbitrary",), SAMPLE_VMEM_LIMIT),
        name="sample_layer",
    )(sinks, x, mod4, mod4, g1, wt, bg, gq_t, gk_t, pq, pk, c0, n0, m0, gout, cache_k, cache_v,
      mod4, mod4, mod4, mod4, g2, w_out, w_up, w_down)


def _block_ones(n):
    g = jnp.arange(n, dtype=jnp.int32) // A_DIM
    return (g[:, None] == g[None, :]).astype(BF16)


def kernel(x_prompt, x_sample, c_prompt, c_sample, cache_swa_k, cache_swa_v, state_mlstm_C,
           state_mlstm_n, state_mlstm_m, w_ada, b_ada, g_norm1, w_in, b_gates, g_q, g_k, sinks,
           g_mlstm_out, w_out, g_norm2, w_up, w_down):
    depth = w_ada.shape[0]
    assert depth == 1
    bp, lp, d = x_prompt.shape
    bs, ls, _ = x_sample.shape
    wb = cache_swa_k.shape[2]

    xp, xs = x_prompt, x_sample
    for l in range(depth):
        gq_t = (jnp.tile(g_q[l], A_HEADS) * (LOG2E * A_DIM ** -0.5))[None, :]
        proj_small = (g_norm1[l][None, :], b_gates[l][:, None], gq_t,
                      jnp.tile(g_k[l], A_KV_HEADS)[None, :], _block_ones(A_QW), _block_ones(A_KW))
        g2 = g_norm2[l][None, :]

        mod_rows = -(-(bp + bs) // 8) * 8
        c_all = jnp.concatenate([c_sample, c_prompt, jnp.zeros((mod_rows - bp - bs, d), F32)],
                                axis=0)
        mod4 = _ada_mod(c_all, w_ada[l], b_ada[l][None, :]).reshape(mod_rows, 6, 1, d)

        xs, ko, vo, c, n, m, wa, wb_, wg, wo, wu, wd = _sample_layer(
            xs, mod4, sinks[l], state_mlstm_C[l], state_mlstm_n[l], state_mlstm_m[l][..., None],
            g_mlstm_out[l], cache_swa_k[l].reshape(bs, wb, A_KW),
            cache_swa_v[l].reshape(bs, wb, A_KW), proj_small, w_in[l].T, g2,
            w_out[l], w_up[l], w_down[l])
        st_s = (ko.reshape(bs, wb, A_KV_HEADS, A_DIM), vo.reshape(bs, wb, A_KV_HEADS, A_DIM),
                c, n, m[..., 0])

        g1, bg, gq_t, gk_t, pq, pk = proj_small
        xp, kt, vt, c, n, m = _layer(xp, mod4, bs, sinks[l], g_mlstm_out[l],
                                     (g1, wa, wb_, wg, bg, gq_t, gk_t, pq, pk), (g2, wo, wu, wd),
                                     rows=ROW_TILE, chunk=MLSTM_CHUNK)
        win = WIN_CHUNKS * CHUNK
        st_p = (kt.reshape(bp, win, A_KV_HEADS, A_DIM), vt.reshape(bp, win, A_KV_HEADS, A_DIM),
                c, n, m[..., 0])

    return (xp, xs) + tuple(s[None] for s in st_p) + tuple(s[None] for s in st_s)
```

```python
import functools

import jax
import jax.numpy as jnp
from jax import lax
from jax.experimental import pallas as pl
from jax.experimental.pallas import tpu as pltpu

F32 = jnp.float32
BF16 = jnp.bfloat16

D_MODEL = 1024
CHUNK = 64
M_HEADS = 4
M_DIM = 128
M_W = M_HEADS * M_DIM
A_HEADS = 8
A_KV_HEADS = 2
A_DIM = 64
A_QW = A_HEADS * A_DIM
A_KW = A_KV_HEADS * A_DIM
WIN_CHUNKS = 2
D_FF = 4 * D_MODEL
EPS = 1e-6
PAST_LEN = 4096

LANES = 128
BF16_ROWS = 16
VMEM_LIMIT = 56 * 1024 * 1024
NEG_INF = float("-inf")
LOG2E = 1.4426950408889634
ROW_TILE = 512
MLSTM_CHUNK = 128
FF_BLOCK = 1024
NT_DIMS = (((1,), (1,)), ((), ()))
MOD_SH1, MOD_SC1, MOD_GA1, MOD_SH2, MOD_SC2, MOD_GA2 = range(6)


def _params(sem, vmem_limit=VMEM_LIMIT):
    return pltpu.CompilerParams(dimension_semantics=sem, vmem_limit_bytes=vmem_limit)


def _const_spec(shape):
    nd = len(shape)
    return pl.BlockSpec(shape, lambda *_: (0,) * nd, pipeline_mode=pl.Buffered(1))


def _ada_body(c_ref, w_ref, b_ref, o_ref):
    c = c_ref[...]
    s = c * jax.nn.sigmoid(c)
    o_ref[...] = jnp.dot(s.astype(BF16), w_ref[...].astype(BF16),
                         preferred_element_type=F32) + b_ref[...]


def _ada_mod(c_all, w_ada, b_ada):
    rows = c_all.shape[0]
    return pl.pallas_call(
        _ada_body,
        grid=(6,),
        in_specs=[pl.BlockSpec((rows, D_MODEL), lambda j: (0, 0)),
                  pl.BlockSpec((D_MODEL, D_MODEL), lambda j: (0, j)),
                  pl.BlockSpec((1, D_MODEL), lambda j: (0, j))],
        out_specs=pl.BlockSpec((rows, D_MODEL), lambda j: (0, j)),
        out_shape=jax.ShapeDtypeStruct((rows, 6 * D_MODEL), F32),
        compiler_params=_params(("arbitrary",)),
        name="ada_mod",
    )(c_all, w_ada, b_ada)


def _segment_prefix(x, op, identity, seg):
    assert x.shape[1] % LANES == 0 and x.shape[1] % seg == 0
    pos = lax.broadcasted_iota(jnp.int32, x.shape, 1) % seg
    d = 1
    while d < seg:
        x = op(x, jnp.where(pos >= d, pltpu.roll(x, d, axis=1), identity))
        d *= 2
    return x


GATE_ROWS = 24
PROJ_OUTPUTS = ("mq", "mk", "mv", "og", "gt", "aq", "ak", "av")


def _in_proj_tile(x, sc, sh, g1, wa_ref, wb_ref, wg_ref, bg, gq, gk, pq_ref, pk_ref, store,
                  gate_chunk):
    s, r, d = x.shape
    m = s * r
    ms = jnp.mean(x * x, axis=-1, keepdims=True)
    y = x * lax.rsqrt(ms + EPS) * g1
    h = y * (1.0 + sc) + sh
    hb = h.reshape(m, d).astype(BF16)

    def proj(w_ref, lo, width):
        return lax.dot_general(hb, w_ref[lo:lo + width, :], NT_DIMS, preferred_element_type=F32)

    zg = lax.dot_general(wg_ref[...], hb, NT_DIMS,
                         preferred_element_type=F32)[:2 * M_HEADS] + bg
    row = lax.broadcasted_iota(jnp.int32, zg.shape, 0)
    log_sig = jnp.minimum(zg, 0.0) - jnp.log1p(jnp.exp(-jnp.abs(zg)))
    gt = jnp.where(row < M_HEADS, zg, log_sig)
    b8 = _segment_prefix(gt, jnp.add, 0.0, gate_chunk)
    c8 = gt - pltpu.roll(b8, M_HEADS, axis=0)
    cm8 = _segment_prefix(c8, jnp.maximum, NEG_INF, gate_chunk)
    store("gt", jnp.concatenate([c8, b8, cm8], axis=0))

    store("mq", proj(wa_ref, 0, M_W).astype(BF16))
    store("mk", (proj(wa_ref, M_W, M_W) * (M_DIM ** -0.5)).astype(BF16))
    store("mv", proj(wa_ref, 2 * M_W, M_W).astype(BF16))
    store("og", jax.nn.sigmoid(proj(wa_ref, 3 * M_W, M_W)).astype(BF16))

    aq = proj(wb_ref, 0, A_QW)
    ss_q = jnp.dot((aq * aq).astype(BF16), pq_ref[...], preferred_element_type=F32)
    store("aq", (aq * lax.rsqrt(ss_q * (1.0 / A_DIM) + EPS) * gq).astype(BF16))
    ak = proj(wb_ref, A_QW, A_KW)
    sq = ak * ak
    hi = sq.astype(BF16)
    lo = (sq - hi.astype(F32)).astype(BF16)
    ss_k = (jnp.dot(hi, pk_ref[...], preferred_element_type=F32)
            + jnp.dot(lo, pk_ref[...], preferred_element_type=F32))
    store("ak", ak * lax.rsqrt(ss_k * (1.0 / A_DIM) + EPS) * gk)
    store("av", proj(wb_ref, A_QW + A_KW, A_KW))


def _proj_weight_specs(d):
    return [_const_spec((1, d)), _const_spec((4 * M_W, d)), _const_spec((A_QW + 2 * A_KW, d)),
            _const_spec((BF16_ROWS, d)), _const_spec((2 * M_HEADS, 1)),
            _const_spec((1, A_QW)), _const_spec((1, A_KW)),
            _const_spec((A_QW, A_QW)), _const_spec((A_KW, A_KW))]


def _col_bcast(r):
    t = r.shape[1]
    if t % LANES == 0:
        return jnp.broadcast_to(r, (LANES, t)).T
    src = lax.broadcasted_iota(jnp.int32, (t, t), 0)
    dst = lax.broadcasted_iota(jnp.int32, (t, t), 1)
    col = jnp.sum(jnp.where(src == dst, jnp.broadcast_to(r, (t, t)), 0.0), axis=1, keepdims=True)
    return jnp.broadcast_to(col, (t, LANES))


def _eye_dim():
    return (lax.broadcasted_iota(jnp.int32, (M_DIM, M_DIM), 0)
            == lax.broadcasted_iota(jnp.int32, (M_DIM, M_DIM), 1))


def _mlstm_load_state(c0_ref, n0_ref, m0_ref, caug_ref, m_ref):
    for hd in range(M_HEADS):
        caug_ref[hd, :, :M_DIM] = c0_ref[0, hd]
        n_row = jnp.broadcast_to(n0_ref[0, hd:hd + 1, :], (M_DIM, M_DIM))
        n_col = jnp.sum(jnp.where(_eye_dim(), n_row, 0.0), axis=1, keepdims=True)
        caug_ref[hd, :, M_DIM:] = jnp.broadcast_to(n_col, (M_DIM, M_DIM))
    m_ref[...] = m0_ref[0]


def _mlstm_store_state(caug_ref, m_ref, c_out_ref, n_out_ref, m_out_ref):
    for hd in range(M_HEADS):
        c_out_ref[0, hd] = caug_ref[hd, :, :M_DIM]
        n_out_ref[0, hd:hd + 1, :] = jnp.sum(
            jnp.where(_eye_dim(), caug_ref[hd, :, M_DIM:], 0.0), axis=0, keepdims=True)
    m_out_ref[0] = m_ref[...]


def _mlstm_tile(q_ref, k_ref, v_ref, og_ref, gt_ref, gout_ref, caug_ref, m_ref, write_h, chunk):
    t = chunk
    n_chunks = q_ref.shape[1] // t
    bs = LANES if t % LANES == 0 else t
    tril = (lax.broadcasted_iota(jnp.int32, (bs, bs), 1)
            <= lax.broadcasted_iota(jnp.int32, (bs, bs), 0))
    m_prev = m_ref[...]
    for j in range(n_chunks):
        rows = slice(j * t, (j + 1) * t)
        c = gt_ref[0:M_HEADS, rows]
        b = gt_ref[12:12 + M_HEADS, rows]
        g = jnp.maximum(gt_ref[16:16 + M_HEADS, rows], m_prev)
        g_last = g[:, t - 1:t]
        c2 = c * LOG2E
        g2 = g * LOG2E
        e_all = jnp.exp2(-LOG2E * (b + g))
        w_src_all = jnp.exp2(c2 - LOG2E * g_last)
        w_inter_all = jnp.exp2(LOG2E * m_prev - g2)
        w_old_all = jnp.exp2(LOG2E * (m_prev - g_last))
        m_prev = b[:, t - 1:t] + g_last

        for hd in range(M_HEADS):
            sl = slice(hd * M_DIM, (hd + 1) * M_DIM)
            hr = slice(hd, hd + 1)
            g_b = _col_bcast(g2[hr])
            e_b = _col_bcast(e_all[hr])
            w_src = _col_bcast(w_src_all[hr])
            w_inter = _col_bcast(w_inter_all[hr])

            q = q_ref[0, rows, sl]
            k = k_ref[0, rows, sl]
            v = v_ref[0, rows, sl]
            caug = caug_ref[hd]
            inter = jnp.dot(q, caug.astype(BF16), preferred_element_type=F32)
            gain = gout_ref[hr, :]
            for rb in range(t // bs):
                rsl = slice(rb * bs, (rb + 1) * bs)
                ncols = (rb + 1) * bs
                s_mat = lax.dot_general(q[rsl], k[:ncols], NT_DIMS,
                                        preferred_element_type=F32)
                decay = []
                for cb in range(rb + 1):
                    arg = c2[hr, cb * bs:(cb + 1) * bs] - g_b[rsl, :bs]
                    decay.append(jnp.exp2(jnp.where(tril, arg, NEG_INF) if cb == rb else arg))
                s_mat = s_mat * (decay[0] if rb == 0 else jnp.concatenate(decay, axis=1))
                num = (jnp.dot(s_mat.astype(BF16), v[:ncols], preferred_element_type=F32)
                       + w_inter[rsl] * inter[rsl, :M_DIM])
                den = jnp.sum(s_mat, axis=1, keepdims=True) + w_inter[rsl] * inter[rsl, M_DIM:]
                hh = num / jnp.maximum(jnp.abs(den), e_b[rsl])
                hn = hh * lax.rsqrt(jnp.mean(hh * hh, axis=-1, keepdims=True) + EPS) * gain
                out_rows = slice(j * t + rb * bs, j * t + (rb + 1) * bs)
                write_h(out_rows, sl, (hn * og_ref[0, out_rows, sl].astype(F32)).astype(BF16))

            rhs = jnp.concatenate([v.astype(F32) * w_src, w_src], axis=1).astype(BF16)
            upd = lax.dot_general(k, rhs, (((0,), (0,)), ((), ())), preferred_element_type=F32)
            caug_ref[hd] = w_old_all[hr] * caug + upd
    m_ref[...] = m_prev


def _kv_halves(a):
    low = lax.broadcasted_iota(jnp.int32, a.shape, 1) < A_DIM
    lo_half = jnp.where(low, a, 0.0)
    hi_half = jnp.where(low, 0.0, a)
    pairs = [(lo_half, pltpu.roll(lo_half, A_DIM, axis=1)),
             (pltpu.roll(hi_half, A_DIM, axis=1), hi_half)]
    return [tuple(h.astype(BF16) for h in pair) for pair in pairs]


def _swa_biases(q_pos, k_pos):
    shift = CHUNK.bit_length() - 1
    assert 1 << shift == CHUNK
    qc = lax.shift_right_arithmetic(q_pos, shift)
    kc = lax.shift_right_arithmetic(k_pos, shift)
    allowed = (kc <= qc) & (kc >= qc - WIN_CHUNKS) & (k_pos >= 0)
    dist = jnp.abs(q_pos - k_pos).astype(F32)
    slopes = [2.0 ** (-8.0 * (head + 1) / A_HEADS) for head in range(A_HEADS)]
    return [jnp.where(allowed, (-LOG2E * slope) * dist, NEG_INF) for slope in slopes]


def _swa_block(q, kz, vz, biases, sink_ref, write_o, valid=None):
    tq = q.shape[0]
    pairs_per_kv = A_HEADS // A_KV_HEADS // 2
    for j in range(A_KV_HEADS):
        pairs = [j * pairs_per_kv + i for i in range(pairs_per_kv)]
        q_stack = jnp.concatenate([q[:, p * LANES:(p + 1) * LANES] for p in pairs], axis=0)
        acc = [None] * pairs_per_kv
        for e in range(2):
            logits_all = lax.dot_general(q_stack, kz[j][e], NT_DIMS, preferred_element_type=F32)
            probs, inv_denoms = [], []
            for i, p in enumerate(pairs):
                head = 2 * p + e
                sink = sink_ref[head] * LOG2E
                logits = logits_all[i * tq:(i + 1) * tq] + biases[head]
                if valid is not None:
                    logits = jnp.where(valid, logits, NEG_INF)
                mx = jnp.maximum(jnp.max(logits, axis=1, keepdims=True), sink)
                pr = jnp.exp2(logits - mx)
                inv_denoms.append(
                    1.0 / (jnp.sum(pr, axis=1, keepdims=True) + jnp.exp2(sink - mx)))
                probs.append(pr.astype(BF16))
            out_all = jnp.dot(jnp.concatenate(probs, axis=0), vz[j][e],
                              preferred_element_type=F32)
            for i in range(pairs_per_kv):
                part = out_all[i * tq:(i + 1) * tq] * inv_denoms[i]
                acc[i] = part if acc[i] is None else acc[i] + part
        for i, p in enumerate(pairs):
            write_o(slice(p * LANES, (p + 1) * LANES), acc[i].astype(BF16))


SWA_Q_ROWS = WIN_CHUNKS * CHUNK


def _swa_tile(sink_ref, q_ref, k_all, v_all, tile_start, write_o):
    tq = SWA_Q_ROWS
    kz = _kv_halves(k_all)
    vz = _kv_halves(v_all)
    rel_k = lax.broadcasted_iota(jnp.int32, (1, 2 * tq), 1)
    biases = _swa_biases(tq + lax.broadcasted_iota(jnp.int32, (tq, 1), 0), rel_k)
    for r in range(q_ref.shape[1] // tq):
        rows = slice(r * tq, (r + 1) * tq)
        band = slice(r * tq, (r + 2) * tq)
        valid = (tile_start - tq + rel_k >= 0) if r == 0 else None
        _swa_block(q_ref[0, rows, :],
                   [tuple(h[band] for h in pair) for pair in kz],
                   [tuple(h[band] for h in pair) for pair in vz],
                   biases, sink_ref,
                   lambda lanes, value, rows=rows: write_o(rows, lanes, value), valid=valid)


def _mix_residual_norm(x, mixed, ga1, sc2, sh2, g2, w_out):
    s, r, d = x.shape
    mix = jnp.dot(mixed, w_out, preferred_element_type=F32)
    x1 = x + ga1 * mix.reshape(s, r, d)
    ms = jnp.mean(x1 * x1, axis=-1, keepdims=True)
    h2 = (x1 * lax.rsqrt(ms + EPS) * g2) * (1.0 + sc2) + sh2
    return x1, h2.reshape(s * r, d).astype(BF16)


def _ffn_block(hb, w_up, w_down):
    u = jnp.dot(hb, w_up, preferred_element_type=F32)
    u = jnp.square(jnp.maximum(u, 0.0)).astype(BF16)
    return jnp.dot(u, w_down, preferred_element_type=F32)


def _out_ffn_tile(x, mixed, ga1, sc2, sh2, ga2, g2, wo_ref, wu_ref, wd_ref):
    s, r, d = x.shape
    x1, hb = _mix_residual_norm(x, mixed, ga1, sc2, sh2, g2, wo_ref[...])
    ffn = jnp.zeros((s * r, d), F32)
    for c in range(D_FF // FF_BLOCK):
        cols = slice(c * FF_BLOCK, (c + 1) * FF_BLOCK)
        ffn = ffn + _ffn_block(hb, wu_ref[:, cols], wd_ref[cols, :])
    return x1 + ga2 * ffn.reshape(s, r, d)


def _ffn_weight_specs(d):
    return [_const_spec((1, d)), _const_spec((d, d)), _const_spec((d, D_FF)),
            _const_spec((D_FF, d))]


def _layer_body(sink_ref, xp_ref, sc1_ref, sh1_ref, g1_ref, wa_ref, wb_ref, wg_ref, bg_ref,
                gq_ref, gk_ref, pq_ref, pk_ref, gout_ref,
                xf_ref, ga1_ref, sc2_ref, sh2_ref, ga2_ref, g2_ref, wo_ref, wu_ref, wd_ref,
                y_ref, kt_ref, vt_ref, c_out_ref, n_out_ref, m_out_ref,
                mq_s, mk_s, mv_s, og_s, gt_s, aq_s, ak_s, av_s, kprev_s, vprev_s,
                mix_s, caug_ref, m_ref, *, n_tiles, tiles_per_seq, chunk):
    s = pl.program_id(0)
    rows = xp_ref.shape[1]
    tile = jnp.minimum(s, n_tiles - 1)
    j = tile % tiles_per_seq
    valid = s < n_tiles
    proj_s = {"mq": mq_s, "mk": mk_s, "mv": mv_s, "og": og_s, "gt": gt_s,
              "aq": aq_s, "ak": ak_s, "av": av_s}

    @pl.when(s == 0)
    def _():
        kprev_s[...] = jnp.zeros(kprev_s.shape, kprev_s.dtype)
        vprev_s[...] = jnp.zeros(vprev_s.shape, vprev_s.dtype)

    @pl.when(jnp.logical_and(j == 0, valid))
    def _():
        caug_ref[...] = jnp.zeros(caug_ref.shape, caug_ref.dtype)
        m_ref[...] = jnp.zeros(m_ref.shape, m_ref.dtype)

    def ffn_stage():
        y_ref[...] = _out_ffn_tile(
            xf_ref[...], mix_s[...], ga1_ref[...], sc2_ref[...], sh2_ref[...], ga2_ref[...],
            g2_ref[...], wo_ref, wu_ref, wd_ref)

    def mix_stage():
        def store(name, value):
            proj_s[name][...] = value if name == "gt" else value[None]

        _in_proj_tile(xp_ref[...], sc1_ref[...], sh1_ref[...], g1_ref[...], wa_ref, wb_ref,
                      wg_ref, bg_ref[...], gq_ref[...], gk_ref[...], pq_ref, pk_ref, store, chunk)

        def write_h(r, lanes, value):
            mix_s[r, lanes] = value

        def write_o(r, lanes, value):
            mix_s[r, slice(M_W + lanes.start, M_W + lanes.stop)] = value

        _mlstm_tile(mq_s, mk_s, mv_s, og_s, gt_s, gout_ref, caug_ref, m_ref, write_h, chunk)
        k_cur = ak_s[0]
        v_cur = av_s[0]
        _swa_tile(sink_ref, aq_s,
                  jnp.concatenate([kprev_s[...], k_cur], axis=0),
                  jnp.concatenate([vprev_s[...], v_cur], axis=0), j * rows, write_o)
        kprev_s[...] = k_cur[rows - SWA_Q_ROWS:]
        vprev_s[...] = v_cur[rows - SWA_Q_ROWS:]

    @pl.when(s == 0)
    def _():
        mix_stage()

    @pl.when(jnp.logical_and(s > 0, valid))
    def _():
        ffn_stage()
        mix_stage()

    @pl.when(s == n_tiles)
    def _():
        ffn_stage()

    @pl.when(jnp.logical_and(j == tiles_per_seq - 1, valid))
    def _():
        kt_ref[0] = ak_s[0, rows - SWA_Q_ROWS:, :]
        vt_ref[0] = av_s[0, rows - SWA_Q_ROWS:, :]
        _mlstm_store_state(caug_ref, m_ref, c_out_ref, n_out_ref, m_out_ref)


def _layer(x, mod4, mod_row0, sinks, gout, proj_w, ffn_w, *, rows, chunk):
    ns, length, d = x.shape
    tps = length // rows
    n_tiles = ns * tps

    def stage(lag):
        def tile_idx(i):
            t = jnp.clip(i - lag, 0, n_tiles - 1)
            return t // tps, t % tps
        x_tile = pl.BlockSpec((1, rows, d), lambda i: (*tile_idx(i), 0))
        mod = lambda k: pl.BlockSpec((1, None, 1, d),
                                     lambda i: (mod_row0 + tile_idx(i)[0], k, 0, 0))
        per_seq = lambda shape: pl.BlockSpec(
            (1,) + shape, lambda i: (tile_idx(i)[0],) + (0,) * len(shape))
        return x_tile, mod, per_seq

    xp_tile, p_mod, p_seq = stage(0)
    xf_tile, f_mod, _ = stage(1)
    st_c, st_n, st_m = p_seq((M_HEADS, M_DIM, M_DIM)), p_seq((M_HEADS, M_DIM)), p_seq((M_HEADS, 1))
    tail = p_seq((SWA_Q_ROWS, A_KW))
    one = lambda shape, dtype: pltpu.VMEM((1,) + shape, dtype)
    return pl.pallas_call(
        functools.partial(_layer_body, n_tiles=n_tiles, tiles_per_seq=tps, chunk=chunk),
        grid=(n_tiles + 1,),
        in_specs=([pl.BlockSpec(memory_space=pltpu.SMEM), xp_tile, p_mod(MOD_SC1), p_mod(MOD_SH1)]
                  + _proj_weight_specs(d)
                  + [_const_spec((M_HEADS, M_DIM)), xf_tile,
                     f_mod(MOD_GA1), f_mod(MOD_SC2), f_mod(MOD_SH2), f_mod(MOD_GA2)]
                  + _ffn_weight_specs(d)),
        out_specs=(xf_tile, tail, tail, st_c, st_n, st_m),
        out_shape=(jax.ShapeDtypeStruct((ns, length, d), F32),
                   jax.ShapeDtypeStruct((ns, SWA_Q_ROWS, A_KW), F32),
                   jax.ShapeDtypeStruct((ns, SWA_Q_ROWS, A_KW), F32),
                   jax.ShapeDtypeStruct((ns, M_HEADS, M_DIM, M_DIM), F32),
                   jax.ShapeDtypeStruct((ns, M_HEADS, M_DIM), F32),
                   jax.ShapeDtypeStruct((ns, M_HEADS, 1), F32)),
        scratch_shapes=[one((rows, M_W), BF16), one((rows, M_W), BF16),
                        one((rows, M_W), BF16), one((rows, M_W), BF16),
                        pltpu.VMEM((GATE_ROWS, rows), F32), one((rows, A_QW), BF16),
                        one((rows, A_KW), F32), one((rows, A_KW), F32),
                        pltpu.VMEM((SWA_Q_ROWS, A_KW), F32), pltpu.VMEM((SWA_Q_ROWS, A_KW), F32),
                        pltpu.VMEM((rows, M_W + A_QW), BF16),
                        pltpu.VMEM((M_HEADS, M_DIM, 2 * M_DIM), F32),
                        pltpu.VMEM((M_HEADS, 1), F32)],
        compiler_params=_params(("arbitrary",)),
        name="layer",
    )(sinks, x, mod4, mod4, *proj_w, gout, x, mod4, mod4, mod4, mod4, *ffn_w)


class _LaneWindow:
    def __init__(self, ref, offset):
        self.ref, self.offset = ref, offset

    def __getitem__(self, idx):
        rows, cols = idx
        return self.ref[rows, slice(cols.start + self.offset, cols.stop + self.offset)]


def _sample_mixers(sink_ref, x_ref, sc1_ref, sh1_ref, g1_ref, wa_ref, wb_ref, wg_ref, bg_ref,
                   gq_ref, gk_ref, pq_ref, pk_ref, c0_ref, n0_ref, m0_ref, gout_ref,
                   ck_ref, cv_ref, ko_ref, vo_ref, c_out_ref, n_out_ref, m_out_ref,
                   mq_s, mk_s, mv_s, og_s, gt_s, aq_s, ak_s, av_s, mix_s, caug_s, m_s):
    ns, t, _ = x_ref.shape
    cached = ck_ref.shape[1]
    proj_s = {"mq": mq_s, "mk": mk_s, "mv": mv_s, "og": og_s, "gt": gt_s,
              "aq": aq_s, "ak": ak_s, "av": av_s}

    def store(name, value):
        proj_s[name][...] = value if name == "gt" else value.reshape(ns, t, value.shape[-1])

    _in_proj_tile(x_ref[...], sc1_ref[...], sh1_ref[...], g1_ref[...], wa_ref, wb_ref, wg_ref,
                  bg_ref[...], gq_ref[...], gk_ref[...], pq_ref, pk_ref, store, t)

    seq = lambda ref, b: ref.at[pl.ds(b, 1)]
    for b in range(ns):
        caug_b, m_b = caug_s.at[b], m_s.at[b]
        _mlstm_load_state(seq(c0_ref, b), seq(n0_ref, b), seq(m0_ref, b), caug_b, m_b)

        def write_h(rows, lanes, value, b=b):
            mix_s[b * t + rows.start:b * t + rows.stop, lanes] = value

        _mlstm_tile(seq(mq_s, b), seq(mk_s, b), seq(mv_s, b), seq(og_s, b),
                    _LaneWindow(gt_s, b * t), gout_ref, caug_b, m_b, write_h, t)
        _mlstm_store_state(caug_b, m_b, seq(c_out_ref, b), seq(n_out_ref, b), seq(m_out_ref, b))

    tk = 2 * LANES
    pad = jnp.zeros((tk - cached - t, A_KW), F32)
    q_pos = PAST_LEN + lax.broadcasted_iota(jnp.int32, (t, 1), 0)
    idx = lax.broadcasted_iota(jnp.int32, (1, tk), 1)
    k_pos = jnp.where(idx < cached + t, PAST_LEN - cached + idx, -1)
    biases = _swa_biases(q_pos, k_pos)
    for b in range(ns):
        k_new, v_new = ak_s[b], av_s[b]

        def write_o(lanes, value, b=b):
            mix_s[b * t:(b + 1) * t, M_W + lanes.start:M_W + lanes.stop] = value

        _swa_block(aq_s[b], _kv_halves(jnp.concatenate([ck_ref[b], k_new, pad], axis=0)),
                   _kv_halves(jnp.concatenate([cv_ref[b], v_new, pad], axis=0)),
                   biases, sink_ref, write_o)
        ko_ref[b] = jnp.concatenate([ck_ref[b, t:, :], k_new], axis=0)
        vo_ref[b] = jnp.concatenate([cv_ref[b, t:, :], v_new], axis=0)


SAMPLE_FF_BLOCK = 512
SAMPLE_VMEM_LIMIT = 60 * 1024 * 1024


def _sample_body(sink_ref, x_ref, sc1_ref, sh1_ref, g1_ref, wt_ref, bg_ref, gq_ref, gk_ref,
                 pq_ref, pk_ref, c0_ref, n0_ref, m0_ref, gout_ref, ck_ref, cv_ref,
                 ga1_ref, sc2_ref, sh2_ref, ga2_ref, g2_ref, wo_ref, wu_ref, wd_ref,
                 y_ref, ko_ref, vo_ref, c_out_ref, n_out_ref, m_out_ref,
                 wa_o, wb_o, wg_o, wo_o, wu_o, wd_o,
                 mq_s, mk_s, mv_s, og_s, gt_s, aq_s, ak_s, av_s, mix_s, caug_s, m_s,
                 x1_s, hb_s, acc_s):
    step = pl.program_id(0)
    ns, t, d = x_ref.shape

    @pl.when(step == 0)
    def _():
        g_lo = 4 * M_W
        g_hi = g_lo + 2 * M_HEADS
        wa_o[...] = wt_ref[:g_lo, :].astype(BF16)
        wg_o[...] = jnp.concatenate(
            [wt_ref[g_lo:g_hi, :], jnp.zeros((BF16_ROWS - 2 * M_HEADS, d), F32)],
            axis=0).astype(BF16)
        wb_o[...] = wt_ref[g_hi:, :].astype(BF16)
        wo_o[...] = wo_ref[...].astype(BF16)
        _sample_mixers(sink_ref, x_ref, sc1_ref, sh1_ref, g1_ref, wa_o, wb_o, wg_o, bg_ref,
                       gq_ref, gk_ref, pq_ref, pk_ref, c0_ref, n0_ref, m0_ref, gout_ref,
                       ck_ref, cv_ref, ko_ref, vo_ref, c_out_ref, n_out_ref, m_out_ref,
                       mq_s, mk_s, mv_s, og_s, gt_s, aq_s, ak_s, av_s, mix_s, caug_s, m_s)
        x1, hb = _mix_residual_norm(x_ref[...], mix_s[...], ga1_ref[...], sc2_ref[...],
                                    sh2_ref[...], g2_ref[...], wo_o[...])
        x1_s[...] = x1
        hb_s[...] = hb
        acc_s[...] = jnp.zeros(acc_s.shape, acc_s.dtype)

    @pl.when(step >= 1)
    def _():
        w_up = wu_ref[...].astype(BF16)
        w_down = wd_ref[...].astype(BF16)
        wu_o[...] = w_up
        wd_o[...] = w_down
        acc_s[...] += _ffn_block(hb_s[...], w_up, w_down)

    @pl.when(step == pl.num_programs(0) - 1)
    def _():
        y_ref[...] = x1_s[...] + ga2_ref[...] * acc_s[...].reshape(ns, t, d)


def _sample_layer(x, mod4, sinks, c0, n0, m0, gout, cache_k, cache_v, proj_small, wt, g2,
                  w_out, w_up, w_down):
    ns, t, d = x.shape
    cached = cache_k.shape[1]
    assert cached >= t
    g1, bg, gq_t, gk_t, pq, pk = proj_small
    fb = SAMPLE_FF_BLOCK
    n_blocks = D_FF // fb
    blk = lambda i: jnp.maximum(i - 1, 0)
    whole = lambda shape: pl.BlockSpec(shape, lambda i: (0,) * len(shape))
    mod = lambda k: pl.BlockSpec((ns, None, 1, d), lambda i: (0, k, 0, 0))
    st_c, st_n, st_m = (whole((ns, M_HEADS, M_DIM, M_DIM)), whole((ns, M_HEADS, M_DIM)),
                        whole((ns, M_HEADS, 1)))
    kv = whole((ns, cached, A_KW))
    up_blk = pl.BlockSpec((d, fb), lambda i: (0, blk(i)))
    down_blk = pl.BlockSpec((fb, d), lambda i: (blk(i), 0))
    n_in = wt.shape[0]
    n_b = n_in - 4 * M_W - 2 * M_HEADS
    return pl.pallas_call(
        _sample_body,
        grid=(1 + n_blocks,),
        in_specs=[pl.BlockSpec(memory_space=pltpu.SMEM), whole((ns, t, d)),
                  mod(MOD_SC1), mod(MOD_SH1),
                  _const_spec((1, d)), _const_spec((n_in, d)), _const_spec((2 * M_HEADS, 1)),
                  _const_spec((1, A_QW)), _const_spec((1, A_KW)),
                  _const_spec((A_QW, A_QW)), _const_spec((A_KW, A_KW)),
                  st_c, st_n, st_m, _const_spec((M_HEADS, M_DIM)), kv, kv,
                  mod(MOD_GA1), mod(MOD_SC2), mod(MOD_SH2), mod(MOD_GA2),
                  _const_spec((1, d)), _const_spec((d, d)), up_blk, down_blk],
        out_specs=(whole((ns, t, d)), kv, kv, st_c, st_n, st_m,
                   _const_spec((4 * M_W, d)), _const_spec((n_b, d)), _const_spec((BF16_ROWS, d)),
                   _const_spec((d, d)), up_blk, down_blk),
        out_shape=(jax.ShapeDtypeStruct((ns, t, d), F32),
                   jax.ShapeDtypeStruct((ns, cached, A_KW), F32),
                   jax.ShapeDtypeStruct((ns, cached, A_KW), F32),
                   jax.ShapeDtypeStruct((ns, M_HEADS, M_DIM, M_DIM), F32),
                   jax.ShapeDtypeStruct((ns, M_HEADS, M_DIM), F32),
                   jax.ShapeDtypeStruct((ns, M_HEADS, 1), F32),
                   jax.ShapeDtypeStruct((4 * M_W, d), BF16),
                   jax.ShapeDtypeStruct((n_b, d), BF16),
                   jax.ShapeDtypeStruct((BF16_ROWS, d), BF16),
                   jax.ShapeDtypeStruct((d, d), BF16),
                   jax.ShapeDtypeStruct((d, D_FF), BF16),
                   jax.ShapeDtypeStruct((D_FF, d), BF16)),
        scratch_shapes=[pltpu.VMEM((ns, t, M_W), BF16), pltpu.VMEM((ns, t, M_W), BF16),
                        pltpu.VMEM((ns, t, M_W), BF16), pltpu.VMEM((ns, t, M_W), BF16),
                        pltpu.VMEM((GATE_ROWS, ns * t), F32), pltpu.VMEM((ns, t, A_QW), BF16),
                        pltpu.VMEM((ns, t, A_KW), F32), pltpu.VMEM((ns, t, A_KW), F32),
                        pltpu.VMEM((ns * t, M_W + A_QW), BF16),
                        pltpu.VMEM((ns, M_HEADS, M_DIM, 2 * M_DIM), F32),
                        pltpu.VMEM((ns, M_HEADS, 1), F32),
                        pltpu.VMEM((ns, t, d), F32), pltpu.VMEM((ns * t, d), BF16),
                        pltpu.VMEM((ns * t, d), F32)],
        compiler_params=_params(("arbitrary",), SAMPLE_VMEM_LIMIT),
        name="sample_layer",
    )(sinks, x, mod4, mod4, g1, wt, bg, gq_t, gk_t, pq, pk, c0, n0, m0, gout, cache_k, cache_v,
      mod4, mod4, mod4, mod4, g2, w_out, w_up, w_down)


def _block_ones(n):
    g = jnp.arange(n, dtype=jnp.int32) // A_DIM
    return (g[:, None] == g[None, :]).astype(BF16)


def kernel(x_prompt, x_sample, c_prompt, c_sample, cache_swa_k, cache_swa_v, state_mlstm_C,
           state_mlstm_n, state_mlstm_m, w_ada, b_ada, g_norm1, w_in, b_gates, g_q, g_k, sinks,
           g_mlstm_out, w_out, g_norm2, w_up, w_down):
    depth = w_ada.shape[0]
    assert depth == 1
    bp, lp, d = x_prompt.shape
    bs, ls, _ = x_sample.shape
    wb = cache_swa_k.shape[2]

    xp, xs = x_prompt, x_sample
    for l in range(depth):
        gq_t = (jnp.tile(g_q[l], A_HEADS) * (LOG2E * A_DIM ** -0.5))[None, :]
        proj_small = (g_norm1[l][None, :], b_gates[l][:, None], gq_t,
                      jnp.tile(g_k[l], A_KV_HEADS)[None, :], _block_ones(A_QW), _block_ones(A_KW))
        g2 = g_norm2[l][None, :]

        mod_rows = -(-(bp + bs) // 8) * 8
        c_all = jnp.concatenate([c_sample, c_prompt, jnp.zeros((mod_rows - bp - bs, d), F32)],
                                axis=0)
        mod4 = _ada_mod(c_all, w_ada[l], b_ada[l][None, :]).reshape(mod_rows, 6, 1, d)

        xs, ko, vo, c, n, m, wa, wb_, wg, wo, wu, wd = _sample_layer(
            xs, mod4, sinks[l], state_mlstm_C[l], state_mlstm_n[l], state_mlstm_m[l][..., None],
            g_mlstm_out[l], cache_swa_k[l].reshape(bs, wb, A_KW),
            cache_swa_v[l].reshape(bs, wb, A_KW), proj_small, w_in[l].T, g2,
            w_out[l], w_up[l], w_down[l])
        st_s = (ko.reshape(bs, wb, A_KV_HEADS, A_DIM), vo.reshape(bs, wb, A_KV_HEADS, A_DIM),
                c, n, m[..., 0])

        g1, bg, gq_t, gk_t, pq, pk = proj_small
        xp, kt, vt, c, n, m = _layer(xp, mod4, bs, sinks[l], g_mlstm_out[l],
                                     (g1, wa, wb_, wg, bg, gq_t, gk_t, pq, pk), (g2, wo, wu, wd),
                                     rows=ROW_TILE, chunk=MLSTM_CHUNK)
        win = WIN_CHUNKS * CHUNK
        st_p = (kt.reshape(bp, win, A_KV_HEADS, A_DIM), vt.reshape(bp, win, A_KV_HEADS, A_DIM),
                c, n, m[..., 0])

    return (xp, xs) + tuple(s[None] for s in st_p) + tuple(s[None] for s in st_s)
```

```python
import functools

import jax
import jax.numpy as jnp
from jax import lax
from jax.experimental import pallas as pl
from jax.experimental.pallas import tpu as pltpu

F32 = jnp.float32
BF16 = jnp.bfloat16

D_MODEL = 1024
CHUNK = 64
M_HEADS = 4
M_DIM = 128
M_W = M_HEADS * M_DIM
A_HEADS = 8
A_KV_HEADS = 2
A_DIM = 64
A_QW = A_HEADS * A_DIM
A_KW = A_KV_HEADS * A_DIM
WIN_CHUNKS = 2
D_FF = 4 * D_MODEL
EPS = 1e-6
PAST_LEN = 4096

LANES = 128
BF16_ROWS = 16
VMEM_LIMIT = 56 * 1024 * 1024
NEG_INF = float("-inf")
LOG2E = 1.4426950408889634
ROW_TILE = 512
MLSTM_CHUNK = 128
FF_BLOCK = 1024
NT_DIMS = (((1,), (1,)), ((), ()))
MOD_SH1, MOD_SC1, MOD_GA1, MOD_SH2, MOD_SC2, MOD_GA2 = range(6)


def _params(sem, vmem_limit=VMEM_LIMIT):
    return pltpu.CompilerParams(dimension_semantics=sem, vmem_limit_bytes=vmem_limit)


def _const_spec(shape):
    nd = len(shape)
    return pl.BlockSpec(shape, lambda *_: (0,) * nd, pipeline_mode=pl.Buffered(1))


def _ada_body(c_ref, w_ref, b_ref, o_ref):
    c = c_ref[...]
    s = c * jax.nn.sigmoid(c)
    o_ref[...] = jnp.dot(s.astype(BF16), w_ref[...].astype(BF16),
                         preferred_element_type=F32) + b_ref[...]


def _ada_mod(c_all, w_ada, b_ada):
    rows = c_all.shape[0]
    return pl.pallas_call(
        _ada_body,
        grid=(6,),
        in_specs=[pl.BlockSpec((rows, D_MODEL), lambda j: (0, 0)),
                  pl.BlockSpec((D_MODEL, D_MODEL), lambda j: (0, j)),
                  pl.BlockSpec((1, D_MODEL), lambda j: (0, j))],
        out_specs=pl.BlockSpec((rows, D_MODEL), lambda j: (0, j)),
        out_shape=jax.ShapeDtypeStruct((rows, 6 * D_MODEL), F32),
        compiler_params=_params(("arbitrary",)),
        name="ada_mod",
    )(c_all, w_ada, b_ada)


def _segment_prefix(x, op, identity, seg):
    assert x.shape[1] % LANES == 0 and x.shape[1] % seg == 0
    pos = lax.broadcasted_iota(jnp.int32, x.shape, 1) % seg
    d = 1
    while d < seg:
        x = op(x, jnp.where(pos >= d, pltpu.roll(x, d, axis=1), identity))
        d *= 2
    return x


GATE_ROWS = 24
PROJ_OUTPUTS = ("mq", "mk", "mv", "og", "gt", "aq", "ak", "av")


def _in_proj_tile(x, sc, sh, g1, wa_ref, wb_ref, wg_ref, bg, gq, gk, pq_ref, pk_ref, store,
                  gate_chunk):
    s, r, d = x.shape
    m = s * r
    ms = jnp.mean(x * x, axis=-1, keepdims=True)
    y = x * lax.rsqrt(ms + EPS) * g1
    h = y * (1.0 + sc) + sh
    hb = h.reshape(m, d).astype(BF16)

    def proj(w_ref, lo, width):
        return lax.dot_general(hb, w_ref[lo:lo + width, :], NT_DIMS, preferred_element_type=F32)

    zg = lax.dot_general(wg_ref[...], hb, NT_DIMS,
                         preferred_element_type=F32)[:2 * M_HEADS] + bg
    row = lax.broadcasted_iota(jnp.int32, zg.shape, 0)
    log_sig = jnp.minimum(zg, 0.0) - jnp.log1p(jnp.exp(-jnp.abs(zg)))
    gt = jnp.where(row < M_HEADS, zg, log_sig)
    b8 = _segment_prefix(gt, jnp.add, 0.0, gate_chunk)
    c8 = gt - pltpu.roll(b8, M_HEADS, axis=0)
    cm8 = _segment_prefix(c8, jnp.maximum, NEG_INF, gate_chunk)
    store("gt", jnp.concatenate([c8, b8, cm8], axis=0))

    store("mq", proj(wa_ref, 0, M_W).astype(BF16))
    store("mk", (proj(wa_ref, M_W, M_W) * (M_DIM ** -0.5)).astype(BF16))
    store("mv", proj(wa_ref, 2 * M_W, M_W).astype(BF16))
    store("og", jax.nn.sigmoid(proj(wa_ref, 3 * M_W, M_W)).astype(BF16))

    aq = proj(wb_ref, 0, A_QW)
    ss_q = jnp.dot((aq * aq).astype(BF16), pq_ref[...], preferred_element_type=F32)
    store("aq", (aq * lax.rsqrt(ss_q * (1.0 / A_DIM) + EPS) * gq).astype(BF16))
    ak = proj(wb_ref, A_QW, A_KW)
    sq = ak * ak
    hi = sq.astype(BF16)
    lo = (sq - hi.astype(F32)).astype(BF16)
    ss_k = (jnp.dot(hi, pk_ref[...], preferred_element_type=F32)
            + jnp.dot(lo, pk_ref[...], preferred_element_type=F32))
    store("ak", ak * lax.rsqrt(ss_k * (1.0 / A_DIM) + EPS) * gk)
    store("av", proj(wb_ref, A_QW + A_KW, A_KW))


def _proj_weight_specs(d):
    return [_const_spec((1, d)), _const_spec((4 * M_W, d)), _const_spec((A_QW + 2 * A_KW, d)),
            _const_spec((BF16_ROWS, d)), _const_spec((2 * M_HEADS, 1)),
            _const_spec((1, A_QW)), _const_spec((1, A_KW)),
            _const_spec((A_QW, A_QW)), _const_spec((A_KW, A_KW))]


def _col_bcast(r):
    t = r.shape[1]
    if t % LANES == 0:
        return jnp.broadcast_to(r, (LANES, t)).T
    src = lax.broadcasted_iota(jnp.int32, (t, t), 0)
    dst = lax.broadcasted_iota(jnp.int32, (t, t), 1)
    col = jnp.sum(jnp.where(src == dst, jnp.broadcast_to(r, (t, t)), 0.0), axis=1, keepdims=True)
    return jnp.broadcast_to(col, (t, LANES))


def _eye_dim():
    return (lax.broadcasted_iota(jnp.int32, (M_DIM, M_DIM), 0)
            == lax.broadcasted_iota(jnp.int32, (M_DIM, M_DIM), 1))


def _mlstm_load_state(c0_ref, n0_ref, m0_ref, caug_ref, m_ref):
    for hd in range(M_HEADS):
        caug_ref[hd, :, :M_DIM] = c0_ref[0, hd]
        n_row = jnp.broadcast_to(n0_ref[0, hd:hd + 1, :], (M_DIM, M_DIM))
        n_col = jnp.sum(jnp.where(_eye_dim(), n_row, 0.0), axis=1, keepdims=True)
        caug_ref[hd, :, M_DIM:] = jnp.broadcast_to(n_col, (M_DIM, M_DIM))
    m_ref[...] = m0_ref[0]


def _mlstm_store_state(caug_ref, m_ref, c_out_ref, n_out_ref, m_out_ref):
    for hd in range(M_HEADS):
        c_out_ref[0, hd] = caug_ref[hd, :, :M_DIM]
        n_out_ref[0, hd:hd + 1, :] = jnp.sum(
            jnp.where(_eye_dim(), caug_ref[hd, :, M_DIM:], 0.0), axis=0, keepdims=True)
    m_out_ref[0] = m_ref[...]


def _mlstm_tile(q_ref, k_ref, v_ref, og_ref, gt_ref, gout_ref, caug_ref, m_ref, write_h, chunk):
    t = chunk
    n_chunks = q_ref.shape[1] // t
    bs = LANES if t % LANES == 0 else t
    tril = (lax.broadcasted_iota(jnp.int32, (bs, bs), 1)
            <= lax.broadcasted_iota(jnp.int32, (bs, bs), 0))
    m_prev = m_ref[...]
    for j in range(n_chunks):
        rows = slice(j * t, (j + 1) * t)
        c = gt_ref[0:M_HEADS, rows]
        b = gt_ref[12:12 + M_HEADS, rows]
        g = jnp.maximum(gt_ref[16:16 + M_HEADS, rows], m_prev)
        g_last = g[:, t - 1:t]
        c2 = c * LOG2E
        g2 = g * LOG2E
        e_all = jnp.exp2(-LOG2E * (b + g))
        w_src_all = jnp.exp2(c2 - LOG2E * g_last)
        w_inter_all = jnp.exp2(LOG2E * m_prev - g2)
        w_old_all = jnp.exp2(LOG2E * (m_prev - g_last))
        m_prev = b[:, t - 1:t] + g_last

        for hd in range(M_HEADS):
            sl = slice(hd * M_DIM, (hd + 1) * M_DIM)
            hr = slice(hd, hd + 1)
            g_b = _col_bcast(g2[hr])
            e_b = _col_bcast(e_all[hr])
            w_src = _col_bcast(w_src_all[hr])
            w_inter = _col_bcast(w_inter_all[hr])

            q = q_ref[0, rows, sl]
            k = k_ref[0, rows, sl]
            v = v_ref[0, rows, sl]
            caug = caug_ref[hd]
            inter = jnp.dot(q, caug.astype(BF16), preferred_element_type=F32)
            gain = gout_ref[hr, :]
            for rb in range(t // bs):
                rsl = slice(rb * bs, (rb + 1) * bs)
                ncols = (rb + 1) * bs
                s_mat = lax.dot_general(q[rsl], k[:ncols], NT_DIMS,
                                        preferred_element_type=F32)
                decay = []
                for cb in range(rb + 1):
                    arg = c2[hr, cb * bs:(cb + 1) * bs] - g_b[rsl, :bs]
                    decay.append(jnp.exp2(jnp.where(tril, arg, NEG_INF) if cb == rb else arg))
                s_mat = s_mat * (decay[0] if rb == 0 else jnp.concatenate(decay, axis=1))
                num = (jnp.dot(s_mat.astype(BF16), v[:ncols], preferred_element_type=F32)
                       + w_inter[rsl] * inter[rsl, :M_DIM])
                den = jnp.sum(s_mat, axis=1, keepdims=True) + w_inter[rsl] * inter[rsl, M_DIM:]
                hh = num / jnp.maximum(jnp.abs(den), e_b[rsl])
                hn = hh * lax.rsqrt(jnp.mean(hh * hh, axis=-1, keepdims=True) + EPS) * gain
                out_rows = slice(j * t + rb * bs, j * t + (rb + 1) * bs)
                write_h(out_rows, sl, (hn * og_ref[0, out_rows, sl].astype(F32)).astype(BF16))

            rhs = jnp.concatenate([v.astype(F32) * w_src, w_src], axis=1).astype(BF16)
            upd = lax.dot_general(k, rhs, (((0,), (0,)), ((), ())), preferred_element_type=F32)
            caug_ref[hd] = w_old_all[hr] * caug + upd
    m_ref[...] = m_prev


def _kv_halves(a):
    low = lax.broadcasted_iota(jnp.int32, a.shape, 1) < A_DIM
    lo_half = jnp.where(low, a, 0.0)
    hi_half = jnp.where(low, 0.0, a)
    pairs = [(lo_half, pltpu.roll(lo_half, A_DIM, axis=1)),
             (pltpu.roll(hi_half, A_DIM, axis=1), hi_half)]
    return [tuple(h.astype(BF16) for h in pair) for pair in pairs]


def _swa_biases(q_pos, k_pos):
    shift = CHUNK.bit_length() - 1
    assert 1 << shift == CHUNK
    qc = lax.shift_right_arithmetic(q_pos, shift)
    kc = lax.shift_right_arithmetic(k_pos, shift)
    allowed = (kc <= qc) & (kc >= qc - WIN_CHUNKS) & (k_pos >= 0)
    dist = jnp.abs(q_pos - k_pos).astype(F32)
    slopes = [2.0 ** (-8.0 * (head + 1) / A_HEADS) for head in range(A_HEADS)]
    return [jnp.where(allowed, (-LOG2E * slope) * dist, NEG_INF) for slope in slopes]


def _swa_block(q, kz, vz, biases, sink_ref, write_o, valid=None):
    tq = q.shape[0]
    pairs_per_kv = A_HEADS // A_KV_HEADS // 2
    for j in range(A_KV_HEADS):
        pairs = [j * pairs_per_kv + i for i in range(pairs_per_kv)]
        q_stack = jnp.concatenate([q[:, p * LANES:(p + 1) * LANES] for p in pairs], axis=0)
        acc = [None] * pairs_per_kv
        for e in range(2):
            logits_all = lax.dot_general(q_stack, kz[j][e], NT_DIMS, preferred_element_type=F32)
            probs, inv_denoms = [], []
            for i, p in enumerate(pairs):
                head = 2 * p + e
                sink = sink_ref[head] * LOG2E
                logits = logits_all[i * tq:(i + 1) * tq] + biases[head]
                if valid is not None:
                    logits = jnp.where(valid, logits, NEG_INF)
                mx = jnp.maximum(jnp.max(logits, axis=1, keepdims=True), sink)
                pr = jnp.exp2(logits - mx)
                inv_denoms.append(
                    1.0 / (jnp.sum(pr, axis=1, keepdims=True) + jnp.exp2(sink - mx)))
                probs.append(pr.astype(BF16))
            out_all = jnp.dot(jnp.concatenate(probs, axis=0), vz[j][e],
                              preferred_element_type=F32)
            for i in range(pairs_per_kv):
                part = out_all[i * tq:(i + 1) * tq] * inv_denoms[i]
                acc[i] = part if acc[i] is None else acc[i] + part
        for i, p in enumerate(pairs):
            write_o(slice(p * LANES, (p + 1) * LANES), acc[i].astype(BF16))


SWA_Q_ROWS = WIN_CHUNKS * CHUNK


def _swa_tile(sink_ref, q_ref, k_all, v_all, tile_start, write_o):
    tq = SWA_Q_ROWS
    kz = _kv_halves(k_all)
    vz = _kv_halves(v_all)
    rel_k = lax.broadcasted_iota(jnp.int32, (1, 2 * tq), 1)
    biases = _swa_biases(tq + lax.broadcasted_iota(jnp.int32, (tq, 1), 0), rel_k)
    for r in range(q_ref.shape[1] // tq):
        rows = slice(r * tq, (r + 1) * tq)
        band = slice(r * tq, (r + 2) * tq)
        valid = (tile_start - tq + rel_k >= 0) if r == 0 else None
        _swa_block(q_ref[0, rows, :],
                   [tuple(h[band] for h in pair) for pair in kz],
                   [tuple(h[band] for h in pair) for pair in vz],
                   biases, sink_ref,
                   lambda lanes, value, rows=rows: write_o(rows, lanes, value), valid=valid)


def _mix_residual_norm(x, mixed, ga1, sc2, sh2, g2, w_out):
    s, r, d = x.shape
    mix = jnp.dot(mixed, w_out, preferred_element_type=F32)
    x1 = x + ga1 * mix.reshape(s, r, d)
    ms = jnp.mean(x1 * x1, axis=-1, keepdims=True)
    h2 = (x1 * lax.rsqrt(ms + EPS) * g2) * (1.0 + sc2) + sh2
    return x1, h2.reshape(s * r, d).astype(BF16)


def _ffn_block(hb, w_up, w_down):
    u = jnp.dot(hb, w_up, preferred_element_type=F32)
    u = jnp.square(jnp.maximum(u, 0.0)).astype(BF16)
    return jnp.dot(u, w_down, preferred_element_type=F32)


def _out_ffn_tile(x, mixed, ga1, sc2, sh2, ga2, g2, wo_ref, wu_ref, wd_ref):
    s, r, d = x.shape
    x1, hb = _mix_residual_norm(x, mixed, ga1, sc2, sh2, g2, wo_ref[...])
    ffn = jnp.zeros((s * r, d), F32)
    for c in range(D_FF // FF_BLOCK):
        cols = slice(c * FF_BLOCK, (c + 1) * FF_BLOCK)
        ffn = ffn + _ffn_block(hb, wu_ref[:, cols], wd_ref[cols, :])
    return x1 + ga2 * ffn.reshape(s, r, d)


def _ffn_weight_specs(d):
    return [_const_spec((1, d)), _const_spec((d, d)), _const_spec((d, D_FF)),
            _const_spec((D_FF, d))]


def _layer_body(sink_ref, xp_ref, sc1_ref, sh1_ref, g1_ref, wa_ref, wb_ref, wg_ref, bg_ref,
                gq_ref, gk_ref, pq_ref, pk_ref, gout_ref,
                xf_ref, ga1_ref, sc2_ref, sh2_ref, ga2_ref, g2_ref, wo_ref, wu_ref, wd_ref,
                y_ref, kt_ref, vt_ref, c_out_ref, n_out_ref, m_out_ref,
                mq_s, mk_s, mv_s, og_s, gt_s, aq_s, ak_s, av_s, kprev_s, vprev_s,
                mix_s, caug_ref, m_ref, *, n_tiles, tiles_per_seq, chunk):
    s = pl.program_id(0)
    rows = xp_ref.shape[1]
    tile = jnp.minimum(s, n_tiles - 1)
    j = tile % tiles_per_seq
    valid = s < n_tiles
    proj_s = {"mq": mq_s, "mk": mk_s, "mv": mv_s, "og": og_s, "gt": gt_s,
              "aq": aq_s, "ak": ak_s, "av": av_s}

    @pl.when(s == 0)
    def _():
        kprev_s[...] = jnp.zeros(kprev_s.shape, kprev_s.dtype)
        vprev_s[...] = jnp.zeros(vprev_s.shape, vprev_s.dtype)
        mix_s[...] = jnp.zeros(mix_s.shape, mix_s.dtype)

    @pl.when(jnp.logical_and(j == 0, valid))
    def _():
        caug_ref[...] = jnp.zeros(caug_ref.shape, caug_ref.dtype)
        m_ref[...] = jnp.zeros(m_ref.shape, m_ref.dtype)

    y_ref[...] = _out_ffn_tile(
        xf_ref[...], mix_s[...], ga1_ref[...], sc2_ref[...], sh2_ref[...], ga2_ref[...],
        g2_ref[...], wo_ref, wu_ref, wd_ref)

    def store(name, value):
        proj_s[name][...] = value if name == "gt" else value[None]

    _in_proj_tile(xp_ref[...], sc1_ref[...], sh1_ref[...], g1_ref[...], wa_ref, wb_ref, wg_ref,
                  bg_ref[...], gq_ref[...], gk_ref[...], pq_ref, pk_ref, store, chunk)

    def write_h(r, lanes, value):
        mix_s[r, lanes] = value

    def write_o(r, lanes, value):
        mix_s[r, slice(M_W + lanes.start, M_W + lanes.stop)] = value

    _mlstm_tile(mq_s, mk_s, mv_s, og_s, gt_s, gout_ref, caug_ref, m_ref, write_h, chunk)
    k_cur = ak_s[0]
    v_cur = av_s[0]
    _swa_tile(sink_ref, aq_s,
              jnp.concatenate([kprev_s[...], k_cur], axis=0),
              jnp.concatenate([vprev_s[...], v_cur], axis=0), j * rows, write_o)
    kprev_s[...] = k_cur[rows - SWA_Q_ROWS:]
    vprev_s[...] = v_cur[rows - SWA_Q_ROWS:]

    @pl.when(jnp.logical_and(j == tiles_per_seq - 1, valid))
    def _():
        kt_ref[0] = k_cur[rows - SWA_Q_ROWS:]
        vt_ref[0] = v_cur[rows - SWA_Q_ROWS:]
        _mlstm_store_state(caug_ref, m_ref, c_out_ref, n_out_ref, m_out_ref)


def _layer(x, mod4, mod_row0, sinks, gout, proj_w, ffn_w, *, rows, chunk):
    ns, length, d = x.shape
    tps = length // rows
    n_tiles = ns * tps

    def stage(lag):
        def tile_idx(i):
            t = jnp.clip(i - lag, 0, n_tiles - 1)
            return t // tps, t % tps
        x_tile = pl.BlockSpec((1, rows, d), lambda i: (*tile_idx(i), 0))
        mod = lambda k: pl.BlockSpec((1, None, 1, d),
                                     lambda i: (mod_row0 + tile_idx(i)[0], k, 0, 0))
        per_seq = lambda shape: pl.BlockSpec(
            (1,) + shape, lambda i: (tile_idx(i)[0],) + (0,) * len(shape))
        return x_tile, mod, per_seq

    xp_tile, p_mod, p_seq = stage(0)
    xf_tile, f_mod, _ = stage(1)
    st_c, st_n, st_m = p_seq((M_HEADS, M_DIM, M_DIM)), p_seq((M_HEADS, M_DIM)), p_seq((M_HEADS, 1))
    tail = p_seq((SWA_Q_ROWS, A_KW))
    one = lambda shape, dtype: pltpu.VMEM((1,) + shape, dtype)
    return pl.pallas_call(
        functools.partial(_layer_body, n_tiles=n_tiles, tiles_per_seq=tps, chunk=chunk),
        grid=(n_tiles + 1,),
        in_specs=([pl.BlockSpec(memory_space=pltpu.SMEM), xp_tile, p_mod(MOD_SC1), p_mod(MOD_SH1)]
                  + _proj_weight_specs(d)
                  + [_const_spec((M_HEADS, M_DIM)), xf_tile,
                     f_mod(MOD_GA1), f_mod(MOD_SC2), f_mod(MOD_SH2), f_mod(MOD_GA2)]
                  + _ffn_weight_specs(d)),
        out_specs=(xf_tile, tail, tail, st_c, st_n, st_m),
        out_shape=(jax.ShapeDtypeStruct((ns, length, d), F32),
                   jax.ShapeDtypeStruct((ns, SWA_Q_ROWS, A_KW), F32),
                   jax.ShapeDtypeStruct((ns, SWA_Q_ROWS, A_KW), F32),
                   jax.ShapeDtypeStruct((ns, M_HEADS, M_DIM, M_DIM), F32),
                   jax.ShapeDtypeStruct((ns, M_HEADS, M_DIM), F32),
                   jax.ShapeDtypeStruct((ns, M_HEADS, 1), F32)),
        scratch_shapes=[one((rows, M_W), BF16), one((rows, M_W), BF16),
                        one((rows, M_W), BF16), one((rows, M_W), BF16),
                        pltpu.VMEM((GATE_ROWS, rows), F32), one((rows, A_QW), BF16),
                        one((rows, A_KW), F32), one((rows, A_KW), F32),
                        pltpu.VMEM((SWA_Q_ROWS, A_KW), F32), pltpu.VMEM((SWA_Q_ROWS, A_KW), F32),
                        pltpu.VMEM((rows, M_W + A_QW), BF16),
                        pltpu.VMEM((M_HEADS, M_DIM, 2 * M_DIM), F32),
                        pltpu.VMEM((M_HEADS, 1), F32)],
        compiler_params=_params(("arbitrary",)),
        name="layer",
    )(sinks, x, mod4, mod4, *proj_w, gout, x, mod4, mod4, mod4, mod4, *ffn_w)


class _LaneWindow:
    def __init__(self, ref, offset):
        self.ref, self.offset = ref, offset

    def __getitem__(self, idx):
        rows, cols = idx
        return self.ref[rows, slice(cols.start + self.offset, cols.stop + self.offset)]


def _sample_mixers(sink_ref, x_ref, sc1_ref, sh1_ref, g1_ref, wa_ref, wb_ref, wg_ref, bg_ref,
                   gq_ref, gk_ref, pq_ref, pk_ref, c0_ref, n0_ref, m0_ref, gout_ref,
                   ck_ref, cv_ref, ko_ref, vo_ref, c_out_ref, n_out_ref, m_out_ref,
                   mq_s, mk_s, mv_s, og_s, gt_s, aq_s, ak_s, av_s, mix_s, caug_s, m_s):
    ns, t, _ = x_ref.shape
    cached = ck_ref.shape[1]
    proj_s = {"mq": mq_s, "mk": mk_s, "mv": mv_s, "og": og_s, "gt": gt_s,
              "aq": aq_s, "ak": ak_s, "av": av_s}

    def store(name, value):
        proj_s[name][...] = value if name == "gt" else value.reshape(ns, t, value.shape[-1])

    _in_proj_tile(x_ref[...], sc1_ref[...], sh1_ref[...], g1_ref[...], wa_ref, wb_ref, wg_ref,
                  bg_ref[...], gq_ref[...], gk_ref[...], pq_ref, pk_ref, store, t)

    seq = lambda ref, b: ref.at[pl.ds(b, 1)]
    for b in range(ns):
        caug_b, m_b = caug_s.at[b], m_s.at[b]
        _mlstm_load_state(seq(c0_ref, b), seq(n0_ref, b), seq(m0_ref, b), caug_b, m_b)

        def write_h(rows, lanes, value, b=b):
            mix_s[b * t + rows.start:b * t + rows.stop, lanes] = value

        _mlstm_tile(seq(mq_s, b), seq(mk_s, b), seq(mv_s, b), seq(og_s, b),
                    _LaneWindow(gt_s, b * t), gout_ref, caug_b, m_b, write_h, t)
        _mlstm_store_state(caug_b, m_b, seq(c_out_ref, b), seq(n_out_ref, b), seq(m_out_ref, b))

    tk = 2 * LANES
    pad = jnp.zeros((tk - cached - t, A_KW), F32)
    q_pos = PAST_LEN + lax.broadcasted_iota(jnp.int32, (t, 1), 0)
    idx = lax.broadcasted_iota(jnp.int32, (1, tk), 1)
    k_pos = jnp.where(idx < cached + t, PAST_LEN - cached + idx, -1)
    biases = _swa_biases(q_pos, k_pos)
    for b in range(ns):
        k_new, v_new = ak_s[b], av_s[b]

        def write_o(lanes, value, b=b):
            mix_s[b * t:(b + 1) * t, M_W + lanes.start:M_W + lanes.stop] = value

        _swa_block(aq_s[b], _kv_halves(jnp.concatenate([ck_ref[b], k_new, pad], axis=0)),
                   _kv_halves(jnp.concatenate([cv_ref[b], v_new, pad], axis=0)),
                   biases, sink_ref, write_o)
        ko_ref[b] = jnp.concatenate([ck_ref[b, t:, :], k_new], axis=0)
        vo_ref[b] = jnp.concatenate([cv_ref[b, t:, :], v_new], axis=0)


SAMPLE_FF_BLOCK = 512
SAMPLE_VMEM_LIMIT = 60 * 1024 * 1024


def _sample_body(sink_ref, x_ref, sc1_ref, sh1_ref, g1_ref, wt_ref, bg_ref, gq_ref, gk_ref,
                 pq_ref, pk_ref, c0_ref, n0_ref, m0_ref, gout_ref, ck_ref, cv_ref,
                 ga1_ref, sc2_ref, sh2_ref, ga2_ref, g2_ref, wo_ref, wu_ref, wd_ref,
                 y_ref, ko_ref, vo_ref, c_out_ref, n_out_ref, m_out_ref,
                 wa_o, wb_o, wg_o, wo_o, wu_o, wd_o,
                 mq_s, mk_s, mv_s, og_s, gt_s, aq_s, ak_s, av_s, mix_s, caug_s, m_s,
                 x1_s, hb_s, acc_s):
    step = pl.program_id(0)
    ns, t, d = x_ref.shape

    @pl.when(step == 0)
    def _():
        g_lo = 4 * M_W
        g_hi = g_lo + 2 * M_HEADS
        wa_o[...] = wt_ref[:g_lo, :].astype(BF16)
        wg_o[...] = jnp.concatenate(
            [wt_ref[g_lo:g_hi, :], jnp.zeros((BF16_ROWS - 2 * M_HEADS, d), F32)],
            axis=0).astype(BF16)
        wb_o[...] = wt_ref[g_hi:, :].astype(BF16)
        wo_o[...] = wo_ref[...].astype(BF16)
        _sample_mixers(sink_ref, x_ref, sc1_ref, sh1_ref, g1_ref, wa_o, wb_o, wg_o, bg_ref,
                       gq_ref, gk_ref, pq_ref, pk_ref, c0_ref, n0_ref, m0_ref, gout_ref,
                       ck_ref, cv_ref, ko_ref, vo_ref, c_out_ref, n_out_ref, m_out_ref,
                       mq_s, mk_s, mv_s, og_s, gt_s, aq_s, ak_s, av_s, mix_s, caug_s, m_s)
        x1, hb = _mix_residual_norm(x_ref[...], mix_s[...], ga1_ref[...], sc2_ref[...],
                                    sh2_ref[...], g2_ref[...], wo_o[...])
        x1_s[...] = x1
        hb_s[...] = hb
        acc_s[...] = jnp.zeros(acc_s.shape, acc_s.dtype)

    @pl.when(step >= 1)
    def _():
        w_up = wu_ref[...].astype(BF16)
        w_down = wd_ref[...].astype(BF16)
        wu_o[...] = w_up
        wd_o[...] = w_down
        acc_s[...] += _ffn_block(hb_s[...], w_up, w_down)

    @pl.when(step == pl.num_programs(0) - 1)
    def _():
        y_ref[...] = x1_s[...] + ga2_ref[...] * acc_s[...].reshape(ns, t, d)


def _sample_layer(x, mod4, sinks, c0, n0, m0, gout, cache_k, cache_v, proj_small, wt, g2,
                  w_out, w_up, w_down):
    ns, t, d = x.shape
    cached = cache_k.shape[1]
    assert cached >= t
    g1, bg, gq_t, gk_t, pq, pk = proj_small
    fb = SAMPLE_FF_BLOCK
    n_blocks = D_FF // fb
    blk = lambda i: jnp.maximum(i - 1, 0)
    whole = lambda shape: pl.BlockSpec(shape, lambda i: (0,) * len(shape))
    mod = lambda k: pl.BlockSpec((ns, None, 1, d), lambda i: (0, k, 0, 0))
    st_c, st_n, st_m = (whole((ns, M_HEADS, M_DIM, M_DIM)), whole((ns, M_HEADS, M_DIM)),
                        whole((ns, M_HEADS, 1)))
    kv = whole((ns, cached, A_KW))
    up_blk = pl.BlockSpec((d, fb), lambda i: (0, blk(i)))
    down_blk = pl.BlockSpec((fb, d), lambda i: (blk(i), 0))
    n_in = wt.shape[0]
    n_b = n_in - 4 * M_W - 2 * M_HEADS
    return pl.pallas_call(
        _sample_body,
        grid=(1 + n_blocks,),
        in_specs=[pl.BlockSpec(memory_space=pltpu.SMEM), whole((ns, t, d)),
                  mod(MOD_SC1), mod(MOD_SH1),
                  _const_spec((1, d)), _const_spec((n_in, d)), _const_spec((2 * M_HEADS, 1)),
                  _const_spec((1, A_QW)), _const_spec((1, A_KW)),
                  _const_spec((A_QW, A_QW)), _const_spec((A_KW, A_KW)),
                  st_c, st_n, st_m, _const_spec((M_HEADS, M_DIM)), kv, kv,
                  mod(MOD_GA1), mod(MOD_SC2), mod(MOD_SH2), mod(MOD_GA2),
                  _const_spec((1, d)), _const_spec((d, d)), up_blk, down_blk],
        out_specs=(whole((ns, t, d)), kv, kv, st_c, st_n, st_m,
                   _const_spec((4 * M_W, d)), _const_spec((n_b, d)), _const_spec((BF16_ROWS, d)),
                   _const_spec((d, d)), up_blk, down_blk),
        out_shape=(jax.ShapeDtypeStruct((ns, t, d), F32),
                   jax.ShapeDtypeStruct((ns, cached, A_KW), F32),
                   jax.ShapeDtypeStruct((ns, cached, A_KW), F32),
                   jax.ShapeDtypeStruct((ns, M_HEADS, M_DIM, M_DIM), F32),
                   jax.ShapeDtypeStruct((ns, M_HEADS, M_DIM), F32),
                   jax.ShapeDtypeStruct((ns, M_HEADS, 1), F32),
                   jax.ShapeDtypeStruct((4 * M_W, d), BF16),
                   jax.ShapeDtypeStruct((n_b, d), BF16),
                   jax.ShapeDtypeStruct((BF16_ROWS, d), BF16),
                   jax.ShapeDtypeStruct((d, d), BF16),
                   jax.ShapeDtypeStruct((d, D_FF), BF16),
                   jax.ShapeDtypeStruct((D_FF, d), BF16)),
        scratch_shapes=[pltpu.VMEM((ns, t, M_W), BF16), pltpu.VMEM((ns, t, M_W), BF16),
                        pltpu.VMEM((ns, t, M_W), BF16), pltpu.VMEM((ns, t, M_W), BF16),
                        pltpu.VMEM((GATE_ROWS, ns * t), F32), pltpu.VMEM((ns, t, A_QW), BF16),
                        pltpu.VMEM((ns, t, A_KW), F32), pltpu.VMEM((ns, t, A_KW), F32),
                        pltpu.VMEM((ns * t, M_W + A_QW), BF16),
                        pltpu.VMEM((ns, M_HEADS, M_DIM, 2 * M_DIM), F32),
                        pltpu.VMEM((ns, M_HEADS, 1), F32),
                        pltpu.VMEM((ns, t, d), F32), pltpu.VMEM((ns * t, d), BF16),
                        pltpu.VMEM((ns * t, d), F32)],
        compiler_params=_params(("arbitrary",), SAMPLE_VMEM_LIMIT),
        name="sample_layer",
    )(sinks, x, mod4, mod4, g1, wt, bg, gq_t, gk_t, pq, pk, c0, n0, m0, gout, cache_k, cache_v,
      mod4, mod4, mod4, mod4, g2, w_out, w_up, w_down)


def _block_ones(n):
    g = jnp.arange(n, dtype=jnp.int32) // A_DIM
    return (g[:, None] == g[None, :]).astype(BF16)


def kernel(x_prompt, x_sample, c_prompt, c_sample, cache_swa_k, cache_swa_v, state_mlstm_C,
           state_mlstm_n, state_mlstm_m, w_ada, b_ada, g_norm1, w_in, b_gates, g_q, g_k, sinks,
           g_mlstm_out, w_out, g_norm2, w_up, w_down):
    depth = w_ada.shape[0]
    assert depth == 1
    bp, lp, d = x_prompt.shape
    bs, ls, _ = x_sample.shape
    wb = cache_swa_k.shape[2]

    xp, xs = x_prompt, x_sample
    for l in range(depth):
        gq_t = (jnp.tile(g_q[l], A_HEADS) * (LOG2E * A_DIM ** -0.5))[None, :]
        proj_small = (g_norm1[l][None, :], b_gates[l][:, None], gq_t,
                      jnp.tile(g_k[l], A_KV_HEADS)[None, :], _block_ones(A_QW), _block_ones(A_KW))
        g2 = g_norm2[l][None, :]

        mod_rows = -(-(bp + bs) // 8) * 8
        c_all = jnp.concatenate([c_sample, c_prompt, jnp.zeros((mod_rows - bp - bs, d), F32)],
                                axis=0)
        mod4 = _ada_mod(c_all, w_ada[l], b_ada[l][None, :]).reshape(mod_rows, 6, 1, d)

        xs, ko, vo, c, n, m, wa, wb_, wg, wo, wu, wd = _sample_layer(
            xs, mod4, sinks[l], state_mlstm_C[l], state_mlstm_n[l], state_mlstm_m[l][..., None],
            g_mlstm_out[l], cache_swa_k[l].reshape(bs, wb, A_KW),
            cache_swa_v[l].reshape(bs, wb, A_KW), proj_small, w_in[l].T, g2,
            w_out[l], w_up[l], w_down[l])
        st_s = (ko.reshape(bs, wb, A_KV_HEADS, A_DIM), vo.reshape(bs, wb, A_KV_HEADS, A_DIM),
                c, n, m[..., 0])

        g1, bg, gq_t, gk_t, pq, pk = proj_small
        xp, kt, vt, c, n, m = _layer(xp, mod4, bs, sinks[l], g_mlstm_out[l],
                                     (g1, wa, wb_, wg, bg, gq_t, gk_t, pq, pk), (g2, wo, wu, wd),
                                     rows=ROW_TILE, chunk=MLSTM_CHUNK)
        win = WIN_CHUNKS * CHUNK
        st_p = (kt.reshape(bp, win, A_KV_HEADS, A_DIM), vt.reshape(bp, win, A_KV_HEADS, A_DIM),
                c, n, m[..., 0])

    return (xp, xs) + tuple(s[None] for s in st_p) + tuple(s[None] for s in st_s)
```

```python
import functools

import jax
import jax.numpy as jnp
from jax import lax
from jax.experimental import pallas as pl
from jax.experimental.pallas import tpu as pltpu

F32 = jnp.float32
BF16 = jnp.bfloat16

D_MODEL = 1024
CHUNK = 64
M_HEADS = 4
M_DIM = 128
M_W = M_HEADS * M_DIM
A_HEADS = 8
A_KV_HEADS = 2
A_DIM = 64
A_QW = A_HEADS * A_DIM
A_KW = A_KV_HEADS * A_DIM
WIN_CHUNKS = 2
D_FF = 4 * D_MODEL
EPS = 1e-6
PAST_LEN = 4096

LANES = 128
BF16_ROWS = 16
VMEM_LIMIT = 60 * 1024 * 1024
NEG_INF = float("-inf")
LOG2E = 1.4426950408889634
ROW_TILE = 512
MLSTM_CHUNK = 128
FF_BLOCK = 1024
NT_DIMS = (((1,), (1,)), ((), ()))
MOD_SH1, MOD_SC1, MOD_GA1, MOD_SH2, MOD_SC2, MOD_GA2 = range(6)


def _params(sem):
    return pltpu.CompilerParams(dimension_semantics=sem, vmem_limit_bytes=VMEM_LIMIT)


def _const_spec(shape):
    nd = len(shape)
    return pl.BlockSpec(shape, lambda *_: (0,) * nd, pipeline_mode=pl.Buffered(1))


def _ada_body(c_ref, w_ref, b_ref, o_ref):
    c = c_ref[...]
    s = c * jax.nn.sigmoid(c)
    o_ref[...] = jnp.dot(s.astype(BF16), w_ref[...].astype(BF16),
                         preferred_element_type=F32) + b_ref[...]


def _ada_mod(c_all, w_ada, b_ada):
    rows = c_all.shape[0]
    return pl.pallas_call(
        _ada_body,
        grid=(6,),
        in_specs=[pl.BlockSpec((rows, D_MODEL), lambda j: (0, 0)),
                  pl.BlockSpec((D_MODEL, D_MODEL), lambda j: (0, j)),
                  pl.BlockSpec((1, D_MODEL), lambda j: (0, j))],
        out_specs=pl.BlockSpec((rows, D_MODEL), lambda j: (0, j)),
        out_shape=jax.ShapeDtypeStruct((rows, 6 * D_MODEL), F32),
        compiler_params=_params(("arbitrary",)),
        name="ada_mod",
    )(c_all, w_ada, b_ada)


def _segment_prefix(x, op, identity, seg):
    assert x.shape[1] % LANES == 0 and x.shape[1] % seg == 0
    pos = lax.broadcasted_iota(jnp.int32, x.shape, 1) % seg
    d = 1
    while d < seg:
        x = op(x, jnp.where(pos >= d, pltpu.roll(x, d, axis=1), identity))
        d *= 2
    return x


GATE_ROWS = 24
PROJ_OUTPUTS = ("mq", "mk", "mv", "og", "gt", "aq", "ak", "av")


def _in_proj_tile(x, sc, sh, g1, wa_ref, wb_ref, wg_ref, bg, gq, gk, pq_ref, pk_ref, store,
                  gate_chunk):
    s, r, d = x.shape
    m = s * r
    ms = jnp.mean(x * x, axis=-1, keepdims=True)
    y = x * lax.rsqrt(ms + EPS) * g1
    h = y * (1.0 + sc) + sh
    hb = h.reshape(m, d).astype(BF16)

    def proj(w_ref, lo, width):
        return lax.dot_general(hb, w_ref[lo:lo + width, :], NT_DIMS, preferred_element_type=F32)

    zg = lax.dot_general(wg_ref[...], hb, NT_DIMS,
                         preferred_element_type=F32)[:2 * M_HEADS] + bg
    row = lax.broadcasted_iota(jnp.int32, zg.shape, 0)
    log_sig = jnp.minimum(zg, 0.0) - jnp.log1p(jnp.exp(-jnp.abs(zg)))
    gt = jnp.where(row < M_HEADS, zg, log_sig)
    b8 = _segment_prefix(gt, jnp.add, 0.0, gate_chunk)
    c8 = gt - pltpu.roll(b8, M_HEADS, axis=0)
    cm8 = _segment_prefix(c8, jnp.maximum, NEG_INF, gate_chunk)
    store("gt", jnp.concatenate([c8, b8, cm8], axis=0))

    store("mq", proj(wa_ref, 0, M_W).astype(BF16))
    store("mk", (proj(wa_ref, M_W, M_W) * (M_DIM ** -0.5)).astype(BF16))
    store("mv", proj(wa_ref, 2 * M_W, M_W).astype(BF16))
    store("og", jax.nn.sigmoid(proj(wa_ref, 3 * M_W, M_W)).astype(BF16))

    aq = proj(wb_ref, 0, A_QW)
    ss_q = jnp.dot((aq * aq).astype(BF16), pq_ref[...], preferred_element_type=F32)
    store("aq", (aq * lax.rsqrt(ss_q * (1.0 / A_DIM) + EPS) * gq).astype(BF16))
    ak = proj(wb_ref, A_QW, A_KW)
    sq = ak * ak
    hi = sq.astype(BF16)
    lo = (sq - hi.astype(F32)).astype(BF16)
    ss_k = (jnp.dot(hi, pk_ref[...], preferred_element_type=F32)
            + jnp.dot(lo, pk_ref[...], preferred_element_type=F32))
    store("ak", ak * lax.rsqrt(ss_k * (1.0 / A_DIM) + EPS) * gk)
    store("av", proj(wb_ref, A_QW + A_KW, A_KW))


def _proj_weight_specs(d):
    return [_const_spec((1, d)), _const_spec((4 * M_W, d)), _const_spec((A_QW + 2 * A_KW, d)),
            _const_spec((BF16_ROWS, d)), _const_spec((2 * M_HEADS, 1)),
            _const_spec((1, A_QW)), _const_spec((1, A_KW)),
            _const_spec((A_QW, A_QW)), _const_spec((A_KW, A_KW))]


def _col_bcast(r):
    t = r.shape[1]
    if t % LANES == 0:
        return jnp.broadcast_to(r, (LANES, t)).T
    src = lax.broadcasted_iota(jnp.int32, (t, t), 0)
    dst = lax.broadcasted_iota(jnp.int32, (t, t), 1)
    col = jnp.sum(jnp.where(src == dst, jnp.broadcast_to(r, (t, t)), 0.0), axis=1, keepdims=True)
    return jnp.broadcast_to(col, (t, LANES))


def _eye_dim():
    return (lax.broadcasted_iota(jnp.int32, (M_DIM, M_DIM), 0)
            == lax.broadcasted_iota(jnp.int32, (M_DIM, M_DIM), 1))


def _mlstm_load_state(c0_ref, n0_ref, m0_ref, caug_ref, m_ref):
    for hd in range(M_HEADS):
        caug_ref[hd, :, :M_DIM] = c0_ref[0, hd]
        n_row = jnp.broadcast_to(n0_ref[0, hd:hd + 1, :], (M_DIM, M_DIM))
        n_col = jnp.sum(jnp.where(_eye_dim(), n_row, 0.0), axis=1, keepdims=True)
        caug_ref[hd, :, M_DIM:] = jnp.broadcast_to(n_col, (M_DIM, M_DIM))
    m_ref[...] = m0_ref[0]


def _mlstm_store_state(caug_ref, m_ref, c_out_ref, n_out_ref, m_out_ref):
    for hd in range(M_HEADS):
        c_out_ref[0, hd] = caug_ref[hd, :, :M_DIM]
        n_out_ref[0, hd:hd + 1, :] = jnp.sum(
            jnp.where(_eye_dim(), caug_ref[hd, :, M_DIM:], 0.0), axis=0, keepdims=True)
    m_out_ref[0] = m_ref[...]


def _mlstm_tile(q_ref, k_ref, v_ref, og_ref, gt_ref, gout_ref, caug_ref, m_ref, write_h, chunk):
    t = chunk
    n_chunks = q_ref.shape[1] // t
    bs = LANES if t % LANES == 0 else t
    tril = (lax.broadcasted_iota(jnp.int32, (bs, bs), 1)
            <= lax.broadcasted_iota(jnp.int32, (bs, bs), 0))
    m_prev = m_ref[...]
    for j in range(n_chunks):
        rows = slice(j * t, (j + 1) * t)
        c = gt_ref[0:M_HEADS, rows]
        b = gt_ref[12:12 + M_HEADS, rows]
        g = jnp.maximum(gt_ref[16:16 + M_HEADS, rows], m_prev)
        g_last = g[:, t - 1:t]
        c2 = c * LOG2E
        g2 = g * LOG2E
        e_all = jnp.exp2(-LOG2E * (b + g))
        w_src_all = jnp.exp2(c2 - LOG2E * g_last)
        w_inter_all = jnp.exp2(LOG2E * m_prev - g2)
        w_old_all = jnp.exp2(LOG2E * (m_prev - g_last))
        m_prev = b[:, t - 1:t] + g_last

        for hd in range(M_HEADS):
            sl = slice(hd * M_DIM, (hd + 1) * M_DIM)
            hr = slice(hd, hd + 1)
            g_b = _col_bcast(g2[hr])
            e_b = _col_bcast(e_all[hr])
            w_src = _col_bcast(w_src_all[hr])
            w_inter = _col_bcast(w_inter_all[hr])

            q = q_ref[0, rows, sl]
            k = k_ref[0, rows, sl]
            v = v_ref[0, rows, sl]
            caug = caug_ref[hd]
            inter = jnp.dot(q, caug.astype(BF16), preferred_element_type=F32)
            gain = gout_ref[hr, :]
            for rb in range(t // bs):
                rsl = slice(rb * bs, (rb + 1) * bs)
                ncols = (rb + 1) * bs
                s_mat = lax.dot_general(q[rsl], k[:ncols], NT_DIMS,
                                        preferred_element_type=F32)
                decay = []
                for cb in range(rb + 1):
                    arg = c2[hr, cb * bs:(cb + 1) * bs] - g_b[rsl, :bs]
                    decay.append(jnp.exp2(jnp.where(tril, arg, NEG_INF) if cb == rb else arg))
                s_mat = s_mat * (decay[0] if rb == 0 else jnp.concatenate(decay, axis=1))
                num = (jnp.dot(s_mat.astype(BF16), v[:ncols], preferred_element_type=F32)
                       + w_inter[rsl] * inter[rsl, :M_DIM])
                den = jnp.sum(s_mat, axis=1, keepdims=True) + w_inter[rsl] * inter[rsl, M_DIM:]
                hh = num / jnp.maximum(jnp.abs(den), e_b[rsl])
                hn = hh * lax.rsqrt(jnp.mean(hh * hh, axis=-1, keepdims=True) + EPS) * gain
                out_rows = slice(j * t + rb * bs, j * t + (rb + 1) * bs)
                write_h(out_rows, sl, (hn * og_ref[0, out_rows, sl].astype(F32)).astype(BF16))

            rhs = jnp.concatenate([v.astype(F32) * w_src, w_src], axis=1).astype(BF16)
            upd = lax.dot_general(k, rhs, (((0,), (0,)), ((), ())), preferred_element_type=F32)
            caug_ref[hd] = w_old_all[hr] * caug + upd
    m_ref[...] = m_prev


def _kv_halves(a):
    low = lax.broadcasted_iota(jnp.int32, a.shape, 1) < A_DIM
    lo_half = jnp.where(low, a, 0.0)
    hi_half = jnp.where(low, 0.0, a)
    pairs = [(lo_half, pltpu.roll(lo_half, A_DIM, axis=1)),
             (pltpu.roll(hi_half, A_DIM, axis=1), hi_half)]
    return [tuple(h.astype(BF16) for h in pair) for pair in pairs]


def _swa_biases(q_pos, k_pos):
    shift = CHUNK.bit_length() - 1
    assert 1 << shift == CHUNK
    qc = lax.shift_right_arithmetic(q_pos, shift)
    kc = lax.shift_right_arithmetic(k_pos, shift)
    allowed = (kc <= qc) & (kc >= qc - WIN_CHUNKS) & (k_pos >= 0)
    dist = jnp.abs(q_pos - k_pos).astype(F32)
    slopes = [2.0 ** (-8.0 * (head + 1) / A_HEADS) for head in range(A_HEADS)]
    return [jnp.where(allowed, (-LOG2E * slope) * dist, NEG_INF) for slope in slopes]


def _swa_block(q, kz, vz, biases, sink_ref, write_o, valid=None):
    tq = q.shape[0]
    pairs_per_kv = A_HEADS // A_KV_HEADS // 2
    for j in range(A_KV_HEADS):
        pairs = [j * pairs_per_kv + i for i in range(pairs_per_kv)]
        q_stack = jnp.concatenate([q[:, p * LANES:(p + 1) * LANES] for p in pairs], axis=0)
        acc = [None] * pairs_per_kv
        for e in range(2):
            logits_all = lax.dot_general(q_stack, kz[j][e], NT_DIMS, preferred_element_type=F32)
            probs, inv_denoms = [], []
            for i, p in enumerate(pairs):
                head = 2 * p + e
                sink = sink_ref[head] * LOG2E
                logits = logits_all[i * tq:(i + 1) * tq] + biases[head]
                if valid is not None:
                    logits = jnp.where(valid, logits, NEG_INF)
                mx = jnp.maximum(jnp.max(logits, axis=1, keepdims=True), sink)
                pr = jnp.exp2(logits - mx)
                inv_denoms.append(
                    1.0 / (jnp.sum(pr, axis=1, keepdims=True) + jnp.exp2(sink - mx)))
                probs.append(pr.astype(BF16))
            out_all = jnp.dot(jnp.concatenate(probs, axis=0), vz[j][e],
                              preferred_element_type=F32)
            for i in range(pairs_per_kv):
                part = out_all[i * tq:(i + 1) * tq] * inv_denoms[i]
                acc[i] = part if acc[i] is None else acc[i] + part
        for i, p in enumerate(pairs):
            write_o(slice(p * LANES, (p + 1) * LANES), acc[i].astype(BF16))


SWA_Q_ROWS = WIN_CHUNKS * CHUNK


def _swa_tile(sink_ref, q_ref, k_all, v_all, tile_start, write_o):
    tq = SWA_Q_ROWS
    kz = _kv_halves(k_all)
    vz = _kv_halves(v_all)
    rel_k = lax.broadcasted_iota(jnp.int32, (1, 2 * tq), 1)
    biases = _swa_biases(tq + lax.broadcasted_iota(jnp.int32, (tq, 1), 0), rel_k)
    for r in range(q_ref.shape[1] // tq):
        rows = slice(r * tq, (r + 1) * tq)
        band = slice(r * tq, (r + 2) * tq)
        valid = (tile_start - tq + rel_k >= 0) if r == 0 else None
        _swa_block(q_ref[0, rows, :],
                   [tuple(h[band] for h in pair) for pair in kz],
                   [tuple(h[band] for h in pair) for pair in vz],
                   biases, sink_ref,
                   lambda lanes, value, rows=rows: write_o(rows, lanes, value), valid=valid)


def _mix_residual_norm(x, mixed, ga1, sc2, sh2, g2, w_out):
    s, r, d = x.shape
    mix = jnp.dot(mixed, w_out, preferred_element_type=F32)
    x1 = x + ga1 * mix.reshape(s, r, d)
    ms = jnp.mean(x1 * x1, axis=-1, keepdims=True)
    h2 = (x1 * lax.rsqrt(ms + EPS) * g2) * (1.0 + sc2) + sh2
    return x1, h2.reshape(s * r, d).astype(BF16)


def _ffn_block(hb, w_up, w_down):
    u = jnp.dot(hb, w_up, preferred_element_type=F32)
    u = jnp.square(jnp.maximum(u, 0.0)).astype(BF16)
    return jnp.dot(u, w_down, preferred_element_type=F32)


def _out_ffn_tile(x, mixed, ga1, sc2, sh2, ga2, g2, wo_ref, wu_ref, wd_ref):
    s, r, d = x.shape
    x1, hb = _mix_residual_norm(x, mixed, ga1, sc2, sh2, g2, wo_ref[...])
    ffn = jnp.zeros((s * r, d), F32)
    for c in range(D_FF // FF_BLOCK):
        cols = slice(c * FF_BLOCK, (c + 1) * FF_BLOCK)
        ffn = ffn + _ffn_block(hb, wu_ref[:, cols], wd_ref[cols, :])
    return x1 + ga2 * ffn.reshape(s, r, d)


def _ffn_weight_specs(d):
    return [_const_spec((1, d)), _const_spec((d, d)), _const_spec((d, D_FF)),
            _const_spec((D_FF, d))]


def _layer_body(sink_ref, xp_ref, sc1_ref, sh1_ref, g1_ref, wa_ref, wb_ref, wg_ref, bg_ref,
                gq_ref, gk_ref, pq_ref, pk_ref, gout_ref,
                xf_ref, ga1_ref, sc2_ref, sh2_ref, ga2_ref, g2_ref, wo_ref, wu_ref, wd_ref,
                y_ref, kt_ref, vt_ref, c_out_ref, n_out_ref, m_out_ref,
                mq_s, mk_s, mv_s, og_s, gt_s, aq_s, ak_s, av_s, kprev_s, vprev_s,
                mix_s, caug_ref, m_ref, *, n_tiles, tiles_per_seq, chunk):
    s = pl.program_id(0)
    rows = xp_ref.shape[1]
    tile = jnp.minimum(s, n_tiles - 1)
    j = tile % tiles_per_seq
    valid = s < n_tiles
    proj_s = {"mq": mq_s, "mk": mk_s, "mv": mv_s, "og": og_s, "gt": gt_s,
              "aq": aq_s, "ak": ak_s, "av": av_s}

    @pl.when(s == 0)
    def _():
        kprev_s[...] = jnp.zeros(kprev_s.shape, kprev_s.dtype)
        vprev_s[...] = jnp.zeros(vprev_s.shape, vprev_s.dtype)
        mix_s[...] = jnp.zeros(mix_s.shape, mix_s.dtype)

    @pl.when(jnp.logical_and(j == 0, valid))
    def _():
        caug_ref[...] = jnp.zeros(caug_ref.shape, caug_ref.dtype)
        m_ref[...] = jnp.zeros(m_ref.shape, m_ref.dtype)

    x1, hb = _mix_residual_norm(xf_ref[...], mix_s[...], ga1_ref[...], sc2_ref[...],
                                sh2_ref[...], g2_ref[...], wo_ref[...])

    def store(name, value):
        proj_s[name][...] = value if name == "gt" else value[None]

    _in_proj_tile(xp_ref[...], sc1_ref[...], sh1_ref[...], g1_ref[...], wa_ref, wb_ref, wg_ref,
                  bg_ref[...], gq_ref[...], gk_ref[...], pq_ref, pk_ref, store, chunk)

    def write_h(r, lanes, value):
        mix_s[r, lanes] = value

    def write_o(r, lanes, value):
        mix_s[r, slice(M_W + lanes.start, M_W + lanes.stop)] = value

    _mlstm_tile(mq_s, mk_s, mv_s, og_s, gt_s, gout_ref, caug_ref, m_ref, write_h, chunk)
    k_cur = ak_s[0]
    v_cur = av_s[0]
    _swa_tile(sink_ref, aq_s,
              jnp.concatenate([kprev_s[...], k_cur], axis=0),
              jnp.concatenate([vprev_s[...], v_cur], axis=0), j * rows, write_o)
    kprev_s[...] = k_cur[rows - SWA_Q_ROWS:]
    vprev_s[...] = v_cur[rows - SWA_Q_ROWS:]

    ffn = jnp.zeros(hb.shape, F32)
    for c in range(D_FF // FF_BLOCK):
        cols = slice(c * FF_BLOCK, (c + 1) * FF_BLOCK)
        ffn = ffn + _ffn_block(hb, wu_ref[:, cols], wd_ref[cols, :])
    y_ref[...] = x1 + ga2_ref[...] * ffn.reshape(x1.shape)

    @pl.when(jnp.logical_and(j == tiles_per_seq - 1, valid))
    def _():
        kt_ref[0] = k_cur[rows - SWA_Q_ROWS:]
        vt_ref[0] = v_cur[rows - SWA_Q_ROWS:]
        _mlstm_store_state(caug_ref, m_ref, c_out_ref, n_out_ref, m_out_ref)


def _layer(x, mod4, mod_row0, sinks, gout, proj_w, ffn_w, *, rows, chunk):
    ns, length, d = x.shape
    tps = length // rows
    n_tiles = ns * tps

    def stage(lag):
        def tile_idx(i):
            t = jnp.clip(i - lag, 0, n_tiles - 1)
            return t // tps, t % tps
        x_tile = pl.BlockSpec((1, rows, d), lambda i: (*tile_idx(i), 0))
        mod = lambda k: pl.BlockSpec((1, None, 1, d),
                                     lambda i: (mod_row0 + tile_idx(i)[0], k, 0, 0))
        per_seq = lambda shape: pl.BlockSpec(
            (1,) + shape, lambda i: (tile_idx(i)[0],) + (0,) * len(shape))
        return x_tile, mod, per_seq

    xp_tile, p_mod, p_seq = stage(0)
    xf_tile, f_mod, _ = stage(1)
    st_c, st_n, st_m = p_seq((M_HEADS, M_DIM, M_DIM)), p_seq((M_HEADS, M_DIM)), p_seq((M_HEADS, 1))
    tail = p_seq((SWA_Q_ROWS, A_KW))
    one = lambda shape, dtype: pltpu.VMEM((1,) + shape, dtype)
    return pl.pallas_call(
        functools.partial(_layer_body, n_tiles=n_tiles, tiles_per_seq=tps, chunk=chunk),
        grid=(n_tiles + 1,),
        in_specs=([pl.BlockSpec(memory_space=pltpu.SMEM), xp_tile, p_mod(MOD_SC1), p_mod(MOD_SH1)]
                  + _proj_weight_specs(d)
                  + [_const_spec((M_HEADS, M_DIM)), xf_tile,
                     f_mod(MOD_GA1), f_mod(MOD_SC2), f_mod(MOD_SH2), f_mod(MOD_GA2)]
                  + _ffn_weight_specs(d)),
        out_specs=(xf_tile, tail, tail, st_c, st_n, st_m),
        out_shape=(jax.ShapeDtypeStruct((ns, length, d), F32),
                   jax.ShapeDtypeStruct((ns, SWA_Q_ROWS, A_KW), F32),
                   jax.ShapeDtypeStruct((ns, SWA_Q_ROWS, A_KW), F32),
                   jax.ShapeDtypeStruct((ns, M_HEADS, M_DIM, M_DIM), F32),
                   jax.ShapeDtypeStruct((ns, M_HEADS, M_DIM), F32),
                   jax.ShapeDtypeStruct((ns, M_HEADS, 1), F32)),
        scratch_shapes=[one((rows, M_W), BF16), one((rows, M_W), BF16),
                        one((rows, M_W), BF16), one((rows, M_W), BF16),
                        pltpu.VMEM((GATE_ROWS, rows), F32), one((rows, A_QW), BF16),
                        one((rows, A_KW), F32), one((rows, A_KW), F32),
                        pltpu.VMEM((SWA_Q_ROWS, A_KW), F32), pltpu.VMEM((SWA_Q_ROWS, A_KW), F32),
                        pltpu.VMEM((rows, M_W + A_QW), BF16),
                        pltpu.VMEM((M_HEADS, M_DIM, 2 * M_DIM), F32),
                        pltpu.VMEM((M_HEADS, 1), F32)],
        compiler_params=_params(("arbitrary",)),
        name="layer",
    )(sinks, x, mod4, mod4, *proj_w, gout, x, mod4, mod4, mod4, mod4, *ffn_w)


class _LaneWindow:
    def __init__(self, ref, offset):
        self.ref, self.offset = ref, offset

    def __getitem__(self, idx):
        rows, cols = idx
        return self.ref[rows, slice(cols.start + self.offset, cols.stop + self.offset)]


def _sample_mixers(sink_ref, x_ref, sc1_ref, sh1_ref, g1_ref, wa_ref, wb_ref, wg_ref, bg_ref,
                   gq_ref, gk_ref, pq_ref, pk_ref, c0_ref, n0_ref, m0_ref, gout_ref,
                   ck_ref, cv_ref, ko_ref, vo_ref, c_out_ref, n_out_ref, m_out_ref,
                   mq_s, mk_s, mv_s, og_s, gt_s, aq_s, ak_s, av_s, mix_s, caug_s, m_s):
    ns, t, _ = x_ref.shape
    cached = ck_ref.shape[1]
    proj_s = {"mq": mq_s, "mk": mk_s, "mv": mv_s, "og": og_s, "gt": gt_s,
              "aq": aq_s, "ak": ak_s, "av": av_s}

    def store(name, value):
        proj_s[name][...] = value if name == "gt" else value.reshape(ns, t, value.shape[-1])

    _in_proj_tile(x_ref[...], sc1_ref[...], sh1_ref[...], g1_ref[...], wa_ref, wb_ref, wg_ref,
                  bg_ref[...], gq_ref[...], gk_ref[...], pq_ref, pk_ref, store, t)

    seq = lambda ref, b: ref.at[pl.ds(b, 1)]
    for b in range(ns):
        caug_b, m_b = caug_s.at[b], m_s.at[b]
        _mlstm_load_state(seq(c0_ref, b), seq(n0_ref, b), seq(m0_ref, b), caug_b, m_b)

        def write_h(rows, lanes, value, b=b):
            mix_s[b * t + rows.start:b * t + rows.stop, lanes] = value

        _mlstm_tile(seq(mq_s, b), seq(mk_s, b), seq(mv_s, b), seq(og_s, b),
                    _LaneWindow(gt_s, b * t), gout_ref, caug_b, m_b, write_h, t)
        _mlstm_store_state(caug_b, m_b, seq(c_out_ref, b), seq(n_out_ref, b), seq(m_out_ref, b))

    tk = 2 * LANES
    pad = jnp.zeros((tk - cached - t, A_KW), F32)
    q_pos = PAST_LEN + lax.broadcasted_iota(jnp.int32, (t, 1), 0)
    idx = lax.broadcasted_iota(jnp.int32, (1, tk), 1)
    k_pos = jnp.where(idx < cached + t, PAST_LEN - cached + idx, -1)
    biases = _swa_biases(q_pos, k_pos)
    for b in range(ns):
        k_new, v_new = ak_s[b], av_s[b]

        def write_o(lanes, value, b=b):
            mix_s[b * t:(b + 1) * t, M_W + lanes.start:M_W + lanes.stop] = value

        _swa_block(aq_s[b], _kv_halves(jnp.concatenate([ck_ref[b], k_new, pad], axis=0)),
                   _kv_halves(jnp.concatenate([cv_ref[b], v_new, pad], axis=0)),
                   biases, sink_ref, write_o)
        ko_ref[b] = jnp.concatenate([ck_ref[b, t:, :], k_new], axis=0)
        vo_ref[b] = jnp.concatenate([cv_ref[b, t:, :], v_new], axis=0)


SAMPLE_FF_BLOCK = 512


def _sample_body(sink_ref, x_ref, sc1_ref, sh1_ref, g1_ref, wt_ref, bg_ref, gq_ref, gk_ref,
                 pq_ref, pk_ref, c0_ref, n0_ref, m0_ref, gout_ref, ck_ref, cv_ref,
                 ga1_ref, sc2_ref, sh2_ref, ga2_ref, g2_ref, wo_ref, wu_ref, wd_ref,
                 y_ref, ko_ref, vo_ref, c_out_ref, n_out_ref, m_out_ref,
                 wa_o, wb_o, wg_o, wo_o, wu_o, wd_o,
                 mq_s, mk_s, mv_s, og_s, gt_s, aq_s, ak_s, av_s, mix_s, caug_s, m_s,
                 x1_s, hb_s, acc_s):
    step = pl.program_id(0)
    ns, t, d = x_ref.shape

    @pl.when(step == 0)
    def _():
        g_lo = 4 * M_W
        g_hi = g_lo + 2 * M_HEADS
        wa_o[...] = wt_ref[:g_lo, :].astype(BF16)
        wg_o[...] = jnp.concatenate(
            [wt_ref[g_lo:g_hi, :], jnp.zeros((BF16_ROWS - 2 * M_HEADS, d), F32)],
            axis=0).astype(BF16)
        wb_o[...] = wt_ref[g_hi:, :].astype(BF16)
        wo_o[...] = wo_ref[...].astype(BF16)
        _sample_mixers(sink_ref, x_ref, sc1_ref, sh1_ref, g1_ref, wa_o, wb_o, wg_o, bg_ref,
                       gq_ref, gk_ref, pq_ref, pk_ref, c0_ref, n0_ref, m0_ref, gout_ref,
                       ck_ref, cv_ref, ko_ref, vo_ref, c_out_ref, n_out_ref, m_out_ref,
                       mq_s, mk_s, mv_s, og_s, gt_s, aq_s, ak_s, av_s, mix_s, caug_s, m_s)
        x1, hb = _mix_residual_norm(x_ref[...], mix_s[...], ga1_ref[...], sc2_ref[...],
                                    sh2_ref[...], g2_ref[...], wo_o[...])
        x1_s[...] = x1
        hb_s[...] = hb
        acc_s[...] = jnp.zeros(acc_s.shape, acc_s.dtype)

    @pl.when(step >= 1)
    def _():
        w_up = wu_ref[...].astype(BF16)
        w_down = wd_ref[...].astype(BF16)
        wu_o[...] = w_up
        wd_o[...] = w_down
        acc_s[...] += _ffn_block(hb_s[...], w_up, w_down)

    @pl.when(step == pl.num_programs(0) - 1)
    def _():
        y_ref[...] = x1_s[...] + ga2_ref[...] * acc_s[...].reshape(ns, t, d)


def _sample_layer(x, mod4, sinks, c0, n0, m0, gout, cache_k, cache_v, proj_small, wt, g2,
                  w_out, w_up, w_down):
    ns, t, d = x.shape
    cached = cache_k.shape[1]
    assert cached >= t
    g1, bg, gq_t, gk_t, pq, pk = proj_small
    fb = SAMPLE_FF_BLOCK
    n_blocks = D_FF // fb
    blk = lambda i: jnp.maximum(i - 1, 0)
    whole = lambda shape: pl.BlockSpec(shape, lambda i: (0,) * len(shape))
    mod = lambda k: pl.BlockSpec((ns, None, 1, d), lambda i: (0, k, 0, 0))
    st_c, st_n, st_m = (whole((ns, M_HEADS, M_DIM, M_DIM)), whole((ns, M_HEADS, M_DIM)),
                        whole((ns, M_HEADS, 1)))
    kv = whole((ns, cached, A_KW))
    up_blk = pl.BlockSpec((d, fb), lambda i: (0, blk(i)))
    down_blk = pl.BlockSpec((fb, d), lambda i: (blk(i), 0))
    n_in = wt.shape[0]
    n_b = n_in - 4 * M_W - 2 * M_HEADS
    return pl.pallas_call(
        _sample_body,
        grid=(1 + n_blocks,),
        in_specs=[pl.BlockSpec(memory_space=pltpu.SMEM), whole((ns, t, d)),
                  mod(MOD_SC1), mod(MOD_SH1),
                  _const_spec((1, d)), _const_spec((n_in, d)), _const_spec((2 * M_HEADS, 1)),
                  _const_spec((1, A_QW)), _const_spec((1, A_KW)),
                  _const_spec((A_QW, A_QW)), _const_spec((A_KW, A_KW)),
                  st_c, st_n, st_m, _const_spec((M_HEADS, M_DIM)), kv, kv,
                  mod(MOD_GA1), mod(MOD_SC2), mod(MOD_SH2), mod(MOD_GA2),
                  _const_spec((1, d)), _const_spec((d, d)), up_blk, down_blk],
        out_specs=(whole((ns, t, d)), kv, kv, st_c, st_n, st_m,
                   _const_spec((4 * M_W, d)), _const_spec((n_b, d)), _const_spec((BF16_ROWS, d)),
                   _const_spec((d, d)), up_blk, down_blk),
        out_shape=(jax.ShapeDtypeStruct((ns, t, d), F32),
                   jax.ShapeDtypeStruct((ns, cached, A_KW), F32),
                   jax.ShapeDtypeStruct((ns, cached, A_KW), F32),
                   jax.ShapeDtypeStruct((ns, M_HEADS, M_DIM, M_DIM), F32),
                   jax.ShapeDtypeStruct((ns, M_HEADS, M_DIM), F32),
                   jax.ShapeDtypeStruct((ns, M_HEADS, 1), F32),
                   jax.ShapeDtypeStruct((4 * M_W, d), BF16),
                   jax.ShapeDtypeStruct((n_b, d), BF16),
                   jax.ShapeDtypeStruct((BF16_ROWS, d), BF16),
                   jax.ShapeDtypeStruct((d, d), BF16),
                   jax.ShapeDtypeStruct((d, D_FF), BF16),
                   jax.ShapeDtypeStruct((D_FF, d), BF16)),
        scratch_shapes=[pltpu.VMEM((ns, t, M_W), BF16), pltpu.VMEM((ns, t, M_W), BF16),
                        pltpu.VMEM((ns, t, M_W), BF16), pltpu.VMEM((ns, t, M_W), BF16),
                        pltpu.VMEM((GATE_ROWS, ns * t), F32), pltpu.VMEM((ns, t, A_QW), BF16),
                        pltpu.VMEM((ns, t, A_KW), F32), pltpu.VMEM((ns, t, A_KW), F32),
                        pltpu.VMEM((ns * t, M_W + A_QW), BF16),
                        pltpu.VMEM((ns, M_HEADS, M_DIM, 2 * M_DIM), F32),
                        pltpu.VMEM((ns, M_HEADS, 1), F32),
                        pltpu.VMEM((ns, t, d), F32), pltpu.VMEM((ns * t, d), BF16),
                        pltpu.VMEM((ns * t, d), F32)],
        compiler_params=_params(("arbitrary",)),
        name="sample_layer",
    )(sinks, x, mod4, mod4, g1, wt, bg, gq_t, gk_t, pq, pk, c0, n0, m0, gout, cache_k, cache_v,
      mod4, mod4, mod4, mod4, g2, w_out, w_up, w_down)


def _block_ones(n):
    g = jnp.arange(n, dtype=jnp.int32) // A_DIM
    return (g[:, None] == g[None, :]).astype(BF16)


def kernel(x_prompt, x_sample, c_prompt, c_sample, cache_swa_k, cache_swa_v, state_mlstm_C,
           state_mlstm_n, state_mlstm_m, w_ada, b_ada, g_norm1, w_in, b_gates, g_q, g_k, sinks,
           g_mlstm_out, w_out, g_norm2, w_up, w_down):
    depth = w_ada.shape[0]
    assert depth == 1
    bp, lp, d = x_prompt.shape
    bs, ls, _ = x_sample.shape
    wb = cache_swa_k.shape[2]

    xp, xs = x_prompt, x_sample
    for l in range(depth):
        gq_t = (jnp.tile(g_q[l], A_HEADS) * (LOG2E * A_DIM ** -0.5))[None, :]
        proj_small = (g_norm1[l][None, :], b_gates[l][:, None], gq_t,
                      jnp.tile(g_k[l], A_KV_HEADS)[None, :], _block_ones(A_QW), _block_ones(A_KW))
        g2 = g_norm2[l][None, :]

        mod_rows = -(-(bp + bs) // 8) * 8
        c_all = jnp.concatenate([c_sample, c_prompt, jnp.zeros((mod_rows - bp - bs, d), F32)],
                                axis=0)
        mod4 = _ada_mod(c_all, w_ada[l], b_ada[l][None, :]).reshape(mod_rows, 6, 1, d)

        xs, ko, vo, c, n, m, wa, wb_, wg, wo, wu, wd = _sample_layer(
            xs, mod4, sinks[l], state_mlstm_C[l], state_mlstm_n[l], state_mlstm_m[l][..., None],
            g_mlstm_out[l], cache_swa_k[l].reshape(bs, wb, A_KW),
            cache_swa_v[l].reshape(bs, wb, A_KW), proj_small, w_in[l].T, g2,
            w_out[l], w_up[l], w_down[l])
        st_s = (ko.reshape(bs, wb, A_KV_HEADS, A_DIM), vo.reshape(bs, wb, A_KV_HEADS, A_DIM),
                c, n, m[..., 0])

        g1, bg, gq_t, gk_t, pq, pk = proj_small
        xp, kt, vt, c, n, m = _layer(xp, mod4, bs, sinks[l], g_mlstm_out[l],
                                     (g1, wa, wb_, wg, bg, gq_t, gk_t, pq, pk), (g2, wo, wu, wd),
                                     rows=ROW_TILE, chunk=MLSTM_CHUNK)
        win = WIN_CHUNKS * CHUNK
        st_p = (kt.reshape(bp, win, A_KV_HEADS, A_DIM), vt.reshape(bp, win, A_KV_HEADS, A_DIM),
                c, n, m[..., 0])

    return (xp, xs) + tuple(s[None] for s in st_p) + tuple(s[None] for s in st_s)
```

```python
import functools

import jax
import jax.numpy as jnp
from jax import lax
from jax.experimental import pallas as pl
from jax.experimental.pallas import tpu as pltpu

F32 = jnp.float32
BF16 = jnp.bfloat16

D_MODEL = 1024
CHUNK = 64
M_HEADS = 4
M_DIM = 128
M_W = M_HEADS * M_DIM
A_HEADS = 8
A_KV_HEADS = 2
A_DIM = 64
A_QW = A_HEADS * A_DIM
A_KW = A_KV_HEADS * A_DIM
WIN_CHUNKS = 2
D_FF = 4 * D_MODEL
EPS = 1e-6
PAST_LEN = 4096

LANES = 128
BF16_ROWS = 16
VMEM_LIMIT = 60 * 1024 * 1024
NEG_INF = float("-inf")
LOG2E = 1.4426950408889634
ROW_TILE = 512
MLSTM_CHUNK = 128
FF_BLOCK = 2048
NT_DIMS = (((1,), (1,)), ((), ()))
MOD_SH1, MOD_SC1, MOD_GA1, MOD_SH2, MOD_SC2, MOD_GA2 = range(6)


def _params(sem):
    return pltpu.CompilerParams(dimension_semantics=sem, vmem_limit_bytes=VMEM_LIMIT)


def _const_spec(shape):
    nd = len(shape)
    return pl.BlockSpec(shape, lambda *_: (0,) * nd, pipeline_mode=pl.Buffered(1))


def _ada_body(c_ref, w_ref, b_ref, o_ref):
    c = c_ref[...]
    s = c * jax.nn.sigmoid(c)
    o_ref[...] = jnp.dot(s.astype(BF16), w_ref[...].astype(BF16),
                         preferred_element_type=F32) + b_ref[...]


def _ada_mod(c_all, w_ada, b_ada):
    rows = c_all.shape[0]
    return pl.pallas_call(
        _ada_body,
        grid=(6,),
        in_specs=[pl.BlockSpec((rows, D_MODEL), lambda j: (0, 0)),
                  pl.BlockSpec((D_MODEL, D_MODEL), lambda j: (0, j)),
                  pl.BlockSpec((1, D_MODEL), lambda j: (0, j))],
        out_specs=pl.BlockSpec((rows, D_MODEL), lambda j: (0, j)),
        out_shape=jax.ShapeDtypeStruct((rows, 6 * D_MODEL), F32),
        compiler_params=_params(("arbitrary",)),
        name="ada_mod",
    )(c_all, w_ada, b_ada)


def _segment_prefix(x, op, identity, seg):
    assert x.shape[1] % LANES == 0 and x.shape[1] % seg == 0
    pos = lax.broadcasted_iota(jnp.int32, x.shape, 1) % seg
    d = 1
    while d < seg:
        x = op(x, jnp.where(pos >= d, pltpu.roll(x, d, axis=1), identity))
        d *= 2
    return x


GATE_ROWS = 24
PROJ_OUTPUTS = ("mq", "mk", "mv", "og", "gt", "aq", "ak", "av")


def _in_proj_tile(x, sc, sh, g1, wa_ref, wb_ref, wg_ref, bg, gq, gk, pq_ref, pk_ref, store,
                  gate_chunk):
    s, r, d = x.shape
    m = s * r
    ms = jnp.mean(x * x, axis=-1, keepdims=True)
    y = x * lax.rsqrt(ms + EPS) * g1
    h = y * (1.0 + sc) + sh
    hb = h.reshape(m, d).astype(BF16)

    def proj(w_ref, lo, width):
        return lax.dot_general(hb, w_ref[lo:lo + width, :], NT_DIMS, preferred_element_type=F32)

    zg = lax.dot_general(wg_ref[...], hb, NT_DIMS,
                         preferred_element_type=F32)[:2 * M_HEADS] + bg
    row = lax.broadcasted_iota(jnp.int32, zg.shape, 0)
    log_sig = jnp.minimum(zg, 0.0) - jnp.log1p(jnp.exp(-jnp.abs(zg)))
    gt = jnp.where(row < M_HEADS, zg, log_sig)
    b8 = _segment_prefix(gt, jnp.add, 0.0, gate_chunk)
    c8 = gt - pltpu.roll(b8, M_HEADS, axis=0)
    cm8 = _segment_prefix(c8, jnp.maximum, NEG_INF, gate_chunk)
    store("gt", jnp.concatenate([c8, b8, cm8], axis=0))

    store("mq", proj(wa_ref, 0, M_W).astype(BF16))
    store("mk", (proj(wa_ref, M_W, M_W) * (M_DIM ** -0.5)).astype(BF16))
    store("mv", proj(wa_ref, 2 * M_W, M_W).astype(BF16))
    store("og", jax.nn.sigmoid(proj(wa_ref, 3 * M_W, M_W)).astype(BF16))

    aq = proj(wb_ref, 0, A_QW)
    ss_q = jnp.dot((aq * aq).astype(BF16), pq_ref[...], preferred_element_type=F32)
    store("aq", (aq * lax.rsqrt(ss_q * (1.0 / A_DIM) + EPS) * gq).astype(BF16))
    ak = proj(wb_ref, A_QW, A_KW)
    sq = ak * ak
    hi = sq.astype(BF16)
    lo = (sq - hi.astype(F32)).astype(BF16)
    ss_k = (jnp.dot(hi, pk_ref[...], preferred_element_type=F32)
            + jnp.dot(lo, pk_ref[...], preferred_element_type=F32))
    store("ak", ak * lax.rsqrt(ss_k * (1.0 / A_DIM) + EPS) * gk)
    store("av", proj(wb_ref, A_QW + A_KW, A_KW))


def _proj_weight_specs(d):
    return [_const_spec((1, d)), _const_spec((4 * M_W, d)), _const_spec((A_QW + 2 * A_KW, d)),
            _const_spec((BF16_ROWS, d)), _const_spec((2 * M_HEADS, 1)),
            _const_spec((1, A_QW)), _const_spec((1, A_KW)),
            _const_spec((A_QW, A_QW)), _const_spec((A_KW, A_KW))]


def _col_bcast(r):
    t = r.shape[1]
    if t % LANES == 0:
        return jnp.broadcast_to(r, (LANES, t)).T
    src = lax.broadcasted_iota(jnp.int32, (t, t), 0)
    dst = lax.broadcasted_iota(jnp.int32, (t, t), 1)
    col = jnp.sum(jnp.where(src == dst, jnp.broadcast_to(r, (t, t)), 0.0), axis=1, keepdims=True)
    return jnp.broadcast_to(col, (t, LANES))


def _eye_dim():
    return (lax.broadcasted_iota(jnp.int32, (M_DIM, M_DIM), 0)
            == lax.broadcasted_iota(jnp.int32, (M_DIM, M_DIM), 1))


def _mlstm_load_state(c0_ref, n0_ref, m0_ref, caug_ref, m_ref):
    for hd in range(M_HEADS):
        caug_ref[hd, :, :M_DIM] = c0_ref[0, hd]
        n_row = jnp.broadcast_to(n0_ref[0, hd:hd + 1, :], (M_DIM, M_DIM))
        n_col = jnp.sum(jnp.where(_eye_dim(), n_row, 0.0), axis=1, keepdims=True)
        caug_ref[hd, :, M_DIM:] = jnp.broadcast_to(n_col, (M_DIM, M_DIM))
    m_ref[...] = m0_ref[0]


def _mlstm_store_state(caug_ref, m_ref, c_out_ref, n_out_ref, m_out_ref):
    for hd in range(M_HEADS):
        c_out_ref[0, hd] = caug_ref[hd, :, :M_DIM]
        n_out_ref[0, hd:hd + 1, :] = jnp.sum(
            jnp.where(_eye_dim(), caug_ref[hd, :, M_DIM:], 0.0), axis=0, keepdims=True)
    m_out_ref[0] = m_ref[...]


def _mlstm_tile(q_ref, k_ref, v_ref, og_ref, gt_ref, gout_ref, caug_ref, m_ref, write_h, chunk):
    t = chunk
    n_chunks = q_ref.shape[1] // t
    bs = LANES if t % LANES == 0 else t
    tril = (lax.broadcasted_iota(jnp.int32, (bs, bs), 1)
            <= lax.broadcasted_iota(jnp.int32, (bs, bs), 0))
    m_prev = m_ref[...]
    for j in range(n_chunks):
        rows = slice(j * t, (j + 1) * t)
        c = gt_ref[0:M_HEADS, rows]
        b = gt_ref[12:12 + M_HEADS, rows]
        g = jnp.maximum(gt_ref[16:16 + M_HEADS, rows], m_prev)
        g_last = g[:, t - 1:t]
        c2 = c * LOG2E
        g2 = g * LOG2E
        e_all = jnp.exp2(-LOG2E * (b + g))
        w_src_all = jnp.exp2(c2 - LOG2E * g_last)
        w_inter_all = jnp.exp2(LOG2E * m_prev - g2)
        w_old_all = jnp.exp2(LOG2E * (m_prev - g_last))
        m_prev = b[:, t - 1:t] + g_last

        for hd in range(M_HEADS):
            sl = slice(hd * M_DIM, (hd + 1) * M_DIM)
            hr = slice(hd, hd + 1)
            g_b = _col_bcast(g2[hr])
            e_b = _col_bcast(e_all[hr])
            w_src = _col_bcast(w_src_all[hr])
            w_inter = _col_bcast(w_inter_all[hr])

            q = q_ref[0, rows, sl]
            k = k_ref[0, rows, sl]
            v = v_ref[0, rows, sl]
            caug = caug_ref[hd]
            inter = jnp.dot(q, caug.astype(BF16), preferred_element_type=F32)
            gain = gout_ref[hr, :]
            for rb in range(t // bs):
                rsl = slice(rb * bs, (rb + 1) * bs)
                ncols = (rb + 1) * bs
                s_mat = lax.dot_general(q[rsl], k[:ncols], NT_DIMS,
                                        preferred_element_type=F32)
                decay = []
                for cb in range(rb + 1):
                    arg = c2[hr, cb * bs:(cb + 1) * bs] - g_b[rsl, :bs]
                    decay.append(jnp.exp2(jnp.where(tril, arg, NEG_INF) if cb == rb else arg))
                s_mat = s_mat * (decay[0] if rb == 0 else jnp.concatenate(decay, axis=1))
                num = (jnp.dot(s_mat.astype(BF16), v[:ncols], preferred_element_type=F32)
                       + w_inter[rsl] * inter[rsl, :M_DIM])
                den = jnp.sum(s_mat, axis=1, keepdims=True) + w_inter[rsl] * inter[rsl, M_DIM:]
                hh = num / jnp.maximum(jnp.abs(den), e_b[rsl])
                hn = hh * lax.rsqrt(jnp.mean(hh * hh, axis=-1, keepdims=True) + EPS) * gain
                out_rows = slice(j * t + rb * bs, j * t + (rb + 1) * bs)
                write_h(out_rows, sl, (hn * og_ref[0, out_rows, sl].astype(F32)).astype(BF16))

            rhs = jnp.concatenate([v.astype(F32) * w_src, w_src], axis=1).astype(BF16)
            upd = lax.dot_general(k, rhs, (((0,), (0,)), ((), ())), preferred_element_type=F32)
            caug_ref[hd] = w_old_all[hr] * caug + upd
    m_ref[...] = m_prev


def _kv_halves(a):
    low = lax.broadcasted_iota(jnp.int32, a.shape, 1) < A_DIM
    lo_half = jnp.where(low, a, 0.0)
    hi_half = jnp.where(low, 0.0, a)
    pairs = [(lo_half, pltpu.roll(lo_half, A_DIM, axis=1)),
             (pltpu.roll(hi_half, A_DIM, axis=1), hi_half)]
    return [tuple(h.astype(BF16) for h in pair) for pair in pairs]


def _swa_biases(q_pos, k_pos):
    shift = CHUNK.bit_length() - 1
    assert 1 << shift == CHUNK
    qc = lax.shift_right_arithmetic(q_pos, shift)
    kc = lax.shift_right_arithmetic(k_pos, shift)
    allowed = (kc <= qc) & (kc >= qc - WIN_CHUNKS) & (k_pos >= 0)
    dist = jnp.abs(q_pos - k_pos).astype(F32)
    slopes = [2.0 ** (-8.0 * (head + 1) / A_HEADS) for head in range(A_HEADS)]
    return [jnp.where(allowed, (-LOG2E * slope) * dist, NEG_INF) for slope in slopes]


def _swa_block(q, kz, vz, biases, sink_ref, write_o, valid=None):
    tq = q.shape[0]
    pairs_per_kv = A_HEADS // A_KV_HEADS // 2
    for j in range(A_KV_HEADS):
        pairs = [j * pairs_per_kv + i for i in range(pairs_per_kv)]
        q_stack = jnp.concatenate([q[:, p * LANES:(p + 1) * LANES] for p in pairs], axis=0)
        acc = [None] * pairs_per_kv
        for e in range(2):
            logits_all = lax.dot_general(q_stack, kz[j][e], NT_DIMS, preferred_element_type=F32)
            probs, inv_denoms = [], []
            for i, p in enumerate(pairs):
                head = 2 * p + e
                sink = sink_ref[head] * LOG2E
                logits = logits_all[i * tq:(i + 1) * tq] + biases[head]
                if valid is not None:
                    logits = jnp.where(valid, logits, NEG_INF)
                mx = jnp.maximum(jnp.max(logits, axis=1, keepdims=True), sink)
                pr = jnp.exp2(logits - mx)
                inv_denoms.append(
                    1.0 / (jnp.sum(pr, axis=1, keepdims=True) + jnp.exp2(sink - mx)))
                probs.append(pr.astype(BF16))
            out_all = jnp.dot(jnp.concatenate(probs, axis=0), vz[j][e],
                              preferred_element_type=F32)
            for i in range(pairs_per_kv):
                part = out_all[i * tq:(i + 1) * tq] * inv_denoms[i]
                acc[i] = part if acc[i] is None else acc[i] + part
        for i, p in enumerate(pairs):
            write_o(slice(p * LANES, (p + 1) * LANES), acc[i].astype(BF16))


SWA_Q_ROWS = WIN_CHUNKS * CHUNK


def _swa_tile(sink_ref, q_ref, k_all, v_all, tile_start, write_o):
    tq = SWA_Q_ROWS
    kz = _kv_halves(k_all)
    vz = _kv_halves(v_all)
    rel_k = lax.broadcasted_iota(jnp.int32, (1, 2 * tq), 1)
    biases = _swa_biases(tq + lax.broadcasted_iota(jnp.int32, (tq, 1), 0), rel_k)
    for r in range(q_ref.shape[1] // tq):
        rows = slice(r * tq, (r + 1) * tq)
        band = slice(r * tq, (r + 2) * tq)
        valid = (tile_start - tq + rel_k >= 0) if r == 0 else None
        _swa_block(q_ref[0, rows, :],
                   [tuple(h[band] for h in pair) for pair in kz],
                   [tuple(h[band] for h in pair) for pair in vz],
                   biases, sink_ref,
                   lambda lanes, value, rows=rows: write_o(rows, lanes, value), valid=valid)


def _mix_residual_norm(x, mixed, ga1, sc2, sh2, g2, w_out):
    s, r, d = x.shape
    mix = jnp.dot(mixed, w_out, preferred_element_type=F32)
    x1 = x + ga1 * mix.reshape(s, r, d)
    ms = jnp.mean(x1 * x1, axis=-1, keepdims=True)
    h2 = (x1 * lax.rsqrt(ms + EPS) * g2) * (1.0 + sc2) + sh2
    return x1, h2.reshape(s * r, d).astype(BF16)


def _ffn_block(hb, w_up, w_down):
    u = jnp.dot(hb, w_up, preferred_element_type=F32)
    u = jnp.square(jnp.maximum(u, 0.0)).astype(BF16)
    return jnp.dot(u, w_down, preferred_element_type=F32)


def _ffn_weight_specs(d):
    return [_const_spec((1, d)), _const_spec((d, d)), _const_spec((d, D_FF)),
            _const_spec((D_FF, d))]


def _layer_body(sink_ref, xp_ref, sc1_ref, sh1_ref, g1_ref, wa_ref, wb_ref, wg_ref, bg_ref,
                gq_ref, gk_ref, pq_ref, pk_ref, gout_ref,
                xf_ref, ga1_ref, sc2_ref, sh2_ref, ga2_ref, g2_ref, wo_ref, wu_ref, wd_ref,
                y_ref, kt_ref, vt_ref, c_out_ref, n_out_ref, m_out_ref,
                mq_s, mk_s, mv_s, og_s, gt_s, aq_s, ak_s, av_s, kprev_s, vprev_s,
                mix_s, caug_ref, m_ref, *, n_tiles, tiles_per_seq, chunk):
    s = pl.program_id(0)
    rows = xp_ref.shape[1]
    tile = jnp.minimum(s, n_tiles - 1)
    j = tile % tiles_per_seq
    valid = s < n_tiles
    proj_s = {"mq": mq_s, "mk": mk_s, "mv": mv_s, "og": og_s, "gt": gt_s,
              "aq": aq_s, "ak": ak_s, "av": av_s}

    @pl.when(s == 0)
    def _():
        kprev_s[...] = jnp.zeros(kprev_s.shape, kprev_s.dtype)
        vprev_s[...] = jnp.zeros(vprev_s.shape, vprev_s.dtype)
        mix_s[...] = jnp.zeros(mix_s.shape, mix_s.dtype)

    @pl.when(jnp.logical_and(j == 0, valid))
    def _():
        caug_ref[...] = jnp.zeros(caug_ref.shape, caug_ref.dtype)
        m_ref[...] = jnp.zeros(m_ref.shape, m_ref.dtype)

    x1, hb = _mix_residual_norm(xf_ref[...], mix_s[...], ga1_ref[...], sc2_ref[...],
                                sh2_ref[...], g2_ref[...], wo_ref[...])

    def store(name, value):
        proj_s[name][...] = value if name == "gt" else value[None]

    _in_proj_tile(xp_ref[...], sc1_ref[...], sh1_ref[...], g1_ref[...], wa_ref, wb_ref, wg_ref,
                  bg_ref[...], gq_ref[...], gk_ref[...], pq_ref, pk_ref, store, chunk)

    def write_h(r, lanes, value):
        mix_s[r, lanes] = value

    def write_o(r, lanes, value):
        mix_s[r, slice(M_W + lanes.start, M_W + lanes.stop)] = value

    _mlstm_tile(mq_s, mk_s, mv_s, og_s, gt_s, gout_ref, caug_ref, m_ref, write_h, chunk)
    k_cur = ak_s[0]
    v_cur = av_s[0]
    _swa_tile(sink_ref, aq_s,
              jnp.concatenate([kprev_s[...], k_cur], axis=0),
              jnp.concatenate([vprev_s[...], v_cur], axis=0), j * rows, write_o)
    kprev_s[...] = k_cur[rows - SWA_Q_ROWS:]
    vprev_s[...] = v_cur[rows - SWA_Q_ROWS:]

    ffn = jnp.zeros(hb.shape, F32)
    for c in range(D_FF // FF_BLOCK):
        cols = slice(c * FF_BLOCK, (c + 1) * FF_BLOCK)
        ffn = ffn + _ffn_block(hb, wu_ref[:, cols], wd_ref[cols, :])
    y_ref[...] = x1 + ga2_ref[...] * ffn.reshape(x1.shape)

    @pl.when(jnp.logical_and(j == tiles_per_seq - 1, valid))
    def _():
        kt_ref[0] = k_cur[rows - SWA_Q_ROWS:]
        vt_ref[0] = v_cur[rows - SWA_Q_ROWS:]
        _mlstm_store_state(caug_ref, m_ref, c_out_ref, n_out_ref, m_out_ref)


def _layer(x, mod4, mod_row0, sinks, gout, proj_w, ffn_w, *, rows, chunk):
    ns, length, d = x.shape
    tps = length // rows
    n_tiles = ns * tps

    def stage(lag):
        def tile_idx(i):
            t = jnp.clip(i - lag, 0, n_tiles - 1)
            return t // tps, t % tps
        x_tile = pl.BlockSpec((1, rows, d), lambda i: (*tile_idx(i), 0))
        mod = lambda k: pl.BlockSpec((1, None, 1, d),
                                     lambda i: (mod_row0 + tile_idx(i)[0], k, 0, 0))
        per_seq = lambda shape: pl.BlockSpec(
            (1,) + shape, lambda i: (tile_idx(i)[0],) + (0,) * len(shape))
        return x_tile, mod, per_seq

    xp_tile, p_mod, p_seq = stage(0)
    xf_tile, f_mod, _ = stage(1)
    st_c, st_n, st_m = p_seq((M_HEADS, M_DIM, M_DIM)), p_seq((M_HEADS, M_DIM)), p_seq((M_HEADS, 1))
    tail = p_seq((SWA_Q_ROWS, A_KW))
    one = lambda shape, dtype: pltpu.VMEM((1,) + shape, dtype)
    return pl.pallas_call(
        functools.partial(_layer_body, n_tiles=n_tiles, tiles_per_seq=tps, chunk=chunk),
        grid=(n_tiles + 1,),
        in_specs=([pl.BlockSpec(memory_space=pltpu.SMEM), xp_tile, p_mod(MOD_SC1), p_mod(MOD_SH1)]
                  + _proj_weight_specs(d)
                  + [_const_spec((M_HEADS, M_DIM)), xf_tile,
                     f_mod(MOD_GA1), f_mod(MOD_SC2), f_mod(MOD_SH2), f_mod(MOD_GA2)]
                  + _ffn_weight_specs(d)),
        out_specs=(xf_tile, tail, tail, st_c, st_n, st_m),
        out_shape=(jax.ShapeDtypeStruct((ns, length, d), F32),
                   jax.ShapeDtypeStruct((ns, SWA_Q_ROWS, A_KW), F32),
                   jax.ShapeDtypeStruct((ns, SWA_Q_ROWS, A_KW), F32),
                   jax.ShapeDtypeStruct((ns, M_HEADS, M_DIM, M_DIM), F32),
                   jax.ShapeDtypeStruct((ns, M_HEADS, M_DIM), F32),
                   jax.ShapeDtypeStruct((ns, M_HEADS, 1), F32)),
        scratch_shapes=[one((rows, M_W), BF16), one((rows, M_W), BF16),
                        one((rows, M_W), BF16), one((rows, M_W), BF16),
                        pltpu.VMEM((GATE_ROWS, rows), F32), one((rows, A_QW), BF16),
                        one((rows, A_KW), F32), one((rows, A_KW), F32),
                        pltpu.VMEM((SWA_Q_ROWS, A_KW), F32), pltpu.VMEM((SWA_Q_ROWS, A_KW), F32),
                        pltpu.VMEM((rows, M_W + A_QW), BF16),
                        pltpu.VMEM((M_HEADS, M_DIM, 2 * M_DIM), F32),
                        pltpu.VMEM((M_HEADS, 1), F32)],
        compiler_params=_params(("arbitrary",)),
        name="layer",
    )(sinks, x, mod4, mod4, *proj_w, gout, x, mod4, mod4, mod4, mod4, *ffn_w)


class _LaneWindow:
    def __init__(self, ref, offset):
        self.ref, self.offset = ref, offset

    def __getitem__(self, idx):
        rows, cols = idx
        return self.ref[rows, slice(cols.start + self.offset, cols.stop + self.offset)]


def _sample_mixers(sink_ref, x_ref, sc1_ref, sh1_ref, g1_ref, wa_ref, wb_ref, wg_ref, bg_ref,
                   gq_ref, gk_ref, pq_ref, pk_ref, c0_ref, n0_ref, m0_ref, gout_ref,
                   ck_ref, cv_ref, ko_ref, vo_ref, c_out_ref, n_out_ref, m_out_ref,
                   mq_s, mk_s, mv_s, og_s, gt_s, aq_s, ak_s, av_s, mix_s, caug_s, m_s):
    ns, t, _ = x_ref.shape
    cached = ck_ref.shape[1]
    proj_s = {"mq": mq_s, "mk": mk_s, "mv": mv_s, "og": og_s, "gt": gt_s,
              "aq": aq_s, "ak": ak_s, "av": av_s}

    def store(name, value):
        proj_s[name][...] = value if name == "gt" else value.reshape(ns, t, value.shape[-1])

    _in_proj_tile(x_ref[...], sc1_ref[...], sh1_ref[...], g1_ref[...], wa_ref, wb_ref, wg_ref,
                  bg_ref[...], gq_ref[...], gk_ref[...], pq_ref, pk_ref, store, t)

    seq = lambda ref, b: ref.at[pl.ds(b, 1)]
    for b in range(ns):
        caug_b, m_b = caug_s.at[b], m_s.at[b]
        _mlstm_load_state(seq(c0_ref, b), seq(n0_ref, b), seq(m0_ref, b), caug_b, m_b)

        def write_h(rows, lanes, value, b=b):
            mix_s[b * t + rows.start:b * t + rows.stop, lanes] = value

        _mlstm_tile(seq(mq_s, b), seq(mk_s, b), seq(mv_s, b), seq(og_s, b),
                    _LaneWindow(gt_s, b * t), gout_ref, caug_b, m_b, write_h, t)
        _mlstm_store_state(caug_b, m_b, seq(c_out_ref, b), seq(n_out_ref, b), seq(m_out_ref, b))

    tk = 2 * LANES
    pad = jnp.zeros((tk - cached - t, A_KW), F32)
    q_pos = PAST_LEN + lax.broadcasted_iota(jnp.int32, (t, 1), 0)
    idx = lax.broadcasted_iota(jnp.int32, (1, tk), 1)
    k_pos = jnp.where(idx < cached + t, PAST_LEN - cached + idx, -1)
    biases = _swa_biases(q_pos, k_pos)
    for b in range(ns):
        k_new, v_new = ak_s[b], av_s[b]

        def write_o(lanes, value, b=b):
            mix_s[b * t:(b + 1) * t, M_W + lanes.start:M_W + lanes.stop] = value

        _swa_block(aq_s[b], _kv_halves(jnp.concatenate([ck_ref[b], k_new, pad], axis=0)),
                   _kv_halves(jnp.concatenate([cv_ref[b], v_new, pad], axis=0)),
                   biases, sink_ref, write_o)
        ko_ref[b] = jnp.concatenate([ck_ref[b, t:, :], k_new], axis=0)
        vo_ref[b] = jnp.concatenate([cv_ref[b, t:, :], v_new], axis=0)


SAMPLE_FF_BLOCK = 512


def _sample_body(sink_ref, x_ref, sc1_ref, sh1_ref, g1_ref, wt_ref, bg_ref, gq_ref, gk_ref,
                 pq_ref, pk_ref, c0_ref, n0_ref, m0_ref, gout_ref, ck_ref, cv_ref,
                 ga1_ref, sc2_ref, sh2_ref, ga2_ref, g2_ref, wo_ref, wu_ref, wd_ref,
                 y_ref, ko_ref, vo_ref, c_out_ref, n_out_ref, m_out_ref,
                 wa_o, wb_o, wg_o, wo_o, wu_o, wd_o,
                 mq_s, mk_s, mv_s, og_s, gt_s, aq_s, ak_s, av_s, mix_s, caug_s, m_s,
                 x1_s, hb_s, acc_s):
    step = pl.program_id(0)
    ns, t, d = x_ref.shape

    @pl.when(step == 0)
    def _():
        g_lo = 4 * M_W
        g_hi = g_lo + 2 * M_HEADS
        wa_o[...] = wt_ref[:g_lo, :].astype(BF16)
        wg_o[...] = jnp.concatenate(
            [wt_ref[g_lo:g_hi, :], jnp.zeros((BF16_ROWS - 2 * M_HEADS, d), F32)],
            axis=0).astype(BF16)
        wb_o[...] = wt_ref[g_hi:, :].astype(BF16)
        wo_o[...] = wo_ref[...].astype(BF16)
        _sample_mixers(sink_ref, x_ref, sc1_ref, sh1_ref, g1_ref, wa_o, wb_o, wg_o, bg_ref,
                       gq_ref, gk_ref, pq_ref, pk_ref, c0_ref, n0_ref, m0_ref, gout_ref,
                       ck_ref, cv_ref, ko_ref, vo_ref, c_out_ref, n_out_ref, m_out_ref,
                       mq_s, mk_s, mv_s, og_s, gt_s, aq_s, ak_s, av_s, mix_s, caug_s, m_s)
        x1, hb = _mix_residual_norm(x_ref[...], mix_s[...], ga1_ref[...], sc2_ref[...],
                                    sh2_ref[...], g2_ref[...], wo_o[...])
        x1_s[...] = x1
        hb_s[...] = hb
        acc_s[...] = jnp.zeros(acc_s.shape, acc_s.dtype)

    @pl.when(step >= 1)
    def _():
        w_up = wu_ref[...].astype(BF16)
        w_down = wd_ref[...].astype(BF16)
        wu_o[...] = w_up
        wd_o[...] = w_down
        acc_s[...] += _ffn_block(hb_s[...], w_up, w_down)

    @pl.when(step == pl.num_programs(0) - 1)
    def _():
        y_ref[...] = x1_s[...] + ga2_ref[...] * acc_s[...].reshape(ns, t, d)


def _sample_layer(x, mod4, sinks, c0, n0, m0, gout, cache_k, cache_v, proj_small, wt, g2,
                  w_out, w_up, w_down):
    ns, t, d = x.shape
    cached = cache_k.shape[1]
    assert cached >= t
    g1, bg, gq_t, gk_t, pq, pk = proj_small
    fb = SAMPLE_FF_BLOCK
    n_blocks = D_FF // fb
    blk = lambda i: jnp.maximum(i - 1, 0)
    whole = lambda shape: pl.BlockSpec(shape, lambda i: (0,) * len(shape))
    mod = lambda k: pl.BlockSpec((ns, None, 1, d), lambda i: (0, k, 0, 0))
    st_c, st_n, st_m = (whole((ns, M_HEADS, M_DIM, M_DIM)), whole((ns, M_HEADS, M_DIM)),
                        whole((ns, M_HEADS, 1)))
    kv = whole((ns, cached, A_KW))
    up_blk = pl.BlockSpec((d, fb), lambda i: (0, blk(i)))
    down_blk = pl.BlockSpec((fb, d), lambda i: (blk(i), 0))
    n_in = wt.shape[0]
    n_b = n_in - 4 * M_W - 2 * M_HEADS
    return pl.pallas_call(
        _sample_body,
        grid=(1 + n_blocks,),
        in_specs=[pl.BlockSpec(memory_space=pltpu.SMEM), whole((ns, t, d)),
                  mod(MOD_SC1), mod(MOD_SH1),
                  _const_spec((1, d)), _const_spec((n_in, d)), _const_spec((2 * M_HEADS, 1)),
                  _const_spec((1, A_QW)), _const_spec((1, A_KW)),
                  _const_spec((A_QW, A_QW)), _const_spec((A_KW, A_KW)),
                  st_c, st_n, st_m, _const_spec((M_HEADS, M_DIM)), kv, kv,
                  mod(MOD_GA1), mod(MOD_SC2), mod(MOD_SH2), mod(MOD_GA2),
                  _const_spec((1, d)), _const_spec((d, d)), up_blk, down_blk],
        out_specs=(whole((ns, t, d)), kv, kv, st_c, st_n, st_m,
                   _const_spec((4 * M_W, d)), _const_spec((n_b, d)), _const_spec((BF16_ROWS, d)),
                   _const_spec((d, d)), up_blk, down_blk),
        out_shape=(jax.ShapeDtypeStruct((ns, t, d), F32),
                   jax.ShapeDtypeStruct((ns, cached, A_KW), F32),
                   jax.ShapeDtypeStruct((ns, cached, A_KW), F32),
                   jax.ShapeDtypeStruct((ns, M_HEADS, M_DIM, M_DIM), F32),
                   jax.ShapeDtypeStruct((ns, M_HEADS, M_DIM), F32),
                   jax.ShapeDtypeStruct((ns, M_HEADS, 1), F32),
                   jax.ShapeDtypeStruct((4 * M_W, d), BF16),
                   jax.ShapeDtypeStruct((n_b, d), BF16),
                   jax.ShapeDtypeStruct((BF16_ROWS, d), BF16),
                   jax.ShapeDtypeStruct((d, d), BF16),
                   jax.ShapeDtypeStruct((d, D_FF), BF16),
                   jax.ShapeDtypeStruct((D_FF, d), BF16)),
        scratch_shapes=[pltpu.VMEM((ns, t, M_W), BF16), pltpu.VMEM((ns, t, M_W), BF16),
                        pltpu.VMEM((ns, t, M_W), BF16), pltpu.VMEM((ns, t, M_W), BF16),
                        pltpu.VMEM((GATE_ROWS, ns * t), F32), pltpu.VMEM((ns, t, A_QW), BF16),
                        pltpu.VMEM((ns, t, A_KW), F32), pltpu.VMEM((ns, t, A_KW), F32),
                        pltpu.VMEM((ns * t, M_W + A_QW), BF16),
                        pltpu.VMEM((ns, M_HEADS, M_DIM, 2 * M_DIM), F32),
                        pltpu.VMEM((ns, M_HEADS, 1), F32),
                        pltpu.VMEM((ns, t, d), F32), pltpu.VMEM((ns * t, d), BF16),
                        pltpu.VMEM((ns * t, d), F32)],
        compiler_params=_params(("arbitrary",)),
        name="sample_layer",
    )(sinks, x, mod4, mod4, g1, wt, bg, gq_t, gk_t, pq, pk, c0, n0, m0, gout, cache_k, cache_v,
      mod4, mod4, mod4, mod4, g2, w_out, w_up, w_down)


def _block_ones(n):
    g = jnp.arange(n, dtype=jnp.int32) // A_DIM
    return (g[:, None] == g[None, :]).astype(BF16)


def kernel(x_prompt, x_sample, c_prompt, c_sample, cache_swa_k, cache_swa_v, state_mlstm_C,
           state_mlstm_n, state_mlstm_m, w_ada, b_ada, g_norm1, w_in, b_gates, g_q, g_k, sinks,
           g_mlstm_out, w_out, g_norm2, w_up, w_down):
    depth = w_ada.shape[0]
    assert depth == 1
    bp, lp, d = x_prompt.shape
    bs, ls, _ = x_sample.shape
    wb = cache_swa_k.shape[2]

    xp, xs = x_prompt, x_sample
    for l in range(depth):
        gq_t = (jnp.tile(g_q[l], A_HEADS) * (LOG2E * A_DIM ** -0.5))[None, :]
        proj_small = (g_norm1[l][None, :], b_gates[l][:, None], gq_t,
                      jnp.tile(g_k[l], A_KV_HEADS)[None, :], _block_ones(A_QW), _block_ones(A_KW))
        g2 = g_norm2[l][None, :]

        mod_rows = -(-(bp + bs) // 8) * 8
        c_all = jnp.concatenate([c_sample, c_prompt, jnp.zeros((mod_rows - bp - bs, d), F32)],
                                axis=0)
        mod4 = _ada_mod(c_all, w_ada[l], b_ada[l][None, :]).reshape(mod_rows, 6, 1, d)

        xs, ko, vo, c, n, m, wa, wb_, wg, wo, wu, wd = _sample_layer(
            xs, mod4, sinks[l], state_mlstm_C[l], state_mlstm_n[l], state_mlstm_m[l][..., None],
            g_mlstm_out[l], cache_swa_k[l].reshape(bs, wb, A_KW),
            cache_swa_v[l].reshape(bs, wb, A_KW), proj_small, w_in[l].T, g2,
            w_out[l], w_up[l], w_down[l])
        st_s = (ko.reshape(bs, wb, A_KV_HEADS, A_DIM), vo.reshape(bs, wb, A_KV_HEADS, A_DIM),
                c, n, m[..., 0])

        g1, bg, gq_t, gk_t, pq, pk = proj_small
        xp, kt, vt, c, n, m = _layer(xp, mod4, bs, sinks[l], g_mlstm_out[l],
                                     (g1, wa, wb_, wg, bg, gq_t, gk_t, pq, pk), (g2, wo, wu, wd),
                                     rows=ROW_TILE, chunk=MLSTM_CHUNK)
        win = WIN_CHUNKS * CHUNK
        st_p = (kt.reshape(bp, win, A_KV_HEADS, A_DIM), vt.reshape(bp, win, A_KV_HEADS, A_DIM),
                c, n, m[..., 0])

    return (xp, xs) + tuple(s[None] for s in st_p) + tuple(s[None] for s in st_s)
```

```python
import functools

import jax
import jax.numpy as jnp
from jax import lax
from jax.experimental import pallas as pl
from jax.experimental.pallas import tpu as pltpu

F32 = jnp.float32
BF16 = jnp.bfloat16

D_MODEL = 1024
CHUNK = 64
M_HEADS = 4
M_DIM = 128
M_W = M_HEADS * M_DIM
A_HEADS = 8
A_KV_HEADS = 2
A_DIM = 64
A_QW = A_HEADS * A_DIM
A_KW = A_KV_HEADS * A_DIM
WIN_CHUNKS = 2
D_FF = 4 * D_MODEL
EPS = 1e-6
PAST_LEN = 4096

LANES = 128
BF16_ROWS = 16
VMEM_LIMIT = 60 * 1024 * 1024
NEG_INF = float("-inf")
LOG2E = 1.4426950408889634
ROW_TILE = 512
MLSTM_CHUNK = 128
FF_BLOCK = 2048
NT_DIMS = (((1,), (1,)), ((), ()))
MOD_SH1, MOD_SC1, MOD_GA1, MOD_SH2, MOD_SC2, MOD_GA2 = range(6)


def _params(sem):
    return pltpu.CompilerParams(dimension_semantics=sem, vmem_limit_bytes=VMEM_LIMIT)


def _const_spec(shape):
    nd = len(shape)
    return pl.BlockSpec(shape, lambda *_: (0,) * nd, pipeline_mode=pl.Buffered(1))


def _ada_body(c_ref, w_ref, b_ref, o_ref):
    c = c_ref[...]
    s = c * jax.nn.sigmoid(c)
    o_ref[...] = jnp.dot(s.astype(BF16), w_ref[...].astype(BF16),
                         preferred_element_type=F32) + b_ref[...]


def _ada_mod(c_all, w_ada, b_ada):
    rows = c_all.shape[0]
    return pl.pallas_call(
        _ada_body,
        grid=(6,),
        in_specs=[pl.BlockSpec((rows, D_MODEL), lambda j: (0, 0)),
                  pl.BlockSpec((D_MODEL, D_MODEL), lambda j: (0, j)),
                  pl.BlockSpec((1, D_MODEL), lambda j: (0, j))],
        out_specs=pl.BlockSpec((rows, D_MODEL), lambda j: (0, j)),
        out_shape=jax.ShapeDtypeStruct((rows, 6 * D_MODEL), F32),
        compiler_params=_params(("arbitrary",)),
        name="ada_mod",
    )(c_all, w_ada, b_ada)


def _segment_prefix(x, op, identity, seg):
    assert x.shape[1] % LANES == 0 and x.shape[1] % seg == 0
    pos = lax.broadcasted_iota(jnp.int32, x.shape, 1) % seg
    d = 1
    while d < seg:
        x = op(x, jnp.where(pos >= d, pltpu.roll(x, d, axis=1), identity))
        d *= 2
    return x


GATE_ROWS = 24
PROJ_OUTPUTS = ("mq", "mk", "mv", "og", "gt", "aq", "ak", "av")


def _in_proj_tile(x, sc, sh, g1, wa_ref, wb_ref, wg_ref, bg, gq, gk, pq_ref, pk_ref, store,
                  gate_chunk):
    s, r, d = x.shape
    m = s * r
    ms = jnp.mean(x * x, axis=-1, keepdims=True)
    y = x * lax.rsqrt(ms + EPS) * g1
    h = y * (1.0 + sc) + sh
    hb = h.reshape(m, d).astype(BF16)

    def proj(w_ref, lo, width):
        return lax.dot_general(hb, w_ref[lo:lo + width, :], NT_DIMS, preferred_element_type=F32)

    zg = lax.dot_general(wg_ref[...], hb, NT_DIMS,
                         preferred_element_type=F32)[:2 * M_HEADS] + bg
    row = lax.broadcasted_iota(jnp.int32, zg.shape, 0)
    log_sig = jnp.minimum(zg, 0.0) - jnp.log1p(jnp.exp(-jnp.abs(zg)))
    gt = jnp.where(row < M_HEADS, zg, log_sig)
    b8 = _segment_prefix(gt, jnp.add, 0.0, gate_chunk)
    c8 = gt - pltpu.roll(b8, M_HEADS, axis=0)
    cm8 = _segment_prefix(c8, jnp.maximum, NEG_INF, gate_chunk)
    store("gt", jnp.concatenate([c8, b8, cm8], axis=0))

    store("mq", proj(wa_ref, 0, M_W).astype(BF16))
    store("mk", (proj(wa_ref, M_W, M_W) * (M_DIM ** -0.5)).astype(BF16))
    store("mv", proj(wa_ref, 2 * M_W, M_W).astype(BF16))
    store("og", jax.nn.sigmoid(proj(wa_ref, 3 * M_W, M_W)).astype(BF16))

    aq = proj(wb_ref, 0, A_QW)
    ss_q = jnp.dot((aq * aq).astype(BF16), pq_ref[...], preferred_element_type=F32)
    store("aq", (aq * lax.rsqrt(ss_q * (1.0 / A_DIM) + EPS) * gq).astype(BF16))
    ak = proj(wb_ref, A_QW, A_KW)
    sq = ak * ak
    hi = sq.astype(BF16)
    lo = (sq - hi.astype(F32)).astype(BF16)
    ss_k = (jnp.dot(hi, pk_ref[...], preferred_element_type=F32)
            + jnp.dot(lo, pk_ref[...], preferred_element_type=F32))
    store("ak", ak * lax.rsqrt(ss_k * (1.0 / A_DIM) + EPS) * gk)
    store("av", proj(wb_ref, A_QW + A_KW, A_KW))


def _proj_weight_specs(d):
    return [_const_spec((1, d)), _const_spec((4 * M_W, d)), _const_spec((A_QW + 2 * A_KW, d)),
            _const_spec((BF16_ROWS, d)), _const_spec((2 * M_HEADS, 1)),
            _const_spec((1, A_QW)), _const_spec((1, A_KW)),
            _const_spec((A_QW, A_QW)), _const_spec((A_KW, A_KW))]


def _col_bcast(r):
    t = r.shape[1]
    if t % LANES == 0:
        return jnp.broadcast_to(r, (LANES, t)).T
    src = lax.broadcasted_iota(jnp.int32, (t, t), 0)
    dst = lax.broadcasted_iota(jnp.int32, (t, t), 1)
    col = jnp.sum(jnp.where(src == dst, jnp.broadcast_to(r, (t, t)), 0.0), axis=1, keepdims=True)
    return jnp.broadcast_to(col, (t, LANES))


def _eye_dim():
    return (lax.broadcasted_iota(jnp.int32, (M_DIM, M_DIM), 0)
            == lax.broadcasted_iota(jnp.int32, (M_DIM, M_DIM), 1))


def _mlstm_load_state(c0_ref, n0_ref, m0_ref, caug_ref, m_ref):
    for hd in range(M_HEADS):
        caug_ref[hd, :, :M_DIM] = c0_ref[0, hd]
        n_row = jnp.broadcast_to(n0_ref[0, hd:hd + 1, :], (M_DIM, M_DIM))
        n_col = jnp.sum(jnp.where(_eye_dim(), n_row, 0.0), axis=1, keepdims=True)
        caug_ref[hd, :, M_DIM:] = jnp.broadcast_to(n_col, (M_DIM, M_DIM))
    m_ref[...] = m0_ref[0]


def _mlstm_store_state(caug_ref, m_ref, c_out_ref, n_out_ref, m_out_ref):
    for hd in range(M_HEADS):
        c_out_ref[0, hd] = caug_ref[hd, :, :M_DIM]
        n_out_ref[0, hd:hd + 1, :] = jnp.sum(
            jnp.where(_eye_dim(), caug_ref[hd, :, M_DIM:], 0.0), axis=0, keepdims=True)
    m_out_ref[0] = m_ref[...]


def _mlstm_tile(q_ref, k_ref, v_ref, og_ref, gt_ref, gout_ref, caug_ref, m_ref, write_h, chunk):
    t = chunk
    n_chunks = q_ref.shape[1] // t
    bs = LANES if t % LANES == 0 else t
    tril = (lax.broadcasted_iota(jnp.int32, (bs, bs), 1)
            <= lax.broadcasted_iota(jnp.int32, (bs, bs), 0))
    m_prev = m_ref[...]
    for j in range(n_chunks):
        rows = slice(j * t, (j + 1) * t)
        c = gt_ref[0:M_HEADS, rows]
        b = gt_ref[12:12 + M_HEADS, rows]
        g = jnp.maximum(gt_ref[16:16 + M_HEADS, rows], m_prev)
        g_last = g[:, t - 1:t]
        c2 = c * LOG2E
        g2 = g * LOG2E
        e_all = jnp.exp2(-LOG2E * (b + g))
        w_src_all = jnp.exp2(c2 - LOG2E * g_last)
        w_inter_all = jnp.exp2(LOG2E * m_prev - g2)
        w_old_all = jnp.exp2(LOG2E * (m_prev - g_last))
        m_prev = b[:, t - 1:t] + g_last

        for hd in range(M_HEADS):
            sl = slice(hd * M_DIM, (hd + 1) * M_DIM)
            hr = slice(hd, hd + 1)
            g_b = _col_bcast(g2[hr])
            e_b = _col_bcast(e_all[hr])
            w_src = _col_bcast(w_src_all[hr])
            w_inter = _col_bcast(w_inter_all[hr])

            q = q_ref[0, rows, sl]
            k = k_ref[0, rows, sl]
            v = v_ref[0, rows, sl]
            caug = caug_ref[hd]
            inter = jnp.dot(q, caug.astype(BF16), preferred_element_type=F32)
            gain = gout_ref[hr, :]
            for rb in range(t // bs):
                rsl = slice(rb * bs, (rb + 1) * bs)
                ncols = (rb + 1) * bs
                s_mat = lax.dot_general(q[rsl], k[:ncols], NT_DIMS,
                                        preferred_element_type=F32)
                decay = []
                for cb in range(rb + 1):
                    arg = c2[hr, cb * bs:(cb + 1) * bs] - g_b[rsl, :bs]
                    decay.append(jnp.exp2(jnp.where(tril, arg, NEG_INF) if cb == rb else arg))
                s_mat = s_mat * (decay[0] if rb == 0 else jnp.concatenate(decay, axis=1))
                num = (jnp.dot(s_mat.astype(BF16), v[:ncols], preferred_element_type=F32)
                       + w_inter[rsl] * inter[rsl, :M_DIM])
                den = jnp.sum(s_mat, axis=1, keepdims=True) + w_inter[rsl] * inter[rsl, M_DIM:]
                hh = num / jnp.maximum(jnp.abs(den), e_b[rsl])
                hn = hh * lax.rsqrt(jnp.mean(hh * hh, axis=-1, keepdims=True) + EPS) * gain
                out_rows = slice(j * t + rb * bs, j * t + (rb + 1) * bs)
                write_h(out_rows, sl, (hn * og_ref[0, out_rows, sl].astype(F32)).astype(BF16))

            rhs = jnp.concatenate([v.astype(F32) * w_src, w_src], axis=1).astype(BF16)
            upd = lax.dot_general(k, rhs, (((0,), (0,)), ((), ())), preferred_element_type=F32)
            caug_ref[hd] = w_old_all[hr] * caug + upd
    m_ref[...] = m_prev


def _kv_halves(a):
    low = lax.broadcasted_iota(jnp.int32, a.shape, 1) < A_DIM
    lo_half = jnp.where(low, a, 0.0)
    hi_half = jnp.where(low, 0.0, a)
    pairs = [(lo_half, pltpu.roll(lo_half, A_DIM, axis=1)),
             (pltpu.roll(hi_half, A_DIM, axis=1), hi_half)]
    return [tuple(h.astype(BF16) for h in pair) for pair in pairs]


def _swa_biases(q_pos, k_pos):
    shift = CHUNK.bit_length() - 1
    assert 1 << shift == CHUNK
    qc = lax.shift_right_arithmetic(q_pos, shift)
    kc = lax.shift_right_arithmetic(k_pos, shift)
    allowed = (kc <= qc) & (kc >= qc - WIN_CHUNKS) & (k_pos >= 0)
    dist = jnp.abs(q_pos - k_pos).astype(F32)
    slopes = [2.0 ** (-8.0 * (head + 1) / A_HEADS) for head in range(A_HEADS)]
    return [jnp.where(allowed, (-LOG2E * slope) * dist, NEG_INF) for slope in slopes]


def _swa_block(q, kz, vz, biases, sink_ref, write_o, valid=None):
    tq = q.shape[0]
    pairs_per_kv = A_HEADS // A_KV_HEADS // 2
    for j in range(A_KV_HEADS):
        pairs = [j * pairs_per_kv + i for i in range(pairs_per_kv)]
        q_stack = jnp.concatenate([q[:, p * LANES:(p + 1) * LANES] for p in pairs], axis=0)
        acc = [None] * pairs_per_kv
        for e in range(2):
            logits_all = lax.dot_general(q_stack, kz[j][e], NT_DIMS, preferred_element_type=F32)
            probs, inv_denoms = [], []
            for i, p in enumerate(pairs):
                head = 2 * p + e
                sink = sink_ref[head] * LOG2E
                logits = logits_all[i * tq:(i + 1) * tq] + biases[head]
                if valid is not None:
                    logits = jnp.where(valid, logits, NEG_INF)
                mx = jnp.maximum(jnp.max(logits, axis=1, keepdims=True), sink)
                pr = jnp.exp2(logits - mx)
                inv_denoms.append(
                    1.0 / (jnp.sum(pr, axis=1, keepdims=True) + jnp.exp2(sink - mx)))
                probs.append(pr.astype(BF16))
            out_all = jnp.dot(jnp.concatenate(probs, axis=0), vz[j][e],
                              preferred_element_type=F32)
            for i in range(pairs_per_kv):
                part = out_all[i * tq:(i + 1) * tq] * inv_denoms[i]
                acc[i] = part if acc[i] is None else acc[i] + part
        for i, p in enumerate(pairs):
            write_o(slice(p * LANES, (p + 1) * LANES), acc[i].astype(BF16))


SWA_Q_ROWS = WIN_CHUNKS * CHUNK


def _swa_tile(sink_ref, q_ref, k_all, v_all, tile_start, write_o):
    tq = SWA_Q_ROWS
    kz = _kv_halves(k_all)
    vz = _kv_halves(v_all)
    rel_k = lax.broadcasted_iota(jnp.int32, (1, 2 * tq), 1)
    biases = _swa_biases(tq + lax.broadcasted_iota(jnp.int32, (tq, 1), 0), rel_k)
    for r in range(q_ref.shape[1] // tq):
        rows = slice(r * tq, (r + 1) * tq)
        band = slice(r * tq, (r + 2) * tq)
        valid = (tile_start - tq + rel_k >= 0) if r == 0 else None
        _swa_block(q_ref[0, rows, :],
                   [tuple(h[band] for h in pair) for pair in kz],
                   [tuple(h[band] for h in pair) for pair in vz],
                   biases, sink_ref,
                   lambda lanes, value, rows=rows: write_o(rows, lanes, value), valid=valid)


def _mix_residual_norm(x, mixed, ga1, sc2, sh2, g2, w_out):
    s, r, d = x.shape
    mix = jnp.dot(mixed, w_out, preferred_element_type=F32)
    x1 = x + ga1 * mix.reshape(s, r, d)
    ms = jnp.mean(x1 * x1, axis=-1, keepdims=True)
    h2 = (x1 * lax.rsqrt(ms + EPS) * g2) * (1.0 + sc2) + sh2
    return x1, h2.reshape(s * r, d).astype(BF16)


def _ffn_block(hb, w_up, w_down):
    u = jnp.dot(hb, w_up, preferred_element_type=F32)
    u = jnp.square(jnp.maximum(u, 0.0)).astype(BF16)
    return jnp.dot(u, w_down, preferred_element_type=F32)


def _ffn_weight_specs(d):
    return [_const_spec((1, d)), _const_spec((d, d)), _const_spec((d, D_FF)),
            _const_spec((D_FF, d))]


def _layer_body(sink_ref, xp_ref, sc1_ref, sh1_ref, g1_ref, wa_ref, wb_ref, wg_ref, bg_ref,
                gq_ref, gk_ref, pq_ref, pk_ref, gout_ref,
                xf_ref, ga1_ref, sc2_ref, sh2_ref, ga2_ref, g2_ref, wo_ref, wu_ref, wd_ref,
                y_ref, kt_ref, vt_ref, c_out_ref, n_out_ref, m_out_ref,
                mq_s, mk_s, mv_s, og_s, gt_s, aq_s, ak_s, av_s, kprev_s, vprev_s,
                mix_s, caug_ref, m_ref, *, n_tiles, tiles_per_seq, chunk):
    s = pl.program_id(0)
    rows = xp_ref.shape[1]
    tile = jnp.minimum(s, n_tiles - 1)
    j = tile % tiles_per_seq
    valid = s < n_tiles
    proj_s = {"mq": mq_s, "mk": mk_s, "mv": mv_s, "og": og_s, "gt": gt_s,
              "aq": aq_s, "ak": ak_s, "av": av_s}

    @pl.when(s == 0)
    def _():
        kprev_s[...] = jnp.zeros(kprev_s.shape, kprev_s.dtype)
        vprev_s[...] = jnp.zeros(vprev_s.shape, vprev_s.dtype)
        mix_s[...] = jnp.zeros(mix_s.shape, mix_s.dtype)

    @pl.when(jnp.logical_and(j == 0, valid))
    def _():
        caug_ref[...] = jnp.zeros(caug_ref.shape, caug_ref.dtype)
        m_ref[...] = jnp.zeros(m_ref.shape, m_ref.dtype)

    x1, hb = _mix_residual_norm(xf_ref[...], mix_s[...], ga1_ref[...], sc2_ref[...],
                                sh2_ref[...], g2_ref[...], wo_ref[...])

    def store(name, value):
        proj_s[name][...] = value if name == "gt" else value[None]

    _in_proj_tile(xp_ref[...], sc1_ref[...], sh1_ref[...], g1_ref[...], wa_ref, wb_ref, wg_ref,
                  bg_ref[...], gq_ref[...], gk_ref[...], pq_ref, pk_ref, store, chunk)

    def write_h(r, lanes, value):
        mix_s[r, lanes] = value

    def write_o(r, lanes, value):
        mix_s[r, slice(M_W + lanes.start, M_W + lanes.stop)] = value

    _mlstm_tile(mq_s, mk_s, mv_s, og_s, gt_s, gout_ref, caug_ref, m_ref, write_h, chunk)
    k_cur = ak_s[0]
    v_cur = av_s[0]
    _swa_tile(sink_ref, aq_s,
              jnp.concatenate([kprev_s[...], k_cur], axis=0),
              jnp.concatenate([vprev_s[...], v_cur], axis=0), j * rows, write_o)
    kprev_s[...] = k_cur[rows - SWA_Q_ROWS:]
    vprev_s[...] = v_cur[rows - SWA_Q_ROWS:]

    ffn = jnp.zeros(hb.shape, F32)
    for c in range(D_FF // FF_BLOCK):
        cols = slice(c * FF_BLOCK, (c + 1) * FF_BLOCK)
        ffn = ffn + _ffn_block(hb, wu_ref[:, cols], wd_ref[cols, :])
    y_ref[...] = x1 + ga2_ref[...] * ffn.reshape(x1.shape)

    @pl.when(jnp.logical_and(j == tiles_per_seq - 1, valid))
    def _():
        kt_ref[0] = k_cur[rows - SWA_Q_ROWS:]
        vt_ref[0] = v_cur[rows - SWA_Q_ROWS:]
        _mlstm_store_state(caug_ref, m_ref, c_out_ref, n_out_ref, m_out_ref)


def _layer(x, mod4, mod_row0, sinks, gout, proj_w, ffn_w, *, rows, chunk):
    ns, length, d = x.shape
    tps = length // rows
    n_tiles = ns * tps

    def stage(lag):
        def tile_idx(i):
            t = jnp.clip(i - lag, 0, n_tiles - 1)
            return t // tps, t % tps
        x_tile = pl.BlockSpec((1, rows, d), lambda i: (*tile_idx(i), 0))
        mod = lambda k: pl.BlockSpec((1, None, 1, d),
                                     lambda i: (mod_row0 + tile_idx(i)[0], k, 0, 0))
        per_seq = lambda shape: pl.BlockSpec(
            (1,) + shape, lambda i: (tile_idx(i)[0],) + (0,) * len(shape))
        return x_tile, mod, per_seq

    xp_tile, p_mod, p_seq = stage(0)
    xf_tile, f_mod, _ = stage(1)
    st_c, st_n, st_m = p_seq((M_HEADS, M_DIM, M_DIM)), p_seq((M_HEADS, M_DIM)), p_seq((M_HEADS, 1))
    tail = p_seq((SWA_Q_ROWS, A_KW))
    one = lambda shape, dtype: pltpu.VMEM((1,) + shape, dtype)
    return pl.pallas_call(
        functools.partial(_layer_body, n_tiles=n_tiles, tiles_per_seq=tps, chunk=chunk),
        grid=(n_tiles + 1,),
        in_specs=([pl.BlockSpec(memory_space=pltpu.SMEM), xp_tile, p_mod(MOD_SC1), p_mod(MOD_SH1)]
                  + _proj_weight_specs(d)
                  + [_const_spec((M_HEADS, M_DIM)), xf_tile,
                     f_mod(MOD_GA1), f_mod(MOD_SC2), f_mod(MOD_SH2), f_mod(MOD_GA2)]
                  + _ffn_weight_specs(d)),
        out_specs=(xf_tile, tail, tail, st_c, st_n, st_m),
        out_shape=(jax.ShapeDtypeStruct((ns, length, d), F32),
                   jax.ShapeDtypeStruct((ns, SWA_Q_ROWS, A_KW), F32),
                   jax.ShapeDtypeStruct((ns, SWA_Q_ROWS, A_KW), F32),
                   jax.ShapeDtypeStruct((ns, M_HEADS, M_DIM, M_DIM), F32),
                   jax.ShapeDtypeStruct((ns, M_HEADS, M_DIM), F32),
                   jax.ShapeDtypeStruct((ns, M_HEADS, 1), F32)),
        scratch_shapes=[one((rows, M_W), BF16), one((rows, M_W), BF16),
                        one((rows, M_W), BF16), one((rows, M_W), BF16),
                        pltpu.VMEM((GATE_ROWS, rows), F32), one((rows, A_QW), BF16),
                        one((rows, A_KW), F32), one((rows, A_KW), F32),
                        pltpu.VMEM((SWA_Q_ROWS, A_KW), F32), pltpu.VMEM((SWA_Q_ROWS, A_KW), F32),
                        pltpu.VMEM((rows, M_W + A_QW), BF16),
                        pltpu.VMEM((M_HEADS, M_DIM, 2 * M_DIM), F32),
                        pltpu.VMEM((M_HEADS, 1), F32)],
        compiler_params=_params(("arbitrary",)),
        name="layer",
    )(sinks, x, mod4, mod4, *proj_w, gout, x, mod4, mod4, mod4, mod4, *ffn_w)


class _LaneWindow:
    def __init__(self, ref, offset):
        self.ref, self.offset = ref, offset

    def __getitem__(self, idx):
        rows, cols = idx
        return self.ref[rows, slice(cols.start + self.offset, cols.stop + self.offset)]


def _sample_mixers(sink_ref, x_ref, sc1_ref, sh1_ref, g1_ref, wa_ref, wb_ref, wg_ref, bg_ref,
                   gq_ref, gk_ref, pq_ref, pk_ref, c0_ref, n0_ref, m0_ref, gout_ref,
                   ck_ref, cv_ref, ko_ref, vo_ref, c_out_ref, n_out_ref, m_out_ref,
                   mq_s, mk_s, mv_s, og_s, gt_s, aq_s, ak_s, av_s, mix_s, caug_s, m_s):
    ns, t, _ = x_ref.shape
    cached = ck_ref.shape[1]
    proj_s = {"mq": mq_s, "mk": mk_s, "mv": mv_s, "og": og_s, "gt": gt_s,
              "aq": aq_s, "ak": ak_s, "av": av_s}

    def store(name, value):
        proj_s[name][...] = value if name == "gt" else value.reshape(ns, t, value.shape[-1])

    _in_proj_tile(x_ref[...], sc1_ref[...], sh1_ref[...], g1_ref[...], wa_ref, wb_ref, wg_ref,
                  bg_ref[...], gq_ref[...], gk_ref[...], pq_ref, pk_ref, store, t)

    seq = lambda ref, b: ref.at[pl.ds(b, 1)]
    for b in range(ns):
        caug_b, m_b = caug_s.at[b], m_s.at[b]
        _mlstm_load_state(seq(c0_ref, b), seq(n0_ref, b), seq(m0_ref, b), caug_b, m_b)

        def write_h(rows, lanes, value, b=b):
            mix_s[b * t + rows.start:b * t + rows.stop, lanes] = value

        _mlstm_tile(seq(mq_s, b), seq(mk_s, b), seq(mv_s, b), seq(og_s, b),
                    _LaneWindow(gt_s, b * t), gout_ref, caug_b, m_b, write_h, t)
        _mlstm_store_state(caug_b, m_b, seq(c_out_ref, b), seq(n_out_ref, b), seq(m_out_ref, b))

    tk = 2 * LANES
    pad = jnp.zeros((tk - cached - t, A_KW), F32)
    q_pos = PAST_LEN + lax.broadcasted_iota(jnp.int32, (t, 1), 0)
    idx = lax.broadcasted_iota(jnp.int32, (1, tk), 1)
    k_pos = jnp.where(idx < cached + t, PAST_LEN - cached + idx, -1)
    biases = _swa_biases(q_pos, k_pos)
    for b in range(ns):
        k_new, v_new = ak_s[b], av_s[b]

        def write_o(lanes, value, b=b):
            mix_s[b * t:(b + 1) * t, M_W + lanes.start:M_W + lanes.stop] = value

        _swa_block(aq_s[b], _kv_halves(jnp.concatenate([ck_ref[b], k_new, pad], axis=0)),
                   _kv_halves(jnp.concatenate([cv_ref[b], v_new, pad], axis=0)),
                   biases, sink_ref, write_o)
        ko_ref[b] = jnp.concatenate([ck_ref[b, t:, :], k_new], axis=0)
        vo_ref[b] = jnp.concatenate([cv_ref[b, t:, :], v_new], axis=0)


SAMPLE_FF_BLOCK = 256
WEIGHT_SLOTS = 4


def _weight_block_copies(wu_hbm, wd_hbm, wu_buf, wd_buf, sem, block, slot):
    fb = wu_buf.shape[2]
    start = pl.multiple_of(block * fb, fb)
    return (pltpu.make_async_copy(wu_hbm.at[:, pl.ds(start, fb)], wu_buf.at[slot], sem.at[0, slot]),
            pltpu.make_async_copy(wd_hbm.at[pl.ds(start, fb), :], wd_buf.at[slot], sem.at[1, slot]))


def _sample_body(sink_ref, x_ref, sc1_ref, sh1_ref, g1_ref, wt_ref, bg_ref, gq_ref, gk_ref,
                 pq_ref, pk_ref, c0_ref, n0_ref, m0_ref, gout_ref, ck_ref, cv_ref,
                 ga1_ref, sc2_ref, sh2_ref, ga2_ref, g2_ref, wo_ref, wu_ref, wd_ref,
                 y_ref, ko_ref, vo_ref, c_out_ref, n_out_ref, m_out_ref,
                 wa_o, wb_o, wg_o, wo_o, wu_o, wd_o,
                 mq_s, mk_s, mv_s, og_s, gt_s, aq_s, ak_s, av_s, mix_s, caug_s, m_s,
                 x1_s, hb_s, acc_s, wu_buf, wd_buf, w_sem):
    step = pl.program_id(0)
    ns, t, d = x_ref.shape
    n_blocks = pl.num_programs(0) - 1
    ahead = WEIGHT_SLOTS - 1

    def copies(block, slot):
        return _weight_block_copies(wu_ref, wd_ref, wu_buf, wd_buf, w_sem, block, slot)

    @pl.when(step == 0)
    def _():
        for b in range(ahead):
            for cp in copies(b, b):
                cp.start()
        g_lo = 4 * M_W
        g_hi = g_lo + 2 * M_HEADS
        wa_o[...] = wt_ref[:g_lo, :].astype(BF16)
        wg_o[...] = jnp.concatenate(
            [wt_ref[g_lo:g_hi, :], jnp.zeros((BF16_ROWS - 2 * M_HEADS, d), F32)],
            axis=0).astype(BF16)
        wb_o[...] = wt_ref[g_hi:, :].astype(BF16)
        wo_o[...] = wo_ref[...].astype(BF16)
        _sample_mixers(sink_ref, x_ref, sc1_ref, sh1_ref, g1_ref, wa_o, wb_o, wg_o, bg_ref,
                       gq_ref, gk_ref, pq_ref, pk_ref, c0_ref, n0_ref, m0_ref, gout_ref,
                       ck_ref, cv_ref, ko_ref, vo_ref, c_out_ref, n_out_ref, m_out_ref,
                       mq_s, mk_s, mv_s, og_s, gt_s, aq_s, ak_s, av_s, mix_s, caug_s, m_s)
        x1, hb = _mix_residual_norm(x_ref[...], mix_s[...], ga1_ref[...], sc2_ref[...],
                                    sh2_ref[...], g2_ref[...], wo_o[...])
        x1_s[...] = x1
        hb_s[...] = hb
        acc_s[...] = jnp.zeros(acc_s.shape, acc_s.dtype)

    @pl.when(step >= 1)
    def _():
        block = step - 1
        slot = block % WEIGHT_SLOTS
        nxt = block + ahead

        @pl.when(nxt < n_blocks)
        def _():
            for cp in copies(nxt, nxt % WEIGHT_SLOTS):
                cp.start()

        for cp in copies(block, slot):
            cp.wait()
        w_up = wu_buf[slot].astype(BF16)
        w_down = wd_buf[slot].astype(BF16)
        wu_o[...] = w_up
        wd_o[...] = w_down
        acc_s[...] += _ffn_block(hb_s[...], w_up, w_down)

    @pl.when(step == pl.num_programs(0) - 1)
    def _():
        y_ref[...] = x1_s[...] + ga2_ref[...] * acc_s[...].reshape(ns, t, d)


def _sample_layer(x, mod4, sinks, c0, n0, m0, gout, cache_k, cache_v, proj_small, wt, g2,
                  w_out, w_up, w_down):
    ns, t, d = x.shape
    cached = cache_k.shape[1]
    assert cached >= t
    g1, bg, gq_t, gk_t, pq, pk = proj_small
    fb = SAMPLE_FF_BLOCK
    n_blocks = D_FF // fb
    blk = lambda i: jnp.maximum(i - 1, 0)
    whole = lambda shape: pl.BlockSpec(shape, lambda i: (0,) * len(shape))
    mod = lambda k: pl.BlockSpec((ns, None, 1, d), lambda i: (0, k, 0, 0))
    st_c, st_n, st_m = (whole((ns, M_HEADS, M_DIM, M_DIM)), whole((ns, M_HEADS, M_DIM)),
                        whole((ns, M_HEADS, 1)))
    kv = whole((ns, cached, A_KW))
    up_blk = pl.BlockSpec((d, fb), lambda i: (0, blk(i)))
    down_blk = pl.BlockSpec((fb, d), lambda i: (blk(i), 0))
    assert n_blocks >= WEIGHT_SLOTS
    up_in = down_in = pl.BlockSpec(memory_space=pl.ANY)
    n_in = wt.shape[0]
    n_b = n_in - 4 * M_W - 2 * M_HEADS
    return pl.pallas_call(
        _sample_body,
        grid=(1 + n_blocks,),
        in_specs=[pl.BlockSpec(memory_space=pltpu.SMEM), whole((ns, t, d)),
                  mod(MOD_SC1), mod(MOD_SH1),
                  _const_spec((1, d)), _const_spec((n_in, d)), _const_spec((2 * M_HEADS, 1)),
                  _const_spec((1, A_QW)), _const_spec((1, A_KW)),
                  _const_spec((A_QW, A_QW)), _const_spec((A_KW, A_KW)),
                  st_c, st_n, st_m, _const_spec((M_HEADS, M_DIM)), kv, kv,
                  mod(MOD_GA1), mod(MOD_SC2), mod(MOD_SH2), mod(MOD_GA2),
                  _const_spec((1, d)), _const_spec((d, d)), up_in, down_in],
        out_specs=(whole((ns, t, d)), kv, kv, st_c, st_n, st_m,
                   _const_spec((4 * M_W, d)), _const_spec((n_b, d)), _const_spec((BF16_ROWS, d)),
                   _const_spec((d, d)), up_blk, down_blk),
        out_shape=(jax.ShapeDtypeStruct((ns, t, d), F32),
                   jax.ShapeDtypeStruct((ns, cached, A_KW), F32),
                   jax.ShapeDtypeStruct((ns, cached, A_KW), F32),
                   jax.ShapeDtypeStruct((ns, M_HEADS, M_DIM, M_DIM), F32),
                   jax.ShapeDtypeStruct((ns, M_HEADS, M_DIM), F32),
                   jax.ShapeDtypeStruct((ns, M_HEADS, 1), F32),
                   jax.ShapeDtypeStruct((4 * M_W, d), BF16),
                   jax.ShapeDtypeStruct((n_b, d), BF16),
                   jax.ShapeDtypeStruct((BF16_ROWS, d), BF16),
                   jax.ShapeDtypeStruct((d, d), BF16),
                   jax.ShapeDtypeStruct((d, D_FF), BF16),
                   jax.ShapeDtypeStruct((D_FF, d), BF16)),
        scratch_shapes=[pltpu.VMEM((ns, t, M_W), BF16), pltpu.VMEM((ns, t, M_W), BF16),
                        pltpu.VMEM((ns, t, M_W), BF16), pltpu.VMEM((ns, t, M_W), BF16),
                        pltpu.VMEM((GATE_ROWS, ns * t), F32), pltpu.VMEM((ns, t, A_QW), BF16),
                        pltpu.VMEM((ns, t, A_KW), F32), pltpu.VMEM((ns, t, A_KW), F32),
                        pltpu.VMEM((ns * t, M_W + A_QW), BF16),
                        pltpu.VMEM((ns, M_HEADS, M_DIM, 2 * M_DIM), F32),
                        pltpu.VMEM((ns, M_HEADS, 1), F32),
                        pltpu.VMEM((ns, t, d), F32), pltpu.VMEM((ns * t, d), BF16),
                        pltpu.VMEM((ns * t, d), F32),
                        pltpu.VMEM((WEIGHT_SLOTS, d, fb), F32),
                        pltpu.VMEM((WEIGHT_SLOTS, fb, d), F32),
                        pltpu.SemaphoreType.DMA((2, WEIGHT_SLOTS))],
        compiler_params=_params(("arbitrary",)),
        name="sample_layer",
    )(sinks, x, mod4, mod4, g1, wt, bg, gq_t, gk_t, pq, pk, c0, n0, m0, gout, cache_k, cache_v,
      mod4, mod4, mod4, mod4, g2, w_out, w_up, w_down)


def _block_ones(n):
    g = jnp.arange(n, dtype=jnp.int32) // A_DIM
    return (g[:, None] == g[None, :]).astype(BF16)


def kernel(x_prompt, x_sample, c_prompt, c_sample, cache_swa_k, cache_swa_v, state_mlstm_C,
           state_mlstm_n, state_mlstm_m, w_ada, b_ada, g_norm1, w_in, b_gates, g_q, g_k, sinks,
           g_mlstm_out, w_out, g_norm2, w_up, w_down):
    depth = w_ada.shape[0]
    assert depth == 1
    bp, lp, d = x_prompt.shape
    bs, ls, _ = x_sample.shape
    wb = cache_swa_k.shape[2]

    xp, xs = x_prompt, x_sample
    for l in range(depth):
        gq_t = (jnp.tile(g_q[l], A_HEADS) * (LOG2E * A_DIM ** -0.5))[None, :]
        proj_small = (g_norm1[l][None, :], b_gates[l][:, None], gq_t,
                      jnp.tile(g_k[l], A_KV_HEADS)[None, :], _block_ones(A_QW), _block_ones(A_KW))
        g2 = g_norm2[l][None, :]

        mod_rows = -(-(bp + bs) // 8) * 8
        c_all = jnp.concatenate([c_sample, c_prompt, jnp.zeros((mod_rows - bp - bs, d), F32)],
                                axis=0)
        mod4 = _ada_mod(c_all, w_ada[l], b_ada[l][None, :]).reshape(mod_rows, 6, 1, d)

        xs, ko, vo, c, n, m, wa, wb_, wg, wo, wu, wd = _sample_layer(
            xs, mod4, sinks[l], state_mlstm_C[l], state_mlstm_n[l], state_mlstm_m[l][..., None],
            g_mlstm_out[l], cache_swa_k[l].reshape(bs, wb, A_KW),
            cache_swa_v[l].reshape(bs, wb, A_KW), proj_small, w_in[l].T, g2,
            w_out[l], w_up[l], w_down[l])
        st_s = (ko.reshape(bs, wb, A_KV_HEADS, A_DIM), vo.reshape(bs, wb, A_KV_HEADS, A_DIM),
                c, n, m[..., 0])

        g1, bg, gq_t, gk_t, pq, pk = proj_small
        xp, kt, vt, c, n, m = _layer(xp, mod4, bs, sinks[l], g_mlstm_out[l],
                                     (g1, wa, wb_, wg, bg, gq_t, gk_t, pq, pk), (g2, wo, wu, wd),
                                     rows=ROW_TILE, chunk=MLSTM_CHUNK)
        win = WIN_CHUNKS * CHUNK
        st_p = (kt.reshape(bp, win, A_KV_HEADS, A_DIM), vt.reshape(bp, win, A_KV_HEADS, A_DIM),
                c, n, m[..., 0])

    return (xp, xs) + tuple(s[None] for s in st_p) + tuple(s[None] for s in st_s)
```

```python
import functools

import jax
import jax.numpy as jnp
from jax import lax
from jax.experimental import pallas as pl
from jax.experimental.pallas import tpu as pltpu

F32 = jnp.float32
BF16 = jnp.bfloat16

D_MODEL = 1024
CHUNK = 64
M_HEADS = 4
M_DIM = 128
M_W = M_HEADS * M_DIM
A_HEADS = 8
A_KV_HEADS = 2
A_DIM = 64
A_QW = A_HEADS * A_DIM
A_KW = A_KV_HEADS * A_DIM
WIN_CHUNKS = 2
D_FF = 4 * D_MODEL
EPS = 1e-6
PAST_LEN = 4096

LANES = 128
BF16_ROWS = 16
VMEM_LIMIT = 60 * 1024 * 1024
NEG_INF = float("-inf")
LOG2E = 1.4426950408889634
ROW_TILE = 512
MLSTM_CHUNK = 128
FF_BLOCK = 2048
NT_DIMS = (((1,), (1,)), ((), ()))
MOD_SH1, MOD_SC1, MOD_GA1, MOD_SH2, MOD_SC2, MOD_GA2 = range(6)


def _params(sem):
    return pltpu.CompilerParams(dimension_semantics=sem, vmem_limit_bytes=VMEM_LIMIT)


def _const_spec(shape):
    nd = len(shape)
    return pl.BlockSpec(shape, lambda *_: (0,) * nd, pipeline_mode=pl.Buffered(1))


def _ada_body(c_ref, w_ref, b_ref, o_ref):
    c = c_ref[...]
    s = c * jax.nn.sigmoid(c)
    o_ref[...] = jnp.dot(s.astype(BF16), w_ref[...].astype(BF16),
                         preferred_element_type=F32) + b_ref[...]


def _ada_mod(c_all, w_ada, b_ada):
    rows = c_all.shape[0]
    return pl.pallas_call(
        _ada_body,
        grid=(6,),
        in_specs=[pl.BlockSpec((rows, D_MODEL), lambda j: (0, 0)),
                  pl.BlockSpec((D_MODEL, D_MODEL), lambda j: (0, j)),
                  pl.BlockSpec((1, D_MODEL), lambda j: (0, j))],
        out_specs=pl.BlockSpec((rows, D_MODEL), lambda j: (0, j)),
        out_shape=jax.ShapeDtypeStruct((rows, 6 * D_MODEL), F32),
        compiler_params=_params(("arbitrary",)),
        name="ada_mod",
    )(c_all, w_ada, b_ada)


def _segment_prefix(x, op, identity, seg):
    assert x.shape[1] % LANES == 0 and x.shape[1] % seg == 0
    pos = lax.broadcasted_iota(jnp.int32, x.shape, 1) % seg
    d = 1
    while d < seg:
        x = op(x, jnp.where(pos >= d, pltpu.roll(x, d, axis=1), identity))
        d *= 2
    return x


GATE_ROWS = 24
PROJ_OUTPUTS = ("mq", "mk", "mv", "og", "gt", "aq", "ak", "av")


def _in_proj_tile(x, sc, sh, g1, wa_ref, wb_ref, wg_ref, bg, gq, gk, pq_ref, pk_ref, store,
                  gate_chunk):
    s, r, d = x.shape
    m = s * r
    ms = jnp.mean(x * x, axis=-1, keepdims=True)
    y = x * lax.rsqrt(ms + EPS) * g1
    h = y * (1.0 + sc) + sh
    hb = h.reshape(m, d).astype(BF16)

    def proj(w_ref, lo, width):
        return lax.dot_general(hb, w_ref[lo:lo + width, :], NT_DIMS, preferred_element_type=F32)

    zg = lax.dot_general(wg_ref[...], hb, NT_DIMS,
                         preferred_element_type=F32)[:2 * M_HEADS] + bg
    row = lax.broadcasted_iota(jnp.int32, zg.shape, 0)
    log_sig = jnp.minimum(zg, 0.0) - jnp.log1p(jnp.exp(-jnp.abs(zg)))
    gt = jnp.where(row < M_HEADS, zg, log_sig)
    b8 = _segment_prefix(gt, jnp.add, 0.0, gate_chunk)
    c8 = gt - pltpu.roll(b8, M_HEADS, axis=0)
    cm8 = _segment_prefix(c8, jnp.maximum, NEG_INF, gate_chunk)
    store("gt", jnp.concatenate([c8, b8, cm8], axis=0))

    store("mq", proj(wa_ref, 0, M_W).astype(BF16))
    store("mk", (proj(wa_ref, M_W, M_W) * (M_DIM ** -0.5)).astype(BF16))
    store("mv", proj(wa_ref, 2 * M_W, M_W).astype(BF16))
    store("og", jax.nn.sigmoid(proj(wa_ref, 3 * M_W, M_W)).astype(BF16))

    aq = proj(wb_ref, 0, A_QW)
    ss_q = jnp.dot((aq * aq).astype(BF16), pq_ref[...], preferred_element_type=F32)
    store("aq", (aq * lax.rsqrt(ss_q * (1.0 / A_DIM) + EPS) * gq).astype(BF16))
    ak = proj(wb_ref, A_QW, A_KW)
    sq = ak * ak
    hi = sq.astype(BF16)
    lo = (sq - hi.astype(F32)).astype(BF16)
    ss_k = (jnp.dot(hi, pk_ref[...], preferred_element_type=F32)
            + jnp.dot(lo, pk_ref[...], preferred_element_type=F32))
    store("ak", ak * lax.rsqrt(ss_k * (1.0 / A_DIM) + EPS) * gk)
    store("av", proj(wb_ref, A_QW + A_KW, A_KW))


def _proj_weight_specs(d):
    return [_const_spec((1, d)), _const_spec((4 * M_W, d)), _const_spec((A_QW + 2 * A_KW, d)),
            _const_spec((BF16_ROWS, d)), _const_spec((2 * M_HEADS, 1)),
            _const_spec((1, A_QW)), _const_spec((1, A_KW)),
            _const_spec((A_QW, A_QW)), _const_spec((A_KW, A_KW))]


def _col_bcast(r):
    t = r.shape[1]
    if t % LANES == 0:
        return jnp.broadcast_to(r, (LANES, t)).T
    src = lax.broadcasted_iota(jnp.int32, (t, t), 0)
    dst = lax.broadcasted_iota(jnp.int32, (t, t), 1)
    col = jnp.sum(jnp.where(src == dst, jnp.broadcast_to(r, (t, t)), 0.0), axis=1, keepdims=True)
    return jnp.broadcast_to(col, (t, LANES))


def _eye_dim():
    return (lax.broadcasted_iota(jnp.int32, (M_DIM, M_DIM), 0)
            == lax.broadcasted_iota(jnp.int32, (M_DIM, M_DIM), 1))


def _mlstm_load_state(c0_ref, n0_ref, m0_ref, caug_ref, m_ref):
    for hd in range(M_HEADS):
        caug_ref[hd, :, :M_DIM] = c0_ref[0, hd]
        n_row = jnp.broadcast_to(n0_ref[0, hd:hd + 1, :], (M_DIM, M_DIM))
        n_col = jnp.sum(jnp.where(_eye_dim(), n_row, 0.0), axis=1, keepdims=True)
        caug_ref[hd, :, M_DIM:] = jnp.broadcast_to(n_col, (M_DIM, M_DIM))
    m_ref[...] = m0_ref[0]


def _mlstm_store_state(caug_ref, m_ref, c_out_ref, n_out_ref, m_out_ref):
    for hd in range(M_HEADS):
        c_out_ref[0, hd] = caug_ref[hd, :, :M_DIM]
        n_out_ref[0, hd:hd + 1, :] = jnp.sum(
            jnp.where(_eye_dim(), caug_ref[hd, :, M_DIM:], 0.0), axis=0, keepdims=True)
    m_out_ref[0] = m_ref[...]


def _mlstm_tile(q_ref, k_ref, v_ref, og_ref, gt_ref, gout_ref, caug_ref, m_ref, write_h, chunk):
    t = chunk
    n_chunks = q_ref.shape[1] // t
    bs = LANES if t % LANES == 0 else t
    tril = (lax.broadcasted_iota(jnp.int32, (bs, bs), 1)
            <= lax.broadcasted_iota(jnp.int32, (bs, bs), 0))
    m_prev = m_ref[...]
    for j in range(n_chunks):
        rows = slice(j * t, (j + 1) * t)
        c = gt_ref[0:M_HEADS, rows]
        b = gt_ref[12:12 + M_HEADS, rows]
        g = jnp.maximum(gt_ref[16:16 + M_HEADS, rows], m_prev)
        g_last = g[:, t - 1:t]
        c2 = c * LOG2E
        g2 = g * LOG2E
        e_all = jnp.exp2(-LOG2E * (b + g))
        w_src_all = jnp.exp2(c2 - LOG2E * g_last)
        w_inter_all = jnp.exp2(LOG2E * m_prev - g2)
        w_old_all = jnp.exp2(LOG2E * (m_prev - g_last))
        m_prev = b[:, t - 1:t] + g_last

        for hd in range(M_HEADS):
            sl = slice(hd * M_DIM, (hd + 1) * M_DIM)
            hr = slice(hd, hd + 1)
            g_b = _col_bcast(g2[hr])
            e_b = _col_bcast(e_all[hr])
            w_src = _col_bcast(w_src_all[hr])
            w_inter = _col_bcast(w_inter_all[hr])

            q = q_ref[0, rows, sl]
            k = k_ref[0, rows, sl]
            v = v_ref[0, rows, sl]
            caug = caug_ref[hd]
            inter = jnp.dot(q, caug.astype(BF16), preferred_element_type=F32)
            gain = gout_ref[hr, :]
            for rb in range(t // bs):
                rsl = slice(rb * bs, (rb + 1) * bs)
                ncols = (rb + 1) * bs
                s_mat = lax.dot_general(q[rsl], k[:ncols], NT_DIMS,
                                        preferred_element_type=F32)
                decay = []
                for cb in range(rb + 1):
                    arg = c2[hr, cb * bs:(cb + 1) * bs] - g_b[rsl, :bs]
                    decay.append(jnp.exp2(jnp.where(tril, arg, NEG_INF) if cb == rb else arg))
                s_mat = s_mat * (decay[0] if rb == 0 else jnp.concatenate(decay, axis=1))
                num = (jnp.dot(s_mat.astype(BF16), v[:ncols], preferred_element_type=F32)
                       + w_inter[rsl] * inter[rsl, :M_DIM])
                den = jnp.sum(s_mat, axis=1, keepdims=True) + w_inter[rsl] * inter[rsl, M_DIM:]
                hh = num / jnp.maximum(jnp.abs(den), e_b[rsl])
                hn = hh * lax.rsqrt(jnp.mean(hh * hh, axis=-1, keepdims=True) + EPS) * gain
                out_rows = slice(j * t + rb * bs, j * t + (rb + 1) * bs)
                write_h(out_rows, sl, (hn * og_ref[0, out_rows, sl].astype(F32)).astype(BF16))

            rhs = jnp.concatenate([v.astype(F32) * w_src, w_src], axis=1).astype(BF16)
            upd = lax.dot_general(k, rhs, (((0,), (0,)), ((), ())), preferred_element_type=F32)
            caug_ref[hd] = w_old_all[hr] * caug + upd
    m_ref[...] = m_prev


def _kv_halves(a):
    low = lax.broadcasted_iota(jnp.int32, a.shape, 1) < A_DIM
    lo_half = jnp.where(low, a, 0.0)
    hi_half = jnp.where(low, 0.0, a)
    pairs = [(lo_half, pltpu.roll(lo_half, A_DIM, axis=1)),
             (pltpu.roll(hi_half, A_DIM, axis=1), hi_half)]
    return [tuple(h.astype(BF16) for h in pair) for pair in pairs]


def _swa_biases(q_pos, k_pos):
    shift = CHUNK.bit_length() - 1
    assert 1 << shift == CHUNK
    qc = lax.shift_right_arithmetic(q_pos, shift)
    kc = lax.shift_right_arithmetic(k_pos, shift)
    allowed = (kc <= qc) & (kc >= qc - WIN_CHUNKS) & (k_pos >= 0)
    dist = jnp.abs(q_pos - k_pos).astype(F32)
    slopes = [2.0 ** (-8.0 * (head + 1) / A_HEADS) for head in range(A_HEADS)]
    return [jnp.where(allowed, (-LOG2E * slope) * dist, NEG_INF) for slope in slopes]


def _swa_block(q, kz, vz, biases, sink_ref, write_o, valid=None):
    tq = q.shape[0]
    pairs_per_kv = A_HEADS // A_KV_HEADS // 2
    for j in range(A_KV_HEADS):
        pairs = [j * pairs_per_kv + i for i in range(pairs_per_kv)]
        q_stack = jnp.concatenate([q[:, p * LANES:(p + 1) * LANES] for p in pairs], axis=0)
        acc = [None] * pairs_per_kv
        for e in range(2):
            logits_all = lax.dot_general(q_stack, kz[j][e], NT_DIMS, preferred_element_type=F32)
            probs, inv_denoms = [], []
            for i, p in enumerate(pairs):
                head = 2 * p + e
                sink = sink_ref[head] * LOG2E
                logits = logits_all[i * tq:(i + 1) * tq] + biases[head]
                if valid is not None:
                    logits = jnp.where(valid, logits, NEG_INF)
                mx = jnp.maximum(jnp.max(logits, axis=1, keepdims=True), sink)
                pr = jnp.exp2(logits - mx)
                inv_denoms.append(
                    1.0 / (jnp.sum(pr, axis=1, keepdims=True) + jnp.exp2(sink - mx)))
                probs.append(pr.astype(BF16))
            out_all = jnp.dot(jnp.concatenate(probs, axis=0), vz[j][e],
                              preferred_element_type=F32)
            for i in range(pairs_per_kv):
                part = out_all[i * tq:(i + 1) * tq] * inv_denoms[i]
                acc[i] = part if acc[i] is None else acc[i] + part
        for i, p in enumerate(pairs):
            write_o(slice(p * LANES, (p + 1) * LANES), acc[i].astype(BF16))


SWA_Q_ROWS = WIN_CHUNKS * CHUNK


def _swa_tile(sink_ref, q_ref, k_all, v_all, tile_start, write_o):
    tq = SWA_Q_ROWS
    kz = _kv_halves(k_all)
    vz = _kv_halves(v_all)
    rel_k = lax.broadcasted_iota(jnp.int32, (1, 2 * tq), 1)
    biases = _swa_biases(tq + lax.broadcasted_iota(jnp.int32, (tq, 1), 0), rel_k)
    for r in range(q_ref.shape[1] // tq):
        rows = slice(r * tq, (r + 1) * tq)
        band = slice(r * tq, (r + 2) * tq)
        valid = (tile_start - tq + rel_k >= 0) if r == 0 else None
        _swa_block(q_ref[0, rows, :],
                   [tuple(h[band] for h in pair) for pair in kz],
                   [tuple(h[band] for h in pair) for pair in vz],
                   biases, sink_ref,
                   lambda lanes, value, rows=rows: write_o(rows, lanes, value), valid=valid)


def _mix_residual_norm(x, mixed, ga1, sc2, sh2, g2, w_out):
    s, r, d = x.shape
    mix = jnp.dot(mixed, w_out, preferred_element_type=F32)
    x1 = x + ga1 * mix.reshape(s, r, d)
    ms = jnp.mean(x1 * x1, axis=-1, keepdims=True)
    h2 = (x1 * lax.rsqrt(ms + EPS) * g2) * (1.0 + sc2) + sh2
    return x1, h2.reshape(s * r, d).astype(BF16)


def _ffn_block(hb, w_up, w_down):
    u = jnp.dot(hb, w_up, preferred_element_type=F32)
    u = jnp.square(jnp.maximum(u, 0.0)).astype(BF16)
    return jnp.dot(u, w_down, preferred_element_type=F32)


def _ffn_weight_specs(d):
    return [_const_spec((1, d)), _const_spec((d, d)), _const_spec((d, D_FF)),
            _const_spec((D_FF, d))]


def _layer_body(sink_ref, xp_ref, sc1_ref, sh1_ref, g1_ref, wa_ref, wb_ref, wg_ref, bg_ref,
                gq_ref, gk_ref, pq_ref, pk_ref, gout_ref,
                xf_ref, ga1_ref, sc2_ref, sh2_ref, ga2_ref, g2_ref, wo_ref, wu_ref, wd_ref,
                y_ref, kt_ref, vt_ref, c_out_ref, n_out_ref, m_out_ref,
                mq_s, mk_s, mv_s, og_s, gt_s, aq_s, ak_s, av_s, kprev_s, vprev_s,
                mix_s, caug_ref, m_ref, *, n_tiles, tiles_per_seq, chunk):
    s = pl.program_id(0)
    rows = xp_ref.shape[1]
    tile = jnp.minimum(s, n_tiles - 1)
    j = tile % tiles_per_seq
    valid = s < n_tiles
    proj_s = {"mq": mq_s, "mk": mk_s, "mv": mv_s, "og": og_s, "gt": gt_s,
              "aq": aq_s, "ak": ak_s, "av": av_s}

    @pl.when(s == 0)
    def _():
        kprev_s[...] = jnp.zeros(kprev_s.shape, kprev_s.dtype)
        vprev_s[...] = jnp.zeros(vprev_s.shape, vprev_s.dtype)
        mix_s[...] = jnp.zeros(mix_s.shape, mix_s.dtype)

    @pl.when(jnp.logical_and(j == 0, valid))
    def _():
        caug_ref[...] = jnp.zeros(caug_ref.shape, caug_ref.dtype)
        m_ref[...] = jnp.zeros(m_ref.shape, m_ref.dtype)

    x1, hb = _mix_residual_norm(xf_ref[...], mix_s[...], ga1_ref[...], sc2_ref[...],
                                sh2_ref[...], g2_ref[...], wo_ref[...])

    def store(name, value):
        proj_s[name][...] = value if name == "gt" else value[None]

    _in_proj_tile(xp_ref[...], sc1_ref[...], sh1_ref[...], g1_ref[...], wa_ref, wb_ref, wg_ref,
                  bg_ref[...], gq_ref[...], gk_ref[...], pq_ref, pk_ref, store, chunk)

    def write_h(r, lanes, value):
        mix_s[r, lanes] = value

    def write_o(r, lanes, value):
        mix_s[r, slice(M_W + lanes.start, M_W + lanes.stop)] = value

    _mlstm_tile(mq_s, mk_s, mv_s, og_s, gt_s, gout_ref, caug_ref, m_ref, write_h, chunk)
    k_cur = ak_s[0]
    v_cur = av_s[0]
    _swa_tile(sink_ref, aq_s,
              jnp.concatenate([kprev_s[...], k_cur], axis=0),
              jnp.concatenate([vprev_s[...], v_cur], axis=0), j * rows, write_o)
    kprev_s[...] = k_cur[rows - SWA_Q_ROWS:]
    vprev_s[...] = v_cur[rows - SWA_Q_ROWS:]

    ffn = jnp.zeros(hb.shape, F32)
    for c in range(D_FF // FF_BLOCK):
        cols = slice(c * FF_BLOCK, (c + 1) * FF_BLOCK)
        ffn = ffn + _ffn_block(hb, wu_ref[:, cols], wd_ref[cols, :])
    y_ref[...] = x1 + ga2_ref[...] * ffn.reshape(x1.shape)

    @pl.when(jnp.logical_and(j == tiles_per_seq - 1, valid))
    def _():
        kt_ref[0] = k_cur[rows - SWA_Q_ROWS:].T
        vt_ref[0] = v_cur[rows - SWA_Q_ROWS:].T
        _mlstm_store_state(caug_ref, m_ref, c_out_ref, n_out_ref, m_out_ref)


def _layer(x, mod4, mod_row0, sinks, gout, proj_w, ffn_w, *, rows, chunk):
    ns, length, d = x.shape
    tps = length // rows
    n_tiles = ns * tps

    def stage(lag):
        def tile_idx(i):
            t = jnp.clip(i - lag, 0, n_tiles - 1)
            return t // tps, t % tps
        x_tile = pl.BlockSpec((1, rows, d), lambda i: (*tile_idx(i), 0))
        mod = lambda k: pl.BlockSpec((1, None, 1, d),
                                     lambda i: (mod_row0 + tile_idx(i)[0], k, 0, 0))
        per_seq = lambda shape: pl.BlockSpec(
            (1,) + shape, lambda i: (tile_idx(i)[0],) + (0,) * len(shape))
        return x_tile, mod, per_seq

    xp_tile, p_mod, p_seq = stage(0)
    xf_tile, f_mod, _ = stage(1)
    st_c, st_n, st_m = p_seq((M_HEADS, M_DIM, M_DIM)), p_seq((M_HEADS, M_DIM)), p_seq((M_HEADS, 1))
    tail = p_seq((A_KW, SWA_Q_ROWS))
    one = lambda shape, dtype: pltpu.VMEM((1,) + shape, dtype)
    return pl.pallas_call(
        functools.partial(_layer_body, n_tiles=n_tiles, tiles_per_seq=tps, chunk=chunk),
        grid=(n_tiles + 1,),
        in_specs=([pl.BlockSpec(memory_space=pltpu.SMEM), xp_tile, p_mod(MOD_SC1), p_mod(MOD_SH1)]
                  + _proj_weight_specs(d)
                  + [_const_spec((M_HEADS, M_DIM)), xf_tile,
                     f_mod(MOD_GA1), f_mod(MOD_SC2), f_mod(MOD_SH2), f_mod(MOD_GA2)]
                  + _ffn_weight_specs(d)),
        out_specs=(xf_tile, tail, tail, st_c, st_n, st_m),
        out_shape=(jax.ShapeDtypeStruct((ns, length, d), F32),
                   jax.ShapeDtypeStruct((ns, A_KW, SWA_Q_ROWS), F32),
                   jax.ShapeDtypeStruct((ns, A_KW, SWA_Q_ROWS), F32),
                   jax.ShapeDtypeStruct((ns, M_HEADS, M_DIM, M_DIM), F32),
                   jax.ShapeDtypeStruct((ns, M_HEADS, M_DIM), F32),
                   jax.ShapeDtypeStruct((ns, M_HEADS, 1), F32)),
        scratch_shapes=[one((rows, M_W), BF16), one((rows, M_W), BF16),
                        one((rows, M_W), BF16), one((rows, M_W), BF16),
                        pltpu.VMEM((GATE_ROWS, rows), F32), one((rows, A_QW), BF16),
                        one((rows, A_KW), F32), one((rows, A_KW), F32),
                        pltpu.VMEM((SWA_Q_ROWS, A_KW), F32), pltpu.VMEM((SWA_Q_ROWS, A_KW), F32),
                        pltpu.VMEM((rows, M_W + A_QW), BF16),
                        pltpu.VMEM((M_HEADS, M_DIM, 2 * M_DIM), F32),
                        pltpu.VMEM((M_HEADS, 1), F32)],
        compiler_params=_params(("arbitrary",)),
        name="layer",
    )(sinks, x, mod4, mod4, *proj_w, gout, x, mod4, mod4, mod4, mod4, *ffn_w)


class _LaneWindow:
    def __init__(self, ref, offset):
        self.ref, self.offset = ref, offset

    def __getitem__(self, idx):
        rows, cols = idx
        return self.ref[rows, slice(cols.start + self.offset, cols.stop + self.offset)]


def _sample_mixers(sink_ref, x_ref, sc1_ref, sh1_ref, g1_ref, wa_ref, wb_ref, wg_ref, bg_ref,
                   gq_ref, gk_ref, pq_ref, pk_ref, c0_ref, n0_ref, m0_ref, gout_ref,
                   ck_ref, cv_ref, ko_ref, vo_ref, c_out_ref, n_out_ref, m_out_ref,
                   mq_s, mk_s, mv_s, og_s, gt_s, aq_s, ak_s, av_s, mix_s, caug_s, m_s):
    ns, t, _ = x_ref.shape
    cached = ck_ref.shape[2]
    proj_s = {"mq": mq_s, "mk": mk_s, "mv": mv_s, "og": og_s, "gt": gt_s,
              "aq": aq_s, "ak": ak_s, "av": av_s}

    def store(name, value):
        proj_s[name][...] = value if name == "gt" else value.reshape(ns, t, value.shape[-1])

    _in_proj_tile(x_ref[...], sc1_ref[...], sh1_ref[...], g1_ref[...], wa_ref, wb_ref, wg_ref,
                  bg_ref[...], gq_ref[...], gk_ref[...], pq_ref, pk_ref, store, t)

    seq = lambda ref, b: ref.at[pl.ds(b, 1)]
    for b in range(ns):
        caug_b, m_b = caug_s.at[b], m_s.at[b]
        _mlstm_load_state(seq(c0_ref, b), seq(n0_ref, b), seq(m0_ref, b), caug_b, m_b)

        def write_h(rows, lanes, value, b=b):
            mix_s[b * t + rows.start:b * t + rows.stop, lanes] = value

        _mlstm_tile(seq(mq_s, b), seq(mk_s, b), seq(mv_s, b), seq(og_s, b),
                    _LaneWindow(gt_s, b * t), gout_ref, caug_b, m_b, write_h, t)
        _mlstm_store_state(caug_b, m_b, seq(c_out_ref, b), seq(n_out_ref, b), seq(m_out_ref, b))

    tk = 2 * LANES
    pad = jnp.zeros((tk - cached - t, A_KW), F32)
    q_pos = PAST_LEN + lax.broadcasted_iota(jnp.int32, (t, 1), 0)
    idx = lax.broadcasted_iota(jnp.int32, (1, tk), 1)
    k_pos = jnp.where(idx < cached + t, PAST_LEN - cached + idx, -1)
    biases = _swa_biases(q_pos, k_pos)
    for b in range(ns):
        k_new, v_new = ak_s[b], av_s[b]

        def write_o(lanes, value, b=b):
            mix_s[b * t:(b + 1) * t, M_W + lanes.start:M_W + lanes.stop] = value

        k_old, v_old = ck_ref[b].T, cv_ref[b].T
        _swa_block(aq_s[b], _kv_halves(jnp.concatenate([k_old, k_new, pad], axis=0)),
                   _kv_halves(jnp.concatenate([v_old, v_new, pad], axis=0)),
                   biases, sink_ref, write_o)
        ko_ref[b] = jnp.concatenate([k_old[t:], k_new], axis=0).T
        vo_ref[b] = jnp.concatenate([v_old[t:], v_new], axis=0).T


SAMPLE_FF_BLOCK = 512


def _sample_body(sink_ref, x_ref, sc1_ref, sh1_ref, g1_ref, wt_ref, bg_ref, gq_ref, gk_ref,
                 pq_ref, pk_ref, c0_ref, n0_ref, m0_ref, gout_ref, ck_ref, cv_ref,
                 ga1_ref, sc2_ref, sh2_ref, ga2_ref, g2_ref, wo_ref, wu_ref, wd_ref,
                 y_ref, ko_ref, vo_ref, c_out_ref, n_out_ref, m_out_ref,
                 wa_o, wb_o, wg_o, wo_o, wu_o, wd_o,
                 mq_s, mk_s, mv_s, og_s, gt_s, aq_s, ak_s, av_s, mix_s, caug_s, m_s,
                 x1_s, hb_s, acc_s):
    step = pl.program_id(0)
    ns, t, d = x_ref.shape

    @pl.when(step == 0)
    def _():
        g_lo = 4 * M_W
        g_hi = g_lo + 2 * M_HEADS
        wa_o[...] = wt_ref[:g_lo, :].astype(BF16)
        wg_o[...] = jnp.concatenate(
            [wt_ref[g_lo:g_hi, :], jnp.zeros((BF16_ROWS - 2 * M_HEADS, d), F32)],
            axis=0).astype(BF16)
        wb_o[...] = wt_ref[g_hi:, :].astype(BF16)
        wo_o[...] = wo_ref[...].astype(BF16)
        _sample_mixers(sink_ref, x_ref, sc1_ref, sh1_ref, g1_ref, wa_o, wb_o, wg_o, bg_ref,
                       gq_ref, gk_ref, pq_ref, pk_ref, c0_ref, n0_ref, m0_ref, gout_ref,
                       ck_ref, cv_ref, ko_ref, vo_ref, c_out_ref, n_out_ref, m_out_ref,
                       mq_s, mk_s, mv_s, og_s, gt_s, aq_s, ak_s, av_s, mix_s, caug_s, m_s)
        x1, hb = _mix_residual_norm(x_ref[...], mix_s[...], ga1_ref[...], sc2_ref[...],
                                    sh2_ref[...], g2_ref[...], wo_o[...])
        x1_s[...] = x1
        hb_s[...] = hb
        acc_s[...] = jnp.zeros(acc_s.shape, acc_s.dtype)

    @pl.when(step >= 1)
    def _():
        w_up = wu_ref[...].astype(BF16)
        w_down = wd_ref[...].astype(BF16)
        wu_o[...] = w_up
        wd_o[...] = w_down
        acc_s[...] += _ffn_block(hb_s[...], w_up, w_down)

    @pl.when(step == pl.num_programs(0) - 1)
    def _():
        y_ref[...] = x1_s[...] + ga2_ref[...] * acc_s[...].reshape(ns, t, d)


def _sample_layer(x, mod4, sinks, c0, n0, m0, gout, cache_k, cache_v, proj_small, wt, g2,
                  w_out, w_up, w_down):
    ns, t, d = x.shape
    cached = cache_k.shape[2]
    assert cached >= t
    g1, bg, gq_t, gk_t, pq, pk = proj_small
    fb = SAMPLE_FF_BLOCK
    n_blocks = D_FF // fb
    blk = lambda i: jnp.maximum(i - 1, 0)
    whole = lambda shape: pl.BlockSpec(shape, lambda i: (0,) * len(shape))
    mod = lambda k: pl.BlockSpec((ns, None, 1, d), lambda i: (0, k, 0, 0))
    st_c, st_n, st_m = (whole((ns, M_HEADS, M_DIM, M_DIM)), whole((ns, M_HEADS, M_DIM)),
                        whole((ns, M_HEADS, 1)))
    kv = whole((ns, A_KW, cached))
    up_blk = pl.BlockSpec((d, fb), lambda i: (0, blk(i)))
    down_blk = pl.BlockSpec((fb, d), lambda i: (blk(i), 0))
    n_in = wt.shape[0]
    n_b = n_in - 4 * M_W - 2 * M_HEADS
    return pl.pallas_call(
        _sample_body,
        grid=(1 + n_blocks,),
        in_specs=[pl.BlockSpec(memory_space=pltpu.SMEM), whole((ns, t, d)),
                  mod(MOD_SC1), mod(MOD_SH1),
                  _const_spec((1, d)), _const_spec((n_in, d)), _const_spec((2 * M_HEADS, 1)),
                  _const_spec((1, A_QW)), _const_spec((1, A_KW)),
                  _const_spec((A_QW, A_QW)), _const_spec((A_KW, A_KW)),
                  st_c, st_n, st_m, _const_spec((M_HEADS, M_DIM)), kv, kv,
                  mod(MOD_GA1), mod(MOD_SC2), mod(MOD_SH2), mod(MOD_GA2),
                  _const_spec((1, d)), _const_spec((d, d)), up_blk, down_blk],
        out_specs=(whole((ns, t, d)), kv, kv, st_c, st_n, st_m,
                   _const_spec((4 * M_W, d)), _const_spec((n_b, d)), _const_spec((BF16_ROWS, d)),
                   _const_spec((d, d)), up_blk, down_blk),
        out_shape=(jax.ShapeDtypeStruct((ns, t, d), F32),
                   jax.ShapeDtypeStruct((ns, A_KW, cached), F32),
                   jax.ShapeDtypeStruct((ns, A_KW, cached), F32),
                   jax.ShapeDtypeStruct((ns, M_HEADS, M_DIM, M_DIM), F32),
                   jax.ShapeDtypeStruct((ns, M_HEADS, M_DIM), F32),
                   jax.ShapeDtypeStruct((ns, M_HEADS, 1), F32),
                   jax.ShapeDtypeStruct((4 * M_W, d), BF16),
                   jax.ShapeDtypeStruct((n_b, d), BF16),
                   jax.ShapeDtypeStruct((BF16_ROWS, d), BF16),
                   jax.ShapeDtypeStruct((d, d), BF16),
                   jax.ShapeDtypeStruct((d, D_FF), BF16),
                   jax.ShapeDtypeStruct((D_FF, d), BF16)),
        scratch_shapes=[pltpu.VMEM((ns, t, M_W), BF16), pltpu.VMEM((ns, t, M_W), BF16),
                        pltpu.VMEM((ns, t, M_W), BF16), pltpu.VMEM((ns, t, M_W), BF16),
                        pltpu.VMEM((GATE_ROWS, ns * t), F32), pltpu.VMEM((ns, t, A_QW), BF16),
                        pltpu.VMEM((ns, t, A_KW), F32), pltpu.VMEM((ns, t, A_KW), F32),
                        pltpu.VMEM((ns * t, M_W + A_QW), BF16),
                        pltpu.VMEM((ns, M_HEADS, M_DIM, 2 * M_DIM), F32),
                        pltpu.VMEM((ns, M_HEADS, 1), F32),
                        pltpu.VMEM((ns, t, d), F32), pltpu.VMEM((ns * t, d), BF16),
                        pltpu.VMEM((ns * t, d), F32)],
        compiler_params=_params(("arbitrary",)),
        name="sample_layer",
    )(sinks, x, mod4, mod4, g1, wt, bg, gq_t, gk_t, pq, pk, c0, n0, m0, gout, cache_k, cache_v,
      mod4, mod4, mod4, mod4, g2, w_out, w_up, w_down)


def _to_feature_major(kv):
    b, pos = kv.shape[:2]
    return jnp.transpose(kv.reshape(b, pos, A_KW), (0, 2, 1))


def _from_feature_major(kv_t):
    b, _, pos = kv_t.shape
    return jnp.transpose(kv_t, (0, 2, 1)).reshape(b, pos, A_KV_HEADS, A_DIM)


def _block_ones(n):
    g = jnp.arange(n, dtype=jnp.int32) // A_DIM
    return (g[:, None] == g[None, :]).astype(BF16)


def kernel(x_prompt, x_sample, c_prompt, c_sample, cache_swa_k, cache_swa_v, state_mlstm_C,
           state_mlstm_n, state_mlstm_m, w_ada, b_ada, g_norm1, w_in, b_gates, g_q, g_k, sinks,
           g_mlstm_out, w_out, g_norm2, w_up, w_down):
    depth = w_ada.shape[0]
    assert depth == 1
    bp, lp, d = x_prompt.shape
    bs, ls, _ = x_sample.shape
    wb = cache_swa_k.shape[2]

    xp, xs = x_prompt, x_sample
    for l in range(depth):
        gq_t = (jnp.tile(g_q[l], A_HEADS) * (LOG2E * A_DIM ** -0.5))[None, :]
        proj_small = (g_norm1[l][None, :], b_gates[l][:, None], gq_t,
                      jnp.tile(g_k[l], A_KV_HEADS)[None, :], _block_ones(A_QW), _block_ones(A_KW))
        g2 = g_norm2[l][None, :]

        mod_rows = -(-(bp + bs) // 8) * 8
        c_all = jnp.concatenate([c_sample, c_prompt, jnp.zeros((mod_rows - bp - bs, d), F32)],
                                axis=0)
        mod4 = _ada_mod(c_all, w_ada[l], b_ada[l][None, :]).reshape(mod_rows, 6, 1, d)

        xs, ko, vo, c, n, m, wa, wb_, wg, wo, wu, wd = _sample_layer(
            xs, mod4, sinks[l], state_mlstm_C[l], state_mlstm_n[l], state_mlstm_m[l][..., None],
            g_mlstm_out[l], _to_feature_major(cache_swa_k[l]),
            _to_feature_major(cache_swa_v[l]), proj_small, w_in[l].T, g2,
            w_out[l], w_up[l], w_down[l])
        st_s = (_from_feature_major(ko), _from_feature_major(vo),
                c, n, m[..., 0])

        g1, bg, gq_t, gk_t, pq, pk = proj_small
        xp, kt, vt, c, n, m = _layer(xp, mod4, bs, sinks[l], g_mlstm_out[l],
                                     (g1, wa, wb_, wg, bg, gq_t, gk_t, pq, pk), (g2, wo, wu, wd),
                                     rows=ROW_TILE, chunk=MLSTM_CHUNK)
        win = WIN_CHUNKS * CHUNK
        st_p = (_from_feature_major(kt), _from_feature_major(vt),
                c, n, m[..., 0])

    return (xp, xs) + tuple(s[None] for s in st_p) + tuple(s[None] for s in st_s)
```
